```python
import jax
import jax.numpy as jnp
from jax import lax
import numpy as np

D_MODEL = 1024
BATCH = 32
SEQ = 2048
DEPTH = 2

GRID_W = 64
CTX_LEN = 256
N_HEADS_ATTN = 8
N_KV = 2
HEAD_DIM = 64
AXIS_DIM = HEAD_DIM // 2
ATTN_W = N_HEADS_ATTN * HEAD_DIM
KV_W = N_KV * HEAD_DIM
WINDOW = 128
QBLOCK = 128
ROPE_BASE = 10000.0
LRU_W = 512
LRU_BLOCKS = 8
LRU_BS = LRU_W // LRU_BLOCKS
CONV_W = 4
CONV_LEFT = 2
LRU_C = 8.0
MIX_IN = ATTN_W + 2 * KV_W + 2 * LRU_W
MIX_OUT = ATTN_W + LRU_W
RWKV_H = 16
RWKV_HD = D_MODEL // RWKV_H
DECAY_LORA = 64
ICLR_LORA = 64
GATE_LORA = 160
GN_EPS = 64e-5
N_EXPERTS = 16
EXPERT_FF = 1024
CAPACITY = 2
NORM_EPS = 1e-6
NEG_INF = -1e30
N_A_LAYERS = (DEPTH + 1) // 2
N_C_LAYERS = DEPTH // 2

kernel_name = 'hybrid_dit_swa_rglru_rwkv7_ecmoe'


def rmsnorm(x, g):
    xf = x.astype(jnp.float32)
    y = xf * lax.rsqrt(jnp.mean(xf * xf, axis=-1, keepdims=True) + NORM_EPS)
    return (y * g.astype(jnp.float32)).astype(x.dtype)


def modulate(h, shift, scale):
    return h * (1 + scale) + shift


def axial_angles(rows):
    row = jnp.repeat(jnp.arange(rows), GRID_W).astype(jnp.float32)
    col = jnp.tile(jnp.arange(GRID_W), rows).astype(jnp.float32)
    inv = ROPE_BASE ** (-jnp.arange(0, AXIS_DIM, 2, dtype=jnp.float32) / AXIS_DIM)
    return row[:, None] * inv, col[:, None] * inv


def rope_axis(x, ang):
    c = jnp.cos(ang)[:, None, :]
    s = jnp.sin(ang)[:, None, :]
    x1, x2 = jnp.split(x.astype(jnp.float32), 2, axis=-1)
    return jnp.concatenate([x1 * c - x2 * s, x2 * c + x1 * s], axis=-1)


def axial_rope(x, ang_row, ang_col):
    return jnp.concatenate([rope_axis(x[..., :AXIS_DIM], ang_row),
                            rope_axis(x[..., AXIS_DIM:], ang_col)], axis=-1).astype(x.dtype)


def local_attention(q, k, v, kc, vc, sink):
    B, S, H, hd = q.shape
    G = H // N_KV
    Lc = kc.shape[1]
    span = QBLOCK + 2 * WINDOW
    qg = (q * hd ** -0.5).reshape(B, S, N_KV, G, hd)
    pad = ((0, 0), (WINDOW, WINDOW), (0, 0), (0, 0))
    kp = jnp.pad(k, pad)
    vp = jnp.pad(v, pad)
    rel = jnp.arange(span)[None, :] - WINDOW - jnp.arange(QBLOCK)[:, None]
    band = jnp.abs(rel) <= WINDOW
    sink_b = jnp.broadcast_to(sink.reshape(N_KV, G, 1, 1).astype(jnp.float32), (B, N_KV, G, QBLOCK, 1))

    def block(i):
        start = i * QBLOCK
        qb = lax.dynamic_slice_in_dim(qg, start, QBLOCK, axis=1)
        kb = lax.dynamic_slice_in_dim(kp, start, span, axis=1)
        vb = lax.dynamic_slice_in_dim(vp, start, span, axis=1)
        kpos = start - WINDOW + jnp.arange(span)
        valid = band & ((kpos >= 0) & (kpos < S))[None, :]
        s_ctx = jnp.einsum('bqkgd,bckd->bkgqc', qb, kc).astype(jnp.float32)
        s_win = jnp.where(valid, jnp.einsum('bqkgd,bjkd->bkgqj', qb, kb).astype(jnp.float32), NEG_INF)
        p = jax.nn.softmax(jnp.concatenate([s_ctx, s_win, sink_b], axis=-1), axis=-1).astype(v.dtype)
        return (jnp.einsum('bkgqc,bckd->bqkgd', p[..., :Lc], vc)
                + jnp.einsum('bkgqj,bjkd->bqkgd', p[..., Lc:Lc + span], vb))

    out = lax.map(block, jnp.arange(S // QBLOCK))
    return jnp.moveaxis(out, 0, 1).reshape(B, S, H * hd)


def context_attention(qc, kc, vc, sink):
    B, Lc, H, hd = qc.shape
    G = H // N_KV
    qg = (qc * hd ** -0.5).reshape(B, Lc, N_KV, G, hd)
    s = jnp.einsum('bqkgd,bckd->bkgqc', qg, kc).astype(jnp.float32)
    sink_b = jnp.broadcast_to(sink.reshape(N_KV, G, 1, 1).astype(jnp.float32), (B, N_KV, G, Lc, 1))
    p = jax.nn.softmax(jnp.concatenate([s, sink_b], axis=-1), axis=-1)[..., :Lc].astype(vc.dtype)
    return jnp.einsum('bkgqc,bckd->bqkgd', p, vc).reshape(B, Lc, H * hd)


def centred_dwconv(u, w, b):
    n = u.shape[1]
    up = jnp.pad(u, ((0, 0), (CONV_LEFT, CONV_W - 1 - CONV_LEFT), (0, 0)))
    out = up[:, 0:n] * w[0]
    for j in range(1, CONV_W):
        out = out + up[:, j:j + n] * w[j]
    return out + b


def block_diag_linear(u, w, b):
    B, N, _ = u.shape
    y = jnp.einsum('bnhi,hij->bnhj', u.reshape(B, N, LRU_BLOCKS, LRU_BS), w)
    return y.reshape(B, N, LRU_W) + b


def rglru_coeffs(u, wa, ba, wi, bi, lam):
    r = jax.nn.sigmoid(block_diag_linear(u, wa, ba))
    i = jax.nn.sigmoid(block_diag_linear(u, wi, bi))
    log_a = (-LRU_C * r * jax.nn.softplus(-lam)).astype(jnp.float32)
    a = jnp.exp(log_a)
    mult = jnp.sqrt(jnp.maximum(-jnp.expm1(2 * log_a), 0.0))
    return a, mult * (i * u).astype(jnp.float32)


def linear_scan(a, b, h0):
    def combine(e1, e2):
        a1, b1 = e1
        a2, b2 = e2
        return a1 * a2, a2 * b1 + b2
    A, Bc = lax.associative_scan(combine, (a, b), axis=1)
    return A * h0[:, None, :] + Bc


def rglru_bidir(ul, uc, wa, ba, wi, bi, lam, need_ctx_out):
    B = ul.shape[0]
    outs_l, outs_c = [], []
    for d in range(2):
        al, bl = rglru_coeffs(ul, wa[d], ba[d], wi[d], bi[d], lam[d])
        ac, bc = rglru_coeffs(uc, wa[d], ba[d], wi[d], bi[d], lam[d])
        if d == 1:
            al, bl, ac, bc = (jnp.flip(t, 1) for t in (al, bl, ac, bc))
        hc = linear_scan(ac, bc, jnp.zeros((B, LRU_W), jnp.float32))
        hl = linear_scan(al, bl, hc[:, -1])
        outs_l.append(jnp.flip(hl, 1) if d == 1 else hl)
        if need_ctx_out:
            outs_c.append(jnp.flip(hc, 1) if d == 1 else hc)
    yl = (outs_l[0] + outs_l[1]).astype(ul.dtype)
    yc = (outs_c[0] + outs_c[1]).astype(uc.dtype) if need_ctx_out else None
    return yl, yc


def attn_lru_mixer(hl, hc, ang_row, ang_col, w_in, w_out, sink, conv_w, conv_b,
                   wa, ba, wi, bi, lam, need_ctx_out):
    B, S, _ = hl.shape
    Lc = hc.shape[1]
    cuts = [ATTN_W, ATTN_W + KV_W, ATTN_W + 2 * KV_W, ATTN_W + 2 * KV_W + LRU_W]
    q, k, v, ul, gl = jnp.split(hl @ w_in, cuts, axis=-1)
    q = axial_rope(q.reshape(B, S, N_HEADS_ATTN, HEAD_DIM), ang_row, ang_col)
    k = axial_rope(k.reshape(B, S, N_KV, HEAD_DIM), ang_row, ang_col)
    v = v.reshape(B, S, N_KV, HEAD_DIM)
    if need_ctx_out:
        qc, kc, vc, uc, gc = jnp.split(hc @ w_in, cuts, axis=-1)
    else:
        kc, vc, uc = jnp.split(hc @ w_in[:, ATTN_W:cuts[3]], [KV_W, 2 * KV_W], axis=-1)
    kc = kc.reshape(B, Lc, N_KV, HEAD_DIM)
    vc = vc.reshape(B, Lc, N_KV, HEAD_DIM)
    attn_l = local_attention(q, k, v, kc, vc, sink)
    rl, rc = rglru_bidir(centred_dwconv(ul, conv_w, conv_b), centred_dwconv(uc, conv_w, conv_b),
                         wa, ba, wi, bi, lam, need_ctx_out)
    yl = jnp.concatenate([attn_l, rl * jax.nn.gelu(gl)], axis=-1) @ w_out
    if not need_ctx_out:
        return yl, None
    attn_c = context_attention(qc.reshape(B, Lc, N_HEADS_ATTN, HEAD_DIM), kc, vc, sink)
    yc = jnp.concatenate([attn_c, rc * jax.nn.gelu(gc)], axis=-1) @ w_out
    return yl, yc


def centred_shift(h):
    hp = jnp.pad(h, ((0, 0), (1, 1), (0, 0)))
    return 0.5 * (hp[:, :-2] + hp[:, 2:])


def l2_normalize(t):
    tf = t.astype(jnp.float32)
    return (tf * lax.rsqrt(jnp.maximum(jnp.sum(tf * tf, axis=-1, keepdims=True), 1e-24))).astype(t.dtype)


def head_groupnorm(y, g, b):
    B, N, H, hd = y.shape
    yf = y.astype(jnp.float32)
    mu = jnp.mean(yf, axis=-1, keepdims=True)
    var = jnp.mean(jnp.square(yf - mu), axis=-1, keepdims=True)
    yn = ((yf - mu) * lax.rsqrt(var + GN_EPS)).reshape(B, N, H * hd)
    return yn * g.astype(jnp.float32) + b.astype(jnp.float32)


def rwkv7_inputs(h, mu, w_rkv, w0, w1, w2, a0, a1, a2, g1, g2, k_k, k_a, with_out):
    B, N, _ = h.shape

    def heads(t):
        return t.reshape(B, N, RWKV_H, RWKV_HD)

    xx = centred_shift(h) - h
    xw = h + xx * mu[1]
    xa = h + xx * mu[4]
    k = (h + xx * mu[2]) @ w_rkv[1]
    v = heads((h + xx * mu[3]) @ w_rkv[2])
    kk = l2_normalize(heads(k * k_k))
    dirs = []
    for d in range(2):
        w_log = -jax.nn.softplus(-(w0[d] + jnp.tanh(xw @ w1[d]) @ w2[d]).astype(jnp.float32)) - 0.5
        decay = jnp.exp(-jnp.exp(w_log))
        a = jax.nn.sigmoid(a0[d] + (xa @ a1[d]) @ a2[d])
        kd = k * (1 + (a - 1) * k_a)
        dirs.append((heads(decay), heads(kd), heads(a)))
    if with_out:
        r = heads((h + xx * mu[0]) @ w_rkv[0])
        g = jax.nn.sigmoid((h + xx * mu[5]) @ g1) @ g2
    else:
        r = None
        g = None
    return r, g, v, kk, dirs


def wkv_scan(s0, decay, k, v, kk, a, r, reverse):
    def seq(t):
        return jnp.moveaxis(t.astype(jnp.float32), 1, 0)
    xs = (seq(decay), seq(k), seq(v), seq(-kk), seq(kk * a))
    if r is not None:
        xs = xs + (seq(r),)

    def step(s, inp):
        w_t, k_t, v_t, a_t, b_t = inp[:5]
        sa = jnp.einsum('bhvk,bhk->bhv', s, a_t)
        s = s * w_t[:, :, None, :] + sa[..., None] * b_t[:, :, None, :] + v_t[..., None] * k_t[:, :, None, :]
        y = jnp.einsum('bhvk,bhk->bhv', s, inp[5]) if r is not None else None
        return s, y

    s_fin, ys = lax.scan(step, s0, xs, reverse=reverse)
    return s_fin, (jnp.moveaxis(ys, 0, 1) if r is not None else None)


def rwkv7_mixer(hl, hc, mu, w_rkv, w0, w1, w2, a0, a1, a2, g1, g2, k_k, k_a, r_k,
                gn_g, gn_b, w_o, need_ctx_out):
    B = hl.shape[0]
    rl, gl, vl, kkl, dirs_l = rwkv7_inputs(hl, mu, w_rkv, w0, w1, w2, a0, a1, a2, g1, g2, k_k, k_a, True)
    rc, gc, vc, kkc, dirs_c = rwkv7_inputs(hc, mu, w_rkv, w0, w1, w2, a0, a1, a2, g1, g2, k_k, k_a, need_ctx_out)
    ys_l, bon_l, ys_c, bon_c = [], [], [], []
    for d in range(2):
        reverse = d == 1
        dec_c, k_c, a_c = dirs_c[d]
        s0 = jnp.zeros((B, RWKV_H, RWKV_HD, RWKV_HD), jnp.float32)
        s_ctx, y_c = wkv_scan(s0, dec_c, k_c, vc, kkc, a_c, rc, reverse)
        dec_l, k_l, a_l = dirs_l[d]
        _, y_l = wkv_scan(s_ctx, dec_l, k_l, vl, kkl, a_l, rl, reverse)
        ys_l.append(y_l)
        bon_l.append(jnp.sum(rl * k_l * r_k, axis=-1, keepdims=True) * vl)
        if need_ctx_out:
            ys_c.append(y_c)
            bon_c.append(jnp.sum(rc * k_c * r_k, axis=-1, keepdims=True) * vc)

    def finish(ys, bons, g):
        Bg, N, _ = g.shape
        o = head_groupnorm(ys[0] + ys[1], gn_g, gn_b).astype(g.dtype) + (bons[0] + bons[1]).reshape(Bg, N, D_MODEL)
        return (o * g) @ w_o

    yl = finish(ys_l, bon_l, gl)
    yc = finish(ys_c, bon_c, gc) if need_ctx_out else None
    return yl, yc


def expert_choice_moe(h, w_router, w1, w3, w2):
    B, N, _ = h.shape
    cap = CAPACITY * N // N_EXPERTS
    aff = jax.nn.softmax(jnp.einsum('bnd,de->ben', h, w_router).astype(jnp.float32), axis=1)
    gate, idx = lax.top_k(aff, cap)
    bidx = jnp.arange(B)[:, None, None]
    xin = h[bidx, idx]
    hid = jax.nn.silu(jnp.einsum('becd,edf->becf', xin, w1)) * jnp.einsum('becd,edf->becf', xin, w3)
    y = jnp.einsum('becf,efd->becd', hid, w2) * gate[..., None].astype(h.dtype)
    return jnp.zeros_like(h).at[bidx, idx].add(y)


def setup_inputs(seed: int = 0) -> dict:
    key = jax.random.key(seed)
    keys = iter(jax.random.split(key, 64))

    def nrm(shape, scale):
        return jax.random.normal(next(keys), shape, jnp.float32) * scale

    def uni(shape, lo, hi):
        return jax.random.uniform(next(keys), shape, jnp.float32, lo, hi)

    D = D_MODEL
    NA = N_A_LAYERS
    NC = N_C_LAYERS
    lru_a = uni((NA, 2, LRU_W), 0.9, 0.999) ** (1.0 / LRU_C)
    return {
        'x': nrm((BATCH, SEQ, D), 1.0),
        'c': nrm((BATCH, D), 1.0),
        'ctx': nrm((BATCH, CTX_LEN, D), 1.0),
        'c_ctx': nrm((D,), 1.0),
        'mod_w': nrm((DEPTH, D, 6 * D), 0.5 * D ** -0.5),
        'mod_b': nrm((DEPTH, 6 * D), 0.01),
        'norm_mix': 1.0 + nrm((DEPTH, D), 0.05),
        'norm_ffn': 1.0 + nrm((DEPTH, D), 0.05),
        'router_w': nrm((DEPTH, D, N_EXPERTS), D ** -0.5),
        'exp_w1': nrm((DEPTH, N_EXPERTS, D, EXPERT_FF), D ** -0.5),
        'exp_w3': nrm((DEPTH, N_EXPERTS, D, EXPERT_FF), D ** -0.5),
        'exp_w2': nrm((DEPTH, N_EXPERTS, EXPERT_FF, D), EXPERT_FF ** -0.5),
        'mix_in': nrm((NA, D, MIX_IN), D ** -0.5),
        'mix_out': nrm((NA, MIX_OUT, D), MIX_OUT ** -0.5),
        'attn_sink': nrm((NA, N_HEADS_ATTN), 1.0),
        'lru_conv_w': nrm((NA, CONV_W, LRU_W), CONV_W ** -0.5),
        'lru_conv_b': nrm((NA, LRU_W), 0.01),
        'lru_wa': nrm((NA, 2, LRU_BLOCKS, LRU_BS, LRU_BS), LRU_BS ** -0.5),
        'lru_ba': nrm((NA, 2, LRU_W), 0.01),
        'lru_wi': nrm((NA, 2, LRU_BLOCKS, LRU_BS, LRU_BS), LRU_BS ** -0.5),
        'lru_bi': nrm((NA, 2, LRU_W), 0.01),
        'lru_lam': jnp.log(lru_a) - jnp.log1p(-lru_a),
        'rw_mu': uni((NC, 6, D), 0.0, 1.0),
        'rw_rkv': nrm((NC, 3, D, D), D ** -0.5),
        'rw_w0': uni((NC, 2, D), -6.0, -1.0),
        'rw_w1': nrm((NC, 2, D, DECAY_LORA), D ** -0.5),
        'rw_w2': nrm((NC, 2, DECAY_LORA, D), 0.1 * DECAY_LORA ** -0.5),
        'rw_a0': nrm((NC, 2, D), 0.1),
        'rw_a1': nrm((NC, 2, D, ICLR_LORA), D ** -0.5),
        'rw_a2': nrm((NC, 2, ICLR_LORA, D), 0.1 * ICLR_LORA ** -0.5),
        'rw_g1': nrm((NC, D, GATE_LORA), D ** -0.5),
        'rw_g2': nrm((NC, GATE_LORA, D), GATE_LORA ** -0.5),
        'rw_kk': 1.0 + nrm((NC, D), 0.1),
        'rw_ka': 1.0 + nrm((NC, D), 0.1),
        'rw_rk': nrm((NC, RWKV_H, RWKV_HD), 0.1),
        'rw_gn_g': 1.0 + nrm((NC, D), 0.05),
        'rw_gn_b': nrm((NC, D), 0.01),
        'rw_wo': nrm((NC, D, D), D ** -0.5),
        'final_norm': 1.0 + nrm((D,), 0.05),
    }


def reference(x, c, ctx, c_ctx, mod_w, mod_b, norm_mix, norm_ffn, router_w, exp_w1, exp_w3, exp_w2,
              mix_in, mix_out, attn_sink, lru_conv_w, lru_conv_b, lru_wa, lru_ba, lru_wi, lru_bi, lru_lam,
              rw_mu, rw_rkv, rw_w0, rw_w1, rw_w2, rw_a0, rw_a1, rw_a2, rw_g1, rw_g2, rw_kk, rw_ka, rw_rk,
              rw_gn_g, rw_gn_b, rw_wo, final_norm):
    ROWS = x.shape[1] // GRID_W
    ang_row, ang_col = axial_angles(ROWS)
    s_lat = jax.nn.silu(c)
    s_ctx = jax.nn.silu(c_ctx)
    xl, xc = x, ctx
    for layer in range(DEPTH):
        last = layer == DEPTH - 1
        i = layer // 2
        m_l = jnp.split((s_lat @ mod_w[layer] + mod_b[layer])[:, None, :], 6, axis=-1)
        m_c = jnp.split(s_ctx @ mod_w[layer] + mod_b[layer], 6, axis=-1)
        hl = modulate(rmsnorm(xl, norm_mix[layer]), m_l[0], m_l[1])
        hc = modulate(rmsnorm(xc, norm_mix[layer]), m_c[0], m_c[1])
        if layer % 2 == 0:
            yl, yc = attn_lru_mixer(hl, hc, ang_row, ang_col, mix_in[i], mix_out[i], attn_sink[i],
                                    lru_conv_w[i], lru_conv_b[i], lru_wa[i], lru_ba[i], lru_wi[i],
                                    lru_bi[i], lru_lam[i], not last)
        else:
            yl, yc = rwkv7_mixer(hl, hc, rw_mu[i], rw_rkv[i], rw_w0[i], rw_w1[i], rw_w2[i], rw_a0[i],
                                 rw_a1[i], rw_a2[i], rw_g1[i], rw_g2[i], rw_kk[i], rw_ka[i], rw_rk[i],
                                 rw_gn_g[i], rw_gn_b[i], rw_wo[i], not last)
        xl = xl + m_l[2] * yl
        hl = modulate(rmsnorm(xl, norm_ffn[layer]), m_l[3], m_l[4])
        xl = xl + m_l[5] * expert_choice_moe(hl, router_w[layer], exp_w1[layer], exp_w3[layer], exp_w2[layer])
        if not last:
            xc = xc + m_c[2] * yc
            hc = modulate(rmsnorm(xc, norm_ffn[layer]), m_c[3], m_c[4])
            xc = xc + m_c[5] * expert_choice_moe(hc, router_w[layer], exp_w1[layer], exp_w3[layer], exp_w2[layer])
    return rmsnorm(xl, final_norm)
```

```python
import functools
import math

import jax
import jax.numpy as jnp
from jax import lax
from jax.experimental import pallas as pl
from jax.experimental.pallas import tpu as pltpu

F32 = jnp.float32
BF16 = jnp.bfloat16

GRID_W = 64
N_HEADS_ATTN = 8
N_KV = 2
HEAD_DIM = 64
AXIS_DIM = HEAD_DIM // 2
ATTN_W = N_HEADS_ATTN * HEAD_DIM
KV_W = N_KV * HEAD_DIM
WINDOW = 128
QBLOCK = 128
ROPE_BASE = 10000.0
LRU_W = 512
LRU_BLOCKS = 8
CONV_W = 4
CONV_LEFT = 2
LRU_C = 8.0
RWKV_H = 16
RWKV_HD = 64
GN_EPS = 64e-5
N_EXPERTS = 16
CAPACITY = 2
NORM_EPS = 1e-6
NEG_INF = -1e30

LANES = 128
VMEM_LIMIT_BYTES = 56 * 1024 * 1024
WKV_TC = 16


def _params(*sem):
    return pltpu.CompilerParams(dimension_semantics=sem, vmem_limit_bytes=VMEM_LIMIT_BYTES)


def _mm_kernel(x_ref, w_ref, o_ref, *, precise):
    if precise:
        o_ref[...] = jnp.dot(x_ref[...], w_ref[...], precision=lax.Precision.HIGHEST,
                             preferred_element_type=F32)
    else:
        o_ref[...] = jnp.dot(x_ref[...].astype(BF16), w_ref[...].astype(BF16),
                             preferred_element_type=F32)


def _mm(x, w, *, tm=512, tn=None, precise=False):
    M, K = x.shape
    N = w.shape[1]
    tm = min(tm, M)
    tn = N if tn is None else min(tn, N)
    assert M % tm == 0 and N % tn == 0, (M, tm, N, tn)
    return pl.pallas_call(
        functools.partial(_mm_kernel, precise=precise),
        grid=(M // tm, N // tn),
        in_specs=[pl.BlockSpec((tm, K), lambda i, j: (i, 0)),
                  pl.BlockSpec((K, tn), lambda i, j: (0, j))],
        out_specs=pl.BlockSpec((tm, tn), lambda i, j: (i, j)),
        out_shape=jax.ShapeDtypeStruct((M, N), F32),
        compiler_params=_params("parallel", "parallel"),
        name="mm",
    )(x, w)


def _norm_mod_kernel(x_ref, g_ref, sh_ref, sc_ref, o_ref):
    x = x_ref[0]
    y = x * lax.rsqrt(jnp.mean(x * x, axis=-1, keepdims=True) + NORM_EPS) * g_ref[...]
    o_ref[0] = y * (1.0 + sc_ref[0]) + sh_ref[0]


def _norm_mod(x, g, shift, scale):
    B, N, D = x.shape
    ts = min(512, N)
    return pl.pallas_call(
        _norm_mod_kernel,
        grid=(B, N // ts),
        in_specs=[pl.BlockSpec((1, ts, D), lambda b, i: (b, i, 0)),
                  pl.BlockSpec((1, D), lambda b, i: (0, 0)),
                  pl.BlockSpec((1, 1, D), lambda b, i: (b, 0, 0)),
                  pl.BlockSpec((1, 1, D), lambda b, i: (b, 0, 0))],
        out_specs=pl.BlockSpec((1, ts, D), lambda b, i: (b, i, 0)),
        out_shape=jax.ShapeDtypeStruct((B, N, D), F32),
        compiler_params=_params("parallel", "parallel"),
        name="norm_mod",
    )(x, g.reshape(1, D), shift, scale)


def _rope_kernel(x_ref, cos_ref, sin_ref, o_ref):
    width = x_ref.shape[-1]
    for c in range(width // LANES):
        sl = slice(c * LANES, (c + 1) * LANES)
        x = x_ref[0, :, sl]
        lane = lax.broadcasted_iota(jnp.int32, x.shape, 1)
        first_half = (lane % AXIS_DIM) < (AXIS_DIM // 2)
        partner = jnp.where(first_half, pltpu.roll(x, LANES - AXIS_DIM // 2, 1),
                            pltpu.roll(x, AXIS_DIM // 2, 1))
        o_ref[0, :, sl] = x * cos_ref[:, sl] + partner * sin_ref[:, sl]


def _rope_tables(S):
    rows = S // GRID_W
    row = jnp.repeat(jnp.arange(rows), GRID_W).astype(F32)
    col = jnp.tile(jnp.arange(GRID_W), rows).astype(F32)
    inv = ROPE_BASE ** (-jnp.arange(0, AXIS_DIM, 2, dtype=F32) / AXIS_DIM)
    ang_row, ang_col = row[:, None] * inv, col[:, None] * inv

    def axis_tables(ang):
        c, s = jnp.cos(ang), jnp.sin(ang)
        return jnp.concatenate([c, c], -1), jnp.concatenate([-s, s], -1)

    cr, sr = axis_tables(ang_row)
    cc, sc = axis_tables(ang_col)
    cos_h = jnp.concatenate([cr, cc], -1)
    sin_h = jnp.concatenate([sr, sc], -1)
    q_scale = HEAD_DIM ** -0.5
    cos = jnp.concatenate([jnp.tile(cos_h, (1, N_HEADS_ATTN)) * q_scale, jnp.tile(cos_h, (1, N_KV))], -1)
    sin = jnp.concatenate([jnp.tile(sin_h, (1, N_HEADS_ATTN)) * q_scale, jnp.tile(sin_h, (1, N_KV))], -1)
    return cos, sin


def _rope(proj, cos, sin):
    B, S, _ = proj.shape
    width = ATTN_W + KV_W
    ts = min(512, S)
    return pl.pallas_call(
        _rope_kernel,
        grid=(B, S // ts),
        in_specs=[pl.BlockSpec((1, ts, width), lambda b, i: (b, i, 0)),
                  pl.BlockSpec((ts, width), lambda b, i: (i, 0)),
                  pl.BlockSpec((ts, width), lambda b, i: (i, 0))],
        out_specs=pl.BlockSpec((1, ts, width), lambda b, i: (b, i, 0)),
        out_shape=jax.ShapeDtypeStruct((B, S, width), F32),
        compiler_params=_params("parallel", "parallel"),
        name="rope",
    )(proj, cos, sin)


def _attn_heads(q, kcat, vcat, valid, sink_ref):
    outs = []
    G = N_HEADS_ATTN // N_KV
    for j in range(N_KV):
        kj = kcat[:, j * HEAD_DIM:(j + 1) * HEAD_DIM]
        vj = vcat[:, j * HEAD_DIM:(j + 1) * HEAD_DIM]
        for g in range(G):
            h = j * G + g
            qh = q[:, h * HEAD_DIM:(h + 1) * HEAD_DIM].astype(BF16)
            s = lax.dot_general(qh, kj, (((1,), (1,)), ((), ())), preferred_element_type=F32)
            if valid is not None:
                s = jnp.where(valid, s, NEG_INF)
            sink = sink_ref[h]
            m = jnp.maximum(jnp.max(s, axis=-1, keepdims=True), sink)
            e = jnp.exp(s - m)
            den = jnp.sum(e, axis=-1, keepdims=True) + jnp.exp(sink - m)
            o = jnp.dot(e.astype(BF16), vj, preferred_element_type=F32)
            outs.append(o / den)
    return jnp.concatenate(outs, axis=-1)


def _attn_kernel(sink_ref, q_ref, kp_ref, kc_ref, kn_ref, vp_ref, vc_ref, vn_ref, ck_ref, cv_ref,
                 o_ref, *, S, Lc):
    i = pl.program_id(1)
    kcat = jnp.concatenate([ck_ref[0], kp_ref[0], kc_ref[0], kn_ref[0]], axis=0).astype(BF16)
    vcat = jnp.concatenate([cv_ref[0], vp_ref[0], vc_ref[0], vn_ref[0]], axis=0).astype(BF16)
    L = Lc + 3 * QBLOCK
    row = lax.broadcasted_iota(jnp.int32, (QBLOCK, L), 0)
    col = lax.broadcasted_iota(jnp.int32, (QBLOCK, L), 1)
    rel = col - Lc - QBLOCK - row
    kpos = (i - 1) * QBLOCK + col - Lc
    valid = (col < Lc) | ((jnp.abs(rel) <= WINDOW) & (kpos >= 0) & (kpos < S))
    o_ref[0] = _attn_heads(q_ref[0], kcat, vcat, valid, sink_ref)


def _attention(sink, qk, proj_l, proj_c):
    B, S, _ = qk.shape
    Lc = proj_c.shape[1]
    nb = S // QBLOCK
    kcol, vcol = ATTN_W // KV_W, ATTN_W // KV_W + 1

    def blk(colblk, off):
        return pl.BlockSpec((1, QBLOCK, KV_W),
                            lambda b, i: (b, jnp.clip(i + off, 0, nb - 1), colblk))

    return pl.pallas_call(
        functools.partial(_attn_kernel, S=S, Lc=Lc),
        grid=(B, nb),
        in_specs=[pl.BlockSpec(memory_space=pltpu.SMEM),
                  pl.BlockSpec((1, QBLOCK, ATTN_W), lambda b, i: (b, i, 0)),
                  blk(kcol, -1), blk(kcol, 0), blk(kcol, 1),
                  blk(vcol, -1), blk(vcol, 0), blk(vcol, 1),
                  pl.BlockSpec((1, Lc, KV_W), lambda b, i: (b, 0, kcol)),
                  pl.BlockSpec((1, Lc, KV_W), lambda b, i: (b, 0, vcol))],
        out_specs=pl.BlockSpec((1, QBLOCK, ATTN_W), lambda b, i: (b, i, 0)),
        out_shape=jax.ShapeDtypeStruct((B, S, ATTN_W), F32),
        compiler_params=_params("parallel", "parallel"),
        name="attention",
    )(sink, qk, qk, qk, qk, proj_l, proj_l, proj_l, proj_c, proj_c)


def _ctx_attn_kernel(sink_ref, q_ref, ck_ref, cv_ref, o_ref):
    q = q_ref[0] * (HEAD_DIM ** -0.5)
    o_ref[0] = _attn_heads(q, ck_ref[0].astype(BF16), cv_ref[0].astype(BF16), None, sink_ref)


def _ctx_attention(sink, proj_c):
    B, Lc, _ = proj_c.shape
    kcol, vcol = ATTN_W // KV_W, ATTN_W // KV_W + 1
    return pl.pallas_call(
        _ctx_attn_kernel,
        grid=(B,),
        in_specs=[pl.BlockSpec(memory_space=pltpu.SMEM),
                  pl.BlockSpec((1, Lc, ATTN_W), lambda b: (b, 0, 0)),
                  pl.BlockSpec((1, Lc, KV_W), lambda b: (b, 0, kcol)),
                  pl.BlockSpec((1, Lc, KV_W), lambda b: (b, 0, vcol))],
        out_specs=pl.BlockSpec((1, Lc, ATTN_W), lambda b: (b, 0, 0)),
        out_shape=jax.ShapeDtypeStruct((B, Lc, ATTN_W), F32),
        compiler_params=_params("parallel"),
        name="ctx_attention",
    )(sink, proj_c, proj_c, proj_c)


def _seg_rev_block(j, nC, nL):
    return jnp.where(j < nC, nC - 1 - j, nC + nL - 1 - (j - nC))


def _lru_kernel(af_ref, bf_ref, ar_ref, br_ref, hf_ref, hr_ref, sf_ref, sr_ref, *, Tc):
    @pl.when(pl.program_id(1) == 0)
    def _():
        sf_ref[...] = jnp.zeros_like(sf_ref)
        sr_ref[...] = jnp.zeros_like(sr_ref)

    def step(s, carry):
        hf, hr = carry
        hf = af_ref[0, pl.ds(s, 1), :] * hf + bf_ref[0, pl.ds(s, 1), :]
        hf_ref[0, pl.ds(s, 1), :] = hf
        t = Tc - 1 - s
        hr = ar_ref[0, pl.ds(t, 1), :] * hr + br_ref[0, pl.ds(t, 1), :]
        hr_ref[0, pl.ds(t, 1), :] = hr
        return hf, hr

    hf, hr = lax.fori_loop(0, Tc, step, (sf_ref[...], sr_ref[...]), unroll=8)
    sf_ref[...] = hf
    sr_ref[...] = hr


def _lru_scan(a, b, Lc):
    B, T, _ = a.shape
    S = T - Lc
    Tc = math.gcd(math.gcd(Lc, S), 256)
    nC, nL = Lc // Tc, S // Tc

    def fwd(col):
        return pl.BlockSpec((1, Tc, LRU_W), lambda bi, j: (bi, j, col))

    def rev(col):
        return pl.BlockSpec((1, Tc, LRU_W), lambda bi, j: (bi, _seg_rev_block(j, nC, nL), col))

    return pl.pallas_call(
        functools.partial(_lru_kernel, Tc=Tc),
        grid=(B, nC + nL),
        in_specs=[fwd(0), fwd(0), rev(1), rev(1)],
        out_specs=[fwd(0), rev(0)],
        out_shape=[jax.ShapeDtypeStruct((B, T, LRU_W), F32)] * 2,
        scratch_shapes=[pltpu.VMEM((1, LRU_W), F32)] * 2,
        compiler_params=_params("parallel", "arbitrary"),
        name="lru_scan",
    )(a, b, a, b)


def _wkv_kernel(w_ref, k_ref, b_ref, a_ref, v_ref, r_ref, y_ref, s_ref, *, Tc):
    d = pl.program_id(0)

    @pl.when(pl.program_id(2) == 0)
    def _():
        s_ref[...] = jnp.zeros_like(s_ref)

    def step(s, _):
        tt = jnp.where(d == 0, s, Tc - 1 - s)
        w = w_ref[0, tt]
        kd = k_ref[0, tt]
        bb = b_ref[0, tt]
        a = a_ref[tt]
        r = r_ref[tt]

        def row(v, _):
            st = s_ref[v]
            sa = jnp.sum(st * a, axis=0, keepdims=True)
            vv = v_ref[tt, pl.ds(v, 1), :]
            st = st * w + sa * bb + vv * kd
            s_ref[v] = st
            y_ref[0, tt, pl.ds(v, 1), :] = jnp.sum(st * r, axis=0, keepdims=True)
            return 0

        lax.fori_loop(0, RWKV_HD, row, 0, unroll=2)
        return 0

    lax.fori_loop(0, Tc, step, 0)


def _wkv(w2, k2, b2, a, v, r, Lc):
    _, T, hd, I = w2.shape
    S = T - Lc
    Tc = WKV_TC
    nC, nL = Lc // Tc, S // Tc

    def tmap(d, j):
        return jnp.where(d == 0, j, _seg_rev_block(j, nC, nL))

    def lmap(d, j):
        jj = jnp.maximum(j - nC, 0)
        return jnp.where(d == 0, jj, nL - 1 - jj)

    dir_spec = pl.BlockSpec((1, Tc, hd, LANES), lambda d, g, j: (d, tmap(d, j), 0, g))
    all_spec = pl.BlockSpec((Tc, hd, LANES), lambda d, g, j: (tmap(d, j), 0, g))
    lat_spec = pl.BlockSpec((Tc, hd, LANES), lambda d, g, j: (lmap(d, j), 0, g))
    return pl.pallas_call(
        functools.partial(_wkv_kernel, Tc=Tc),
        grid=(2, I // LANES, nC + nL),
        in_specs=[dir_spec, dir_spec, dir_spec, all_spec, all_spec, lat_spec],
        out_specs=pl.BlockSpec((1, Tc, hd, LANES), lambda d, g, j: (d, lmap(d, j), 0, g)),
        out_shape=jax.ShapeDtypeStruct((2, S, hd, I), F32),
        scratch_shapes=[pltpu.VMEM((hd, hd, LANES), F32)],
        compiler_params=_params("parallel", "parallel", "arbitrary"),
        name="wkv",
    )(w2, k2, b2, a, v, r)


def _router_kernel(h_ref, w_ref, o_ref):
    logits = lax.dot_general(w_ref[...], h_ref[0], (((1,), (1,)), ((), ())),
                             precision=lax.Precision.HIGHEST, preferred_element_type=F32)
    m = jnp.max(logits, axis=0, keepdims=True)
    e = jnp.exp(logits - m)
    o_ref[0] = e / jnp.sum(e, axis=0, keepdims=True)


def _router(h, w_router):
    B, N, D = h.shape
    E = w_router.shape[1]
    tn = min(512, N)
    return pl.pallas_call(
        _router_kernel,
        grid=(B, N // tn),
        in_specs=[pl.BlockSpec((1, tn, D), lambda b, i: (b, i, 0)),
                  pl.BlockSpec((E, D), lambda b, i: (0, 0))],
        out_specs=pl.BlockSpec((1, E, tn), lambda b, i: (b, 0, i)),
        out_shape=jax.ShapeDtypeStruct((B, E, N), F32),
        compiler_params=_params("parallel", "parallel"),
        name="router",
    )(h, w_router.T)


def _moe_ffn_kernel(idx_ref, gate_ref, h_ref, w1_ref, w3_ref, w2_ref, o_ref, xin_ref, *, cap):
    def gather(r, _):
        xin_ref[pl.ds(r, 1), :] = h_ref[0, pl.ds(idx_ref[0, 0, r], 1), :]
        return 0

    lax.fori_loop(0, cap, gather, 0, unroll=8)
    x = xin_ref[...].astype(BF16)
    h1 = jnp.dot(x, w1_ref[0], preferred_element_type=F32)
    h3 = jnp.dot(x, w3_ref[0], preferred_element_type=F32)
    hid = (h1 * jax.nn.sigmoid(h1) * h3).astype(BF16)
    o_ref[0] = jnp.dot(hid, w2_ref[0], preferred_element_type=F32) * gate_ref[0]


def _moe_scatter_kernel(idx_ref, y_ref, o_ref, *, cap):
    @pl.when(pl.program_id(1) == 0)
    def _():
        o_ref[...] = jnp.zeros_like(o_ref)

    def add(r, _):
        row = pl.ds(idx_ref[0, 0, r], 1)
        o_ref[0, row, :] = o_ref[0, row, :] + y_ref[0, pl.ds(r, 1), :]
        return 0

    lax.fori_loop(0, cap, add, 0, unroll=8)


def _moe(h, w_router, w1, w3, w2):
    B, N, D = h.shape
    E, _, FF = w1.shape
    cap = CAPACITY * N // E
    aff = _router(h, w_router)
    gate, idx = lax.top_k(aff, cap)
    idx3 = idx.reshape(B * E, 1, cap).astype(jnp.int32)
    idx_spec = pl.BlockSpec((1, 1, cap), lambda b, e: (b * E + e, 0, 0), memory_space=pltpu.SMEM)
    y = pl.pallas_call(
        functools.partial(_moe_ffn_kernel, cap=cap),
        grid=(B, E),
        in_specs=[idx_spec,
                  pl.BlockSpec((1, cap, 1), lambda b, e: (b * E + e, 0, 0)),
                  pl.BlockSpec((1, N, D), lambda b, e: (b, 0, 0)),
                  pl.BlockSpec((1, D, FF), lambda b, e: (e, 0, 0)),
                  pl.BlockSpec((1, D, FF), lambda b, e: (e, 0, 0)),
                  pl.BlockSpec((1, FF, D), lambda b, e: (e, 0, 0))],
        out_specs=pl.BlockSpec((1, cap, D), lambda b, e: (b * E + e, 0, 0)),
        out_shape=jax.ShapeDtypeStruct((B * E, cap, D), F32),
        scratch_shapes=[pltpu.VMEM((cap, D), F32)],
        compiler_params=_params("parallel", "arbitrary"),
        name="moe_ffn",
    )(idx3, gate.reshape(B * E, cap, 1), h, w1, w3, w2)
    return pl.pallas_call(
        functools.partial(_moe_scatter_kernel, cap=cap),
        grid=(B, E),
        in_specs=[idx_spec,
                  pl.BlockSpec((1, cap, D), lambda b, e: (b * E + e, 0, 0))],
        out_specs=pl.BlockSpec((1, N, D), lambda b, e: (b, 0, 0)),
        out_shape=jax.ShapeDtypeStruct((B, N, D), F32),
        compiler_params=_params("parallel", "arbitrary"),
        name="moe_scatter",
    )(idx3, y)


def _block_diag(w):
    H, bi, bj = w.shape
    eye = jnp.eye(H, dtype=w.dtype)
    return jnp.einsum('hij,hg->higj', w, eye).reshape(H * bi, H * bj)


def _dwconv(u, w, b):
    n = u.shape[1]
    up = jnp.pad(u, ((0, 0), (CONV_LEFT, CONV_W - 1 - CONV_LEFT), (0, 0)))
    out = up[:, 0:n] * w[0]
    for j in range(1, CONV_W):
        out = out + up[:, j:j + n] * w[j]
    return out + b


def _attn_lru_mixer(hl, hc, cos, sin, w_in, w_out, sink, conv_w, conv_b, wa, ba, wi, bi, lam):
    B, S, D = hl.shape
    Lc = hc.shape[1]
    mix_in = w_in.shape[1]
    w_in_b = w_in.astype(BF16)
    proj_l = _mm(hl.reshape(B * S, D), w_in_b).reshape(B, S, mix_in)
    proj_c = _mm(hc.reshape(B * Lc, D), w_in_b).reshape(B, Lc, mix_in)
    qk = _rope(proj_l, cos, sin)
    attn_l = _attention(sink, qk, proj_l, proj_c)
    attn_c = _ctx_attention(sink, proj_c)

    u0, g0 = ATTN_W + 2 * KV_W, ATTN_W + 2 * KV_W + LRU_W
    u = jnp.concatenate([_dwconv(proj_c[..., u0:g0], conv_w, conv_b),
                         _dwconv(proj_l[..., u0:g0], conv_w, conv_b)], axis=1)
    T = Lc + S
    w_gates = jnp.concatenate([_block_diag(wa[0]), _block_diag(wi[0]),
                               _block_diag(wa[1]), _block_diag(wi[1])], axis=1).astype(BF16)
    b_gates = jnp.concatenate([ba[0], bi[0], ba[1], bi[1]])
    gates = jax.nn.sigmoid(_mm(u.reshape(B * T, LRU_W), w_gates).reshape(B, T, 4 * LRU_W) + b_gates)
    a_dirs, b_dirs = [], []
    for d in range(2):
        r = gates[..., (2 * d) * LRU_W:(2 * d + 1) * LRU_W]
        i = gates[..., (2 * d + 1) * LRU_W:(2 * d + 2) * LRU_W]
        log_a = -LRU_C * r * jax.nn.softplus(-lam[d])
        a_dirs.append(jnp.exp(log_a))
        b_dirs.append(jnp.sqrt(jnp.maximum(-jnp.expm1(2 * log_a), 0.0)) * (i * u))
    hf, hr = _lru_scan(jnp.concatenate(a_dirs, -1), jnp.concatenate(b_dirs, -1), Lc)
    rec = hf + hr
    zl = jnp.concatenate([attn_l, rec[:, Lc:] * jax.nn.gelu(proj_l[..., g0:])], axis=-1)
    zc = jnp.concatenate([attn_c, rec[:, :Lc] * jax.nn.gelu(proj_c[..., g0:])], axis=-1)
    w_out_b = w_out.astype(BF16)
    yl = _mm(zl.reshape(B * S, -1), w_out_b).reshape(B, S, D)
    yc = _mm(zc.reshape(B * Lc, -1), w_out_b).reshape(B, Lc, D)
    return yl, yc


def _to_tki(x):
    B, N, _ = x.shape
    return x.reshape(B, N, RWKV_H, RWKV_HD).transpose(1, 3, 0, 2).reshape(N, RWKV_HD, B * RWKV_H)


def _from_tki(y, B):
    N = y.shape[0]
    return y.reshape(N, RWKV_HD, B, RWKV_H).transpose(2, 0, 3, 1).reshape(B, N, RWKV_H * RWKV_HD)


def _rwkv_inputs(h, mu, w_rkv, w0, w1, w2, a0, a1, a2, g1, g2, k_k, k_a, with_out):
    B, N, D = h.shape
    hp = jnp.pad(h, ((0, 0), (1, 1), (0, 0)))
    xx = 0.5 * (hp[:, :-2] + hp[:, 2:]) - h

    def mix(i):
        return (h + xx * mu[i]).reshape(B * N, D)

    def heads(t):
        return t.reshape(B, N, RWKV_H, RWKV_HD)

    k = _mm(mix(2), w_rkv[1].astype(BF16)).reshape(B, N, D)
    v = _mm(mix(3), w_rkv[2].astype(BF16)).reshape(B, N, D)
    kf = heads(k * k_k)
    kk = (kf * lax.rsqrt(jnp.maximum(jnp.sum(kf * kf, axis=-1, keepdims=True), 1e-24))).reshape(B, N, D)

    zeros_w = jnp.zeros_like(w2[0])
    w2_bd = jnp.concatenate([jnp.concatenate([w2[0], zeros_w], 1),
                             jnp.concatenate([zeros_w, w2[1]], 1)], 0).astype(BF16)
    zeros_a = jnp.zeros_like(a2[0])
    a2_bd = jnp.concatenate([jnp.concatenate([a2[0], zeros_a], 1),
                             jnp.concatenate([zeros_a, a2[1]], 1)], 0).astype(BF16)
    lw = jnp.tanh(_mm(mix(1), jnp.concatenate([w1[0], w1[1]], 1).astype(BF16)))
    wpre = _mm(lw, w2_bd).reshape(B, N, 2 * D)
    la = _mm(mix(4), jnp.concatenate([a1[0], a1[1]], 1).astype(BF16))
    apre = _mm(la, a2_bd).reshape(B, N, 2 * D)
    dirs = []
    for d in range(2):
        w_log = -jax.nn.softplus(-(w0[d] + wpre[..., d * D:(d + 1) * D])) - 0.5
        decay = jnp.exp(-jnp.exp(w_log))
        a = jax.nn.sigmoid(a0[d] + apre[..., d * D:(d + 1) * D])
        kd = k * (1 + (a - 1) * k_a)
        dirs.append((decay, kd, a))
    if with_out:
        r = _mm(mix(0), w_rkv[0].astype(BF16)).reshape(B, N, D)
        g = _mm(jax.nn.sigmoid(_mm(mix(5), g1.astype(BF16))), g2.astype(BF16)).reshape(B, N, D)
    else:
        r = g = None
    return r, g, v, kk, dirs


def _rwkv7_mixer(hl, hc, mu, w_rkv, w0, w1, w2, a0, a1, a2, g1, g2, k_k, k_a, r_k, gn_g, gn_b, w_o):
    B, S, D = hl.shape
    Lc = hc.shape[1]
    rl, gl, vl, kkl, dirs_l = _rwkv_inputs(hl, mu, w_rkv, w0, w1, w2, a0, a1, a2, g1, g2, k_k, k_a, True)
    _, _, vc, kkc, dirs_c = _rwkv_inputs(hc, mu, w_rkv, w0, w1, w2, a0, a1, a2, g1, g2, k_k, k_a, False)

    def seq(c, l):
        return jnp.concatenate([_to_tki(c), _to_tki(l)], axis=0)

    w2s = jnp.stack([seq(dirs_c[d][0], dirs_l[d][0]) for d in range(2)])
    k2s = jnp.stack([seq(dirs_c[d][1], dirs_l[d][1]) for d in range(2)])
    b2s = jnp.stack([seq(kkc * dirs_c[d][2], kkl * dirs_l[d][2]) for d in range(2)])
    y2 = _wkv(w2s, k2s, b2s, seq(-kkc, -kkl), seq(vc, vl), _to_tki(rl), Lc)
    y = _from_tki(y2[0] + y2[1], B)

    def heads(t):
        return t.reshape(B, S, RWKV_H, RWKV_HD)

    yh = heads(y)
    mean = jnp.mean(yh, axis=-1, keepdims=True)
    var = jnp.mean(jnp.square(yh - mean), axis=-1, keepdims=True)
    yn = ((yh - mean) * lax.rsqrt(var + GN_EPS)).reshape(B, S, D) * gn_g + gn_b
    rk = heads(rl) * r_k
    bonus = (jnp.sum(rk * heads(dirs_l[0][1] + dirs_l[1][1]), axis=-1, keepdims=True) * heads(vl)).reshape(B, S, D)
    o = (yn + bonus) * gl
    return _mm(o.reshape(B * S, D), w_o.astype(BF16)).reshape(B, S, D)


def kernel(x, c, ctx, c_ctx, mod_w, mod_b, norm_mix, norm_ffn, router_w, exp_w1, exp_w3, exp_w2, mix_in, mix_out, attn_sink, lru_conv_w, lru_conv_b, lru_wa, lru_ba, lru_wi, lru_bi, lru_lam, rw_mu, rw_rkv, rw_w0, rw_w1, rw_w2, rw_a0, rw_a1, rw_a2, rw_g1, rw_g2, rw_kk, rw_ka, rw_rk, rw_gn_g, rw_gn_b, rw_wo, final_norm):
    B, S, D = x.shape
    Lc = ctx.shape[1]
    depth = mod_w.shape[0]
    assert depth == 2 and S % QBLOCK == 0 and (B * RWKV_H) % LANES == 0
    assert Lc % WKV_TC == 0 and S % WKV_TC == 0
    cos, sin = _rope_tables(S)

    n_rows = -(-(B + 1) // 8) * 8
    cond = jnp.concatenate([jax.nn.silu(c), jax.nn.silu(c_ctx)[None],
                            jnp.zeros((n_rows - B - 1, D), F32)], axis=0)

    xl, xc = x, ctx
    for layer in range(depth):
        last = layer == depth - 1
        mod = _mm(cond, mod_w[layer], tm=n_rows, tn=1024, precise=True) + mod_b[layer]
        m_l = [t[:, None, :] for t in jnp.split(mod[:B], 6, axis=-1)]
        m_c = [jnp.broadcast_to(t[None], (B, 1, D)) for t in jnp.split(mod[B:B + 1], 6, axis=-1)]
        hl = _norm_mod(xl, norm_mix[layer], m_l[0], m_l[1])
        hc = _norm_mod(xc, norm_mix[layer], m_c[0], m_c[1])
        if layer % 2 == 0:
            i = layer // 2
            yl, yc = _attn_lru_mixer(hl, hc, cos, sin, mix_in[i], mix_out[i], attn_sink[i], lru_conv_w[i],
                                     lru_conv_b[i], lru_wa[i], lru_ba[i], lru_wi[i], lru_bi[i], lru_lam[i])
        else:
            i = layer // 2
            yl = _rwkv7_mixer(hl, hc, rw_mu[i], rw_rkv[i], rw_w0[i], rw_w1[i], rw_w2[i], rw_a0[i], rw_a1[i],
                              rw_a2[i], rw_g1[i], rw_g2[i], rw_kk[i], rw_ka[i], rw_rk[i], rw_gn_g[i],
                              rw_gn_b[i], rw_wo[i])
            yc = None
        w1b, w3b, w2b = exp_w1[layer].astype(BF16), exp_w3[layer].astype(BF16), exp_w2[layer].astype(BF16)
        xl = xl + m_l[2] * yl
        hl = _norm_mod(xl, norm_ffn[layer], m_l[3], m_l[4])
        xl = xl + m_l[5] * _moe(hl, router_w[layer], w1b, w3b, w2b)
        if not last:
            xc = xc + m_c[2] * yc
            hc = _norm_mod(xc, norm_ffn[layer], m_c[3], m_c[4])
            xc = xc + m_c[5] * _moe(hc, router_w[layer], w1b, w3b, w2b)
    zero = jnp.zeros((B, 1, D), F32)
    return _norm_mod(xl, final_norm, zero, zero)
```

```python
import functools
import math

import jax
import jax.numpy as jnp
from jax import lax
from jax.experimental import pallas as pl
from jax.experimental.pallas import tpu as pltpu

F32 = jnp.float32
BF16 = jnp.bfloat16

GRID_W = 64
N_HEADS_ATTN = 8
N_KV = 2
HEAD_DIM = 64
AXIS_DIM = HEAD_DIM // 2
ATTN_W = N_HEADS_ATTN * HEAD_DIM
KV_W = N_KV * HEAD_DIM
WINDOW = 128
QBLOCK = 128
ROPE_BASE = 10000.0
LRU_W = 512
LRU_BLOCKS = 8
CONV_W = 4
CONV_LEFT = 2
LRU_C = 8.0
RWKV_H = 16
RWKV_HD = 64
GN_EPS = 64e-5
N_EXPERTS = 16
CAPACITY = 2
NORM_EPS = 1e-6
NEG_INF = -1e30

LANES = 128
VMEM_LIMIT_BYTES = 56 * 1024 * 1024
WKV_TC = 16


def _params(*sem):
    return pltpu.CompilerParams(dimension_semantics=sem, vmem_limit_bytes=VMEM_LIMIT_BYTES)


def _mm_kernel(x_ref, w_ref, o_ref, *, precise):
    if precise:
        o_ref[...] = jnp.dot(x_ref[...], w_ref[...], precision=lax.Precision.HIGHEST,
                             preferred_element_type=F32)
    else:
        o_ref[...] = jnp.dot(x_ref[...].astype(BF16), w_ref[...].astype(BF16),
                             preferred_element_type=F32)


def _mm(x, w, *, tm=512, tn=None, precise=False):
    M, K = x.shape
    N = w.shape[1]
    tm = min(tm, M)
    tn = N if tn is None else min(tn, N)
    assert M % tm == 0 and N % tn == 0, (M, tm, N, tn)
    return pl.pallas_call(
        functools.partial(_mm_kernel, precise=precise),
        grid=(M // tm, N // tn),
        in_specs=[pl.BlockSpec((tm, K), lambda i, j: (i, 0)),
                  pl.BlockSpec((K, tn), lambda i, j: (0, j))],
        out_specs=pl.BlockSpec((tm, tn), lambda i, j: (i, j)),
        out_shape=jax.ShapeDtypeStruct((M, N), F32),
        compiler_params=_params("parallel", "parallel"),
        name="mm",
    )(x, w)


def _norm_mod_kernel(x_ref, g_ref, sh_ref, sc_ref, o_ref):
    x = x_ref[0]
    y = x * lax.rsqrt(jnp.mean(x * x, axis=-1, keepdims=True) + NORM_EPS) * g_ref[...]
    o_ref[0] = y * (1.0 + sc_ref[0]) + sh_ref[0]


def _norm_mod(x, g, shift, scale):
    B, N, D = x.shape
    ts = min(512, N)
    return pl.pallas_call(
        _norm_mod_kernel,
        grid=(B, N // ts),
        in_specs=[pl.BlockSpec((1, ts, D), lambda b, i: (b, i, 0)),
                  pl.BlockSpec((1, D), lambda b, i: (0, 0)),
                  pl.BlockSpec((1, 1, D), lambda b, i: (b, 0, 0)),
                  pl.BlockSpec((1, 1, D), lambda b, i: (b, 0, 0))],
        out_specs=pl.BlockSpec((1, ts, D), lambda b, i: (b, i, 0)),
        out_shape=jax.ShapeDtypeStruct((B, N, D), F32),
        compiler_params=_params("parallel", "parallel"),
        name="norm_mod",
    )(x, g.reshape(1, D), shift, scale)


def _rope_kernel(x_ref, cos_ref, sin_ref, o_ref):
    width = x_ref.shape[-1]
    for c in range(width // LANES):
        sl = slice(c * LANES, (c + 1) * LANES)
        x = x_ref[0, :, sl]
        lane = lax.broadcasted_iota(jnp.int32, x.shape, 1)
        first_half = (lane % AXIS_DIM) < (AXIS_DIM // 2)
        partner = jnp.where(first_half, pltpu.roll(x, LANES - AXIS_DIM // 2, 1),
                            pltpu.roll(x, AXIS_DIM // 2, 1))
        o_ref[0, :, sl] = x * cos_ref[:, sl] + partner * sin_ref[:, sl]


def _rope_tables(S):
    rows = S // GRID_W
    row = jnp.repeat(jnp.arange(rows), GRID_W).astype(F32)
    col = jnp.tile(jnp.arange(GRID_W), rows).astype(F32)
    inv = ROPE_BASE ** (-jnp.arange(0, AXIS_DIM, 2, dtype=F32) / AXIS_DIM)
    ang_row, ang_col = row[:, None] * inv, col[:, None] * inv

    def axis_tables(ang):
        c, s = jnp.cos(ang), jnp.sin(ang)
        return jnp.concatenate([c, c], -1), jnp.concatenate([-s, s], -1)

    cr, sr = axis_tables(ang_row)
    cc, sc = axis_tables(ang_col)
    cos_h = jnp.concatenate([cr, cc], -1)
    sin_h = jnp.concatenate([sr, sc], -1)
    q_scale = HEAD_DIM ** -0.5
    cos = jnp.concatenate([jnp.tile(cos_h, (1, N_HEADS_ATTN)) * q_scale, jnp.tile(cos_h, (1, N_KV))], -1)
    sin = jnp.concatenate([jnp.tile(sin_h, (1, N_HEADS_ATTN)) * q_scale, jnp.tile(sin_h, (1, N_KV))], -1)
    return cos, sin


def _rope(proj, cos, sin):
    B, S, _ = proj.shape
    width = ATTN_W + KV_W
    ts = min(512, S)
    return pl.pallas_call(
        _rope_kernel,
        grid=(B, S // ts),
        in_specs=[pl.BlockSpec((1, ts, width), lambda b, i: (b, i, 0)),
                  pl.BlockSpec((ts, width), lambda b, i: (i, 0)),
                  pl.BlockSpec((ts, width), lambda b, i: (i, 0))],
        out_specs=pl.BlockSpec((1, ts, width), lambda b, i: (b, i, 0)),
        out_shape=jax.ShapeDtypeStruct((B, S, width), F32),
        compiler_params=_params("parallel", "parallel"),
        name="rope",
    )(proj, cos, sin)


def _attn_heads(q, kcat, vcat, valid, sink_ref):
    outs = []
    G = N_HEADS_ATTN // N_KV
    for j in range(N_KV):
        kj = kcat[:, j * HEAD_DIM:(j + 1) * HEAD_DIM]
        vj = vcat[:, j * HEAD_DIM:(j + 1) * HEAD_DIM]
        for g in range(G):
            h = j * G + g
            qh = q[:, h * HEAD_DIM:(h + 1) * HEAD_DIM].astype(BF16)
            s = lax.dot_general(qh, kj, (((1,), (1,)), ((), ())), preferred_element_type=F32)
            if valid is not None:
                s = jnp.where(valid, s, NEG_INF)
            sink = sink_ref[h]
            m = jnp.maximum(jnp.max(s, axis=-1, keepdims=True), sink)
            e = jnp.exp(s - m)
            den = jnp.sum(e, axis=-1, keepdims=True) + jnp.exp(sink - m)
            o = jnp.dot(e.astype(BF16), vj, preferred_element_type=F32)
            outs.append(o / den)
    return jnp.concatenate(outs, axis=-1)


def _attn_kernel(sink_ref, q_ref, kp_ref, kc_ref, kn_ref, vp_ref, vc_ref, vn_ref, ck_ref, cv_ref,
                 o_ref, *, S, Lc):
    i = pl.program_id(1)
    kcat = jnp.concatenate([ck_ref[0], kp_ref[0], kc_ref[0], kn_ref[0]], axis=0).astype(BF16)
    vcat = jnp.concatenate([cv_ref[0], vp_ref[0], vc_ref[0], vn_ref[0]], axis=0).astype(BF16)
    L = Lc + 3 * QBLOCK
    row = lax.broadcasted_iota(jnp.int32, (QBLOCK, L), 0)
    col = lax.broadcasted_iota(jnp.int32, (QBLOCK, L), 1)
    rel = col - Lc - QBLOCK - row
    kpos = (i - 1) * QBLOCK + col - Lc
    valid = (col < Lc) | ((jnp.abs(rel) <= WINDOW) & (kpos >= 0) & (kpos < S))
    o_ref[0] = _attn_heads(q_ref[0], kcat, vcat, valid, sink_ref)


def _attention(sink, qk, proj_l, proj_c):
    B, S, _ = qk.shape
    Lc = proj_c.shape[1]
    nb = S // QBLOCK
    kcol, vcol = ATTN_W // KV_W, ATTN_W // KV_W + 1

    def blk(colblk, off):
        return pl.BlockSpec((1, QBLOCK, KV_W),
                            lambda b, i: (b, jnp.clip(i + off, 0, nb - 1), colblk))

    return pl.pallas_call(
        functools.partial(_attn_kernel, S=S, Lc=Lc),
        grid=(B, nb),
        in_specs=[pl.BlockSpec(memory_space=pltpu.SMEM),
                  pl.BlockSpec((1, QBLOCK, ATTN_W), lambda b, i: (b, i, 0)),
                  blk(kcol, -1), blk(kcol, 0), blk(kcol, 1),
                  blk(vcol, -1), blk(vcol, 0), blk(vcol, 1),
                  pl.BlockSpec((1, Lc, KV_W), lambda b, i: (b, 0, kcol)),
                  pl.BlockSpec((1, Lc, KV_W), lambda b, i: (b, 0, vcol))],
        out_specs=pl.BlockSpec((1, QBLOCK, ATTN_W), lambda b, i: (b, i, 0)),
        out_shape=jax.ShapeDtypeStruct((B, S, ATTN_W), F32),
        compiler_params=_params("parallel", "parallel"),
        name="attention",
    )(sink, qk, qk, qk, qk, proj_l, proj_l, proj_l, proj_c, proj_c)


def _ctx_attn_kernel(sink_ref, q_ref, ck_ref, cv_ref, o_ref):
    q = q_ref[0] * (HEAD_DIM ** -0.5)
    o_ref[0] = _attn_heads(q, ck_ref[0].astype(BF16), cv_ref[0].astype(BF16), None, sink_ref)


def _ctx_attention(sink, proj_c):
    B, Lc, _ = proj_c.shape
    kcol, vcol = ATTN_W // KV_W, ATTN_W // KV_W + 1
    return pl.pallas_call(
        _ctx_attn_kernel,
        grid=(B,),
        in_specs=[pl.BlockSpec(memory_space=pltpu.SMEM),
                  pl.BlockSpec((1, Lc, ATTN_W), lambda b: (b, 0, 0)),
                  pl.BlockSpec((1, Lc, KV_W), lambda b: (b, 0, kcol)),
                  pl.BlockSpec((1, Lc, KV_W), lambda b: (b, 0, vcol))],
        out_specs=pl.BlockSpec((1, Lc, ATTN_W), lambda b: (b, 0, 0)),
        out_shape=jax.ShapeDtypeStruct((B, Lc, ATTN_W), F32),
        compiler_params=_params("parallel"),
        name="ctx_attention",
    )(sink, proj_c, proj_c, proj_c)


def _seg_rev_block(j, nC, nL):
    return jnp.where(j < nC, nC - 1 - j, nC + nL - 1 - (j - nC))


def _lru_kernel(af_ref, bf_ref, ar_ref, br_ref, hf_ref, hr_ref, sf_ref, sr_ref, *, Tc):
    @pl.when(pl.program_id(1) == 0)
    def _():
        sf_ref[...] = jnp.zeros_like(sf_ref)
        sr_ref[...] = jnp.zeros_like(sr_ref)

    def step(s, carry):
        hf, hr = carry
        hf = af_ref[0, pl.ds(s, 1), :] * hf + bf_ref[0, pl.ds(s, 1), :]
        hf_ref[0, pl.ds(s, 1), :] = hf
        t = Tc - 1 - s
        hr = ar_ref[0, pl.ds(t, 1), :] * hr + br_ref[0, pl.ds(t, 1), :]
        hr_ref[0, pl.ds(t, 1), :] = hr
        return hf, hr

    hf, hr = lax.fori_loop(0, Tc, step, (sf_ref[...], sr_ref[...]), unroll=8)
    sf_ref[...] = hf
    sr_ref[...] = hr


def _lru_scan(a, b, Lc):
    B, T, _ = a.shape
    S = T - Lc
    Tc = math.gcd(math.gcd(Lc, S), 256)
    nC, nL = Lc // Tc, S // Tc

    def fwd(col):
        return pl.BlockSpec((1, Tc, LRU_W), lambda bi, j: (bi, j, col))

    def rev(col):
        return pl.BlockSpec((1, Tc, LRU_W), lambda bi, j: (bi, _seg_rev_block(j, nC, nL), col))

    return pl.pallas_call(
        functools.partial(_lru_kernel, Tc=Tc),
        grid=(B, nC + nL),
        in_specs=[fwd(0), fwd(0), rev(1), rev(1)],
        out_specs=[fwd(0), rev(0)],
        out_shape=[jax.ShapeDtypeStruct((B, T, LRU_W), F32)] * 2,
        scratch_shapes=[pltpu.VMEM((1, LRU_W), F32)] * 2,
        compiler_params=_params("parallel", "arbitrary"),
        name="lru_scan",
    )(a, b, a, b)


def _softplus(z):
    return jnp.maximum(z, 0.0) + jnp.log(1.0 + jnp.exp(-jnp.abs(z)))


def _nt_dot(wt, x):
    return lax.dot_general(wt, x, (((1,), (1,)), ((), ())), preferred_element_type=F32)


def _rwkv_prep_kernel(hp_ref, h_ref, hn_ref, mu_ref, wr_ref, wk_ref, wv_ref, w1_ref, w2_ref, a1_ref,
                      a2_ref, g1_ref, g2_ref, w0_ref, a0_ref, ka_ref, rk_ref,
                      dec_ref, aa_ref, k_ref, v_ref, r_ref, g_ref, bc_ref, *, B, ctx_tiles, tiles):
    i = pl.program_id(0)
    D = h_ref.shape[1]
    tm = h_ref.shape[0]
    h = h_ref[...]
    seq_start = (i == 0) | (i == ctx_tiles)
    seq_end = (i == ctx_tiles - 1) | (i == tiles - 1)
    hp = jnp.where(seq_start, 0.0, hp_ref[...])
    hn = jnp.where(seq_end, 0.0, hn_ref[...])
    xx = 0.5 * (jnp.concatenate([hp, h[:tm - B]], axis=0) + jnp.concatenate([h[B:], hn], axis=0)) - h

    def mix(j):
        return (h + xx * mu_ref[j:j + 1, :]).astype(BF16)

    r = _nt_dot(wr_ref[...], mix(0))
    k = _nt_dot(wk_ref[...], mix(2))
    v = _nt_dot(wv_ref[...], mix(3))
    lw = jnp.tanh(_nt_dot(w1_ref[...], mix(1))).astype(BF16)
    wpre = jnp.dot(w2_ref[...], lw, preferred_element_type=F32)
    la = _nt_dot(a1_ref[...], mix(4)).astype(BF16)
    apre = jnp.dot(a2_ref[...], la, preferred_element_type=F32)
    gg = jax.nn.sigmoid(_nt_dot(g1_ref[...], mix(5))).astype(BF16)
    g_ref[...] = jnp.dot(g2_ref[...], gg, preferred_element_type=F32)
    hpg = LANES // B
    hd = RWKV_HD

    def to_scan_layout(x, ref):
        for g in range(RWKV_H // hpg):
            for c in range(tm // LANES):
                tiles = [x[(g * hpg + hh) * hd:(g * hpg + hh + 1) * hd, c * LANES:(c + 1) * LANES]
                         for hh in range(hpg)]
                for tl, tile in enumerate(_slab_transpose(tiles, B)):
                    ref[g, c * hpg + tl] = tile

    iclr = []
    for d in range(2):
        w_log = -_softplus(-(w0_ref[d] + wpre[d * D:(d + 1) * D])) - 0.5
        to_scan_layout(jnp.exp(-jnp.exp(w_log)), dec_ref.at[d])
        a = jax.nn.sigmoid(a0_ref[d] + apre[d * D:(d + 1) * D])
        iclr.append(a)
        to_scan_layout(a, aa_ref.at[d])
    kd_sum = k * (2.0 + (iclr[0] + iclr[1] - 2.0) * ka_ref[...])
    bc = jnp.sum((r * kd_sum * rk_ref[...]).reshape(RWKV_H, hd, tm), axis=1)
    for g in range(RWKV_H // hpg):
        for c in range(tm // LANES):
            rows = [jnp.broadcast_to(bc[g * hpg + hh:g * hpg + hh + 1, c * LANES:(c + 1) * LANES], (8, LANES))
                    for hh in range(hpg)]
            for tl, tile in enumerate(_slab_transpose(rows, B)):
                bc_ref[g, c * hpg + tl] = tile[0:1]
    to_scan_layout(k, k_ref)
    to_scan_layout(v, v_ref)
    to_scan_layout(r, r_ref)


def _slab_transpose(tiles, B):
    n = len(tiles)
    tiles = list(tiles)
    slab = lax.broadcasted_iota(jnp.int32, tiles[0].shape, 1) // B
    s = n // 2
    while s >= 1:
        upper = (slab & s) != 0
        for i in range(n):
            if i & s == 0:
                lo, hi = tiles[i], tiles[i + s]
                tiles[i] = jnp.where(upper, pltpu.roll(hi, s * B, 1), lo)
                tiles[i + s] = jnp.where(upper, hi, pltpu.roll(lo, LANES - s * B, 1))
        s //= 2
    return tiles


def _const_spec(shape):
    nd = len(shape)
    return pl.BlockSpec(shape, lambda i: (0,) * nd, pipeline_mode=pl.Buffered(1))


def _rwkv_prep(h_tm, B, Lc, mu, w_rkv, w0, w1, w2, a0, a1, a2, g1, g2, k_a, r_k):
    TB, D = h_tm.shape
    tm = 256
    tiles, ctx_tiles = TB // tm, Lc * B // tm
    hb = tm // B
    T = TB // B
    last_hb = T - 1
    G = RWKV_H // (LANES // B)
    hd = RWKV_HD

    def t_bf16(w):
        return w.T.astype(BF16)

    zeros_w = jnp.zeros_like(w2[0].T)
    w2t = jnp.concatenate([jnp.concatenate([w2[0].T, zeros_w], 1),
                           jnp.concatenate([zeros_w, w2[1].T], 1)], 0).astype(BF16)
    zeros_a = jnp.zeros_like(a2[0].T)
    a2t = jnp.concatenate([jnp.concatenate([a2[0].T, zeros_a], 1),
                           jnp.concatenate([zeros_a, a2[1].T], 1)], 0).astype(BF16)
    consts = [mu, t_bf16(w_rkv[0]), t_bf16(w_rkv[1]), t_bf16(w_rkv[2]),
              t_bf16(jnp.concatenate([w1[0], w1[1]], 1)), w2t,
              t_bf16(jnp.concatenate([a1[0], a1[1]], 1)), a2t, t_bf16(g1), t_bf16(g2),
              w0.reshape(2, D, 1), a0.reshape(2, D, 1), k_a.reshape(D, 1), r_k.reshape(D, 1)]
    dir_spec = pl.BlockSpec((2, G, hb, hd, LANES), lambda i: (0, 0, i, 0, 0))
    all_spec = pl.BlockSpec((G, hb, hd, LANES), lambda i: (0, i, 0, 0))
    dir_shape = jax.ShapeDtypeStruct((2, G, T, hd, LANES), F32)
    all_shape = jax.ShapeDtypeStruct((G, T, hd, LANES), F32)
    return pl.pallas_call(
        functools.partial(_rwkv_prep_kernel, B=B, ctx_tiles=ctx_tiles, tiles=tiles),
        grid=(tiles,),
        in_specs=[pl.BlockSpec((B, D), lambda i: (jnp.maximum(i * hb - 1, 0), 0)),
                  pl.BlockSpec((tm, D), lambda i: (i, 0)),
                  pl.BlockSpec((B, D), lambda i: (jnp.minimum((i + 1) * hb, last_hb), 0))]
                 + [_const_spec(c.shape) for c in consts],
        out_specs=[dir_spec, dir_spec, all_spec, all_spec, all_spec,
                   pl.BlockSpec((D, tm), lambda i: (0, i)),
                   pl.BlockSpec((G, hb, 1, LANES), lambda i: (0, i, 0, 0))],
        out_shape=[dir_shape, dir_shape, all_shape, all_shape, all_shape,
                   jax.ShapeDtypeStruct((D, TB), F32),
                   jax.ShapeDtypeStruct((G, T, 1, LANES), F32)],
        compiler_params=_params("parallel"),
        name="rwkv_prep",
    )(h_tm, h_tm, h_tm, *consts)


def _wkv_kernel(dec_ref, aa_ref, k_ref, v_ref, r_ref, kkc_ref, kac_ref, y_ref, s_ref, st_ref, *, Tc, nC):
    d = pl.program_id(0)
    j = pl.program_id(2)

    @pl.when(j == 0)
    def _():
        s_ref[...] = jnp.zeros_like(s_ref)

    def run(with_y):
        def step(s, _):
            tt = jnp.where(d == 0, s, Tc - 1 - s)
            aa = aa_ref[0, 0, tt]
            kk = k_ref[0, tt]
            kf = kk * kkc_ref[0]
            kn = kf * lax.rsqrt(jnp.maximum(jnp.sum(kf * kf, axis=0, keepdims=True), 1e-24))
            st_ref[0] = -kn
            st_ref[1] = kn * aa
            st_ref[2] = kk * (1.0 + (aa - 1.0) * kac_ref[0])

            def row(v, _):
                st = s_ref[v]
                sa = jnp.sum(st * st_ref[0], axis=0, keepdims=True)
                vv = v_ref[0, tt, pl.ds(v, 1), :]
                st = st * dec_ref[0, 0, tt] + sa * st_ref[1] + vv * st_ref[2]
                s_ref[v] = st
                if with_y:
                    y_ref[0, 0, tt, pl.ds(v, 1), :] = jnp.sum(st * r_ref[0, tt], axis=0, keepdims=True)
                return 0

            lax.fori_loop(0, RWKV_HD, row, 0, unroll=8)
            return 0

        lax.fori_loop(0, Tc, step, 0)

    @pl.when(j < nC)
    def _():
        run(False)

    @pl.when(j >= nC)
    def _():
        run(True)


def _wkv(dec, aa, k, v, r, k_k, k_a, B, Lc):
    _, G, T, hd, _ = dec.shape
    Tc = WKV_TC
    S = T - Lc
    nC, nL = Lc // Tc, S // Tc

    def tmap(d, j):
        return jnp.where(d == 0, j, _seg_rev_block(j, nC, nL))

    def lmap(d, j):
        jj = jnp.maximum(j - nC, 0)
        return jnp.where(d == 0, jj, nL - 1 - jj)

    dir_spec = pl.BlockSpec((1, 1, Tc, hd, LANES), lambda d, g, j: (d, g, tmap(d, j), 0, 0))
    all_spec = pl.BlockSpec((1, Tc, hd, LANES), lambda d, g, j: (g, tmap(d, j), 0, 0))
    const_spec = pl.BlockSpec((1, hd, LANES), lambda d, g, j: (g, 0, 0))
    return pl.pallas_call(
        functools.partial(_wkv_kernel, Tc=Tc, nC=nC),
        grid=(2, G, nC + nL),
        in_specs=[dir_spec, dir_spec, all_spec, all_spec, all_spec, const_spec, const_spec],
        out_specs=pl.BlockSpec((1, 1, Tc, hd, LANES), lambda d, g, j: (d, g, lmap(d, j), 0, 0)),
        out_shape=jax.ShapeDtypeStruct((2, G, S, hd, LANES), F32),
        scratch_shapes=[pltpu.VMEM((hd, hd, LANES), F32),
                        pltpu.VMEM((3, hd, LANES), F32)],
        compiler_params=_params("parallel", "parallel", "arbitrary"),
        name="wkv",
    )(dec, aa, k, v, r, _lane_const(k_k, B), _lane_const(k_a, B))


def _lane_const(c, B):
    hpg = LANES // B
    return jnp.repeat(c.reshape(RWKV_H // hpg, hpg, RWKV_HD).transpose(0, 2, 1), B, axis=-1)


def _rwkv_finish_kernel(y_ref, v_ref, bc_ref, g_ref, gng_ref, gnb_ref, wo_ref, o_ref, *, B):
    G, steps, hd, _ = v_ref.shape
    hpg = LANES // B
    rows = []
    for g in range(G):
        head_cols = [[] for _ in range(hpg)]
        for c in range(steps // hpg):
            tiles = []
            for tl in range(hpg):
                t = c * hpg + tl
                y = y_ref[0, g, t] + y_ref[1, g, t]
                mean = jnp.mean(y, axis=0, keepdims=True)
                var = jnp.mean(jnp.square(y - mean), axis=0, keepdims=True)
                yn = (y - mean) * lax.rsqrt(var + GN_EPS)
                tiles.append(yn * gng_ref[g] + gnb_ref[g] + bc_ref[g, t] * v_ref[g, t])
            for hh, tile in enumerate(_slab_transpose(tiles, B)):
                head_cols[hh].append(tile)
        rows.extend(jnp.concatenate(cols, axis=1) for cols in head_cols)
    o = (jnp.concatenate(rows, axis=0) * g_ref[...]).astype(BF16)
    out_t = jnp.dot(wo_ref[...], o, preferred_element_type=F32)
    o_ref[...] = out_t.T


def _rwkv_finish(y2, v, bc, g, gn_g, gn_b, w_o, B, Lc):
    _, G, S, hd, _ = y2.shape
    D = g.shape[0]
    tm = 256
    steps = tm // B
    off = Lc // steps
    return pl.pallas_call(
        functools.partial(_rwkv_finish_kernel, B=B),
        grid=(S // steps,),
        in_specs=[pl.BlockSpec((2, G, steps, hd, LANES), lambda i: (0, 0, i, 0, 0)),
                  pl.BlockSpec((G, steps, hd, LANES), lambda i: (0, i + off, 0, 0)),
                  pl.BlockSpec((G, steps, 1, LANES), lambda i: (0, i + off, 0, 0)),
                  pl.BlockSpec((D, tm), lambda i: (0, i + off)),
                  _const_spec((G, hd, LANES)), _const_spec((G, hd, LANES)), _const_spec((D, D))],
        out_specs=pl.BlockSpec((tm, D), lambda i: (i, 0)),
        out_shape=jax.ShapeDtypeStruct((S * B, D), F32),
        compiler_params=_params("parallel"),
        name="rwkv_finish",
    )(y2, v, bc, g, _lane_const(gn_g, B), _lane_const(gn_b, B), w_o.T.astype(BF16))


def _rwkv7_mixer(hl, hc, mu, w_rkv, w0, w1, w2, a0, a1, a2, g1, g2, k_k, k_a, r_k, gn_g, gn_b, w_o):
    B, S, D = hl.shape
    Lc = hc.shape[1]
    h_tm = jnp.concatenate([hc.transpose(1, 0, 2), hl.transpose(1, 0, 2)], axis=0).reshape((Lc + S) * B, D)
    dec, aa, k, v, r, g, bc = _rwkv_prep(h_tm, B, Lc, mu, w_rkv, w0, w1, w2, a0, a1, a2, g1, g2,
                                         k_a, r_k.reshape(D))
    y2 = _wkv(dec, aa, k, v, r, k_k, k_a, B, Lc)
    y_tm = _rwkv_finish(y2, v, bc, g, gn_g, gn_b, w_o, B, Lc)
    return y_tm.reshape(S, B, D).transpose(1, 0, 2)


def _router_kernel(h_ref, w_ref, o_ref):
    logits = lax.dot_general(w_ref[...], h_ref[0], (((1,), (1,)), ((), ())),
                             precision=lax.Precision.HIGHEST, preferred_element_type=F32)
    m = jnp.max(logits, axis=0, keepdims=True)
    e = jnp.exp(logits - m)
    o_ref[0] = e / jnp.sum(e, axis=0, keepdims=True)


def _router(h, w_router):
    B, N, D = h.shape
    E = w_router.shape[1]
    tn = min(512, N)
    return pl.pallas_call(
        _router_kernel,
        grid=(B, N // tn),
        in_specs=[pl.BlockSpec((1, tn, D), lambda b, i: (b, i, 0)),
                  pl.BlockSpec((E, D), lambda b, i: (0, 0))],
        out_specs=pl.BlockSpec((1, E, tn), lambda b, i: (b, 0, i)),
        out_shape=jax.ShapeDtypeStruct((B, E, N), F32),
        compiler_params=_params("parallel", "parallel"),
        name="router",
    )(h, w_router.T)


def _moe_ffn_kernel(idx_ref, gate_ref, h_ref, w1_ref, w3_ref, w2_ref, o_ref, xin_ref, *, cap):
    def gather(r, _):
        xin_ref[pl.ds(r, 1), :] = h_ref[0, pl.ds(idx_ref[0, 0, r], 1), :]
        return 0

    lax.fori_loop(0, cap, gather, 0, unroll=8)
    x = xin_ref[...].astype(BF16)
    h1 = jnp.dot(x, w1_ref[0], preferred_element_type=F32)
    h3 = jnp.dot(x, w3_ref[0], preferred_element_type=F32)
    hid = (h1 * jax.nn.sigmoid(h1) * h3).astype(BF16)
    o_ref[0] = jnp.dot(hid, w2_ref[0], preferred_element_type=F32) * gate_ref[0]


def _moe_scatter_kernel(idx_ref, y_ref, o_ref, *, cap):
    @pl.when(pl.program_id(1) == 0)
    def _():
        o_ref[...] = jnp.zeros_like(o_ref)

    def add(r, _):
        row = pl.ds(idx_ref[0, 0, r], 1)
        o_ref[0, row, :] = o_ref[0, row, :] + y_ref[0, pl.ds(r, 1), :]
        return 0

    lax.fori_loop(0, cap, add, 0, unroll=8)


def _moe(h, w_router, w1, w3, w2):
    B, N, D = h.shape
    E, _, FF = w1.shape
    cap = CAPACITY * N // E
    aff = _router(h, w_router)
    gate, idx = lax.top_k(aff, cap)
    idx3 = idx.reshape(B * E, 1, cap).astype(jnp.int32)
    idx_spec = pl.BlockSpec((1, 1, cap), lambda b, e: (b * E + e, 0, 0), memory_space=pltpu.SMEM)
    y = pl.pallas_call(
        functools.partial(_moe_ffn_kernel, cap=cap),
        grid=(B, E),
        in_specs=[idx_spec,
                  pl.BlockSpec((1, cap, 1), lambda b, e: (b * E + e, 0, 0)),
                  pl.BlockSpec((1, N, D), lambda b, e: (b, 0, 0)),
                  pl.BlockSpec((1, D, FF), lambda b, e: (e, 0, 0)),
                  pl.BlockSpec((1, D, FF), lambda b, e: (e, 0, 0)),
                  pl.BlockSpec((1, FF, D), lambda b, e: (e, 0, 0))],
        out_specs=pl.BlockSpec((1, cap, D), lambda b, e: (b * E + e, 0, 0)),
        out_shape=jax.ShapeDtypeStruct((B * E, cap, D), F32),
        scratch_shapes=[pltpu.VMEM((cap, D), F32)],
        compiler_params=_params("parallel", "arbitrary"),
        name="moe_ffn",
    )(idx3, gate.reshape(B * E, cap, 1), h, w1, w3, w2)
    return pl.pallas_call(
        functools.partial(_moe_scatter_kernel, cap=cap),
        grid=(B, E),
        in_specs=[idx_spec,
                  pl.BlockSpec((1, cap, D), lambda b, e: (b * E + e, 0, 0))],
        out_specs=pl.BlockSpec((1, N, D), lambda b, e: (b, 0, 0)),
        out_shape=jax.ShapeDtypeStruct((B, N, D), F32),
        compiler_params=_params("parallel", "arbitrary"),
        name="moe_scatter",
    )(idx3, y)


def _block_diag(w):
    H, bi, bj = w.shape
    eye = jnp.eye(H, dtype=w.dtype)
    return jnp.einsum('hij,hg->higj', w, eye).reshape(H * bi, H * bj)


def _dwconv(u, w, b):
    n = u.shape[1]
    up = jnp.pad(u, ((0, 0), (CONV_LEFT, CONV_W - 1 - CONV_LEFT), (0, 0)))
    out = up[:, 0:n] * w[0]
    for j in range(1, CONV_W):
        out = out + up[:, j:j + n] * w[j]
    return out + b


def _attn_lru_mixer(hl, hc, cos, sin, w_in, w_out, sink, conv_w, conv_b, wa, ba, wi, bi, lam):
    B, S, D = hl.shape
    Lc = hc.shape[1]
    mix_in = w_in.shape[1]
    w_in_b = w_in.astype(BF16)
    proj_l = _mm(hl.reshape(B * S, D), w_in_b).reshape(B, S, mix_in)
    proj_c = _mm(hc.reshape(B * Lc, D), w_in_b).reshape(B, Lc, mix_in)
    qk = _rope(proj_l, cos, sin)
    attn_l = _attention(sink, qk, proj_l, proj_c)
    attn_c = _ctx_attention(sink, proj_c)

    u0, g0 = ATTN_W + 2 * KV_W, ATTN_W + 2 * KV_W + LRU_W
    u = jnp.concatenate([_dwconv(proj_c[..., u0:g0], conv_w, conv_b),
                         _dwconv(proj_l[..., u0:g0], conv_w, conv_b)], axis=1)
    T = Lc + S
    w_gates = jnp.concatenate([_block_diag(wa[0]), _block_diag(wi[0]),
                               _block_diag(wa[1]), _block_diag(wi[1])], axis=1).astype(BF16)
    b_gates = jnp.concatenate([ba[0], bi[0], ba[1], bi[1]])
    gates = jax.nn.sigmoid(_mm(u.reshape(B * T, LRU_W), w_gates).reshape(B, T, 4 * LRU_W) + b_gates)
    a_dirs, b_dirs = [], []
    for d in range(2):
        r = gates[..., (2 * d) * LRU_W:(2 * d + 1) * LRU_W]
        i = gates[..., (2 * d + 1) * LRU_W:(2 * d + 2) * LRU_W]
        log_a = -LRU_C * r * jax.nn.softplus(-lam[d])
        a_dirs.append(jnp.exp(log_a))
        b_dirs.append(jnp.sqrt(jnp.maximum(-jnp.expm1(2 * log_a), 0.0)) * (i * u))
    hf, hr = _lru_scan(jnp.concatenate(a_dirs, -1), jnp.concatenate(b_dirs, -1), Lc)
    rec = hf + hr
    zl = jnp.concatenate([attn_l, rec[:, Lc:] * jax.nn.gelu(proj_l[..., g0:])], axis=-1)
    zc = jnp.concatenate([attn_c, rec[:, :Lc] * jax.nn.gelu(proj_c[..., g0:])], axis=-1)
    w_out_b = w_out.astype(BF16)
    yl = _mm(zl.reshape(B * S, -1), w_out_b).reshape(B, S, D)
    yc = _mm(zc.reshape(B * Lc, -1), w_out_b).reshape(B, Lc, D)
    return yl, yc


def kernel(x, c, ctx, c_ctx, mod_w, mod_b, norm_mix, norm_ffn, router_w, exp_w1, exp_w3, exp_w2, mix_in, mix_out, attn_sink, lru_conv_w, lru_conv_b, lru_wa, lru_ba, lru_wi, lru_bi, lru_lam, rw_mu, rw_rkv, rw_w0, rw_w1, rw_w2, rw_a0, rw_a1, rw_a2, rw_g1, rw_g2, rw_kk, rw_ka, rw_rk, rw_gn_g, rw_gn_b, rw_wo, final_norm):
    B, S, D = x.shape
    Lc = ctx.shape[1]
    depth = mod_w.shape[0]
    assert depth == 2 and S % QBLOCK == 0
    assert LANES % B == 0 and RWKV_H % (LANES // B) == 0 and B % 8 == 0
    assert Lc % WKV_TC == 0 and S % WKV_TC == 0 and (Lc * B) % 256 == 0
    cos, sin = _rope_tables(S)

    n_rows = -(-(B + 1) // 8) * 8
    cond = jnp.concatenate([jax.nn.silu(c), jax.nn.silu(c_ctx)[None],
                            jnp.zeros((n_rows - B - 1, D), F32)], axis=0)

    xl, xc = x, ctx
    for layer in range(depth):
        last = layer == depth - 1
        mod = _mm(cond, mod_w[layer], tm=n_rows, tn=1024, precise=True) + mod_b[layer]
        m_l = [t[:, None, :] for t in jnp.split(mod[:B], 6, axis=-1)]
        m_c = [jnp.broadcast_to(t[None], (B, 1, D)) for t in jnp.split(mod[B:B + 1], 6, axis=-1)]
        hl = _norm_mod(xl, norm_mix[layer], m_l[0], m_l[1])
        hc = _norm_mod(xc, norm_mix[layer], m_c[0], m_c[1])
        if layer % 2 == 0:
            i = layer // 2
            yl, yc = _attn_lru_mixer(hl, hc, cos, sin, mix_in[i], mix_out[i], attn_sink[i], lru_conv_w[i],
                                     lru_conv_b[i], lru_wa[i], lru_ba[i], lru_wi[i], lru_bi[i], lru_lam[i])
        else:
            i = layer // 2
            yl = _rwkv7_mixer(hl, hc, rw_mu[i], rw_rkv[i], rw_w0[i], rw_w1[i], rw_w2[i], rw_a0[i], rw_a1[i],
                              rw_a2[i], rw_g1[i], rw_g2[i], rw_kk[i], rw_ka[i], rw_rk[i], rw_gn_g[i],
                              rw_gn_b[i], rw_wo[i])
            yc = None
        w1b, w3b, w2b = exp_w1[layer].astype(BF16), exp_w3[layer].astype(BF16), exp_w2[layer].astype(BF16)
        xl = xl + m_l[2] * yl
        hl = _norm_mod(xl, norm_ffn[layer], m_l[3], m_l[4])
        xl = xl + m_l[5] * _moe(hl, router_w[layer], w1b, w3b, w2b)
        if not last:
            xc = xc + m_c[2] * yc
            hc = _norm_mod(xc, norm_ffn[layer], m_c[3], m_c[4])
            xc = xc + m_c[5] * _moe(hc, router_w[layer], w1b, w3b, w2b)
    zero = jnp.zeros((B, 1, D), F32)
    return _norm_mod(xl, final_norm, zero, zero)
```

```python
import functools
import math

import jax
import jax.numpy as jnp
from jax import lax
from jax.experimental import pallas as pl
from jax.experimental.pallas import tpu as pltpu

F32 = jnp.float32
BF16 = jnp.bfloat16

GRID_W = 64
N_HEADS_ATTN = 8
N_KV = 2
HEAD_DIM = 64
AXIS_DIM = HEAD_DIM // 2
ATTN_W = N_HEADS_ATTN * HEAD_DIM
KV_W = N_KV * HEAD_DIM
WINDOW = 128
QBLOCK = 128
ROPE_BASE = 10000.0
LRU_W = 512
LRU_BLOCKS = 8
CONV_W = 4
CONV_LEFT = 2
LRU_C = 8.0
RWKV_H = 16
RWKV_HD = 64
GN_EPS = 64e-5
N_EXPERTS = 16
CAPACITY = 2
NORM_EPS = 1e-6
NEG_INF = -1e30

LANES = 128
VMEM_LIMIT_BYTES = 56 * 1024 * 1024
WKV_TC = 16


def _params(*sem):
    return pltpu.CompilerParams(dimension_semantics=sem, vmem_limit_bytes=VMEM_LIMIT_BYTES)


def _mm_kernel(x_ref, w_ref, o_ref, *, precise):
    if precise:
        o_ref[...] = jnp.dot(x_ref[...], w_ref[...], precision=lax.Precision.HIGHEST,
                             preferred_element_type=F32)
    else:
        o_ref[...] = jnp.dot(x_ref[...].astype(BF16), w_ref[...].astype(BF16),
                             preferred_element_type=F32)


def _mm(x, w, *, tm=512, tn=None, precise=False):
    M, K = x.shape
    N = w.shape[1]
    tm = min(tm, M)
    tn = N if tn is None else min(tn, N)
    assert M % tm == 0 and N % tn == 0, (M, tm, N, tn)
    return pl.pallas_call(
        functools.partial(_mm_kernel, precise=precise),
        grid=(M // tm, N // tn),
        in_specs=[pl.BlockSpec((tm, K), lambda i, j: (i, 0)),
                  pl.BlockSpec((K, tn), lambda i, j: (0, j))],
        out_specs=pl.BlockSpec((tm, tn), lambda i, j: (i, j)),
        out_shape=jax.ShapeDtypeStruct((M, N), F32),
        compiler_params=_params("parallel", "parallel"),
        name="mm",
    )(x, w)


def _resid_norm_kernel(*refs, has_delta, delta_rows, want_x, out_rows):
    refs = list(refs)
    x_ref = refs.pop(0)
    d_ref, gate_ref = (refs.pop(0), refs.pop(0)) if has_delta else (None, None)
    g_ref, sh_ref, sc_ref = refs.pop(0), refs.pop(0), refs.pop(0)
    xo_ref = refs.pop(0) if (has_delta and want_x) else None
    h_ref = refs.pop(0)
    x = x_ref[0]
    if has_delta:
        delta = _from_chunk_rows(d_ref.at[0], x.shape[0]) if delta_rows else d_ref[0]
        x = x + gate_ref[0] * delta
        if want_x:
            xo_ref[0] = x
    y = x * lax.rsqrt(jnp.mean(x * x, axis=-1, keepdims=True) + NORM_EPS) * g_ref[...]
    h = y * (1.0 + sc_ref[0]) + sh_ref[0]
    if out_rows:
        _to_chunk_rows(h_ref.at[0], h)
    else:
        h_ref[0] = h


def _from_chunk_rows(ref, n):
    chunks = ref.shape[0] // n
    return jnp.concatenate([ref[pl.ds(j, n, stride=chunks), :] for j in range(chunks)], axis=-1)


def _to_chunk_rows(ref, x):
    n = x.shape[0]
    chunks = x.shape[1] // LANES
    for j in range(chunks):
        ref[pl.ds(j, n, stride=chunks), :] = x[:, j * LANES:(j + 1) * LANES]


def _resid_norm(x, delta, gate, g, shift, scale, *, delta_rows=False, want_x=True, out_rows=False):
    B, N, D = x.shape
    ts = min(512, N)
    has_delta = delta is not None
    chunks = D // LANES
    tok = pl.BlockSpec((1, ts, D), lambda b, i: (b, i, 0))
    tok_rows = pl.BlockSpec((1, ts * chunks, LANES), lambda b, i: (b, i, 0))
    per_b = pl.BlockSpec((1, 1, D), lambda b, i: (b, 0, 0))
    in_specs, args = [tok], [x]
    if has_delta:
        in_specs += [tok_rows if delta_rows else tok, per_b]
        args += [delta, gate]
    in_specs += [pl.BlockSpec((1, D), lambda b, i: (0, 0)), per_b, per_b]
    args += [g.reshape(1, D), shift, scale]
    out_specs, out_shape = [], []
    if has_delta and want_x:
        out_specs.append(tok)
        out_shape.append(jax.ShapeDtypeStruct((B, N, D), F32))
    out_specs.append(tok_rows if out_rows else tok)
    out_shape.append(jax.ShapeDtypeStruct((B, N * chunks, LANES) if out_rows else (B, N, D), F32))
    outs = pl.pallas_call(
        functools.partial(_resid_norm_kernel, has_delta=has_delta, delta_rows=delta_rows, want_x=want_x,
                          out_rows=out_rows),
        grid=(B, N // ts),
        in_specs=in_specs,
        out_specs=out_specs,
        out_shape=out_shape,
        compiler_params=_params("parallel", "parallel"),
        name="resid_norm",
    )(*args)
    return outs if len(outs) > 1 else outs[0]


def _rope_kernel(x_ref, cos_ref, sin_ref, o_ref):
    width = x_ref.shape[-1]
    for c in range(width // LANES):
        sl = slice(c * LANES, (c + 1) * LANES)
        x = x_ref[0, :, sl]
        lane = lax.broadcasted_iota(jnp.int32, x.shape, 1)
        first_half = (lane % AXIS_DIM) < (AXIS_DIM // 2)
        partner = jnp.where(first_half, pltpu.roll(x, LANES - AXIS_DIM // 2, 1),
                            pltpu.roll(x, AXIS_DIM // 2, 1))
        o_ref[0, :, sl] = x * cos_ref[:, sl] + partner * sin_ref[:, sl]


def _rope_tables(S):
    rows = S // GRID_W
    row = jnp.repeat(jnp.arange(rows), GRID_W).astype(F32)
    col = jnp.tile(jnp.arange(GRID_W), rows).astype(F32)
    inv = ROPE_BASE ** (-jnp.arange(0, AXIS_DIM, 2, dtype=F32) / AXIS_DIM)
    ang_row, ang_col = row[:, None] * inv, col[:, None] * inv

    def axis_tables(ang):
        c, s = jnp.cos(ang), jnp.sin(ang)
        return jnp.concatenate([c, c], -1), jnp.concatenate([-s, s], -1)

    cr, sr = axis_tables(ang_row)
    cc, sc = axis_tables(ang_col)
    cos_h = jnp.concatenate([cr, cc], -1)
    sin_h = jnp.concatenate([sr, sc], -1)
    q_scale = HEAD_DIM ** -0.5
    cos = jnp.concatenate([jnp.tile(cos_h, (1, N_HEADS_ATTN)) * q_scale, jnp.tile(cos_h, (1, N_KV))], -1)
    sin = jnp.concatenate([jnp.tile(sin_h, (1, N_HEADS_ATTN)) * q_scale, jnp.tile(sin_h, (1, N_KV))], -1)
    return cos, sin


def _rope(proj, cos, sin):
    B, S, _ = proj.shape
    width = ATTN_W + KV_W
    ts = min(512, S)
    return pl.pallas_call(
        _rope_kernel,
        grid=(B, S // ts),
        in_specs=[pl.BlockSpec((1, ts, width), lambda b, i: (b, i, 0)),
                  pl.BlockSpec((ts, width), lambda b, i: (i, 0)),
                  pl.BlockSpec((ts, width), lambda b, i: (i, 0))],
        out_specs=pl.BlockSpec((1, ts, width), lambda b, i: (b, i, 0)),
        out_shape=jax.ShapeDtypeStruct((B, S, width), F32),
        compiler_params=_params("parallel", "parallel"),
        name="rope",
    )(proj, cos, sin)


def _attn_heads(q, kcat, vcat, valid, sink_ref):
    outs = []
    G = N_HEADS_ATTN // N_KV
    for j in range(N_KV):
        kj = kcat[:, j * HEAD_DIM:(j + 1) * HEAD_DIM]
        vj = vcat[:, j * HEAD_DIM:(j + 1) * HEAD_DIM]
        for g in range(G):
            h = j * G + g
            qh = q[:, h * HEAD_DIM:(h + 1) * HEAD_DIM].astype(BF16)
            s = lax.dot_general(qh, kj, (((1,), (1,)), ((), ())), preferred_element_type=F32)
            if valid is not None:
                s = jnp.where(valid, s, NEG_INF)
            sink = sink_ref[h]
            m = jnp.maximum(jnp.max(s, axis=-1, keepdims=True), sink)
            e = jnp.exp(s - m)
            den = jnp.sum(e, axis=-1, keepdims=True) + jnp.exp(sink - m)
            o = jnp.dot(e.astype(BF16), vj, preferred_element_type=F32)
            outs.append(o / den)
    return jnp.concatenate(outs, axis=-1)


def _attn_kernel(sink_ref, q_ref, kp_ref, kc_ref, kn_ref, vp_ref, vc_ref, vn_ref, ck_ref, cv_ref,
                 o_ref, *, S, Lc):
    i = pl.program_id(1)
    kcat = jnp.concatenate([ck_ref[0], kp_ref[0], kc_ref[0], kn_ref[0]], axis=0).astype(BF16)
    vcat = jnp.concatenate([cv_ref[0], vp_ref[0], vc_ref[0], vn_ref[0]], axis=0).astype(BF16)
    L = Lc + 3 * QBLOCK
    row = lax.broadcasted_iota(jnp.int32, (QBLOCK, L), 0)
    col = lax.broadcasted_iota(jnp.int32, (QBLOCK, L), 1)
    rel = col - Lc - QBLOCK - row
    kpos = (i - 1) * QBLOCK + col - Lc
    valid = (col < Lc) | ((jnp.abs(rel) <= WINDOW) & (kpos >= 0) & (kpos < S))
    o_ref[0] = _attn_heads(q_ref[0], kcat, vcat, valid, sink_ref)


def _attention(sink, qk, proj_l, proj_c):
    B, S, _ = qk.shape
    Lc = proj_c.shape[1]
    nb = S // QBLOCK
    kcol, vcol = ATTN_W // KV_W, ATTN_W // KV_W + 1

    def blk(colblk, off):
        return pl.BlockSpec((1, QBLOCK, KV_W),
                            lambda b, i: (b, jnp.clip(i + off, 0, nb - 1), colblk))

    return pl.pallas_call(
        functools.partial(_attn_kernel, S=S, Lc=Lc),
        grid=(B, nb),
        in_specs=[pl.BlockSpec(memory_space=pltpu.SMEM),
                  pl.BlockSpec((1, QBLOCK, ATTN_W), lambda b, i: (b, i, 0)),
                  blk(kcol, -1), blk(kcol, 0), blk(kcol, 1),
                  blk(vcol, -1), blk(vcol, 0), blk(vcol, 1),
                  pl.BlockSpec((1, Lc, KV_W), lambda b, i: (b, 0, kcol)),
                  pl.BlockSpec((1, Lc, KV_W), lambda b, i: (b, 0, vcol))],
        out_specs=pl.BlockSpec((1, QBLOCK, ATTN_W), lambda b, i: (b, i, 0)),
        out_shape=jax.ShapeDtypeStruct((B, S, ATTN_W), F32),
        compiler_params=_params("parallel", "parallel"),
        name="attention",
    )(sink, qk, qk, qk, qk, proj_l, proj_l, proj_l, proj_c, proj_c)


def _ctx_attn_kernel(sink_ref, q_ref, ck_ref, cv_ref, o_ref):
    q = q_ref[0] * (HEAD_DIM ** -0.5)
    o_ref[0] = _attn_heads(q, ck_ref[0].astype(BF16), cv_ref[0].astype(BF16), None, sink_ref)


def _ctx_attention(sink, proj_c):
    B, Lc, _ = proj_c.shape
    kcol, vcol = ATTN_W // KV_W, ATTN_W // KV_W + 1
    return pl.pallas_call(
        _ctx_attn_kernel,
        grid=(B,),
        in_specs=[pl.BlockSpec(memory_space=pltpu.SMEM),
                  pl.BlockSpec((1, Lc, ATTN_W), lambda b: (b, 0, 0)),
                  pl.BlockSpec((1, Lc, KV_W), lambda b: (b, 0, kcol)),
                  pl.BlockSpec((1, Lc, KV_W), lambda b: (b, 0, vcol))],
        out_specs=pl.BlockSpec((1, Lc, ATTN_W), lambda b: (b, 0, 0)),
        out_shape=jax.ShapeDtypeStruct((B, Lc, ATTN_W), F32),
        compiler_params=_params("parallel"),
        name="ctx_attention",
    )(sink, proj_c, proj_c, proj_c)


def _seg_rev_block(j, nC, nL):
    return jnp.where(j < nC, nC - 1 - j, nC + nL - 1 - (j - nC))


def _lru_kernel(af_ref, bf_ref, ar_ref, br_ref, hf_ref, hr_ref, sf_ref, sr_ref, *, Tc):
    @pl.when(pl.program_id(1) == 0)
    def _():
        sf_ref[...] = jnp.zeros_like(sf_ref)
        sr_ref[...] = jnp.zeros_like(sr_ref)

    def step(s, carry):
        hf, hr = carry
        hf = af_ref[0, pl.ds(s, 1), :] * hf + bf_ref[0, pl.ds(s, 1), :]
        hf_ref[0, pl.ds(s, 1), :] = hf
        t = Tc - 1 - s
        hr = ar_ref[0, pl.ds(t, 1), :] * hr + br_ref[0, pl.ds(t, 1), :]
        hr_ref[0, pl.ds(t, 1), :] = hr
        return hf, hr

    hf, hr = lax.fori_loop(0, Tc, step, (sf_ref[...], sr_ref[...]), unroll=8)
    sf_ref[...] = hf
    sr_ref[...] = hr


def _lru_scan(a, b, Lc):
    B, T, _ = a.shape
    S = T - Lc
    Tc = math.gcd(math.gcd(Lc, S), 256)
    nC, nL = Lc // Tc, S // Tc

    def fwd(col):
        return pl.BlockSpec((1, Tc, LRU_W), lambda bi, j: (bi, j, col))

    def rev(col):
        return pl.BlockSpec((1, Tc, LRU_W), lambda bi, j: (bi, _seg_rev_block(j, nC, nL), col))

    return pl.pallas_call(
        functools.partial(_lru_kernel, Tc=Tc),
        grid=(B, nC + nL),
        in_specs=[fwd(0), fwd(0), rev(1), rev(1)],
        out_specs=[fwd(0), rev(0)],
        out_shape=[jax.ShapeDtypeStruct((B, T, LRU_W), F32)] * 2,
        scratch_shapes=[pltpu.VMEM((1, LRU_W), F32)] * 2,
        compiler_params=_params("parallel", "arbitrary"),
        name="lru_scan",
    )(a, b, a, b)


def _softplus(z):
    return jnp.maximum(z, 0.0) + jnp.log(1.0 + jnp.exp(-jnp.abs(z)))


def _nt_dot(wt, x):
    return lax.dot_general(wt, x, (((1,), (1,)), ((), ())), preferred_element_type=F32)


def _rwkv_prep_kernel(hp_ref, h_ref, hn_ref, mu_ref, wr_ref, wk_ref, wv_ref, w1_ref, w2_ref, a1_ref,
                      a2_ref, g1_ref, g2_ref, w0_ref, a0_ref, ka_ref, rk_ref,
                      dec_ref, aa_ref, k_ref, v_ref, r_ref, g_ref, bc_ref, *, B, ctx_tiles, tiles):
    i = pl.program_id(0)
    D = h_ref.shape[1]
    tm = h_ref.shape[0]
    h = h_ref[...]
    seq_start = (i == 0) | (i == ctx_tiles)
    seq_end = (i == ctx_tiles - 1) | (i == tiles - 1)
    hp = jnp.where(seq_start, 0.0, hp_ref[...])
    hn = jnp.where(seq_end, 0.0, hn_ref[...])
    xx = 0.5 * (jnp.concatenate([hp, h[:tm - B]], axis=0) + jnp.concatenate([h[B:], hn], axis=0)) - h

    def mix(j):
        return (h + xx * mu_ref[j:j + 1, :]).astype(BF16)

    r = _nt_dot(wr_ref[...], mix(0))
    k = _nt_dot(wk_ref[...], mix(2))
    v = _nt_dot(wv_ref[...], mix(3))
    lw = jnp.tanh(_nt_dot(w1_ref[...], mix(1))).astype(BF16)
    wpre = jnp.dot(w2_ref[...], lw, preferred_element_type=F32)
    la = _nt_dot(a1_ref[...], mix(4)).astype(BF16)
    apre = jnp.dot(a2_ref[...], la, preferred_element_type=F32)
    gg = jax.nn.sigmoid(_nt_dot(g1_ref[...], mix(5))).astype(BF16)
    g_ref[...] = jnp.dot(g2_ref[...], gg, preferred_element_type=F32)
    hpg = LANES // B
    hd = RWKV_HD

    def to_scan_layout(x, ref):
        for g in range(RWKV_H // hpg):
            for c in range(tm // LANES):
                tiles = [x[(g * hpg + hh) * hd:(g * hpg + hh + 1) * hd, c * LANES:(c + 1) * LANES]
                         for hh in range(hpg)]
                for tl, tile in enumerate(_slab_transpose(tiles, B)):
                    ref[g, c * hpg + tl] = tile

    iclr = []
    for d in range(2):
        w_log = -_softplus(-(w0_ref[d] + wpre[d * D:(d + 1) * D])) - 0.5
        to_scan_layout(jnp.exp(-jnp.exp(w_log)), dec_ref.at[d])
        a = jax.nn.sigmoid(a0_ref[d] + apre[d * D:(d + 1) * D])
        iclr.append(a)
        to_scan_layout(a, aa_ref.at[d])
    kd_sum = k * (2.0 + (iclr[0] + iclr[1] - 2.0) * ka_ref[...])
    bc = jnp.sum((r * kd_sum * rk_ref[...]).reshape(RWKV_H, hd, tm), axis=1)
    for g in range(RWKV_H // hpg):
        for c in range(tm // LANES):
            rows = [jnp.broadcast_to(bc[g * hpg + hh:g * hpg + hh + 1, c * LANES:(c + 1) * LANES], (8, LANES))
                    for hh in range(hpg)]
            for tl, tile in enumerate(_slab_transpose(rows, B)):
                bc_ref[g, c * hpg + tl] = tile[0:1]
    to_scan_layout(k, k_ref)
    to_scan_layout(v, v_ref)
    to_scan_layout(r, r_ref)


def _slab_transpose(tiles, B):
    n = len(tiles)
    tiles = list(tiles)
    slab = lax.broadcasted_iota(jnp.int32, tiles[0].shape, 1) // B
    s = n // 2
    while s >= 1:
        upper = (slab & s) != 0
        for i in range(n):
            if i & s == 0:
                lo, hi = tiles[i], tiles[i + s]
                tiles[i] = jnp.where(upper, pltpu.roll(hi, s * B, 1), lo)
                tiles[i + s] = jnp.where(upper, hi, pltpu.roll(lo, LANES - s * B, 1))
        s //= 2
    return tiles


def _const_spec(shape):
    nd = len(shape)
    return pl.BlockSpec(shape, lambda i: (0,) * nd, pipeline_mode=pl.Buffered(1))


def _rwkv_prep(h_tm, B, Lc, mu, w_rkv, w0, w1, w2, a0, a1, a2, g1, g2, k_a, r_k):
    TB, D = h_tm.shape
    tm = 256
    tiles, ctx_tiles = TB // tm, Lc * B // tm
    hb = tm // B
    T = TB // B
    last_hb = T - 1
    G = RWKV_H // (LANES // B)
    hd = RWKV_HD

    def t_bf16(w):
        return w.T.astype(BF16)

    zeros_w = jnp.zeros_like(w2[0].T)
    w2t = jnp.concatenate([jnp.concatenate([w2[0].T, zeros_w], 1),
                           jnp.concatenate([zeros_w, w2[1].T], 1)], 0).astype(BF16)
    zeros_a = jnp.zeros_like(a2[0].T)
    a2t = jnp.concatenate([jnp.concatenate([a2[0].T, zeros_a], 1),
                           jnp.concatenate([zeros_a, a2[1].T], 1)], 0).astype(BF16)
    consts = [mu, t_bf16(w_rkv[0]), t_bf16(w_rkv[1]), t_bf16(w_rkv[2]),
              t_bf16(jnp.concatenate([w1[0], w1[1]], 1)), w2t,
              t_bf16(jnp.concatenate([a1[0], a1[1]], 1)), a2t, t_bf16(g1), t_bf16(g2),
              w0.reshape(2, D, 1), a0.reshape(2, D, 1), k_a.reshape(D, 1), r_k.reshape(D, 1)]
    dir_spec = pl.BlockSpec((2, G, hb, hd, LANES), lambda i: (0, 0, i, 0, 0))
    all_spec = pl.BlockSpec((G, hb, hd, LANES), lambda i: (0, i, 0, 0))
    dir_shape = jax.ShapeDtypeStruct((2, G, T, hd, LANES), F32)
    all_shape = jax.ShapeDtypeStruct((G, T, hd, LANES), F32)
    return pl.pallas_call(
        functools.partial(_rwkv_prep_kernel, B=B, ctx_tiles=ctx_tiles, tiles=tiles),
        grid=(tiles,),
        in_specs=[pl.BlockSpec((B, D), lambda i: (jnp.maximum(i * hb - 1, 0), 0)),
                  pl.BlockSpec((tm, D), lambda i: (i, 0)),
                  pl.BlockSpec((B, D), lambda i: (jnp.minimum((i + 1) * hb, last_hb), 0))]
                 + [_const_spec(c.shape) for c in consts],
        out_specs=[dir_spec, dir_spec, all_spec, all_spec, all_spec,
                   pl.BlockSpec((D, tm), lambda i: (0, i)),
                   pl.BlockSpec((G, hb, 1, LANES), lambda i: (0, i, 0, 0))],
        out_shape=[dir_shape, dir_shape, all_shape, all_shape, all_shape,
                   jax.ShapeDtypeStruct((D, TB), F32),
                   jax.ShapeDtypeStruct((G, T, 1, LANES), F32)],
        compiler_params=_params("parallel"),
        name="rwkv_prep",
    )(h_tm, h_tm, h_tm, *consts)


def _wkv_kernel(dec_ref, aa_ref, k_ref, v_ref, r_ref, kkc_ref, kac_ref, y_ref, s_ref, st_ref, *, Tc, nC):
    d = pl.program_id(0)
    j = pl.program_id(2)

    @pl.when(j == 0)
    def _():
        s_ref[...] = jnp.zeros_like(s_ref)

    def run(with_y):
        def step(s, _):
            tt = jnp.where(d == 0, s, Tc - 1 - s)
            aa = aa_ref[0, 0, tt]
            kk = k_ref[0, tt]
            kf = kk * kkc_ref[0]
            kn = kf * lax.rsqrt(jnp.maximum(jnp.sum(kf * kf, axis=0, keepdims=True), 1e-24))
            st_ref[0] = -kn
            st_ref[1] = kn * aa
            st_ref[2] = kk * (1.0 + (aa - 1.0) * kac_ref[0])

            def row(v, _):
                st = s_ref[v]
                sa = jnp.sum(st * st_ref[0], axis=0, keepdims=True)
                vv = v_ref[0, tt, pl.ds(v, 1), :]
                st = st * dec_ref[0, 0, tt] + sa * st_ref[1] + vv * st_ref[2]
                s_ref[v] = st
                if with_y:
                    y_ref[0, 0, tt, pl.ds(v, 1), :] = jnp.sum(st * r_ref[0, tt], axis=0, keepdims=True)
                return 0

            lax.fori_loop(0, RWKV_HD, row, 0, unroll=8)
            return 0

        lax.fori_loop(0, Tc, step, 0)

    @pl.when(j < nC)
    def _():
        run(False)

    @pl.when(j >= nC)
    def _():
        run(True)


def _wkv(dec, aa, k, v, r, k_k, k_a, B, Lc):
    _, G, T, hd, _ = dec.shape
    Tc = WKV_TC
    S = T - Lc
    nC, nL = Lc // Tc, S // Tc

    def tmap(d, j):
        return jnp.where(d == 0, j, _seg_rev_block(j, nC, nL))

    def lmap(d, j):
        jj = jnp.maximum(j - nC, 0)
        return jnp.where(d == 0, jj, nL - 1 - jj)

    dir_spec = pl.BlockSpec((1, 1, Tc, hd, LANES), lambda d, g, j: (d, g, tmap(d, j), 0, 0))
    all_spec = pl.BlockSpec((1, Tc, hd, LANES), lambda d, g, j: (g, tmap(d, j), 0, 0))
    const_spec = pl.BlockSpec((1, hd, LANES), lambda d, g, j: (g, 0, 0))
    return pl.pallas_call(
        functools.partial(_wkv_kernel, Tc=Tc, nC=nC),
        grid=(2, G, nC + nL),
        in_specs=[dir_spec, dir_spec, all_spec, all_spec, all_spec, const_spec, const_spec],
        out_specs=pl.BlockSpec((1, 1, Tc, hd, LANES), lambda d, g, j: (d, g, lmap(d, j), 0, 0)),
        out_shape=jax.ShapeDtypeStruct((2, G, S, hd, LANES), F32),
        scratch_shapes=[pltpu.VMEM((hd, hd, LANES), F32),
                        pltpu.VMEM((3, hd, LANES), F32)],
        compiler_params=_params("parallel", "parallel", "arbitrary"),
        name="wkv",
    )(dec, aa, k, v, r, _lane_const(k_k, B), _lane_const(k_a, B))


def _lane_const(c, B):
    hpg = LANES // B
    return jnp.repeat(c.reshape(RWKV_H // hpg, hpg, RWKV_HD).transpose(0, 2, 1), B, axis=-1)


def _rwkv_finish_kernel(y_ref, v_ref, bc_ref, g_ref, gng_ref, gnb_ref, wo_ref, o_ref, *, B):
    G, steps, hd, _ = v_ref.shape
    hpg = LANES // B
    rows = []
    for g in range(G):
        head_cols = [[] for _ in range(hpg)]
        for c in range(steps // hpg):
            tiles = []
            for tl in range(hpg):
                t = c * hpg + tl
                y = y_ref[0, g, t] + y_ref[1, g, t]
                mean = jnp.mean(y, axis=0, keepdims=True)
                var = jnp.mean(jnp.square(y - mean), axis=0, keepdims=True)
                yn = (y - mean) * lax.rsqrt(var + GN_EPS)
                tiles.append(yn * gng_ref[g] + gnb_ref[g] + bc_ref[g, t] * v_ref[g, t])
            for hh, tile in enumerate(_slab_transpose(tiles, B)):
                head_cols[hh].append(tile)
        rows.extend(jnp.concatenate(cols, axis=1) for cols in head_cols)
    o = (jnp.concatenate(rows, axis=0) * g_ref[...]).astype(BF16)
    out_t = jnp.dot(wo_ref[...], o, preferred_element_type=F32)
    o_ref[...] = out_t.T


def _rwkv_finish(y2, v, bc, g, gn_g, gn_b, w_o, B, Lc):
    _, G, S, hd, _ = y2.shape
    D = g.shape[0]
    tm = 256
    steps = tm // B
    off = Lc // steps
    return pl.pallas_call(
        functools.partial(_rwkv_finish_kernel, B=B),
        grid=(S // steps,),
        in_specs=[pl.BlockSpec((2, G, steps, hd, LANES), lambda i: (0, 0, i, 0, 0)),
                  pl.BlockSpec((G, steps, hd, LANES), lambda i: (0, i + off, 0, 0)),
                  pl.BlockSpec((G, steps, 1, LANES), lambda i: (0, i + off, 0, 0)),
                  pl.BlockSpec((D, tm), lambda i: (0, i + off)),
                  _const_spec((G, hd, LANES)), _const_spec((G, hd, LANES)), _const_spec((D, D))],
        out_specs=pl.BlockSpec((tm, D), lambda i: (i, 0)),
        out_shape=jax.ShapeDtypeStruct((S * B, D), F32),
        compiler_params=_params("parallel"),
        name="rwkv_finish",
    )(y2, v, bc, g, _lane_const(gn_g, B), _lane_const(gn_b, B), w_o.T.astype(BF16))


def _rwkv7_mixer(hl, hc, mu, w_rkv, w0, w1, w2, a0, a1, a2, g1, g2, k_k, k_a, r_k, gn_g, gn_b, w_o):
    B, S, D = hl.shape
    Lc = hc.shape[1]
    h_tm = jnp.concatenate([hc.transpose(1, 0, 2), hl.transpose(1, 0, 2)], axis=0).reshape((Lc + S) * B, D)
    dec, aa, k, v, r, g, bc = _rwkv_prep(h_tm, B, Lc, mu, w_rkv, w0, w1, w2, a0, a1, a2, g1, g2,
                                         k_a, r_k.reshape(D))
    y2 = _wkv(dec, aa, k, v, r, k_k, k_a, B, Lc)
    y_tm = _rwkv_finish(y2, v, bc, g, gn_g, gn_b, w_o, B, Lc)
    return y_tm.reshape(S, B, D).transpose(1, 0, 2)


def _router_kernel(h_ref, w_ref, o_ref):
    h = _from_chunk_rows(h_ref.at[0], o_ref.shape[2])
    logits = lax.dot_general(w_ref[...], h, (((1,), (1,)), ((), ())),
                             precision=lax.Precision.HIGHEST, preferred_element_type=F32)
    m = jnp.max(logits, axis=0, keepdims=True)
    e = jnp.exp(logits - m)
    o_ref[0] = e / jnp.sum(e, axis=0, keepdims=True)


def _router(h, N, w_router):
    B = h.shape[0]
    D, E = w_router.shape
    chunks = D // LANES
    tn = min(512, N)
    return pl.pallas_call(
        _router_kernel,
        grid=(B, N // tn),
        in_specs=[pl.BlockSpec((1, tn * chunks, LANES), lambda b, i: (b, i, 0)),
                  pl.BlockSpec((E, D), lambda b, i: (0, 0))],
        out_specs=pl.BlockSpec((1, E, tn), lambda b, i: (b, 0, i)),
        out_shape=jax.ShapeDtypeStruct((B, E, N), F32),
        compiler_params=_params("parallel", "parallel"),
        name="router",
    )(h, w_router.T)


def _moe_kernel(idx_ref, gate_ref, h_ref, w1_ref, w3_ref, w2_ref, o_ref, xin_ref, y_ref, *, cap, chunks):

    @pl.when(pl.program_id(1) == 0)
    def _():
        o_ref[...] = jnp.zeros_like(o_ref)

    def token_rows(t):
        return pl.ds(pl.multiple_of(t * chunks, chunks), chunks)

    def gather(r, _):
        xin_ref[token_rows(r), :] = h_ref[0, token_rows(idx_ref[0, 0, r]), :]
        return 0

    lax.fori_loop(0, cap, gather, 0, unroll=8)
    x = _from_chunk_rows(xin_ref, cap).astype(BF16)
    h1 = jnp.dot(x, w1_ref[0], preferred_element_type=F32)
    h3 = jnp.dot(x, w3_ref[0], preferred_element_type=F32)
    hid = (h1 * jax.nn.sigmoid(h1) * h3).astype(BF16)
    _to_chunk_rows(y_ref, jnp.dot(hid, w2_ref[0], preferred_element_type=F32) * gate_ref[0])

    batch = 8

    def scatter_add(i, _):
        rows = [token_rows(idx_ref[0, 0, i * batch + u]) for u in range(batch)]
        sums = [o_ref[0, rows[u], :] + y_ref[token_rows(i * batch + u), :] for u in range(batch)]
        for u in range(batch):
            o_ref[0, rows[u], :] = sums[u]
        return 0

    lax.fori_loop(0, cap // batch, scatter_add, 0)


def _moe(h, N, w_router, w1, w3, w2):
    B = h.shape[0]
    E, D, FF = w1.shape
    chunks = D // LANES
    cap = CAPACITY * N // E
    aff = _router(h, N, w_router)
    gate, idx = lax.top_k(aff, cap)
    idx3 = idx.reshape(B * E, 1, cap).astype(jnp.int32)
    return pl.pallas_call(
        functools.partial(_moe_kernel, cap=cap, chunks=chunks),
        grid=(B, E),
        in_specs=[pl.BlockSpec((1, 1, cap), lambda b, e: (b * E + e, 0, 0), memory_space=pltpu.SMEM),
                  pl.BlockSpec((1, cap, 1), lambda b, e: (b * E + e, 0, 0)),
                  pl.BlockSpec((1, N * chunks, LANES), lambda b, e: (b, 0, 0), pipeline_mode=pl.Buffered(1)),
                  pl.BlockSpec((1, D, FF), lambda b, e: (e, 0, 0)),
                  pl.BlockSpec((1, D, FF), lambda b, e: (e, 0, 0)),
                  pl.BlockSpec((1, FF, D), lambda b, e: (e, 0, 0))],
        out_specs=pl.BlockSpec((1, N * chunks, LANES), lambda b, e: (b, 0, 0)),
        out_shape=jax.ShapeDtypeStruct((B, N * chunks, LANES), F32),
        scratch_shapes=[pltpu.VMEM((cap * chunks, LANES), F32), pltpu.VMEM((cap * chunks, LANES), F32)],
        compiler_params=_params("parallel", "arbitrary"),
        name="moe",
    )(idx3, gate.reshape(B * E, cap, 1), h, w1, w3, w2)


def _block_diag(w):
    H, bi, bj = w.shape
    eye = jnp.eye(H, dtype=w.dtype)
    return jnp.einsum('hij,hg->higj', w, eye).reshape(H * bi, H * bj)


def _dwconv(u, w, b):
    n = u.shape[1]
    up = jnp.pad(u, ((0, 0), (CONV_LEFT, CONV_W - 1 - CONV_LEFT), (0, 0)))
    out = up[:, 0:n] * w[0]
    for j in range(1, CONV_W):
        out = out + up[:, j:j + n] * w[j]
    return out + b


def _attn_lru_mixer(hl, hc, cos, sin, w_in, w_out, sink, conv_w, conv_b, wa, ba, wi, bi, lam):
    B, S, D = hl.shape
    Lc = hc.shape[1]
    mix_in = w_in.shape[1]
    w_in_b = w_in.astype(BF16)
    proj_l = _mm(hl.reshape(B * S, D), w_in_b).reshape(B, S, mix_in)
    proj_c = _mm(hc.reshape(B * Lc, D), w_in_b).reshape(B, Lc, mix_in)
    qk = _rope(proj_l, cos, sin)
    attn_l = _attention(sink, qk, proj_l, proj_c)
    attn_c = _ctx_attention(sink, proj_c)

    u0, g0 = ATTN_W + 2 * KV_W, ATTN_W + 2 * KV_W + LRU_W
    u = jnp.concatenate([_dwconv(proj_c[..., u0:g0], conv_w, conv_b),
                         _dwconv(proj_l[..., u0:g0], conv_w, conv_b)], axis=1)
    T = Lc + S
    w_gates = jnp.concatenate([_block_diag(wa[0]), _block_diag(wi[0]),
                               _block_diag(wa[1]), _block_diag(wi[1])], axis=1).astype(BF16)
    b_gates = jnp.concatenate([ba[0], bi[0], ba[1], bi[1]])
    gates = jax.nn.sigmoid(_mm(u.reshape(B * T, LRU_W), w_gates).reshape(B, T, 4 * LRU_W) + b_gates)
    a_dirs, b_dirs = [], []
    for d in range(2):
        r = gates[..., (2 * d) * LRU_W:(2 * d + 1) * LRU_W]
        i = gates[..., (2 * d + 1) * LRU_W:(2 * d + 2) * LRU_W]
        log_a = -LRU_C * r * jax.nn.softplus(-lam[d])
        a_dirs.append(jnp.exp(log_a))
        b_dirs.append(jnp.sqrt(jnp.maximum(-jnp.expm1(2 * log_a), 0.0)) * (i * u))
    hf, hr = _lru_scan(jnp.concatenate(a_dirs, -1), jnp.concatenate(b_dirs, -1), Lc)
    rec = hf + hr
    zl = jnp.concatenate([attn_l, rec[:, Lc:] * jax.nn.gelu(proj_l[..., g0:])], axis=-1)
    zc = jnp.concatenate([attn_c, rec[:, :Lc] * jax.nn.gelu(proj_c[..., g0:])], axis=-1)
    w_out_b = w_out.astype(BF16)
    yl = _mm(zl.reshape(B * S, -1), w_out_b).reshape(B, S, D)
    yc = _mm(zc.reshape(B * Lc, -1), w_out_b).reshape(B, Lc, D)
    return yl, yc


def kernel(x, c, ctx, c_ctx, mod_w, mod_b, norm_mix, norm_ffn, router_w, exp_w1, exp_w3, exp_w2, mix_in, mix_out, attn_sink, lru_conv_w, lru_conv_b, lru_wa, lru_ba, lru_wi, lru_bi, lru_lam, rw_mu, rw_rkv, rw_w0, rw_w1, rw_w2, rw_a0, rw_a1, rw_a2, rw_g1, rw_g2, rw_kk, rw_ka, rw_rk, rw_gn_g, rw_gn_b, rw_wo, final_norm):
    B, S, D = x.shape
    Lc = ctx.shape[1]
    depth = mod_w.shape[0]
    assert depth == 2 and S % QBLOCK == 0
    assert LANES % B == 0 and RWKV_H % (LANES // B) == 0 and B % 8 == 0
    assert Lc % WKV_TC == 0 and S % WKV_TC == 0 and (Lc * B) % 256 == 0
    cos, sin = _rope_tables(S)

    n_rows = -(-(B + 1) // 8) * 8
    cond = jnp.concatenate([jax.nn.silu(c), jax.nn.silu(c_ctx)[None],
                            jnp.zeros((n_rows - B - 1, D), F32)], axis=0)

    mods_l, mods_c = [], []
    for layer in range(depth):
        mod = _mm(cond, mod_w[layer], tm=n_rows, tn=1024, precise=True) + mod_b[layer]
        mods_l.append([t[:, None, :] for t in jnp.split(mod[:B], 6, axis=-1)])
        mods_c.append([jnp.broadcast_to(t[None], (B, 1, D)) for t in jnp.split(mod[B:B + 1], 6, axis=-1)])

    xl, xc = x, ctx
    hl = _resid_norm(xl, None, None, norm_mix[0], mods_l[0][0], mods_l[0][1])
    hc = _resid_norm(xc, None, None, norm_mix[0], mods_c[0][0], mods_c[0][1])
    for layer in range(depth):
        last = layer == depth - 1
        m_l, m_c = mods_l[layer], mods_c[layer]
        if layer % 2 == 0:
            i = layer // 2
            yl, yc = _attn_lru_mixer(hl, hc, cos, sin, mix_in[i], mix_out[i], attn_sink[i], lru_conv_w[i],
                                     lru_conv_b[i], lru_wa[i], lru_ba[i], lru_wi[i], lru_bi[i], lru_lam[i])
        else:
            i = layer // 2
            yl = _rwkv7_mixer(hl, hc, rw_mu[i], rw_rkv[i], rw_w0[i], rw_w1[i], rw_w2[i], rw_a0[i], rw_a1[i],
                              rw_a2[i], rw_g1[i], rw_g2[i], rw_kk[i], rw_ka[i], rw_rk[i], rw_gn_g[i],
                              rw_gn_b[i], rw_wo[i])
            yc = None
        w1b, w3b, w2b = exp_w1[layer].astype(BF16), exp_w3[layer].astype(BF16), exp_w2[layer].astype(BF16)
        xl, hl = _resid_norm(xl, yl, m_l[2], norm_ffn[layer], m_l[3], m_l[4], out_rows=True)
        moe_l = _moe(hl, S, router_w[layer], w1b, w3b, w2b)
        if last:
            zero = jnp.zeros((B, 1, D), F32)
            return _resid_norm(xl, moe_l, m_l[5], final_norm, zero, zero, delta_rows=True, want_x=False)
        n_l, n_c = mods_l[layer + 1], mods_c[layer + 1]
        xl, hl = _resid_norm(xl, moe_l, m_l[5], norm_mix[layer + 1], n_l[0], n_l[1], delta_rows=True)
        xc, hc = _resid_norm(xc, yc, m_c[2], norm_ffn[layer], m_c[3], m_c[4], out_rows=True)
        moe_c = _moe(hc, Lc, router_w[layer], w1b, w3b, w2b)
        xc, hc = _resid_norm(xc, moe_c, m_c[5], norm_mix[layer + 1], n_c[0], n_c[1], delta_rows=True)
```

```python
import functools
import math

import jax
import jax.numpy as jnp
from jax import lax
from jax.experimental import pallas as pl
from jax.experimental.pallas import tpu as pltpu

F32 = jnp.float32
BF16 = jnp.bfloat16

GRID_W = 64
N_HEADS_ATTN = 8
N_KV = 2
HEAD_DIM = 64
AXIS_DIM = HEAD_DIM // 2
ATTN_W = N_HEADS_ATTN * HEAD_DIM
KV_W = N_KV * HEAD_DIM
WINDOW = 128
QBLOCK = 128
ROPE_BASE = 10000.0
LRU_W = 512
LRU_BLOCKS = 8
CONV_W = 4
CONV_LEFT = 2
LRU_C = 8.0
RWKV_H = 16
RWKV_HD = 64
GN_EPS = 64e-5
N_EXPERTS = 16
CAPACITY = 2
NORM_EPS = 1e-6
NEG_INF = -1e30

LANES = 128
VMEM_LIMIT_BYTES = 56 * 1024 * 1024
WKV_TC = 16
MOE_ROWS = 256


def _params(*sem):
    return pltpu.CompilerParams(dimension_semantics=sem, vmem_limit_bytes=VMEM_LIMIT_BYTES)


def _mm_kernel(x_ref, w_ref, o_ref, *, precise):
    if precise:
        o_ref[...] = jnp.dot(x_ref[...], w_ref[...], precision=lax.Precision.HIGHEST,
                             preferred_element_type=F32)
    else:
        o_ref[...] = jnp.dot(x_ref[...].astype(BF16), w_ref[...].astype(BF16),
                             preferred_element_type=F32)


def _mm(x, w, *, tm=512, tn=None, precise=False):
    M, K = x.shape
    N = w.shape[1]
    tm = min(tm, M)
    tn = N if tn is None else min(tn, N)
    assert M % tm == 0 and N % tn == 0, (M, tm, N, tn)
    return pl.pallas_call(
        functools.partial(_mm_kernel, precise=precise),
        grid=(M // tm, N // tn),
        in_specs=[pl.BlockSpec((tm, K), lambda i, j: (i, 0)),
                  pl.BlockSpec((K, tn), lambda i, j: (0, j))],
        out_specs=pl.BlockSpec((tm, tn), lambda i, j: (i, j)),
        out_shape=jax.ShapeDtypeStruct((M, N), F32),
        compiler_params=_params("parallel", "parallel"),
        name="mm",
    )(x, w)


def _resid_norm_kernel(*refs, has_delta, delta_rows, want_x, out_rows):
    refs = list(refs)
    x_ref = refs.pop(0)
    d_ref, gate_ref = (refs.pop(0), refs.pop(0)) if has_delta else (None, None)
    g_ref, sh_ref, sc_ref = refs.pop(0), refs.pop(0), refs.pop(0)
    xo_ref = refs.pop(0) if (has_delta and want_x) else None
    h_ref = refs.pop(0)
    x = x_ref[0]
    if has_delta:
        delta = _from_chunk_rows(d_ref.at[0], x.shape[0]) if delta_rows else d_ref[0]
        x = x + gate_ref[0] * delta
        if want_x:
            xo_ref[0] = x
    y = x * lax.rsqrt(jnp.mean(x * x, axis=-1, keepdims=True) + NORM_EPS) * g_ref[...]
    h = y * (1.0 + sc_ref[0]) + sh_ref[0]
    if out_rows:
        _to_chunk_rows(h_ref.at[0], h)
    else:
        h_ref[0] = h


def _from_chunk_rows(ref, n):
    chunks = ref.shape[0] // n
    return jnp.concatenate([ref[pl.ds(j, n, stride=chunks), :] for j in range(chunks)], axis=-1)


def _to_chunk_rows(ref, x):
    n = x.shape[0]
    chunks = x.shape[1] // LANES
    for j in range(chunks):
        ref[pl.ds(j, n, stride=chunks), :] = x[:, j * LANES:(j + 1) * LANES]


def _resid_norm(x, delta, gate, g, shift, scale, *, delta_rows=False, want_x=True, out_rows=False):
    B, N, D = x.shape
    ts = min(512, N)
    has_delta = delta is not None
    chunks = D // LANES
    tok = pl.BlockSpec((1, ts, D), lambda b, i: (b, i, 0))
    tok_rows = pl.BlockSpec((1, ts * chunks, LANES), lambda b, i: (b, i, 0))
    per_b = pl.BlockSpec((1, 1, D), lambda b, i: (b, 0, 0))
    in_specs, args = [tok], [x]
    if has_delta:
        in_specs += [tok_rows if delta_rows else tok, per_b]
        args += [delta, gate]
    in_specs += [pl.BlockSpec((1, D), lambda b, i: (0, 0)), per_b, per_b]
    args += [g.reshape(1, D), shift, scale]
    out_specs, out_shape = [], []
    if has_delta and want_x:
        out_specs.append(tok)
        out_shape.append(jax.ShapeDtypeStruct((B, N, D), F32))
    out_specs.append(tok_rows if out_rows else tok)
    out_shape.append(jax.ShapeDtypeStruct((B, N * chunks, LANES) if out_rows else (B, N, D), F32))
    outs = pl.pallas_call(
        functools.partial(_resid_norm_kernel, has_delta=has_delta, delta_rows=delta_rows, want_x=want_x,
                          out_rows=out_rows),
        grid=(B, N // ts),
        in_specs=in_specs,
        out_specs=out_specs,
        out_shape=out_shape,
        compiler_params=_params("parallel", "parallel"),
        name="resid_norm",
    )(*args)
    return outs if len(outs) > 1 else outs[0]


def _rope_kernel(x_ref, cos_ref, sin_ref, o_ref):
    width = x_ref.shape[-1]
    for c in range(width // LANES):
        sl = slice(c * LANES, (c + 1) * LANES)
        x = x_ref[0, :, sl]
        lane = lax.broadcasted_iota(jnp.int32, x.shape, 1)
        first_half = (lane % AXIS_DIM) < (AXIS_DIM // 2)
        partner = jnp.where(first_half, pltpu.roll(x, LANES - AXIS_DIM // 2, 1),
                            pltpu.roll(x, AXIS_DIM // 2, 1))
        o_ref[0, :, sl] = x * cos_ref[:, sl] + partner * sin_ref[:, sl]


def _rope_tables(S):
    rows = S // GRID_W
    row = jnp.repeat(jnp.arange(rows), GRID_W).astype(F32)
    col = jnp.tile(jnp.arange(GRID_W), rows).astype(F32)
    inv = ROPE_BASE ** (-jnp.arange(0, AXIS_DIM, 2, dtype=F32) / AXIS_DIM)
    ang_row, ang_col = row[:, None] * inv, col[:, None] * inv

    def axis_tables(ang):
        c, s = jnp.cos(ang), jnp.sin(ang)
        return jnp.concatenate([c, c], -1), jnp.concatenate([-s, s], -1)

    cr, sr = axis_tables(ang_row)
    cc, sc = axis_tables(ang_col)
    cos_h = jnp.concatenate([cr, cc], -1)
    sin_h = jnp.concatenate([sr, sc], -1)
    q_scale = HEAD_DIM ** -0.5
    cos = jnp.concatenate([jnp.tile(cos_h, (1, N_HEADS_ATTN)) * q_scale, jnp.tile(cos_h, (1, N_KV))], -1)
    sin = jnp.concatenate([jnp.tile(sin_h, (1, N_HEADS_ATTN)) * q_scale, jnp.tile(sin_h, (1, N_KV))], -1)
    return cos, sin


def _rope(proj, cos, sin):
    B, S, _ = proj.shape
    width = ATTN_W + KV_W
    ts = min(512, S)
    return pl.pallas_call(
        _rope_kernel,
        grid=(B, S // ts),
        in_specs=[pl.BlockSpec((1, ts, width), lambda b, i: (b, i, 0)),
                  pl.BlockSpec((ts, width), lambda b, i: (i, 0)),
                  pl.BlockSpec((ts, width), lambda b, i: (i, 0))],
        out_specs=pl.BlockSpec((1, ts, width), lambda b, i: (b, i, 0)),
        out_shape=jax.ShapeDtypeStruct((B, S, width), F32),
        compiler_params=_params("parallel", "parallel"),
        name="rope",
    )(proj, cos, sin)


def _attn_heads(q, kcat, vcat, valid, sink_ref):
    outs = []
    G = N_HEADS_ATTN // N_KV
    Q = q.shape[0]
    for j in range(N_KV):
        kj = kcat[:, j * HEAD_DIM:(j + 1) * HEAD_DIM]
        vj = vcat[:, j * HEAD_DIM:(j + 1) * HEAD_DIM]
        heads = range(j * G, (j + 1) * G)
        qs = jnp.concatenate([q[:, h * HEAD_DIM:(h + 1) * HEAD_DIM] for h in heads], axis=0).astype(BF16)
        s = lax.dot_general(qs, kj, (((1,), (1,)), ((), ())), preferred_element_type=F32)
        if valid is not None:
            s = jnp.where(valid[None], s.reshape(G, Q, -1), NEG_INF).reshape(G * Q, -1)
        sink = jnp.concatenate([jnp.full((Q, 1), sink_ref[h], F32) for h in heads], axis=0)
        m = jnp.maximum(jnp.max(s, axis=-1, keepdims=True), sink)
        e = jnp.exp(s - m)
        den = jnp.sum(e, axis=-1, keepdims=True) + jnp.exp(sink - m)
        o = jnp.dot(e.astype(BF16), vj, preferred_element_type=F32) / den
        outs.extend(o[g * Q:(g + 1) * Q] for g in range(G))
    return jnp.concatenate(outs, axis=-1)


def _attn_kernel(sink_ref, q_ref, kp_ref, kc_ref, kn_ref, vp_ref, vc_ref, vn_ref, ck_ref, cv_ref,
                 o_ref, *, S, Lc):
    i = pl.program_id(1)
    kcat = jnp.concatenate([ck_ref[0], kp_ref[0], kc_ref[0], kn_ref[0]], axis=0).astype(BF16)
    vcat = jnp.concatenate([cv_ref[0], vp_ref[0], vc_ref[0], vn_ref[0]], axis=0).astype(BF16)
    L = Lc + 3 * QBLOCK
    row = lax.broadcasted_iota(jnp.int32, (QBLOCK, L), 0)
    col = lax.broadcasted_iota(jnp.int32, (QBLOCK, L), 1)
    rel = col - Lc - QBLOCK - row
    kpos = (i - 1) * QBLOCK + col - Lc
    valid = (col < Lc) | ((jnp.abs(rel) <= WINDOW) & (kpos >= 0) & (kpos < S))
    o_ref[0] = _attn_heads(q_ref[0], kcat, vcat, valid, sink_ref)


def _attention(sink, qk, proj_l, proj_c):
    B, S, _ = qk.shape
    Lc = proj_c.shape[1]
    nb = S // QBLOCK
    kcol, vcol = ATTN_W // KV_W, ATTN_W // KV_W + 1

    def blk(colblk, off):
        return pl.BlockSpec((1, QBLOCK, KV_W),
                            lambda b, i: (b, jnp.clip(i + off, 0, nb - 1), colblk))

    return pl.pallas_call(
        functools.partial(_attn_kernel, S=S, Lc=Lc),
        grid=(B, nb),
        in_specs=[pl.BlockSpec(memory_space=pltpu.SMEM),
                  pl.BlockSpec((1, QBLOCK, ATTN_W), lambda b, i: (b, i, 0)),
                  blk(kcol, -1), blk(kcol, 0), blk(kcol, 1),
                  blk(vcol, -1), blk(vcol, 0), blk(vcol, 1),
                  pl.BlockSpec((1, Lc, KV_W), lambda b, i: (b, 0, kcol)),
                  pl.BlockSpec((1, Lc, KV_W), lambda b, i: (b, 0, vcol))],
        out_specs=pl.BlockSpec((1, QBLOCK, ATTN_W), lambda b, i: (b, i, 0)),
        out_shape=jax.ShapeDtypeStruct((B, S, ATTN_W), F32),
        compiler_params=_params("parallel", "parallel"),
        name="attention",
    )(sink, qk, qk, qk, qk, proj_l, proj_l, proj_l, proj_c, proj_c)


def _ctx_attn_kernel(sink_ref, q_ref, ck_ref, cv_ref, o_ref):
    q = q_ref[0] * (HEAD_DIM ** -0.5)
    o_ref[0] = _attn_heads(q, ck_ref[0].astype(BF16), cv_ref[0].astype(BF16), None, sink_ref)


def _ctx_attention(sink, proj_c):
    B, Lc, _ = proj_c.shape
    kcol, vcol = ATTN_W // KV_W, ATTN_W // KV_W + 1
    return pl.pallas_call(
        _ctx_attn_kernel,
        grid=(B,),
        in_specs=[pl.BlockSpec(memory_space=pltpu.SMEM),
                  pl.BlockSpec((1, Lc, ATTN_W), lambda b: (b, 0, 0)),
                  pl.BlockSpec((1, Lc, KV_W), lambda b: (b, 0, kcol)),
                  pl.BlockSpec((1, Lc, KV_W), lambda b: (b, 0, vcol))],
        out_specs=pl.BlockSpec((1, Lc, ATTN_W), lambda b: (b, 0, 0)),
        out_shape=jax.ShapeDtypeStruct((B, Lc, ATTN_W), F32),
        compiler_params=_params("parallel"),
        name="ctx_attention",
    )(sink, proj_c, proj_c, proj_c)


def _seg_rev_block(j, nC, nL):
    return jnp.where(j < nC, nC - 1 - j, nC + nL - 1 - (j - nC))


def _lru_kernel(af_ref, bf_ref, ar_ref, br_ref, hf_ref, hr_ref, sf_ref, sr_ref, *, Tc):
    @pl.when(pl.program_id(1) == 0)
    def _():
        sf_ref[...] = jnp.zeros_like(sf_ref)
        sr_ref[...] = jnp.zeros_like(sr_ref)

    def step(s, carry):
        hf, hr = carry
        hf = af_ref[0, pl.ds(s, 1), :] * hf + bf_ref[0, pl.ds(s, 1), :]
        hf_ref[0, pl.ds(s, 1), :] = hf
        t = Tc - 1 - s
        hr = ar_ref[0, pl.ds(t, 1), :] * hr + br_ref[0, pl.ds(t, 1), :]
        hr_ref[0, pl.ds(t, 1), :] = hr
        return hf, hr

    hf, hr = lax.fori_loop(0, Tc, step, (sf_ref[...], sr_ref[...]), unroll=8)
    sf_ref[...] = hf
    sr_ref[...] = hr


def _lru_scan(a, b, Lc):
    B, T, _ = a.shape
    S = T - Lc
    Tc = math.gcd(math.gcd(Lc, S), 256)
    nC, nL = Lc // Tc, S // Tc

    def fwd(col):
        return pl.BlockSpec((1, Tc, LRU_W), lambda bi, j: (bi, j, col))

    def rev(col):
        return pl.BlockSpec((1, Tc, LRU_W), lambda bi, j: (bi, _seg_rev_block(j, nC, nL), col))

    return pl.pallas_call(
        functools.partial(_lru_kernel, Tc=Tc),
        grid=(B, nC + nL),
        in_specs=[fwd(0), fwd(0), rev(1), rev(1)],
        out_specs=[fwd(0), rev(0)],
        out_shape=[jax.ShapeDtypeStruct((B, T, LRU_W), F32)] * 2,
        scratch_shapes=[pltpu.VMEM((1, LRU_W), F32)] * 2,
        compiler_params=_params("parallel", "arbitrary"),
        name="lru_scan",
    )(a, b, a, b)


def _softplus(z):
    return jnp.maximum(z, 0.0) + jnp.log(1.0 + jnp.exp(-jnp.abs(z)))


def _nt_dot(wt, x):
    return lax.dot_general(wt, x, (((1,), (1,)), ((), ())), preferred_element_type=F32)


def _rwkv_prep_kernel(hp_ref, h_ref, hn_ref, mu_ref, wr_ref, wk_ref, wv_ref, w1_ref, w2_ref, a1_ref,
                      a2_ref, g1_ref, g2_ref, w0_ref, a0_ref, ka_ref, rk_ref,
                      dec_ref, aa_ref, k_ref, v_ref, r_ref, g_ref, bc_ref, *, B, ctx_tiles, tiles):
    i = pl.program_id(0)
    D = h_ref.shape[1]
    tm = h_ref.shape[0]
    h = h_ref[...]
    seq_start = (i == 0) | (i == ctx_tiles)
    seq_end = (i == ctx_tiles - 1) | (i == tiles - 1)
    hp = jnp.where(seq_start, 0.0, hp_ref[...])
    hn = jnp.where(seq_end, 0.0, hn_ref[...])
    xx = 0.5 * (jnp.concatenate([hp, h[:tm - B]], axis=0) + jnp.concatenate([h[B:], hn], axis=0)) - h

    def mix(j):
        return (h + xx * mu_ref[j:j + 1, :]).astype(BF16)

    r = _nt_dot(wr_ref[...], mix(0))
    k = _nt_dot(wk_ref[...], mix(2))
    v = _nt_dot(wv_ref[...], mix(3))
    lw = jnp.tanh(_nt_dot(w1_ref[...], mix(1))).astype(BF16)
    wpre = jnp.dot(w2_ref[...], lw, preferred_element_type=F32)
    la = _nt_dot(a1_ref[...], mix(4)).astype(BF16)
    apre = jnp.dot(a2_ref[...], la, preferred_element_type=F32)
    gg = jax.nn.sigmoid(_nt_dot(g1_ref[...], mix(5))).astype(BF16)
    g_ref[...] = jnp.dot(g2_ref[...], gg, preferred_element_type=F32)
    hpg = LANES // B
    hd = RWKV_HD

    def to_scan_layout(x, ref):
        for g in range(RWKV_H // hpg):
            for c in range(tm // LANES):
                tiles = [x[(g * hpg + hh) * hd:(g * hpg + hh + 1) * hd, c * LANES:(c + 1) * LANES]
                         for hh in range(hpg)]
                for tl, tile in enumerate(_slab_transpose(tiles, B)):
                    ref[g, c * hpg + tl] = tile

    iclr = []
    for d in range(2):
        w_log = -_softplus(-(w0_ref[d] + wpre[d * D:(d + 1) * D])) - 0.5
        to_scan_layout(jnp.exp(-jnp.exp(w_log)), dec_ref.at[d])
        a = jax.nn.sigmoid(a0_ref[d] + apre[d * D:(d + 1) * D])
        iclr.append(a)
        to_scan_layout(a, aa_ref.at[d])
    kd_sum = k * (2.0 + (iclr[0] + iclr[1] - 2.0) * ka_ref[...])
    bc = jnp.sum((r * kd_sum * rk_ref[...]).reshape(RWKV_H, hd, tm), axis=1)
    for g in range(RWKV_H // hpg):
        for c in range(tm // LANES):
            rows = [jnp.broadcast_to(bc[g * hpg + hh:g * hpg + hh + 1, c * LANES:(c + 1) * LANES], (8, LANES))
                    for hh in range(hpg)]
            for tl, tile in enumerate(_slab_transpose(rows, B)):
                bc_ref[g, c * hpg + tl] = tile[0:1]
    to_scan_layout(k, k_ref)
    to_scan_layout(v, v_ref)
    to_scan_layout(r, r_ref)


def _slab_transpose(tiles, B):
    n = len(tiles)
    tiles = list(tiles)
    slab = lax.broadcasted_iota(jnp.int32, tiles[0].shape, 1) // B
    s = n // 2
    while s >= 1:
        upper = (slab & s) != 0
        for i in range(n):
            if i & s == 0:
                lo, hi = tiles[i], tiles[i + s]
                tiles[i] = jnp.where(upper, pltpu.roll(hi, s * B, 1), lo)
                tiles[i + s] = jnp.where(upper, hi, pltpu.roll(lo, LANES - s * B, 1))
        s //= 2
    return tiles


def _const_spec(shape):
    nd = len(shape)
    return pl.BlockSpec(shape, lambda i: (0,) * nd, pipeline_mode=pl.Buffered(1))


def _rwkv_prep(h_tm, B, Lc, mu, w_rkv, w0, w1, w2, a0, a1, a2, g1, g2, k_a, r_k):
    TB, D = h_tm.shape
    tm = 256
    tiles, ctx_tiles = TB // tm, Lc * B // tm
    hb = tm // B
    T = TB // B
    last_hb = T - 1
    G = RWKV_H // (LANES // B)
    hd = RWKV_HD

    def t_bf16(w):
        return w.T.astype(BF16)

    zeros_w = jnp.zeros_like(w2[0].T)
    w2t = jnp.concatenate([jnp.concatenate([w2[0].T, zeros_w], 1),
                           jnp.concatenate([zeros_w, w2[1].T], 1)], 0).astype(BF16)
    zeros_a = jnp.zeros_like(a2[0].T)
    a2t = jnp.concatenate([jnp.concatenate([a2[0].T, zeros_a], 1),
                           jnp.concatenate([zeros_a, a2[1].T], 1)], 0).astype(BF16)
    consts = [mu, t_bf16(w_rkv[0]), t_bf16(w_rkv[1]), t_bf16(w_rkv[2]),
              t_bf16(jnp.concatenate([w1[0], w1[1]], 1)), w2t,
              t_bf16(jnp.concatenate([a1[0], a1[1]], 1)), a2t, t_bf16(g1), t_bf16(g2),
              w0.reshape(2, D, 1), a0.reshape(2, D, 1), k_a.reshape(D, 1), r_k.reshape(D, 1)]
    dir_spec = pl.BlockSpec((2, G, hb, hd, LANES), lambda i: (0, 0, i, 0, 0))
    all_spec = pl.BlockSpec((G, hb, hd, LANES), lambda i: (0, i, 0, 0))
    dir_shape = jax.ShapeDtypeStruct((2, G, T, hd, LANES), F32)
    all_shape = jax.ShapeDtypeStruct((G, T, hd, LANES), F32)
    return pl.pallas_call(
        functools.partial(_rwkv_prep_kernel, B=B, ctx_tiles=ctx_tiles, tiles=tiles),
        grid=(tiles,),
        in_specs=[pl.BlockSpec((B, D), lambda i: (jnp.maximum(i * hb - 1, 0), 0)),
                  pl.BlockSpec((tm, D), lambda i: (i, 0)),
                  pl.BlockSpec((B, D), lambda i: (jnp.minimum((i + 1) * hb, last_hb), 0))]
                 + [_const_spec(c.shape) for c in consts],
        out_specs=[dir_spec, dir_spec, all_spec, all_spec, all_spec,
                   pl.BlockSpec((D, tm), lambda i: (0, i)),
                   pl.BlockSpec((G, hb, 1, LANES), lambda i: (0, i, 0, 0))],
        out_shape=[dir_shape, dir_shape, all_shape, all_shape, all_shape,
                   jax.ShapeDtypeStruct((D, TB), F32),
                   jax.ShapeDtypeStruct((G, T, 1, LANES), F32)],
        compiler_params=_params("parallel"),
        name="rwkv_prep",
    )(h_tm, h_tm, h_tm, *consts)


def _wkv_kernel(dec_ref, aa_ref, k_ref, v_ref, r_ref, kkc_ref, kac_ref, y_ref,
                s_ref, g_ref, st_ref, p_ref, sa_ref, *, Tc, nC):
    d = pl.program_id(0)
    j = pl.program_id(2)
    hd = RWKV_HD
    sub = 8

    @pl.when(j == 0)
    def _():
        s_ref[...] = jnp.zeros_like(s_ref)

    def partial_rows(x):
        return jnp.sum(x.reshape(hd // sub, sub, LANES), axis=0)

    pitch = hd + 1

    def put_partial(v, x):
        p_ref[pl.ds(v, sub, stride=pitch), :] = partial_rows(x)

    def finish_rows():
        acc = p_ref[pl.ds(0, hd), :]
        for q in range(1, sub):
            acc = acc + p_ref[pl.ds(q * pitch, hd), :]
        return acc

    def run(with_y):
        g_ref[...] = jnp.ones_like(g_ref)

        def step(s, _):
            tt = jnp.where(d == 0, s, Tc - 1 - s)
            aa = aa_ref[0, 0, tt]
            kk = k_ref[0, tt]
            kf = kk * kkc_ref[0]
            kn = kf * lax.rsqrt(jnp.maximum(jnp.sum(kf * kf, axis=0, keepdims=True), 1e-24))
            g_prev = g_ref[...]
            g = g_prev * dec_ref[0, 0, tt]
            g_inv = 1.0 / g
            g_ref[...] = g
            st_ref[0] = -(kn * g_prev)
            st_ref[1] = kn * aa * g_inv
            st_ref[2] = kk * (1.0 + (aa - 1.0) * kac_ref[0]) * g_inv
            if with_y:
                st_ref[3] = r_ref[0, tt] * g

            def row_sa(v, _):
                put_partial(v, s_ref[v] * st_ref[0])
                return 0

            lax.fori_loop(0, hd, row_sa, 0, unroll=8)
            sa_ref[...] = finish_rows()

            def row_update(v, _):
                st = s_ref[v] + sa_ref[pl.ds(v, 1), :] * st_ref[1] + v_ref[0, tt, pl.ds(v, 1), :] * st_ref[2]
                s_ref[v] = st
                if with_y:
                    put_partial(v, st * st_ref[3])
                return 0

            lax.fori_loop(0, hd, row_update, 0, unroll=8)
            if with_y:
                y_ref[0, 0, tt] = finish_rows()
            return 0

        lax.fori_loop(0, Tc, step, 0)

        def rescale(v, _):
            s_ref[v] = s_ref[v] * g_ref[...]
            return 0

        lax.fori_loop(0, hd, rescale, 0, unroll=8)

    @pl.when(j < nC)
    def _():
        run(False)

    @pl.when(j >= nC)
    def _():
        run(True)


def _wkv(dec, aa, k, v, r, k_k, k_a, B, Lc):
    _, G, T, hd, _ = dec.shape
    Tc = WKV_TC
    S = T - Lc
    nC, nL = Lc // Tc, S // Tc

    def tmap(d, j):
        return jnp.where(d == 0, j, _seg_rev_block(j, nC, nL))

    def lmap(d, j):
        jj = jnp.maximum(j - nC, 0)
        return jnp.where(d == 0, jj, nL - 1 - jj)

    dir_spec = pl.BlockSpec((1, 1, Tc, hd, LANES), lambda d, g, j: (d, g, tmap(d, j), 0, 0))
    all_spec = pl.BlockSpec((1, Tc, hd, LANES), lambda d, g, j: (g, tmap(d, j), 0, 0))
    const_spec = pl.BlockSpec((1, hd, LANES), lambda d, g, j: (g, 0, 0))
    return pl.pallas_call(
        functools.partial(_wkv_kernel, Tc=Tc, nC=nC),
        grid=(2, G, nC + nL),
        in_specs=[dir_spec, dir_spec, all_spec, all_spec, all_spec, const_spec, const_spec],
        out_specs=pl.BlockSpec((1, 1, Tc, hd, LANES), lambda d, g, j: (d, g, lmap(d, j), 0, 0)),
        out_shape=jax.ShapeDtypeStruct((2, G, S, hd, LANES), F32),
        scratch_shapes=[pltpu.VMEM((hd, hd, LANES), F32),
                        pltpu.VMEM((hd, LANES), F32),
                        pltpu.VMEM((4, hd, LANES), F32),
                        pltpu.VMEM(((hd + 1) * 8, LANES), F32),
                        pltpu.VMEM((hd, LANES), F32)],
        compiler_params=_params("parallel", "parallel", "arbitrary"),
        name="wkv",
    )(dec, aa, k, v, r, _lane_const(k_k, B), _lane_const(k_a, B))


def _lane_const(c, B):
    hpg = LANES // B
    return jnp.repeat(c.reshape(RWKV_H // hpg, hpg, RWKV_HD).transpose(0, 2, 1), B, axis=-1)


def _rwkv_finish_kernel(y_ref, v_ref, bc_ref, g_ref, gng_ref, gnb_ref, wo_ref, o_ref, *, B):
    G, steps, hd, _ = v_ref.shape
    hpg = LANES // B
    rows = []
    for g in range(G):
        head_cols = [[] for _ in range(hpg)]
        for c in range(steps // hpg):
            tiles = []
            for tl in range(hpg):
                t = c * hpg + tl
                y = y_ref[0, g, t] + y_ref[1, g, t]
                mean = jnp.mean(y, axis=0, keepdims=True)
                var = jnp.mean(jnp.square(y - mean), axis=0, keepdims=True)
                yn = (y - mean) * lax.rsqrt(var + GN_EPS)
                tiles.append(yn * gng_ref[g] + gnb_ref[g] + bc_ref[g, t] * v_ref[g, t])
            for hh, tile in enumerate(_slab_transpose(tiles, B)):
                head_cols[hh].append(tile)
        rows.extend(jnp.concatenate(cols, axis=1) for cols in head_cols)
    o = (jnp.concatenate(rows, axis=0) * g_ref[...]).astype(BF16)
    out_t = jnp.dot(wo_ref[...], o, preferred_element_type=F32)
    o_ref[...] = out_t.T


def _rwkv_finish(y2, v, bc, g, gn_g, gn_b, w_o, B, Lc):
    _, G, S, hd, _ = y2.shape
    D = g.shape[0]
    tm = 256
    steps = tm // B
    off = Lc // steps
    return pl.pallas_call(
        functools.partial(_rwkv_finish_kernel, B=B),
        grid=(S // steps,),
        in_specs=[pl.BlockSpec((2, G, steps, hd, LANES), lambda i: (0, 0, i, 0, 0)),
                  pl.BlockSpec((G, steps, hd, LANES), lambda i: (0, i + off, 0, 0)),
                  pl.BlockSpec((G, steps, 1, LANES), lambda i: (0, i + off, 0, 0)),
                  pl.BlockSpec((D, tm), lambda i: (0, i + off)),
                  _const_spec((G, hd, LANES)), _const_spec((G, hd, LANES)), _const_spec((D, D))],
        out_specs=pl.BlockSpec((tm, D), lambda i: (i, 0)),
        out_shape=jax.ShapeDtypeStruct((S * B, D), F32),
        compiler_params=_params("parallel"),
        name="rwkv_finish",
    )(y2, v, bc, g, _lane_const(gn_g, B), _lane_const(gn_b, B), w_o.T.astype(BF16))


def _rwkv7_mixer(hl, hc, mu, w_rkv, w0, w1, w2, a0, a1, a2, g1, g2, k_k, k_a, r_k, gn_g, gn_b, w_o):
    B, S, D = hl.shape
    Lc = hc.shape[1]
    h_tm = jnp.concatenate([hc.transpose(1, 0, 2), hl.transpose(1, 0, 2)], axis=0).reshape((Lc + S) * B, D)
    dec, aa, k, v, r, g, bc = _rwkv_prep(h_tm, B, Lc, mu, w_rkv, w0, w1, w2, a0, a1, a2, g1, g2,
                                         k_a, r_k.reshape(D))
    y2 = _wkv(dec, aa, k, v, r, k_k, k_a, B, Lc)
    y_tm = _rwkv_finish(y2, v, bc, g, gn_g, gn_b, w_o, B, Lc)
    return y_tm.reshape(S, B, D).transpose(1, 0, 2)


def _router_kernel(h_ref, w_ref, o_ref):
    h = _from_chunk_rows(h_ref.at[0], o_ref.shape[2])
    logits = lax.dot_general(w_ref[...], h, (((1,), (1,)), ((), ())),
                             precision=lax.Precision.HIGHEST, preferred_element_type=F32)
    m = jnp.max(logits, axis=0, keepdims=True)
    e = jnp.exp(logits - m)
    o_ref[0] = e / jnp.sum(e, axis=0, keepdims=True)


def _router(h, N, w_router):
    B = h.shape[0]
    D, E = w_router.shape
    chunks = D // LANES
    tn = min(512, N)
    return pl.pallas_call(
        _router_kernel,
        grid=(B, N // tn),
        in_specs=[pl.BlockSpec((1, tn * chunks, LANES), lambda b, i: (b, i, 0)),
                  pl.BlockSpec((E, D), lambda b, i: (0, 0))],
        out_specs=pl.BlockSpec((1, E, tn), lambda b, i: (b, 0, i)),
        out_shape=jax.ShapeDtypeStruct((B, E, N), F32),
        compiler_params=_params("parallel", "parallel"),
        name="router",
    )(h, w_router.T)


def _moe_kernel(idx_ref, gate_ref, h_ref, w1_ref, w3_ref, w2_ref, o_ref, xin_ref, y_ref, *, cap, chunks):
    bb = h_ref.shape[0]

    @pl.when(pl.program_id(1) == 0)
    def _():
        o_ref[...] = jnp.zeros_like(o_ref)

    def token_rows(t):
        return pl.ds(pl.multiple_of(t * chunks, chunks), chunks)

    for bi in range(bb):
        def gather(r, _, bi=bi):
            slot = bi * cap + r
            xin_ref[token_rows(slot), :] = h_ref[bi, token_rows(idx_ref[0, 0, slot]), :]
            return 0

        lax.fori_loop(0, cap, gather, 0, unroll=8)
    x = _from_chunk_rows(xin_ref, bb * cap).astype(BF16)
    h1 = jnp.dot(x, w1_ref[0], preferred_element_type=F32)
    h3 = jnp.dot(x, w3_ref[0], preferred_element_type=F32)
    hid = (h1 * jax.nn.sigmoid(h1) * h3).astype(BF16)
    _to_chunk_rows(y_ref, jnp.dot(hid, w2_ref[0], preferred_element_type=F32) * gate_ref[0])

    batch = 8
    for bi in range(bb):
        def scatter_add(i, _, bi=bi):
            slots = [bi * cap + i * batch + u for u in range(batch)]
            rows = [token_rows(idx_ref[0, 0, s]) for s in slots]
            sums = [o_ref[bi, rows[u], :] + y_ref[token_rows(slots[u]), :] for u in range(batch)]
            for u in range(batch):
                o_ref[bi, rows[u], :] = sums[u]
            return 0

        lax.fori_loop(0, cap // batch, scatter_add, 0)


def _moe(h, N, w_router, w1, w3, w2):
    B = h.shape[0]
    E, D, FF = w1.shape
    chunks = D // LANES
    cap = CAPACITY * N // E
    bb = max(1, min(B, MOE_ROWS // cap))
    aff = _router(h, N, w_router)
    gate, idx = lax.top_k(aff, cap)

    def group(t):
        return t.reshape(B // bb, bb, E, cap).transpose(0, 2, 1, 3).reshape(B // bb * E, bb * cap)

    rows = bb * cap
    return pl.pallas_call(
        functools.partial(_moe_kernel, cap=cap, chunks=chunks),
        grid=(B // bb, E),
        in_specs=[pl.BlockSpec((1, 1, rows), lambda b, e: (b * E + e, 0, 0), memory_space=pltpu.SMEM),
                  pl.BlockSpec((1, rows, 1), lambda b, e: (b * E + e, 0, 0)),
                  pl.BlockSpec((bb, N * chunks, LANES), lambda b, e: (b, 0, 0), pipeline_mode=pl.Buffered(1)),
                  pl.BlockSpec((1, D, FF), lambda b, e: (e, 0, 0)),
                  pl.BlockSpec((1, D, FF), lambda b, e: (e, 0, 0)),
                  pl.BlockSpec((1, FF, D), lambda b, e: (e, 0, 0))],
        out_specs=pl.BlockSpec((bb, N * chunks, LANES), lambda b, e: (b, 0, 0)),
        out_shape=jax.ShapeDtypeStruct((B, N * chunks, LANES), F32),
        scratch_shapes=[pltpu.VMEM((rows * chunks, LANES), F32), pltpu.VMEM((rows * chunks, LANES), F32)],
        compiler_params=_params("parallel", "arbitrary"),
        name="moe",
    )(group(idx.astype(jnp.int32))[:, None, :], group(gate)[:, :, None], h, w1, w3, w2)


def _block_diag(w):
    H, bi, bj = w.shape
    eye = jnp.eye(H, dtype=w.dtype)
    return jnp.einsum('hij,hg->higj', w, eye).reshape(H * bi, H * bj)


def _dwconv(u, w, b):
    n = u.shape[1]
    up = jnp.pad(u, ((0, 0), (CONV_LEFT, CONV_W - 1 - CONV_LEFT), (0, 0)))
    out = up[:, 0:n] * w[0]
    for j in range(1, CONV_W):
        out = out + up[:, j:j + n] * w[j]
    return out + b


def _attn_lru_mixer(hl, hc, cos, sin, w_in, w_out, sink, conv_w, conv_b, wa, ba, wi, bi, lam):
    B, S, D = hl.shape
    Lc = hc.shape[1]
    mix_in = w_in.shape[1]
    w_in_b = w_in.astype(BF16)
    proj_l = _mm(hl.reshape(B * S, D), w_in_b).reshape(B, S, mix_in)
    proj_c = _mm(hc.reshape(B * Lc, D), w_in_b).reshape(B, Lc, mix_in)
    qk = _rope(proj_l, cos, sin)
    attn_l = _attention(sink, qk, proj_l, proj_c)
    attn_c = _ctx_attention(sink, proj_c)

    u0, g0 = ATTN_W + 2 * KV_W, ATTN_W + 2 * KV_W + LRU_W
    u = jnp.concatenate([_dwconv(proj_c[..., u0:g0], conv_w, conv_b),
                         _dwconv(proj_l[..., u0:g0], conv_w, conv_b)], axis=1)
    T = Lc + S
    w_gates = jnp.concatenate([_block_diag(wa[0]), _block_diag(wi[0]),
                               _block_diag(wa[1]), _block_diag(wi[1])], axis=1).astype(BF16)
    b_gates = jnp.concatenate([ba[0], bi[0], ba[1], bi[1]])
    gates = jax.nn.sigmoid(_mm(u.reshape(B * T, LRU_W), w_gates).reshape(B, T, 4 * LRU_W) + b_gates)
    a_dirs, b_dirs = [], []
    for d in range(2):
        r = gates[..., (2 * d) * LRU_W:(2 * d + 1) * LRU_W]
        i = gates[..., (2 * d + 1) * LRU_W:(2 * d + 2) * LRU_W]
        log_a = -LRU_C * r * jax.nn.softplus(-lam[d])
        a_dirs.append(jnp.exp(log_a))
        b_dirs.append(jnp.sqrt(jnp.maximum(-jnp.expm1(2 * log_a), 0.0)) * (i * u))
    hf, hr = _lru_scan(jnp.concatenate(a_dirs, -1), jnp.concatenate(b_dirs, -1), Lc)
    rec = hf + hr
    zl = jnp.concatenate([attn_l, rec[:, Lc:] * jax.nn.gelu(proj_l[..., g0:])], axis=-1)
    zc = jnp.concatenate([attn_c, rec[:, :Lc] * jax.nn.gelu(proj_c[..., g0:])], axis=-1)
    w_out_b = w_out.astype(BF16)
    yl = _mm(zl.reshape(B * S, -1), w_out_b).reshape(B, S, D)
    yc = _mm(zc.reshape(B * Lc, -1), w_out_b).reshape(B, Lc, D)
    return yl, yc


def kernel(x, c, ctx, c_ctx, mod_w, mod_b, norm_mix, norm_ffn, router_w, exp_w1, exp_w3, exp_w2, mix_in, mix_out, attn_sink, lru_conv_w, lru_conv_b, lru_wa, lru_ba, lru_wi, lru_bi, lru_lam, rw_mu, rw_rkv, rw_w0, rw_w1, rw_w2, rw_a0, rw_a1, rw_a2, rw_g1, rw_g2, rw_kk, rw_ka, rw_rk, rw_gn_g, rw_gn_b, rw_wo, final_norm):
    B, S, D = x.shape
    Lc = ctx.shape[1]
    depth = mod_w.shape[0]
    assert depth == 2 and S % QBLOCK == 0
    assert LANES % B == 0 and RWKV_H % (LANES // B) == 0 and B % 8 == 0
    assert Lc % WKV_TC == 0 and S % WKV_TC == 0 and (Lc * B) % 256 == 0
    cos, sin = _rope_tables(S)

    n_rows = -(-(B + 1) // 8) * 8
    cond = jnp.concatenate([jax.nn.silu(c), jax.nn.silu(c_ctx)[None],
                            jnp.zeros((n_rows - B - 1, D), F32)], axis=0)

    mods_l, mods_c = [], []
    for layer in range(depth):
        mod = _mm(cond, mod_w[layer], tm=n_rows, tn=1024, precise=True) + mod_b[layer]
        mods_l.append([t[:, None, :] for t in jnp.split(mod[:B], 6, axis=-1)])
        mods_c.append([jnp.broadcast_to(t[None], (B, 1, D)) for t in jnp.split(mod[B:B + 1], 6, axis=-1)])

    xl, xc = x, ctx
    hl = _resid_norm(xl, None, None, norm_mix[0], mods_l[0][0], mods_l[0][1])
    hc = _resid_norm(xc, None, None, norm_mix[0], mods_c[0][0], mods_c[0][1])
    for layer in range(depth):
        last = layer == depth - 1
        m_l, m_c = mods_l[layer], mods_c[layer]
        if layer % 2 == 0:
            i = layer // 2
            yl, yc = _attn_lru_mixer(hl, hc, cos, sin, mix_in[i], mix_out[i], attn_sink[i], lru_conv_w[i],
                                     lru_conv_b[i], lru_wa[i], lru_ba[i], lru_wi[i], lru_bi[i], lru_lam[i])
        else:
            i = layer // 2
            yl = _rwkv7_mixer(hl, hc, rw_mu[i], rw_rkv[i], rw_w0[i], rw_w1[i], rw_w2[i], rw_a0[i], rw_a1[i],
                              rw_a2[i], rw_g1[i], rw_g2[i], rw_kk[i], rw_ka[i], rw_rk[i], rw_gn_g[i],
                              rw_gn_b[i], rw_wo[i])
            yc = None
        w1b, w3b, w2b = exp_w1[layer].astype(BF16), exp_w3[layer].astype(BF16), exp_w2[layer].astype(BF16)
        xl, hl = _resid_norm(xl, yl, m_l[2], norm_ffn[layer], m_l[3], m_l[4], out_rows=True)
        moe_l = _moe(hl, S, router_w[layer], w1b, w3b, w2b)
        if last:
            zero = jnp.zeros((B, 1, D), F32)
            return _resid_norm(xl, moe_l, m_l[5], final_norm, zero, zero, delta_rows=True, want_x=False)
        n_l, n_c = mods_l[layer + 1], mods_c[layer + 1]
        xl, hl = _resid_norm(xl, moe_l, m_l[5], norm_mix[layer + 1], n_l[0], n_l[1], delta_rows=True)
        xc, hc = _resid_norm(xc, yc, m_c[2], norm_ffn[layer], m_c[3], m_c[4], out_rows=True)
        moe_c = _moe(hc, Lc, router_w[layer], w1b, w3b, w2b)
        xc, hc = _resid_norm(xc, moe_c, m_c[5], norm_mix[layer + 1], n_c[0], n_c[1], delta_rows=True)
```

```python
import functools
import math

import jax
import jax.numpy as jnp
from jax import lax
from jax.experimental import pallas as pl
from jax.experimental.pallas import tpu as pltpu

F32 = jnp.float32
BF16 = jnp.bfloat16

GRID_W = 64
N_HEADS_ATTN = 8
N_KV = 2
HEAD_DIM = 64
AXIS_DIM = HEAD_DIM // 2
ATTN_W = N_HEADS_ATTN * HEAD_DIM
KV_W = N_KV * HEAD_DIM
WINDOW = 128
QBLOCK = 128
ROPE_BASE = 10000.0
LRU_W = 512
LRU_BLOCKS = 8
CONV_W = 4
CONV_LEFT = 2
LRU_C = 8.0
RWKV_H = 16
RWKV_HD = 64
GN_EPS = 64e-5
N_EXPERTS = 16
CAPACITY = 2
NORM_EPS = 1e-6
NEG_INF = -1e30

LANES = 128
VMEM_LIMIT_BYTES = 56 * 1024 * 1024
WKV_TC = 16
MOE_ROWS = 256


def _params(*sem):
    return pltpu.CompilerParams(dimension_semantics=sem, vmem_limit_bytes=VMEM_LIMIT_BYTES)


def _mm_kernel(x_ref, w_ref, o_ref, *, precise):
    if precise:
        o_ref[...] = jnp.dot(x_ref[...], w_ref[...], precision=lax.Precision.HIGHEST,
                             preferred_element_type=F32)
    else:
        o_ref[...] = jnp.dot(x_ref[...].astype(BF16), w_ref[...].astype(BF16),
                             preferred_element_type=F32)


def _mm(x, w, *, tm=512, tn=None, precise=False):
    M, K = x.shape
    N = w.shape[1]
    tm = min(tm, M)
    tn = N if tn is None else min(tn, N)
    assert M % tm == 0 and N % tn == 0, (M, tm, N, tn)
    return pl.pallas_call(
        functools.partial(_mm_kernel, precise=precise),
        grid=(M // tm, N // tn),
        in_specs=[pl.BlockSpec((tm, K), lambda i, j: (i, 0)),
                  pl.BlockSpec((K, tn), lambda i, j: (0, j))],
        out_specs=pl.BlockSpec((tm, tn), lambda i, j: (i, j)),
        out_shape=jax.ShapeDtypeStruct((M, N), F32),
        compiler_params=_params("parallel", "parallel"),
        name="mm",
    )(x, w)


def _resid_norm_kernel(*refs, has_delta, delta_rows, want_x, out_rows):
    refs = list(refs)
    x_ref = refs.pop(0)
    d_ref, gate_ref = (refs.pop(0), refs.pop(0)) if has_delta else (None, None)
    g_ref, sh_ref, sc_ref = refs.pop(0), refs.pop(0), refs.pop(0)
    xo_ref = refs.pop(0) if (has_delta and want_x) else None
    h_ref = refs.pop(0)
    x = x_ref[0]
    if has_delta:
        delta = _from_chunk_rows(d_ref.at[0], x.shape[0]) if delta_rows else d_ref[0]
        x = x + gate_ref[0] * delta
        if want_x:
            xo_ref[0] = x
    y = x * lax.rsqrt(jnp.mean(x * x, axis=-1, keepdims=True) + NORM_EPS) * g_ref[...]
    h = y * (1.0 + sc_ref[0]) + sh_ref[0]
    if out_rows:
        _to_chunk_rows(h_ref.at[0], h)
    else:
        h_ref[0] = h


def _from_chunk_rows(ref, n):
    chunks = ref.shape[0] // n
    return jnp.concatenate([ref[pl.ds(j, n, stride=chunks), :] for j in range(chunks)], axis=-1)


def _to_chunk_rows(ref, x):
    n = x.shape[0]
    chunks = x.shape[1] // LANES
    for j in range(chunks):
        ref[pl.ds(j, n, stride=chunks), :] = x[:, j * LANES:(j + 1) * LANES]


def _resid_norm(x, delta, gate, g, shift, scale, *, delta_rows=False, want_x=True, out_rows=False):
    B, N, D = x.shape
    ts = min(512, N)
    has_delta = delta is not None
    chunks = D // LANES
    tok = pl.BlockSpec((1, ts, D), lambda b, i: (b, i, 0))
    tok_rows = pl.BlockSpec((1, ts * chunks, LANES), lambda b, i: (b, i, 0))
    per_b = pl.BlockSpec((1, 1, D), lambda b, i: (b, 0, 0))
    in_specs, args = [tok], [x]
    if has_delta:
        in_specs += [tok_rows if delta_rows else tok, per_b]
        args += [delta, gate]
    in_specs += [pl.BlockSpec((1, D), lambda b, i: (0, 0)), per_b, per_b]
    args += [g.reshape(1, D), shift, scale]
    out_specs, out_shape = [], []
    if has_delta and want_x:
        out_specs.append(tok)
        out_shape.append(jax.ShapeDtypeStruct((B, N, D), F32))
    out_specs.append(tok_rows if out_rows else tok)
    out_shape.append(jax.ShapeDtypeStruct((B, N * chunks, LANES) if out_rows else (B, N, D), F32))
    outs = pl.pallas_call(
        functools.partial(_resid_norm_kernel, has_delta=has_delta, delta_rows=delta_rows, want_x=want_x,
                          out_rows=out_rows),
        grid=(B, N // ts),
        in_specs=in_specs,
        out_specs=out_specs,
        out_shape=out_shape,
        compiler_params=_params("parallel", "parallel"),
        name="resid_norm",
    )(*args)
    return outs if len(outs) > 1 else outs[0]


def _rope_kernel(x_ref, cos_ref, sin_ref, o_ref):
    width = x_ref.shape[-1]
    for c in range(width // LANES):
        sl = slice(c * LANES, (c + 1) * LANES)
        x = x_ref[0, :, sl]
        lane = lax.broadcasted_iota(jnp.int32, x.shape, 1)
        first_half = (lane % AXIS_DIM) < (AXIS_DIM // 2)
        partner = jnp.where(first_half, pltpu.roll(x, LANES - AXIS_DIM // 2, 1),
                            pltpu.roll(x, AXIS_DIM // 2, 1))
        o_ref[0, :, sl] = x * cos_ref[:, sl] + partner * sin_ref[:, sl]


def _rope_tables(S):
    rows = S // GRID_W
    row = jnp.repeat(jnp.arange(rows), GRID_W).astype(F32)
    col = jnp.tile(jnp.arange(GRID_W), rows).astype(F32)
    inv = ROPE_BASE ** (-jnp.arange(0, AXIS_DIM, 2, dtype=F32) / AXIS_DIM)
    ang_row, ang_col = row[:, None] * inv, col[:, None] * inv

    def axis_tables(ang):
        c, s = jnp.cos(ang), jnp.sin(ang)
        return jnp.concatenate([c, c], -1), jnp.concatenate([-s, s], -1)

    cr, sr = axis_tables(ang_row)
    cc, sc = axis_tables(ang_col)
    cos_h = jnp.concatenate([cr, cc], -1)
    sin_h = jnp.concatenate([sr, sc], -1)
    q_scale = HEAD_DIM ** -0.5
    cos = jnp.concatenate([jnp.tile(cos_h, (1, N_HEADS_ATTN)) * q_scale, jnp.tile(cos_h, (1, N_KV))], -1)
    sin = jnp.concatenate([jnp.tile(sin_h, (1, N_HEADS_ATTN)) * q_scale, jnp.tile(sin_h, (1, N_KV))], -1)
    return cos, sin


def _rope(proj, cos, sin):
    B, S, _ = proj.shape
    width = ATTN_W + KV_W
    ts = min(512, S)
    return pl.pallas_call(
        _rope_kernel,
        grid=(B, S // ts),
        in_specs=[pl.BlockSpec((1, ts, width), lambda b, i: (b, i, 0)),
                  pl.BlockSpec((ts, width), lambda b, i: (i, 0)),
                  pl.BlockSpec((ts, width), lambda b, i: (i, 0))],
        out_specs=pl.BlockSpec((1, ts, width), lambda b, i: (b, i, 0)),
        out_shape=jax.ShapeDtypeStruct((B, S, width), F32),
        compiler_params=_params("parallel", "parallel"),
        name="rope",
    )(proj, cos, sin)


def _attn_heads(q, kcat, vcat, valid, sink_ref):
    outs = []
    G = N_HEADS_ATTN // N_KV
    Q = q.shape[0]
    for j in range(N_KV):
        kj = kcat[:, j * HEAD_DIM:(j + 1) * HEAD_DIM]
        vj = vcat[:, j * HEAD_DIM:(j + 1) * HEAD_DIM]
        heads = range(j * G, (j + 1) * G)
        qs = jnp.concatenate([q[:, h * HEAD_DIM:(h + 1) * HEAD_DIM] for h in heads], axis=0).astype(BF16)
        s = lax.dot_general(qs, kj, (((1,), (1,)), ((), ())), preferred_element_type=F32)
        if valid is not None:
            s = jnp.where(valid[None], s.reshape(G, Q, -1), NEG_INF).reshape(G * Q, -1)
        sink = jnp.concatenate([jnp.full((Q, 1), sink_ref[h], F32) for h in heads], axis=0)
        m = jnp.maximum(jnp.max(s, axis=-1, keepdims=True), sink)
        e = jnp.exp(s - m)
        den = jnp.sum(e, axis=-1, keepdims=True) + jnp.exp(sink - m)
        o = jnp.dot(e.astype(BF16), vj, preferred_element_type=F32) / den
        outs.extend(o[g * Q:(g + 1) * Q] for g in range(G))
    return jnp.concatenate(outs, axis=-1)


def _attn_kernel(sink_ref, q_ref, kp_ref, kc_ref, kn_ref, vp_ref, vc_ref, vn_ref, ck_ref, cv_ref,
                 o_ref, *, S, Lc):
    i = pl.program_id(1)
    kcat = jnp.concatenate([ck_ref[0], kp_ref[0], kc_ref[0], kn_ref[0]], axis=0).astype(BF16)
    vcat = jnp.concatenate([cv_ref[0], vp_ref[0], vc_ref[0], vn_ref[0]], axis=0).astype(BF16)
    L = Lc + 3 * QBLOCK
    row = lax.broadcasted_iota(jnp.int32, (QBLOCK, L), 0)
    col = lax.broadcasted_iota(jnp.int32, (QBLOCK, L), 1)
    rel = col - Lc - QBLOCK - row
    kpos = (i - 1) * QBLOCK + col - Lc
    valid = (col < Lc) | ((jnp.abs(rel) <= WINDOW) & (kpos >= 0) & (kpos < S))
    o_ref[0] = _attn_heads(q_ref[0], kcat, vcat, valid, sink_ref)


def _attention(sink, qk, proj_l, proj_c):
    B, S, _ = qk.shape
    Lc = proj_c.shape[1]
    nb = S // QBLOCK
    kcol, vcol = ATTN_W // KV_W, ATTN_W // KV_W + 1

    def blk(colblk, off):
        return pl.BlockSpec((1, QBLOCK, KV_W),
                            lambda b, i: (b, jnp.clip(i + off, 0, nb - 1), colblk))

    return pl.pallas_call(
        functools.partial(_attn_kernel, S=S, Lc=Lc),
        grid=(B, nb),
        in_specs=[pl.BlockSpec(memory_space=pltpu.SMEM),
                  pl.BlockSpec((1, QBLOCK, ATTN_W), lambda b, i: (b, i, 0)),
                  blk(kcol, -1), blk(kcol, 0), blk(kcol, 1),
                  blk(vcol, -1), blk(vcol, 0), blk(vcol, 1),
                  pl.BlockSpec((1, Lc, KV_W), lambda b, i: (b, 0, kcol)),
                  pl.BlockSpec((1, Lc, KV_W), lambda b, i: (b, 0, vcol))],
        out_specs=pl.BlockSpec((1, QBLOCK, ATTN_W), lambda b, i: (b, i, 0)),
        out_shape=jax.ShapeDtypeStruct((B, S, ATTN_W), F32),
        compiler_params=_params("parallel", "parallel"),
        name="attention",
    )(sink, qk, qk, qk, qk, proj_l, proj_l, proj_l, proj_c, proj_c)


def _ctx_attn_kernel(sink_ref, q_ref, ck_ref, cv_ref, o_ref):
    q = q_ref[0] * (HEAD_DIM ** -0.5)
    o_ref[0] = _attn_heads(q, ck_ref[0].astype(BF16), cv_ref[0].astype(BF16), None, sink_ref)


def _ctx_attention(sink, proj_c):
    B, Lc, _ = proj_c.shape
    kcol, vcol = ATTN_W // KV_W, ATTN_W // KV_W + 1
    return pl.pallas_call(
        _ctx_attn_kernel,
        grid=(B,),
        in_specs=[pl.BlockSpec(memory_space=pltpu.SMEM),
                  pl.BlockSpec((1, Lc, ATTN_W), lambda b: (b, 0, 0)),
                  pl.BlockSpec((1, Lc, KV_W), lambda b: (b, 0, kcol)),
                  pl.BlockSpec((1, Lc, KV_W), lambda b: (b, 0, vcol))],
        out_specs=pl.BlockSpec((1, Lc, ATTN_W), lambda b: (b, 0, 0)),
        out_shape=jax.ShapeDtypeStruct((B, Lc, ATTN_W), F32),
        compiler_params=_params("parallel"),
        name="ctx_attention",
    )(sink, proj_c, proj_c, proj_c)


def _seg_rev_block(j, nC, nL):
    return jnp.where(j < nC, nC - 1 - j, nC + nL - 1 - (j - nC))


def _lru_kernel(uf_ref, ur_ref, wg_ref, bg_ref, sp_ref, hf_ref, hr_ref, a_ref, b_ref, cf_ref, cr_ref, *, Tc):
    @pl.when(pl.program_id(1) == 0)
    def _():
        cf_ref[...] = jnp.zeros_like(cf_ref)
        cr_ref[...] = jnp.zeros_like(cr_ref)

    sub = 8
    row = lax.broadcasted_iota(jnp.int32, (sub, LRU_W), 0)
    for d, (u_ref, h_ref, c_ref) in enumerate(((uf_ref, hf_ref, cf_ref), (ur_ref, hr_ref, cr_ref))):
        u = u_ref[0]
        gates = jax.nn.sigmoid(jnp.dot(u.astype(BF16), wg_ref[d], preferred_element_type=F32) + bg_ref[d])
        a = jnp.exp(-(gates[:, :LRU_W] * sp_ref[d]))
        a_ref[...] = a
        b_ref[...] = jnp.sqrt(jnp.maximum(1.0 - a * a, 0.0)) * (gates[:, LRU_W:] * u)

        def tile(ti, carry, d=d, h_ref=h_ref):
            t0 = pl.multiple_of((ti if d == 0 else Tc // sub - 1 - ti) * sub, sub)
            av = a_ref[pl.ds(t0, sub), :]
            bv = b_ref[pl.ds(t0, sub), :]
            for s in (1, 2, 4):
                shift, known = (s, row >= s) if d == 0 else (sub - s, row < sub - s)
                a_prev = jnp.where(known, pltpu.roll(av, shift, 0), 1.0)
                b_prev = jnp.where(known, pltpu.roll(bv, shift, 0), 0.0)
                bv = bv + av * b_prev
                av = av * a_prev
            h = av * carry + bv
            h_ref[0, pl.ds(t0, sub), :] = h
            return h[sub - 1:sub] if d == 0 else h[0:1]

        c_ref[...] = lax.fori_loop(0, Tc // sub, tile, c_ref[...])


def _lru(u, Lc, wa, ba, wi, bi, lam):
    B, T, _ = u.shape
    S = T - Lc
    Tc = math.gcd(math.gcd(Lc, S), 256)
    nC, nL = Lc // Tc, S // Tc
    w_gates = jnp.stack([jnp.concatenate([_block_diag(wa[d]), _block_diag(wi[d])], axis=1)
                         for d in range(2)]).astype(BF16)
    b_gates = jnp.stack([jnp.concatenate([ba[d], bi[d]]) for d in range(2)])[:, None, :]
    decay_rate = (LRU_C * jax.nn.softplus(-lam))[:, None, :]
    fwd = pl.BlockSpec((1, Tc, LRU_W), lambda bi_, j: (bi_, j, 0))
    rev = pl.BlockSpec((1, Tc, LRU_W), lambda bi_, j: (bi_, _seg_rev_block(j, nC, nL), 0))

    def const(shape):
        return pl.BlockSpec(shape, lambda bi_, j: (0,) * len(shape))

    return pl.pallas_call(
        functools.partial(_lru_kernel, Tc=Tc),
        grid=(B, nC + nL),
        in_specs=[fwd, rev, const(w_gates.shape), const(b_gates.shape), const(decay_rate.shape)],
        out_specs=[fwd, rev],
        out_shape=[jax.ShapeDtypeStruct((B, T, LRU_W), F32)] * 2,
        scratch_shapes=[pltpu.VMEM((Tc, LRU_W), F32), pltpu.VMEM((Tc, LRU_W), F32),
                        pltpu.VMEM((1, LRU_W), F32), pltpu.VMEM((1, LRU_W), F32)],
        compiler_params=_params("parallel", "arbitrary"),
        name="lru",
    )(u, u, w_gates, b_gates, decay_rate)


def _mix_out_kernel(attn_ref, hf_ref, hr_ref, g0_ref, g1_ref, wo_ref, o_ref):
    gate = jnp.concatenate([g0_ref[0], g1_ref[0]], axis=-1)
    rec = ((hf_ref[0] + hr_ref[0]) * jax.nn.gelu(gate)).astype(BF16)
    o_ref[0] = (jnp.dot(attn_ref[0].astype(BF16), wo_ref[:ATTN_W, :], preferred_element_type=F32)
                + jnp.dot(rec, wo_ref[ATTN_W:, :], preferred_element_type=F32))


def _mix_out(attn, hf, hr, proj, w_out, t_off):
    B, N, _ = attn.shape
    D = w_out.shape[1]
    tm = math.gcd(math.gcd(N, 512), t_off) if t_off else min(512, N)
    half = LRU_W // 2
    gate_blk = (proj.shape[2] - LRU_W) // half
    tok = lambda b, i: (b, i, 0)
    return pl.pallas_call(
        _mix_out_kernel,
        grid=(B, N // tm),
        in_specs=[pl.BlockSpec((1, tm, ATTN_W), tok),
                  pl.BlockSpec((1, tm, LRU_W), lambda b, i: (b, i + t_off // tm, 0)),
                  pl.BlockSpec((1, tm, LRU_W), lambda b, i: (b, i + t_off // tm, 0)),
                  pl.BlockSpec((1, tm, half), lambda b, i: (b, i, gate_blk)),
                  pl.BlockSpec((1, tm, half), lambda b, i: (b, i, gate_blk + 1)),
                  pl.BlockSpec(w_out.shape, lambda b, i: (0, 0))],
        out_specs=pl.BlockSpec((1, tm, D), tok),
        out_shape=jax.ShapeDtypeStruct((B, N, D), F32),
        compiler_params=_params("parallel", "parallel"),
        name="mix_out",
    )(attn, hf, hr, proj, proj, w_out.astype(BF16))


def _softplus(z):
    return jnp.maximum(z, 0.0) + jnp.log(1.0 + jnp.exp(-jnp.abs(z)))


def _nt_dot(wt, x):
    return lax.dot_general(wt, x, (((1,), (1,)), ((), ())), preferred_element_type=F32)


def _rwkv_prep_kernel(hp_ref, h_ref, hn_ref, mu_ref, wr_ref, wk_ref, wv_ref, w1_ref, w2_ref, a1_ref,
                      a2_ref, g1_ref, g2_ref, w0_ref, a0_ref, ka_ref, rk_ref,
                      dec_ref, aa_ref, k_ref, v_ref, r_ref, g_ref, bc_ref, *, B, ctx_tiles, tiles):
    i = pl.program_id(0)
    D = h_ref.shape[1]
    tm = h_ref.shape[0]
    h = h_ref[...]
    seq_start = (i == 0) | (i == ctx_tiles)
    seq_end = (i == ctx_tiles - 1) | (i == tiles - 1)
    hp = jnp.where(seq_start, 0.0, hp_ref[...])
    hn = jnp.where(seq_end, 0.0, hn_ref[...])
    xx = 0.5 * (jnp.concatenate([hp, h[:tm - B]], axis=0) + jnp.concatenate([h[B:], hn], axis=0)) - h

    def mix(j):
        return (h + xx * mu_ref[j:j + 1, :]).astype(BF16)

    r = _nt_dot(wr_ref[...], mix(0))
    k = _nt_dot(wk_ref[...], mix(2))
    v = _nt_dot(wv_ref[...], mix(3))
    lw = jnp.tanh(_nt_dot(w1_ref[...], mix(1))).astype(BF16)
    wpre = jnp.dot(w2_ref[...], lw, preferred_element_type=F32)
    la = _nt_dot(a1_ref[...], mix(4)).astype(BF16)
    apre = jnp.dot(a2_ref[...], la, preferred_element_type=F32)
    gg = jax.nn.sigmoid(_nt_dot(g1_ref[...], mix(5))).astype(BF16)
    g_ref[...] = jnp.dot(g2_ref[...], gg, preferred_element_type=F32)
    hpg = LANES // B
    hd = RWKV_HD

    def to_scan_layout(x, ref):
        for g in range(RWKV_H // hpg):
            for c in range(tm // LANES):
                tiles = [x[(g * hpg + hh) * hd:(g * hpg + hh + 1) * hd, c * LANES:(c + 1) * LANES]
                         for hh in range(hpg)]
                for tl, tile in enumerate(_slab_transpose(tiles, B)):
                    ref[g, c * hpg + tl] = tile

    iclr = []
    for d in range(2):
        w_log = -_softplus(-(w0_ref[d] + wpre[d * D:(d + 1) * D])) - 0.5
        to_scan_layout(jnp.exp(-jnp.exp(w_log)), dec_ref.at[d])
        a = jax.nn.sigmoid(a0_ref[d] + apre[d * D:(d + 1) * D])
        iclr.append(a)
        to_scan_layout(a, aa_ref.at[d])
    kd_sum = k * (2.0 + (iclr[0] + iclr[1] - 2.0) * ka_ref[...])
    bc = jnp.sum((r * kd_sum * rk_ref[...]).reshape(RWKV_H, hd, tm), axis=1)
    for g in range(RWKV_H // hpg):
        for c in range(tm // LANES):
            rows = [jnp.broadcast_to(bc[g * hpg + hh:g * hpg + hh + 1, c * LANES:(c + 1) * LANES], (8, LANES))
                    for hh in range(hpg)]
            for tl, tile in enumerate(_slab_transpose(rows, B)):
                bc_ref[g, c * hpg + tl] = tile[0:1]
    to_scan_layout(k, k_ref)
    to_scan_layout(v, v_ref)
    to_scan_layout(r, r_ref)


def _slab_transpose(tiles, B):
    n = len(tiles)
    tiles = list(tiles)
    slab = lax.broadcasted_iota(jnp.int32, tiles[0].shape, 1) // B
    s = n // 2
    while s >= 1:
        upper = (slab & s) != 0
        for i in range(n):
            if i & s == 0:
                lo, hi = tiles[i], tiles[i + s]
                tiles[i] = jnp.where(upper, pltpu.roll(hi, s * B, 1), lo)
                tiles[i + s] = jnp.where(upper, hi, pltpu.roll(lo, LANES - s * B, 1))
        s //= 2
    return tiles


def _const_spec(shape):
    nd = len(shape)
    return pl.BlockSpec(shape, lambda i: (0,) * nd, pipeline_mode=pl.Buffered(1))


def _rwkv_prep(h_tm, B, Lc, mu, w_rkv, w0, w1, w2, a0, a1, a2, g1, g2, k_a, r_k):
    TB, D = h_tm.shape
    tm = 256
    tiles, ctx_tiles = TB // tm, Lc * B // tm
    hb = tm // B
    T = TB // B
    last_hb = T - 1
    G = RWKV_H // (LANES // B)
    hd = RWKV_HD

    def t_bf16(w):
        return w.T.astype(BF16)

    zeros_w = jnp.zeros_like(w2[0].T)
    w2t = jnp.concatenate([jnp.concatenate([w2[0].T, zeros_w], 1),
                           jnp.concatenate([zeros_w, w2[1].T], 1)], 0).astype(BF16)
    zeros_a = jnp.zeros_like(a2[0].T)
    a2t = jnp.concatenate([jnp.concatenate([a2[0].T, zeros_a], 1),
                           jnp.concatenate([zeros_a, a2[1].T], 1)], 0).astype(BF16)
    consts = [mu, t_bf16(w_rkv[0]), t_bf16(w_rkv[1]), t_bf16(w_rkv[2]),
              t_bf16(jnp.concatenate([w1[0], w1[1]], 1)), w2t,
              t_bf16(jnp.concatenate([a1[0], a1[1]], 1)), a2t, t_bf16(g1), t_bf16(g2),
              w0.reshape(2, D, 1), a0.reshape(2, D, 1), k_a.reshape(D, 1), r_k.reshape(D, 1)]
    dir_spec = pl.BlockSpec((2, G, hb, hd, LANES), lambda i: (0, 0, i, 0, 0))
    all_spec = pl.BlockSpec((G, hb, hd, LANES), lambda i: (0, i, 0, 0))
    dir_shape = jax.ShapeDtypeStruct((2, G, T, hd, LANES), F32)
    all_shape = jax.ShapeDtypeStruct((G, T, hd, LANES), F32)
    return pl.pallas_call(
        functools.partial(_rwkv_prep_kernel, B=B, ctx_tiles=ctx_tiles, tiles=tiles),
        grid=(tiles,),
        in_specs=[pl.BlockSpec((B, D), lambda i: (jnp.maximum(i * hb - 1, 0), 0)),
                  pl.BlockSpec((tm, D), lambda i: (i, 0)),
                  pl.BlockSpec((B, D), lambda i: (jnp.minimum((i + 1) * hb, last_hb), 0))]
                 + [_const_spec(c.shape) for c in consts],
        out_specs=[dir_spec, dir_spec, all_spec, all_spec, all_spec,
                   pl.BlockSpec((D, tm), lambda i: (0, i)),
                   pl.BlockSpec((G, hb, 1, LANES), lambda i: (0, i, 0, 0))],
        out_shape=[dir_shape, dir_shape, all_shape, all_shape, all_shape,
                   jax.ShapeDtypeStruct((D, TB), F32),
                   jax.ShapeDtypeStruct((G, T, 1, LANES), F32)],
        compiler_params=_params("parallel"),
        name="rwkv_prep",
    )(h_tm, h_tm, h_tm, *consts)


def _wkv_kernel(dec_ref, aa_ref, k_ref, v_ref, r_ref, kkc_ref, kac_ref, y_ref,
                s_ref, g_ref, p_ref, q_ref, sa_ref, *, Tc, nC):
    d = pl.program_id(0)
    j = pl.program_id(2)
    hd = RWKV_HD
    sub = 8

    @pl.when(j == 0)
    def _():
        s_ref[...] = jnp.zeros_like(s_ref)

    def partial_rows(x):
        return jnp.sum(x.reshape(hd // sub, sub, LANES), axis=0)

    pitch = hd + 1

    def put_partial(ref, v, x):
        ref[pl.ds(v, sub, stride=pitch), :] = partial_rows(x)

    def finish_rows(ref):
        acc = ref[pl.ds(0, hd), :]
        for q in range(1, sub):
            acc = acc + ref[pl.ds(q * pitch, hd), :]
        return acc

    def run(with_y):
        g_ref[...] = jnp.ones_like(g_ref)

        def step(s, _):
            tt = jnp.where(d == 0, s, Tc - 1 - s)
            aa = aa_ref[0, 0, tt]
            kk = k_ref[0, tt]
            kf = kk * kkc_ref[0]
            kn = kf * lax.rsqrt(jnp.maximum(jnp.sum(kf * kf, axis=0, keepdims=True), 1e-24))
            g_prev = g_ref[...]
            g = g_prev * dec_ref[0, 0, tt]
            g_inv = 1.0 / g
            g_ref[...] = g
            a_t = -(kn * g_prev)
            b_t = kn * aa * g_inv
            k_t = kk * (1.0 + (aa - 1.0) * kac_ref[0]) * g_inv
            r_t = r_ref[0, tt] * g if with_y else None

            def row_reduce(v, _):
                st = s_ref[v]
                put_partial(p_ref, v, st * a_t)
                if with_y:
                    put_partial(q_ref, v, st * r_t)
                return 0

            lax.fori_loop(0, hd, row_reduce, 0, unroll=8)
            sa = finish_rows(p_ref)
            sa_ref[...] = sa
            if with_y:
                b_r = jnp.sum(b_t * r_t, axis=0, keepdims=True)
                k_r = jnp.sum(k_t * r_t, axis=0, keepdims=True)
                y_ref[0, 0, tt] = finish_rows(q_ref) + sa * b_r + v_ref[0, tt] * k_r

            def row_update(v, _):
                s_ref[v] = s_ref[v] + sa_ref[pl.ds(v, 1), :] * b_t + v_ref[0, tt, pl.ds(v, 1), :] * k_t
                return 0

            lax.fori_loop(0, hd, row_update, 0, unroll=8)
            return 0

        lax.fori_loop(0, Tc, step, 0)

        def rescale(v, _):
            s_ref[v] = s_ref[v] * g_ref[...]
            return 0

        lax.fori_loop(0, hd, rescale, 0, unroll=8)

    @pl.when(j < nC)
    def _():
        run(False)

    @pl.when(j >= nC)
    def _():
        run(True)


def _wkv(dec, aa, k, v, r, k_k, k_a, B, Lc):
    _, G, T, hd, _ = dec.shape
    Tc = WKV_TC
    S = T - Lc
    nC, nL = Lc // Tc, S // Tc

    def tmap(d, j):
        return jnp.where(d == 0, j, _seg_rev_block(j, nC, nL))

    def lmap(d, j):
        jj = jnp.maximum(j - nC, 0)
        return jnp.where(d == 0, jj, nL - 1 - jj)

    dir_spec = pl.BlockSpec((1, 1, Tc, hd, LANES), lambda d, g, j: (d, g, tmap(d, j), 0, 0))
    all_spec = pl.BlockSpec((1, Tc, hd, LANES), lambda d, g, j: (g, tmap(d, j), 0, 0))
    const_spec = pl.BlockSpec((1, hd, LANES), lambda d, g, j: (g, 0, 0))
    return pl.pallas_call(
        functools.partial(_wkv_kernel, Tc=Tc, nC=nC),
        grid=(2, G, nC + nL),
        in_specs=[dir_spec, dir_spec, all_spec, all_spec, all_spec, const_spec, const_spec],
        out_specs=pl.BlockSpec((1, 1, Tc, hd, LANES), lambda d, g, j: (d, g, lmap(d, j), 0, 0)),
        out_shape=jax.ShapeDtypeStruct((2, G, S, hd, LANES), F32),
        scratch_shapes=[pltpu.VMEM((hd, hd, LANES), F32),
                        pltpu.VMEM((hd, LANES), F32),
                        pltpu.VMEM(((hd + 1) * 8, LANES), F32),
                        pltpu.VMEM(((hd + 1) * 8, LANES), F32),
                        pltpu.VMEM((hd, LANES), F32)],
        compiler_params=_params("parallel", "parallel", "arbitrary"),
        name="wkv",
    )(dec, aa, k, v, r, _lane_const(k_k, B), _lane_const(k_a, B))


def _lane_const(c, B):
    hpg = LANES // B
    return jnp.repeat(c.reshape(RWKV_H // hpg, hpg, RWKV_HD).transpose(0, 2, 1), B, axis=-1)


def _rwkv_finish_kernel(y_ref, v_ref, bc_ref, g_ref, gng_ref, gnb_ref, wo_ref, o_ref, *, B):
    G, steps, hd, _ = v_ref.shape
    hpg = LANES // B
    rows = []
    for g in range(G):
        head_cols = [[] for _ in range(hpg)]
        for c in range(steps // hpg):
            tiles = []
            for tl in range(hpg):
                t = c * hpg + tl
                y = y_ref[0, g, t] + y_ref[1, g, t]
                mean = jnp.mean(y, axis=0, keepdims=True)
                var = jnp.mean(jnp.square(y - mean), axis=0, keepdims=True)
                yn = (y - mean) * lax.rsqrt(var + GN_EPS)
                tiles.append(yn * gng_ref[g] + gnb_ref[g] + bc_ref[g, t] * v_ref[g, t])
            for hh, tile in enumerate(_slab_transpose(tiles, B)):
                head_cols[hh].append(tile)
        rows.extend(jnp.concatenate(cols, axis=1) for cols in head_cols)
    o = (jnp.concatenate(rows, axis=0) * g_ref[...]).astype(BF16)
    out_t = jnp.dot(wo_ref[...], o, preferred_element_type=F32)
    o_ref[...] = out_t.T


def _rwkv_finish(y2, v, bc, g, gn_g, gn_b, w_o, B, Lc):
    _, G, S, hd, _ = y2.shape
    D = g.shape[0]
    tm = 256
    steps = tm // B
    off = Lc // steps
    return pl.pallas_call(
        functools.partial(_rwkv_finish_kernel, B=B),
        grid=(S // steps,),
        in_specs=[pl.BlockSpec((2, G, steps, hd, LANES), lambda i: (0, 0, i, 0, 0)),
                  pl.BlockSpec((G, steps, hd, LANES), lambda i: (0, i + off, 0, 0)),
                  pl.BlockSpec((G, steps, 1, LANES), lambda i: (0, i + off, 0, 0)),
                  pl.BlockSpec((D, tm), lambda i: (0, i + off)),
                  _const_spec((G, hd, LANES)), _const_spec((G, hd, LANES)), _const_spec((D, D))],
        out_specs=pl.BlockSpec((tm, D), lambda i: (i, 0)),
        out_shape=jax.ShapeDtypeStruct((S * B, D), F32),
        compiler_params=_params("parallel"),
        name="rwkv_finish",
    )(y2, v, bc, g, _lane_const(gn_g, B), _lane_const(gn_b, B), w_o.T.astype(BF16))


def _rwkv7_mixer(hl, hc, mu, w_rkv, w0, w1, w2, a0, a1, a2, g1, g2, k_k, k_a, r_k, gn_g, gn_b, w_o):
    B, S, D = hl.shape
    Lc = hc.shape[1]
    h_tm = jnp.concatenate([hc.transpose(1, 0, 2), hl.transpose(1, 0, 2)], axis=0).reshape((Lc + S) * B, D)
    dec, aa, k, v, r, g, bc = _rwkv_prep(h_tm, B, Lc, mu, w_rkv, w0, w1, w2, a0, a1, a2, g1, g2,
                                         k_a, r_k.reshape(D))
    y2 = _wkv(dec, aa, k, v, r, k_k, k_a, B, Lc)
    y_tm = _rwkv_finish(y2, v, bc, g, gn_g, gn_b, w_o, B, Lc)
    return y_tm.reshape(S, B, D).transpose(1, 0, 2)


def _router_kernel(h_ref, w_ref, o_ref):
    h = _from_chunk_rows(h_ref.at[0], o_ref.shape[2])
    logits = lax.dot_general(w_ref[...], h, (((1,), (1,)), ((), ())),
                             precision=lax.Precision.HIGHEST, preferred_element_type=F32)
    m = jnp.max(logits, axis=0, keepdims=True)
    e = jnp.exp(logits - m)
    o_ref[0] = e / jnp.sum(e, axis=0, keepdims=True)


def _router(h, N, w_router):
    B = h.shape[0]
    D, E = w_router.shape
    chunks = D // LANES
    tn = min(512, N)
    return pl.pallas_call(
        _router_kernel,
        grid=(B, N // tn),
        in_specs=[pl.BlockSpec((1, tn * chunks, LANES), lambda b, i: (b, i, 0)),
                  pl.BlockSpec((E, D), lambda b, i: (0, 0))],
        out_specs=pl.BlockSpec((1, E, tn), lambda b, i: (b, 0, i)),
        out_shape=jax.ShapeDtypeStruct((B, E, N), F32),
        compiler_params=_params("parallel", "parallel"),
        name="router",
    )(h, w_router.T)


def _moe_kernel(idx_ref, gate_ref, h_ref, w1_ref, w3_ref, w2_ref, o_ref, xin_ref, y_ref, *, cap, chunks):
    bb = h_ref.shape[0]

    @pl.when(pl.program_id(1) == 0)
    def _():
        o_ref[...] = jnp.zeros_like(o_ref)

    def token_rows(t):
        return pl.ds(pl.multiple_of(t * chunks, chunks), chunks)

    for bi in range(bb):
        def gather(r, _, bi=bi):
            slot = bi * cap + r
            xin_ref[token_rows(slot), :] = h_ref[bi, token_rows(idx_ref[0, 0, slot]), :]
            return 0

        lax.fori_loop(0, cap, gather, 0, unroll=8)
    x = _from_chunk_rows(xin_ref, bb * cap).astype(BF16)
    h1 = jnp.dot(x, w1_ref[0], preferred_element_type=F32)
    h3 = jnp.dot(x, w3_ref[0], preferred_element_type=F32)
    hid = (h1 * jax.nn.sigmoid(h1) * h3).astype(BF16)
    _to_chunk_rows(y_ref, jnp.dot(hid, w2_ref[0], preferred_element_type=F32) * gate_ref[0])

    batch = 8
    for bi in range(bb):
        def scatter_add(i, _, bi=bi):
            slots = [bi * cap + i * batch + u for u in range(batch)]
            rows = [token_rows(idx_ref[0, 0, s]) for s in slots]
            sums = [o_ref[bi, rows[u], :] + y_ref[token_rows(slots[u]), :] for u in range(batch)]
            for u in range(batch):
                o_ref[bi, rows[u], :] = sums[u]
            return 0

        lax.fori_loop(0, cap // batch, scatter_add, 0)


def _moe(h, N, w_router, w1, w3, w2):
    B = h.shape[0]
    E, D, FF = w1.shape
    chunks = D // LANES
    cap = CAPACITY * N // E
    bb = max(1, min(B, MOE_ROWS // cap))
    aff = _router(h, N, w_router)
    gate, idx = lax.top_k(aff, cap)

    def group(t):
        return t.reshape(B // bb, bb, E, cap).transpose(0, 2, 1, 3).reshape(B // bb * E, bb * cap)

    rows = bb * cap
    return pl.pallas_call(
        functools.partial(_moe_kernel, cap=cap, chunks=chunks),
        grid=(B // bb, E),
        in_specs=[pl.BlockSpec((1, 1, rows), lambda b, e: (b * E + e, 0, 0), memory_space=pltpu.SMEM),
                  pl.BlockSpec((1, rows, 1), lambda b, e: (b * E + e, 0, 0)),
                  pl.BlockSpec((bb, N * chunks, LANES), lambda b, e: (b, 0, 0), pipeline_mode=pl.Buffered(1)),
                  pl.BlockSpec((1, D, FF), lambda b, e: (e, 0, 0)),
                  pl.BlockSpec((1, D, FF), lambda b, e: (e, 0, 0)),
                  pl.BlockSpec((1, FF, D), lambda b, e: (e, 0, 0))],
        out_specs=pl.BlockSpec((bb, N * chunks, LANES), lambda b, e: (b, 0, 0)),
        out_shape=jax.ShapeDtypeStruct((B, N * chunks, LANES), F32),
        scratch_shapes=[pltpu.VMEM((rows * chunks, LANES), F32), pltpu.VMEM((rows * chunks, LANES), F32)],
        compiler_params=_params("parallel", "arbitrary"),
        name="moe",
    )(group(idx.astype(jnp.int32))[:, None, :], group(gate)[:, :, None], h, w1, w3, w2)


def _block_diag(w):
    H, bi, bj = w.shape
    eye = jnp.eye(H, dtype=w.dtype)
    return jnp.einsum('hij,hg->higj', w, eye).reshape(H * bi, H * bj)


def _dwconv(u, w, b):
    n = u.shape[1]
    up = jnp.pad(u, ((0, 0), (CONV_LEFT, CONV_W - 1 - CONV_LEFT), (0, 0)))
    out = up[:, 0:n] * w[0]
    for j in range(1, CONV_W):
        out = out + up[:, j:j + n] * w[j]
    return out + b


def _attn_lru_mixer(hl, hc, cos, sin, w_in, w_out, sink, conv_w, conv_b, wa, ba, wi, bi, lam):
    B, S, D = hl.shape
    Lc = hc.shape[1]
    mix_in = w_in.shape[1]
    w_in_b = w_in.astype(BF16)
    proj_l = _mm(hl.reshape(B * S, D), w_in_b).reshape(B, S, mix_in)
    proj_c = _mm(hc.reshape(B * Lc, D), w_in_b).reshape(B, Lc, mix_in)
    qk = _rope(proj_l, cos, sin)
    attn_l = _attention(sink, qk, proj_l, proj_c)
    attn_c = _ctx_attention(sink, proj_c)

    u0, g0 = ATTN_W + 2 * KV_W, ATTN_W + 2 * KV_W + LRU_W
    u = jnp.concatenate([_dwconv(proj_c[..., u0:g0], conv_w, conv_b),
                         _dwconv(proj_l[..., u0:g0], conv_w, conv_b)], axis=1)
    hf, hr = _lru(u, Lc, wa, ba, wi, bi, lam)
    yl = _mix_out(attn_l, hf, hr, proj_l, w_out, Lc)
    yc = _mix_out(attn_c, hf, hr, proj_c, w_out, 0)
    return yl, yc


def kernel(x, c, ctx, c_ctx, mod_w, mod_b, norm_mix, norm_ffn, router_w, exp_w1, exp_w3, exp_w2, mix_in, mix_out, attn_sink, lru_conv_w, lru_conv_b, lru_wa, lru_ba, lru_wi, lru_bi, lru_lam, rw_mu, rw_rkv, rw_w0, rw_w1, rw_w2, rw_a0, rw_a1, rw_a2, rw_g1, rw_g2, rw_kk, rw_ka, rw_rk, rw_gn_g, rw_gn_b, rw_wo, final_norm):
    B, S, D = x.shape
    Lc = ctx.shape[1]
    depth = mod_w.shape[0]
    assert depth == 2 and S % QBLOCK == 0
    assert LANES % B == 0 and RWKV_H % (LANES // B) == 0 and B % 8 == 0
    assert Lc % WKV_TC == 0 and S % WKV_TC == 0 and (Lc * B) % 256 == 0
    cos, sin = _rope_tables(S)

    n_rows = -(-(B + 1) // 8) * 8
    cond = jnp.concatenate([jax.nn.silu(c), jax.nn.silu(c_ctx)[None],
                            jnp.zeros((n_rows - B - 1, D), F32)], axis=0)

    mods_l, mods_c = [], []
    for layer in range(depth):
        mod = _mm(cond, mod_w[layer], tm=n_rows, tn=1024, precise=True) + mod_b[layer]
        mods_l.append([t[:, None, :] for t in jnp.split(mod[:B], 6, axis=-1)])
        mods_c.append([jnp.broadcast_to(t[None], (B, 1, D)) for t in jnp.split(mod[B:B + 1], 6, axis=-1)])

    xl, xc = x, ctx
    hl = _resid_norm(xl, None, None, norm_mix[0], mods_l[0][0], mods_l[0][1])
    hc = _resid_norm(xc, None, None, norm_mix[0], mods_c[0][0], mods_c[0][1])
    for layer in range(depth):
        last = layer == depth - 1
        m_l, m_c = mods_l[layer], mods_c[layer]
        if layer % 2 == 0:
            i = layer // 2
            yl, yc = _attn_lru_mixer(hl, hc, cos, sin, mix_in[i], mix_out[i], attn_sink[i], lru_conv_w[i],
                                     lru_conv_b[i], lru_wa[i], lru_ba[i], lru_wi[i], lru_bi[i], lru_lam[i])
        else:
            i = layer // 2
            yl = _rwkv7_mixer(hl, hc, rw_mu[i], rw_rkv[i], rw_w0[i], rw_w1[i], rw_w2[i], rw_a0[i], rw_a1[i],
                              rw_a2[i], rw_g1[i], rw_g2[i], rw_kk[i], rw_ka[i], rw_rk[i], rw_gn_g[i],
                              rw_gn_b[i], rw_wo[i])
            yc = None
        w1b, w3b, w2b = exp_w1[layer].astype(BF16), exp_w3[layer].astype(BF16), exp_w2[layer].astype(BF16)
        xl, hl = _resid_norm(xl, yl, m_l[2], norm_ffn[layer], m_l[3], m_l[4], out_rows=True)
        moe_l = _moe(hl, S, router_w[layer], w1b, w3b, w2b)
        if last:
            zero = jnp.zeros((B, 1, D), F32)
            return _resid_norm(xl, moe_l, m_l[5], final_norm, zero, zero, delta_rows=True, want_x=False)
        n_l, n_c = mods_l[layer + 1], mods_c[layer + 1]
        xl, hl = _resid_norm(xl, moe_l, m_l[5], norm_mix[layer + 1], n_l[0], n_l[1], delta_rows=True)
        xc, hc = _resid_norm(xc, yc, m_c[2], norm_ffn[layer], m_c[3], m_c[4], out_rows=True)
        moe_c = _moe(hc, Lc, router_w[layer], w1b, w3b, w2b)
        xc, hc = _resid_norm(xc, moe_c, m_c[5], norm_mix[layer + 1], n_c[0], n_c[1], delta_rows=True)
```

```python
import functools
import math

import jax
import jax.numpy as jnp
from jax import lax
from jax.experimental import pallas as pl
from jax.experimental.pallas import tpu as pltpu

F32 = jnp.float32
BF16 = jnp.bfloat16

GRID_W = 64
N_HEADS_ATTN = 8
N_KV = 2
HEAD_DIM = 64
AXIS_DIM = HEAD_DIM // 2
ATTN_W = N_HEADS_ATTN * HEAD_DIM
KV_W = N_KV * HEAD_DIM
WINDOW = 128
QBLOCK = 128
ROPE_BASE = 10000.0
LRU_W = 512
LRU_BLOCKS = 8
CONV_W = 4
CONV_LEFT = 2
LRU_C = 8.0
RWKV_H = 16
RWKV_HD = 64
GN_EPS = 64e-5
N_EXPERTS = 16
CAPACITY = 2
NORM_EPS = 1e-6
NEG_INF = -1e30

LANES = 128
VMEM_LIMIT_BYTES = 56 * 1024 * 1024
WKV_TC = 32
MOE_ROWS = 256

def _params(*sem):
    return pltpu.CompilerParams(dimension_semantics=sem, vmem_limit_bytes=VMEM_LIMIT_BYTES)


def _mm_kernel(x_ref, w_ref, o_ref, *, precise):
    if precise:
        o_ref[...] = jnp.dot(x_ref[...], w_ref[...], precision=lax.Precision.HIGHEST,
                             preferred_element_type=F32)
    else:
        o_ref[...] = jnp.dot(x_ref[...].astype(BF16), w_ref[...].astype(BF16),
                             preferred_element_type=F32)


def _mm(x, w, *, tm=512, tn=None, precise=False):
    M, K = x.shape
    N = w.shape[1]
    tm = min(tm, M)
    tn = N if tn is None else min(tn, N)
    assert M % tm == 0 and N % tn == 0, (M, tm, N, tn)
    return pl.pallas_call(
        functools.partial(_mm_kernel, precise=precise),
        grid=(M // tm, N // tn),
        in_specs=[pl.BlockSpec((tm, K), lambda i, j: (i, 0)),
                  pl.BlockSpec((K, tn), lambda i, j: (0, j))],
        out_specs=pl.BlockSpec((tm, tn), lambda i, j: (i, j)),
        out_shape=jax.ShapeDtypeStruct((M, N), F32),
        compiler_params=_params("parallel", "parallel"),
        name="mm",
    )(x, w)


def _resid_norm_kernel(*refs, has_delta, delta_rows, want_x, out_rows, route):
    refs = list(refs)
    x_ref = refs.pop(0)
    d_ref, gate_ref = (refs.pop(0), refs.pop(0)) if has_delta else (None, None)
    g_ref, sh_ref, sc_ref = refs.pop(0), refs.pop(0), refs.pop(0)
    wr_ref = refs.pop(0) if route else None
    xo_ref = refs.pop(0) if (has_delta and want_x) else None
    h_ref = refs.pop(0)
    aff_ref = refs.pop(0) if route else None
    x = x_ref[0]
    if has_delta:
        delta = _from_chunk_rows(d_ref.at[0], x.shape[0]) if delta_rows else d_ref[0]
        x = x + gate_ref[0] * delta
        if want_x:
            xo_ref[0] = x
    y = x * lax.rsqrt(jnp.mean(x * x, axis=-1, keepdims=True) + NORM_EPS) * g_ref[...]
    h = y * (1.0 + sc_ref[0]) + sh_ref[0]
    if out_rows:
        _to_chunk_rows(h_ref.at[0], h)
    else:
        h_ref[0] = h
    if route:
        logits = lax.dot_general(wr_ref[...], h, (((1,), (1,)), ((), ())),
                                 precision=lax.Precision.HIGHEST, preferred_element_type=F32)
        e = jnp.exp(logits - jnp.max(logits, axis=0, keepdims=True))
        aff_ref[0] = e / jnp.sum(e, axis=0, keepdims=True)


def _from_chunk_rows(ref, n):
    chunks = ref.shape[0] // n
    return jnp.concatenate([ref[pl.ds(j, n, stride=chunks), :] for j in range(chunks)], axis=-1)


def _to_chunk_rows(ref, x):
    n = x.shape[0]
    chunks = x.shape[1] // LANES
    for j in range(chunks):
        ref[pl.ds(j, n, stride=chunks), :] = x[:, j * LANES:(j + 1) * LANES]


def _resid_norm(x, delta, gate, g, shift, scale, *, delta_rows=False, want_x=True, out_rows=False,
                router_w=None):
    B, N, D = x.shape
    ts = min(512, N)
    has_delta = delta is not None
    chunks = D // LANES
    tok = pl.BlockSpec((1, ts, D), lambda b, i: (b, i, 0))
    tok_rows = pl.BlockSpec((1, ts * chunks, LANES), lambda b, i: (b, i, 0))
    per_b = pl.BlockSpec((1, 1, D), lambda b, i: (b, 0, 0))
    in_specs, args = [tok], [x]
    if has_delta:
        in_specs += [tok_rows if delta_rows else tok, per_b]
        args += [delta, gate]
    in_specs += [pl.BlockSpec((1, D), lambda b, i: (0, 0)), per_b, per_b]
    args += [g.reshape(1, D), shift, scale]
    route = router_w is not None
    if route:
        E = router_w.shape[1]
        in_specs.append(pl.BlockSpec((E, D), lambda b, i: (0, 0)))
        args.append(router_w.T)
    out_specs, out_shape = [], []
    if has_delta and want_x:
        out_specs.append(tok)
        out_shape.append(jax.ShapeDtypeStruct((B, N, D), F32))
    out_specs.append(tok_rows if out_rows else tok)
    out_shape.append(jax.ShapeDtypeStruct((B, N * chunks, LANES) if out_rows else (B, N, D), F32))
    if route:
        out_specs.append(pl.BlockSpec((1, E, ts), lambda b, i: (b, 0, i)))
        out_shape.append(jax.ShapeDtypeStruct((B, E, N), F32))
    outs = pl.pallas_call(
        functools.partial(_resid_norm_kernel, has_delta=has_delta, delta_rows=delta_rows, want_x=want_x,
                          out_rows=out_rows, route=route),
        grid=(B, N // ts),
        in_specs=in_specs,
        out_specs=out_specs,
        out_shape=out_shape,
        compiler_params=_params("parallel", "parallel"),
        name="resid_norm",
    )(*args)
    return outs if len(outs) > 1 else outs[0]


def _norm_mm_kernel(x_ref, g_ref, sh_ref, sc_ref, w_ref, o_ref):
    x = x_ref[0]
    y = x * lax.rsqrt(jnp.mean(x * x, axis=-1, keepdims=True) + NORM_EPS) * g_ref[...]
    h = (y * (1.0 + sc_ref[0]) + sh_ref[0]).astype(BF16)
    o_ref[0] = jnp.dot(h, w_ref[...], preferred_element_type=F32)


def _norm_mm(x, g, shift, scale, w):
    B, N, D = x.shape
    M = w.shape[1]
    ts = min(512, N)
    per_b = pl.BlockSpec((1, 1, D), lambda b, i: (b, 0, 0))
    return pl.pallas_call(
        _norm_mm_kernel,
        grid=(B, N // ts),
        in_specs=[pl.BlockSpec((1, ts, D), lambda b, i: (b, i, 0)),
                  pl.BlockSpec((1, D), lambda b, i: (0, 0)), per_b, per_b,
                  pl.BlockSpec((D, M), lambda b, i: (0, 0))],
        out_specs=pl.BlockSpec((1, ts, M), lambda b, i: (b, i, 0)),
        out_shape=jax.ShapeDtypeStruct((B, N, M), F32),
        compiler_params=_params("parallel", "parallel"),
        name="norm_mm",
    )(x, g.reshape(1, D), shift, scale, w)


def _rope_kernel(x_ref, cos_ref, sin_ref, o_ref):
    width = x_ref.shape[-1]
    for c in range(width // LANES):
        sl = slice(c * LANES, (c + 1) * LANES)
        x = x_ref[0, :, sl]
        lane = lax.broadcasted_iota(jnp.int32, x.shape, 1)
        first_half = (lane % AXIS_DIM) < (AXIS_DIM // 2)
        partner = jnp.where(first_half, pltpu.roll(x, LANES - AXIS_DIM // 2, 1),
                            pltpu.roll(x, AXIS_DIM // 2, 1))
        o_ref[0, :, sl] = x * cos_ref[:, sl] + partner * sin_ref[:, sl]


def _rope_tables(S):
    rows = S // GRID_W
    row = jnp.repeat(jnp.arange(rows), GRID_W).astype(F32)
    col = jnp.tile(jnp.arange(GRID_W), rows).astype(F32)
    inv = ROPE_BASE ** (-jnp.arange(0, AXIS_DIM, 2, dtype=F32) / AXIS_DIM)
    ang_row, ang_col = row[:, None] * inv, col[:, None] * inv

    def axis_tables(ang):
        c, s = jnp.cos(ang), jnp.sin(ang)
        return jnp.concatenate([c, c], -1), jnp.concatenate([-s, s], -1)

    cr, sr = axis_tables(ang_row)
    cc, sc = axis_tables(ang_col)
    cos_h = jnp.concatenate([cr, cc], -1)
    sin_h = jnp.concatenate([sr, sc], -1)
    q_scale = HEAD_DIM ** -0.5
    cos = jnp.concatenate([jnp.tile(cos_h, (1, N_HEADS_ATTN)) * q_scale, jnp.tile(cos_h, (1, N_KV))], -1)
    sin = jnp.concatenate([jnp.tile(sin_h, (1, N_HEADS_ATTN)) * q_scale, jnp.tile(sin_h, (1, N_KV))], -1)
    return cos, sin


def _rope(proj, cos, sin):
    B, S, _ = proj.shape
    width = ATTN_W + KV_W
    ts = min(512, S)
    return pl.pallas_call(
        _rope_kernel,
        grid=(B, S // ts),
        in_specs=[pl.BlockSpec((1, ts, width), lambda b, i: (b, i, 0)),
                  pl.BlockSpec((ts, width), lambda b, i: (i, 0)),
                  pl.BlockSpec((ts, width), lambda b, i: (i, 0))],
        out_specs=pl.BlockSpec((1, ts, width), lambda b, i: (b, i, 0)),
        out_shape=jax.ShapeDtypeStruct((B, S, width), F32),
        compiler_params=_params("parallel", "parallel"),
        name="rope",
    )(proj, cos, sin)


def _attn_heads(q, kcat, vcat, valid, sink_ref):
    outs = []
    G = N_HEADS_ATTN // N_KV
    Q = q.shape[0]
    for j in range(N_KV):
        kj = kcat[:, j * HEAD_DIM:(j + 1) * HEAD_DIM]
        vj = vcat[:, j * HEAD_DIM:(j + 1) * HEAD_DIM]
        heads = range(j * G, (j + 1) * G)
        qs = jnp.concatenate([q[:, h * HEAD_DIM:(h + 1) * HEAD_DIM] for h in heads], axis=0).astype(BF16)
        s = lax.dot_general(qs, kj, (((1,), (1,)), ((), ())), preferred_element_type=F32)
        if valid is not None:
            s = jnp.where(valid[None], s.reshape(G, Q, -1), NEG_INF).reshape(G * Q, -1)
        sink = jnp.concatenate([jnp.full((Q, 1), sink_ref[h], F32) for h in heads], axis=0)
        m = jnp.maximum(jnp.max(s, axis=-1, keepdims=True), sink)
        e = jnp.exp(s - m)
        den = jnp.sum(e, axis=-1, keepdims=True) + jnp.exp(sink - m)
        o = jnp.dot(e.astype(BF16), vj, preferred_element_type=F32) / den
        outs.extend(o[g * Q:(g + 1) * Q] for g in range(G))
    return jnp.concatenate(outs, axis=-1)


def _attn_kernel(sink_ref, q_ref, kp_ref, kc_ref, kn_ref, vp_ref, vc_ref, vn_ref, ck_ref, cv_ref,
                 o_ref, *, S, Lc):
    i = pl.program_id(1)
    kcat = jnp.concatenate([ck_ref[0], kp_ref[0], kc_ref[0], kn_ref[0]], axis=0).astype(BF16)
    vcat = jnp.concatenate([cv_ref[0], vp_ref[0], vc_ref[0], vn_ref[0]], axis=0).astype(BF16)
    L = Lc + 3 * QBLOCK
    row = lax.broadcasted_iota(jnp.int32, (QBLOCK, L), 0)
    col = lax.broadcasted_iota(jnp.int32, (QBLOCK, L), 1)
    rel = col - Lc - QBLOCK - row
    kpos = (i - 1) * QBLOCK + col - Lc
    valid = (col < Lc) | ((jnp.abs(rel) <= WINDOW) & (kpos >= 0) & (kpos < S))
    o_ref[0] = _attn_heads(q_ref[0], kcat, vcat, valid, sink_ref)


def _attention(sink, qk, proj_l, proj_c):
    B, S, _ = qk.shape
    Lc = proj_c.shape[1]
    nb = S // QBLOCK
    kcol, vcol = ATTN_W // KV_W, ATTN_W // KV_W + 1

    def blk(colblk, off):
        return pl.BlockSpec((1, QBLOCK, KV_W),
                            lambda b, i: (b, jnp.clip(i + off, 0, nb - 1), colblk))

    return pl.pallas_call(
        functools.partial(_attn_kernel, S=S, Lc=Lc),
        grid=(B, nb),
        in_specs=[pl.BlockSpec(memory_space=pltpu.SMEM),
                  pl.BlockSpec((1, QBLOCK, ATTN_W), lambda b, i: (b, i, 0)),
                  blk(kcol, -1), blk(kcol, 0), blk(kcol, 1),
                  blk(vcol, -1), blk(vcol, 0), blk(vcol, 1),
                  pl.BlockSpec((1, Lc, KV_W), lambda b, i: (b, 0, kcol)),
                  pl.BlockSpec((1, Lc, KV_W), lambda b, i: (b, 0, vcol))],
        out_specs=pl.BlockSpec((1, QBLOCK, ATTN_W), lambda b, i: (b, i, 0)),
        out_shape=jax.ShapeDtypeStruct((B, S, ATTN_W), F32),
        compiler_params=_params("parallel", "parallel"),
        name="attention",
    )(sink, qk, qk, qk, qk, proj_l, proj_l, proj_l, proj_c, proj_c)


def _ctx_attn_kernel(sink_ref, q_ref, ck_ref, cv_ref, o_ref):
    q = q_ref[0] * (HEAD_DIM ** -0.5)
    o_ref[0] = _attn_heads(q, ck_ref[0].astype(BF16), cv_ref[0].astype(BF16), None, sink_ref)


def _ctx_attention(sink, proj_c):
    B, Lc, _ = proj_c.shape
    kcol, vcol = ATTN_W // KV_W, ATTN_W // KV_W + 1
    return pl.pallas_call(
        _ctx_attn_kernel,
        grid=(B,),
        in_specs=[pl.BlockSpec(memory_space=pltpu.SMEM),
                  pl.BlockSpec((1, Lc, ATTN_W), lambda b: (b, 0, 0)),
                  pl.BlockSpec((1, Lc, KV_W), lambda b: (b, 0, kcol)),
                  pl.BlockSpec((1, Lc, KV_W), lambda b: (b, 0, vcol))],
        out_specs=pl.BlockSpec((1, Lc, ATTN_W), lambda b: (b, 0, 0)),
        out_shape=jax.ShapeDtypeStruct((B, Lc, ATTN_W), F32),
        compiler_params=_params("parallel"),
        name="ctx_attention",
    )(sink, proj_c, proj_c, proj_c)


def _seg_rev_block(j, nC, nL):
    return jnp.where(j < nC, nC - 1 - j, nC + nL - 1 - (j - nC))


def _lru_kernel(uf_ref, ur_ref, wg_ref, bg_ref, sp_ref, hf_ref, hr_ref, a_ref, b_ref, cf_ref, cr_ref, *, Tc):
    @pl.when(pl.program_id(1) == 0)
    def _():
        cf_ref[...] = jnp.zeros_like(cf_ref)
        cr_ref[...] = jnp.zeros_like(cr_ref)

    sub = 8
    row = lax.broadcasted_iota(jnp.int32, (sub, LRU_W), 0)
    for d, (u_ref, h_ref, c_ref) in enumerate(((uf_ref, hf_ref, cf_ref), (ur_ref, hr_ref, cr_ref))):
        u = u_ref[0]
        gates = jax.nn.sigmoid(jnp.dot(u.astype(BF16), wg_ref[d], preferred_element_type=F32) + bg_ref[d])
        a = jnp.exp(-(gates[:, :LRU_W] * sp_ref[d]))
        a_ref[...] = a
        b_ref[...] = jnp.sqrt(jnp.maximum(1.0 - a * a, 0.0)) * (gates[:, LRU_W:] * u)

        def tile(ti, carry, d=d, h_ref=h_ref):
            t0 = pl.multiple_of((ti if d == 0 else Tc // sub - 1 - ti) * sub, sub)
            av = a_ref[pl.ds(t0, sub), :]
            bv = b_ref[pl.ds(t0, sub), :]
            for s in (1, 2, 4):
                shift, known = (s, row >= s) if d == 0 else (sub - s, row < sub - s)
                a_prev = jnp.where(known, pltpu.roll(av, shift, 0), 1.0)
                b_prev = jnp.where(known, pltpu.roll(bv, shift, 0), 0.0)
                bv = bv + av * b_prev
                av = av * a_prev
            h = av * carry + bv
            h_ref[0, pl.ds(t0, sub), :] = h
            return h[sub - 1:sub] if d == 0 else h[0:1]

        c_ref[...] = lax.fori_loop(0, Tc // sub, tile, c_ref[...])


def _lru(u, Lc, wa, ba, wi, bi, lam):
    B, T, _ = u.shape
    S = T - Lc
    Tc = math.gcd(math.gcd(Lc, S), 256)
    nC, nL = Lc // Tc, S // Tc
    w_gates = jnp.stack([jnp.concatenate([_block_diag(wa[d]), _block_diag(wi[d])], axis=1)
                         for d in range(2)]).astype(BF16)
    b_gates = jnp.stack([jnp.concatenate([ba[d], bi[d]]) for d in range(2)])[:, None, :]
    decay_rate = (LRU_C * jax.nn.softplus(-lam))[:, None, :]
    fwd = pl.BlockSpec((1, Tc, LRU_W), lambda bi_, j: (bi_, j, 0))
    rev = pl.BlockSpec((1, Tc, LRU_W), lambda bi_, j: (bi_, _seg_rev_block(j, nC, nL), 0))

    def const(shape):
        return pl.BlockSpec(shape, lambda bi_, j: (0,) * len(shape))

    return pl.pallas_call(
        functools.partial(_lru_kernel, Tc=Tc),
        grid=(B, nC + nL),
        in_specs=[fwd, rev, const(w_gates.shape), const(b_gates.shape), const(decay_rate.shape)],
        out_specs=[fwd, rev],
        out_shape=[jax.ShapeDtypeStruct((B, T, LRU_W), F32)] * 2,
        scratch_shapes=[pltpu.VMEM((Tc, LRU_W), F32), pltpu.VMEM((Tc, LRU_W), F32),
                        pltpu.VMEM((1, LRU_W), F32), pltpu.VMEM((1, LRU_W), F32)],
        compiler_params=_params("parallel", "arbitrary"),
        name="lru",
    )(u, u, w_gates, b_gates, decay_rate)


def _mix_out_kernel(attn_ref, hf_ref, hr_ref, g0_ref, g1_ref, wo_ref, o_ref):
    gate = jnp.concatenate([g0_ref[0], g1_ref[0]], axis=-1)
    rec = ((hf_ref[0] + hr_ref[0]) * jax.nn.gelu(gate)).astype(BF16)
    o_ref[0] = (jnp.dot(attn_ref[0].astype(BF16), wo_ref[:ATTN_W, :], preferred_element_type=F32)
                + jnp.dot(rec, wo_ref[ATTN_W:, :], preferred_element_type=F32))


def _mix_out(attn, hf, hr, proj, w_out, t_off):
    B, N, _ = attn.shape
    D = w_out.shape[1]
    tm = math.gcd(math.gcd(N, 512), t_off) if t_off else min(512, N)
    half = LRU_W // 2
    gate_blk = (proj.shape[2] - LRU_W) // half
    tok = lambda b, i: (b, i, 0)
    return pl.pallas_call(
        _mix_out_kernel,
        grid=(B, N // tm),
        in_specs=[pl.BlockSpec((1, tm, ATTN_W), tok),
                  pl.BlockSpec((1, tm, LRU_W), lambda b, i: (b, i + t_off // tm, 0)),
                  pl.BlockSpec((1, tm, LRU_W), lambda b, i: (b, i + t_off // tm, 0)),
                  pl.BlockSpec((1, tm, half), lambda b, i: (b, i, gate_blk)),
                  pl.BlockSpec((1, tm, half), lambda b, i: (b, i, gate_blk + 1)),
                  pl.BlockSpec(w_out.shape, lambda b, i: (0, 0))],
        out_specs=pl.BlockSpec((1, tm, D), tok),
        out_shape=jax.ShapeDtypeStruct((B, N, D), F32),
        compiler_params=_params("parallel", "parallel"),
        name="mix_out",
    )(attn, hf, hr, proj, proj, w_out.astype(BF16))


def _softplus(z):
    return jnp.maximum(z, 0.0) + jnp.log(1.0 + jnp.exp(-jnp.abs(z)))


def _nt_dot(wt, x):
    return lax.dot_general(wt, x, (((1,), (1,)), ((), ())), preferred_element_type=F32)


def _rwkv_prep_kernel(hp_ref, h_ref, hn_ref, mu_ref, wr_ref, wk_ref, wv_ref, w1_ref, w2_ref, a1_ref,
                      a2_ref, g1_ref, g2_ref, w0_ref, a0_ref, ka_ref, rk_ref,
                      dec_ref, aa_ref, k_ref, v_ref, r_ref, g_ref, bc_ref, *, B, ctx_tiles, tiles):
    i = pl.program_id(0)
    D = h_ref.shape[1]
    tm = h_ref.shape[0]
    h = h_ref[...]
    seq_start = (i == 0) | (i == ctx_tiles)
    seq_end = (i == ctx_tiles - 1) | (i == tiles - 1)
    hp = jnp.where(seq_start, 0.0, hp_ref[...])
    hn = jnp.where(seq_end, 0.0, hn_ref[...])
    xx = 0.5 * (jnp.concatenate([hp, h[:tm - B]], axis=0) + jnp.concatenate([h[B:], hn], axis=0)) - h

    def mix(j):
        return (h + xx * mu_ref[j:j + 1, :]).astype(BF16)

    r = _nt_dot(wr_ref[...], mix(0))
    k = _nt_dot(wk_ref[...], mix(2))
    v = _nt_dot(wv_ref[...], mix(3))
    lw = jnp.tanh(_nt_dot(w1_ref[...], mix(1))).astype(BF16)
    wpre = jnp.dot(w2_ref[...], lw, preferred_element_type=F32)
    la = _nt_dot(a1_ref[...], mix(4)).astype(BF16)
    apre = jnp.dot(a2_ref[...], la, preferred_element_type=F32)
    gg = jax.nn.sigmoid(_nt_dot(g1_ref[...], mix(5))).astype(BF16)
    g_ref[...] = jnp.dot(g2_ref[...], gg, preferred_element_type=F32)
    hpg = LANES // B
    hd = RWKV_HD

    def to_scan_layout(x, ref):
        for g in range(RWKV_H // hpg):
            for c in range(tm // LANES):
                tiles = [x[(g * hpg + hh) * hd:(g * hpg + hh + 1) * hd, c * LANES:(c + 1) * LANES]
                         for hh in range(hpg)]
                for tl, tile in enumerate(_slab_transpose(tiles, B)):
                    ref[g, c * hpg + tl] = tile

    iclr = []
    for d in range(2):
        w_log = -_softplus(-(w0_ref[d] + wpre[d * D:(d + 1) * D])) - 0.5
        to_scan_layout(jnp.exp(-jnp.exp(w_log)), dec_ref.at[d])
        a = jax.nn.sigmoid(a0_ref[d] + apre[d * D:(d + 1) * D])
        iclr.append(a)
        to_scan_layout(a, aa_ref.at[d])
    kd_sum = k * (2.0 + (iclr[0] + iclr[1] - 2.0) * ka_ref[...])
    bc = jnp.sum((r * kd_sum * rk_ref[...]).reshape(RWKV_H, hd, tm), axis=1)
    for g in range(RWKV_H // hpg):
        for c in range(tm // LANES):
            rows = [jnp.broadcast_to(bc[g * hpg + hh:g * hpg + hh + 1, c * LANES:(c + 1) * LANES], (8, LANES))
                    for hh in range(hpg)]
            for tl, tile in enumerate(_slab_transpose(rows, B)):
                bc_ref[g, c * hpg + tl] = tile[0:1]
    to_scan_layout(k, k_ref)
    to_scan_layout(v, v_ref)
    to_scan_layout(r, r_ref)


def _slab_transpose(tiles, B):
    n = len(tiles)
    tiles = list(tiles)
    slab = lax.broadcasted_iota(jnp.int32, tiles[0].shape, 1) // B
    s = n // 2
    while s >= 1:
        upper = (slab & s) != 0
        for i in range(n):
            if i & s == 0:
                lo, hi = tiles[i], tiles[i + s]
                tiles[i] = jnp.where(upper, pltpu.roll(hi, s * B, 1), lo)
                tiles[i + s] = jnp.where(upper, hi, pltpu.roll(lo, LANES - s * B, 1))
        s //= 2
    return tiles


def _const_spec(shape):
    nd = len(shape)
    return pl.BlockSpec(shape, lambda i: (0,) * nd, pipeline_mode=pl.Buffered(1))


def _rwkv_prep(h_tm, B, Lc, mu, w_rkv, w0, w1, w2, a0, a1, a2, g1, g2, k_a, r_k):
    TB, D = h_tm.shape
    tm = 256
    tiles, ctx_tiles = TB // tm, Lc * B // tm
    hb = tm // B
    T = TB // B
    last_hb = T - 1
    G = RWKV_H // (LANES // B)
    hd = RWKV_HD

    def t_bf16(w):
        return w.T.astype(BF16)

    zeros_w = jnp.zeros_like(w2[0].T)
    w2t = jnp.concatenate([jnp.concatenate([w2[0].T, zeros_w], 1),
                           jnp.concatenate([zeros_w, w2[1].T], 1)], 0).astype(BF16)
    zeros_a = jnp.zeros_like(a2[0].T)
    a2t = jnp.concatenate([jnp.concatenate([a2[0].T, zeros_a], 1),
                           jnp.concatenate([zeros_a, a2[1].T], 1)], 0).astype(BF16)
    consts = [mu, t_bf16(w_rkv[0]), t_bf16(w_rkv[1]), t_bf16(w_rkv[2]),
              t_bf16(jnp.concatenate([w1[0], w1[1]], 1)), w2t,
              t_bf16(jnp.concatenate([a1[0], a1[1]], 1)), a2t, t_bf16(g1), t_bf16(g2),
              w0.reshape(2, D, 1), a0.reshape(2, D, 1), k_a.reshape(D, 1), r_k.reshape(D, 1)]
    dir_spec = pl.BlockSpec((2, G, hb, hd, LANES), lambda i: (0, 0, i, 0, 0))
    all_spec = pl.BlockSpec((G, hb, hd, LANES), lambda i: (0, i, 0, 0))
    dir_shape = jax.ShapeDtypeStruct((2, G, T, hd, LANES), F32)
    all_shape = jax.ShapeDtypeStruct((G, T, hd, LANES), F32)
    return pl.pallas_call(
        functools.partial(_rwkv_prep_kernel, B=B, ctx_tiles=ctx_tiles, tiles=tiles),
        grid=(tiles,),
        in_specs=[pl.BlockSpec((B, D), lambda i: (jnp.maximum(i * hb - 1, 0), 0)),
                  pl.BlockSpec((tm, D), lambda i: (i, 0)),
                  pl.BlockSpec((B, D), lambda i: (jnp.minimum((i + 1) * hb, last_hb), 0))]
                 + [_const_spec(c.shape) for c in consts],
        out_specs=[dir_spec, dir_spec, all_spec, all_spec, all_spec,
                   pl.BlockSpec((D, tm), lambda i: (0, i)),
                   pl.BlockSpec((G, hb, 1, LANES), lambda i: (0, i, 0, 0))],
        out_shape=[dir_shape, dir_shape, all_shape, all_shape, all_shape,
                   jax.ShapeDtypeStruct((D, TB), F32),
                   jax.ShapeDtypeStruct((G, T, 1, LANES), F32)],
        compiler_params=_params("parallel"),
        name="rwkv_prep",
    )(h_tm, h_tm, h_tm, *consts)


def _wkv_kernel(dec_ref, aa_ref, k_ref, v_ref, r_ref, kkc_ref, kac_ref, y_ref,
                s_ref, g_ref, p_ref, q_ref, sa_ref, *, Tc, nC):
    d = pl.program_id(0)
    j = pl.program_id(2)
    hd = RWKV_HD
    sub = 8

    @pl.when(j == 0)
    def _():
        s_ref[...] = jnp.zeros_like(s_ref)

    def partial_rows(x):
        return jnp.sum(x.reshape(hd // sub, sub, LANES), axis=0)

    pitch = hd + 1

    def put_partial(ref, v, x):
        ref[pl.ds(v, sub, stride=pitch), :] = partial_rows(x)

    def finish_rows(ref):
        acc = ref[pl.ds(0, hd), :]
        for q in range(1, sub):
            acc = acc + ref[pl.ds(q * pitch, hd), :]
        return acc

    def run(with_y):
        g_ref[...] = jnp.ones_like(g_ref)

        def step(s, _):
            tt = jnp.where(d == 0, s, Tc - 1 - s)
            aa = aa_ref[0, 0, tt]
            kk = k_ref[0, tt]
            kf = kk * kkc_ref[0]
            kn = kf * lax.rsqrt(jnp.maximum(jnp.sum(kf * kf, axis=0, keepdims=True), 1e-24))
            g_prev = g_ref[...]
            g = g_prev * dec_ref[0, 0, tt]
            g_inv = 1.0 / g
            g_ref[...] = g
            a_t = -(kn * g_prev)
            b_t = kn * aa * g_inv
            k_t = kk * (1.0 + (aa - 1.0) * kac_ref[0]) * g_inv
            r_t = r_ref[0, tt] * g if with_y else None

            def row_reduce(v, _):
                st = s_ref[v]
                put_partial(p_ref, v, st * a_t)
                if with_y:
                    put_partial(q_ref, v, st * r_t)
                return 0

            lax.fori_loop(0, hd, row_reduce, 0, unroll=8)
            sa = finish_rows(p_ref)
            sa_ref[...] = sa
            if with_y:
                b_r = jnp.sum(b_t * r_t, axis=0, keepdims=True)
                k_r = jnp.sum(k_t * r_t, axis=0, keepdims=True)
                y_ref[0, 0, tt] = finish_rows(q_ref) + sa * b_r + v_ref[0, tt] * k_r

            def row_update(v, _):
                s_ref[v] = s_ref[v] + sa_ref[pl.ds(v, 1), :] * b_t + v_ref[0, tt, pl.ds(v, 1), :] * k_t
                return 0

            lax.fori_loop(0, hd, row_update, 0, unroll=8)
            return 0

        lax.fori_loop(0, Tc, step, 0)

        def rescale(v, _):
            s_ref[v] = s_ref[v] * g_ref[...]
            return 0

        lax.fori_loop(0, hd, rescale, 0, unroll=8)

    @pl.when(j < nC)
    def _():
        run(False)

    @pl.when(j >= nC)
    def _():
        run(True)


def _wkv(dec, aa, k, v, r, k_k, k_a, B, Lc):
    _, G, T, hd, _ = dec.shape
    Tc = WKV_TC
    S = T - Lc
    nC, nL = Lc // Tc, S // Tc

    def tmap(d, j):
        return jnp.where(d == 0, j, _seg_rev_block(j, nC, nL))

    def lmap(d, j):
        jj = jnp.maximum(j - nC, 0)
        return jnp.where(d == 0, jj, nL - 1 - jj)

    dir_spec = pl.BlockSpec((1, 1, Tc, hd, LANES), lambda d, g, j: (d, g, tmap(d, j), 0, 0))
    all_spec = pl.BlockSpec((1, Tc, hd, LANES), lambda d, g, j: (g, tmap(d, j), 0, 0))
    const_spec = pl.BlockSpec((1, hd, LANES), lambda d, g, j: (g, 0, 0))
    return pl.pallas_call(
        functools.partial(_wkv_kernel, Tc=Tc, nC=nC),
        grid=(2, G, nC + nL),
        in_specs=[dir_spec, dir_spec, all_spec, all_spec, all_spec, const_spec, const_spec],
        out_specs=pl.BlockSpec((1, 1, Tc, hd, LANES), lambda d, g, j: (d, g, lmap(d, j), 0, 0)),
        out_shape=jax.ShapeDtypeStruct((2, G, S, hd, LANES), F32),
        scratch_shapes=[pltpu.VMEM((hd, hd, LANES), F32),
                        pltpu.VMEM((hd, LANES), F32),
                        pltpu.VMEM(((hd + 1) * 8, LANES), F32),
                        pltpu.VMEM(((hd + 1) * 8, LANES), F32),
                        pltpu.VMEM((hd, LANES), F32)],
        compiler_params=_params("parallel", "parallel", "arbitrary"),
        name="wkv",
    )(dec, aa, k, v, r, _lane_const(k_k, B), _lane_const(k_a, B))


def _lane_const(c, B):
    hpg = LANES // B
    return jnp.repeat(c.reshape(RWKV_H // hpg, hpg, RWKV_HD).transpose(0, 2, 1), B, axis=-1)


def _rwkv_finish_kernel(y_ref, v_ref, bc_ref, g_ref, gng_ref, gnb_ref, wo_ref, o_ref, *, B):
    G, steps, hd, _ = v_ref.shape
    hpg = LANES // B
    rows = []
    for g in range(G):
        head_cols = [[] for _ in range(hpg)]
        for c in range(steps // hpg):
            tiles = []
            for tl in range(hpg):
                t = c * hpg + tl
                y = y_ref[0, g, t] + y_ref[1, g, t]
                mean = jnp.mean(y, axis=0, keepdims=True)
                var = jnp.mean(jnp.square(y - mean), axis=0, keepdims=True)
                yn = (y - mean) * lax.rsqrt(var + GN_EPS)
                tiles.append(yn * gng_ref[g] + gnb_ref[g] + bc_ref[g, t] * v_ref[g, t])
            for hh, tile in enumerate(_slab_transpose(tiles, B)):
                head_cols[hh].append(tile)
        rows.extend(jnp.concatenate(cols, axis=1) for cols in head_cols)
    o = (jnp.concatenate(rows, axis=0) * g_ref[...]).astype(BF16)
    out_t = jnp.dot(wo_ref[...], o, preferred_element_type=F32)
    o_ref[...] = out_t.T


def _rwkv_finish(y2, v, bc, g, gn_g, gn_b, w_o, B, Lc):
    _, G, S, hd, _ = y2.shape
    D = g.shape[0]
    tm = 256
    steps = tm // B
    off = Lc // steps
    return pl.pallas_call(
        functools.partial(_rwkv_finish_kernel, B=B),
        grid=(S // steps,),
        in_specs=[pl.BlockSpec((2, G, steps, hd, LANES), lambda i: (0, 0, i, 0, 0)),
                  pl.BlockSpec((G, steps, hd, LANES), lambda i: (0, i + off, 0, 0)),
                  pl.BlockSpec((G, steps, 1, LANES), lambda i: (0, i + off, 0, 0)),
                  pl.BlockSpec((D, tm), lambda i: (0, i + off)),
                  _const_spec((G, hd, LANES)), _const_spec((G, hd, LANES)), _const_spec((D, D))],
        out_specs=pl.BlockSpec((tm, D), lambda i: (i, 0)),
        out_shape=jax.ShapeDtypeStruct((S * B, D), F32),
        compiler_params=_params("parallel"),
        name="rwkv_finish",
    )(y2, v, bc, g, _lane_const(gn_g, B), _lane_const(gn_b, B), w_o.T.astype(BF16))


def _rwkv7_mixer(hl, hc, mu, w_rkv, w0, w1, w2, a0, a1, a2, g1, g2, k_k, k_a, r_k, gn_g, gn_b, w_o):
    B, S, D = hl.shape
    Lc = hc.shape[1]
    h_tm = jnp.concatenate([hc.transpose(1, 0, 2), hl.transpose(1, 0, 2)], axis=0).reshape((Lc + S) * B, D)
    dec, aa, k, v, r, g, bc = _rwkv_prep(h_tm, B, Lc, mu, w_rkv, w0, w1, w2, a0, a1, a2, g1, g2,
                                         k_a, r_k.reshape(D))
    y2 = _wkv(dec, aa, k, v, r, k_k, k_a, B, Lc)
    y_tm = _rwkv_finish(y2, v, bc, g, gn_g, gn_b, w_o, B, Lc)
    return y_tm.reshape(S, B, D).transpose(1, 0, 2)


def _moe_kernel(idx_ref, gate_ref, h_ref, w1_ref, w3_ref, w2_ref, o_ref, xin_ref, y_ref, *, cap, chunks):
    bb = h_ref.shape[0]

    @pl.when(pl.program_id(1) == 0)
    def _():
        o_ref[...] = jnp.zeros_like(o_ref)

    def token_rows(t):
        return pl.ds(pl.multiple_of(t * chunks, chunks), chunks)

    for bi in range(bb):
        def gather(r, _, bi=bi):
            slot = bi * cap + r
            xin_ref[token_rows(slot), :] = h_ref[bi, token_rows(idx_ref[0, 0, slot]), :]
            return 0

        lax.fori_loop(0, cap, gather, 0, unroll=8)
    x = _from_chunk_rows(xin_ref, bb * cap).astype(BF16)
    h1 = jnp.dot(x, w1_ref[0], preferred_element_type=F32)
    h3 = jnp.dot(x, w3_ref[0], preferred_element_type=F32)
    hid = (h1 * jax.nn.sigmoid(h1) * h3).astype(BF16)
    _to_chunk_rows(y_ref, jnp.dot(hid, w2_ref[0], preferred_element_type=F32) * gate_ref[0])

    batch = 8
    for bi in range(bb):
        def scatter_add(i, _, bi=bi):
            slots = [bi * cap + i * batch + u for u in range(batch)]
            rows = [token_rows(idx_ref[0, 0, s]) for s in slots]
            sums = [o_ref[bi, rows[u], :] + y_ref[token_rows(slots[u]), :] for u in range(batch)]
            for u in range(batch):
                o_ref[bi, rows[u], :] = sums[u]
            return 0

        lax.fori_loop(0, cap // batch, scatter_add, 0)


def _moe(h, aff, w1, w3, w2):
    B, _, N = aff.shape
    E, D, FF = w1.shape
    chunks = D // LANES
    cap = CAPACITY * N // E
    bb = max(1, min(B, MOE_ROWS // cap))
    gate, idx = lax.top_k(aff, cap)

    def group(t):
        return t.reshape(B // bb, bb, E, cap).transpose(0, 2, 1, 3).reshape(B // bb * E, bb * cap)

    rows = bb * cap
    return pl.pallas_call(
        functools.partial(_moe_kernel, cap=cap, chunks=chunks),
        grid=(B // bb, E),
        in_specs=[pl.BlockSpec((1, 1, rows), lambda b, e: (b * E + e, 0, 0), memory_space=pltpu.SMEM),
                  pl.BlockSpec((1, rows, 1), lambda b, e: (b * E + e, 0, 0)),
                  pl.BlockSpec((bb, N * chunks, LANES), lambda b, e: (b, 0, 0), pipeline_mode=pl.Buffered(1)),
                  pl.BlockSpec((1, D, FF), lambda b, e: (e, 0, 0)),
                  pl.BlockSpec((1, D, FF), lambda b, e: (e, 0, 0)),
                  pl.BlockSpec((1, FF, D), lambda b, e: (e, 0, 0))],
        out_specs=pl.BlockSpec((bb, N * chunks, LANES), lambda b, e: (b, 0, 0)),
        out_shape=jax.ShapeDtypeStruct((B, N * chunks, LANES), F32),
        scratch_shapes=[pltpu.VMEM((rows * chunks, LANES), F32), pltpu.VMEM((rows * chunks, LANES), F32)],
        compiler_params=_params("parallel", "arbitrary"),
        name="moe",
    )(group(idx.astype(jnp.int32))[:, None, :], group(gate)[:, :, None], h, w1, w3, w2)


def _block_diag(w):
    H, bi, bj = w.shape
    eye = jnp.eye(H, dtype=w.dtype)
    return jnp.einsum('hij,hg->higj', w, eye).reshape(H * bi, H * bj)


def _dwconv(u, w, b):
    n = u.shape[1]
    up = jnp.pad(u, ((0, 0), (CONV_LEFT, CONV_W - 1 - CONV_LEFT), (0, 0)))
    out = up[:, 0:n] * w[0]
    for j in range(1, CONV_W):
        out = out + up[:, j:j + n] * w[j]
    return out + b


def _attn_lru_mixer(proj_l, proj_c, cos, sin, w_out, sink, conv_w, conv_b, wa, ba, wi, bi, lam):
    Lc = proj_c.shape[1]
    qk = _rope(proj_l, cos, sin)
    attn_l = _attention(sink, qk, proj_l, proj_c)
    attn_c = _ctx_attention(sink, proj_c)

    u0, g0 = ATTN_W + 2 * KV_W, ATTN_W + 2 * KV_W + LRU_W
    u = jnp.concatenate([_dwconv(proj_c[..., u0:g0], conv_w, conv_b),
                         _dwconv(proj_l[..., u0:g0], conv_w, conv_b)], axis=1)
    hf, hr = _lru(u, Lc, wa, ba, wi, bi, lam)
    yl = _mix_out(attn_l, hf, hr, proj_l, w_out, Lc)
    yc = _mix_out(attn_c, hf, hr, proj_c, w_out, 0)
    return yl, yc


def kernel(x, c, ctx, c_ctx, mod_w, mod_b, norm_mix, norm_ffn, router_w, exp_w1, exp_w3, exp_w2, mix_in, mix_out, attn_sink, lru_conv_w, lru_conv_b, lru_wa, lru_ba, lru_wi, lru_bi, lru_lam, rw_mu, rw_rkv, rw_w0, rw_w1, rw_w2, rw_a0, rw_a1, rw_a2, rw_g1, rw_g2, rw_kk, rw_ka, rw_rk, rw_gn_g, rw_gn_b, rw_wo, final_norm):
    B, S, D = x.shape
    Lc = ctx.shape[1]
    depth = mod_w.shape[0]
    assert depth == 2 and S % QBLOCK == 0
    assert LANES % B == 0 and RWKV_H % (LANES // B) == 0 and B % 8 == 0
    assert Lc % WKV_TC == 0 and S % WKV_TC == 0 and (Lc * B) % 256 == 0
    cos, sin = _rope_tables(S)

    n_rows = -(-(B + 1) // 8) * 8
    cond = jnp.concatenate([jax.nn.silu(c), jax.nn.silu(c_ctx)[None],
                            jnp.zeros((n_rows - B - 1, D), F32)], axis=0)

    mods_l, mods_c = [], []
    for layer in range(depth):
        mod = _mm(cond, mod_w[layer], tm=n_rows, tn=1024, precise=True) + mod_b[layer]
        mods_l.append([t[:, None, :] for t in jnp.split(mod[:B], 6, axis=-1)])
        mods_c.append([jnp.broadcast_to(t[None], (B, 1, D)) for t in jnp.split(mod[B:B + 1], 6, axis=-1)])

    xl, xc = x, ctx
    hl = hc = None
    for layer in range(depth):
        last = layer == depth - 1
        m_l, m_c = mods_l[layer], mods_c[layer]
        if layer % 2 == 0:
            i = layer // 2
            assert layer == 0
            w_in = mix_in[i].astype(BF16)
            proj_l = _norm_mm(xl, norm_mix[layer], m_l[0], m_l[1], w_in)
            proj_c = _norm_mm(xc, norm_mix[layer], m_c[0], m_c[1], w_in)
            yl, yc = _attn_lru_mixer(proj_l, proj_c, cos, sin, mix_out[i], attn_sink[i], lru_conv_w[i],
                                     lru_conv_b[i], lru_wa[i], lru_ba[i], lru_wi[i], lru_bi[i], lru_lam[i])
        else:
            i = layer // 2
            yl = _rwkv7_mixer(hl, hc, rw_mu[i], rw_rkv[i], rw_w0[i], rw_w1[i], rw_w2[i], rw_a0[i], rw_a1[i],
                              rw_a2[i], rw_g1[i], rw_g2[i], rw_kk[i], rw_ka[i], rw_rk[i], rw_gn_g[i],
                              rw_gn_b[i], rw_wo[i])
            yc = None
        w1b, w3b, w2b = exp_w1[layer].astype(BF16), exp_w3[layer].astype(BF16), exp_w2[layer].astype(BF16)
        xl, hl, aff_l = _resid_norm(xl, yl, m_l[2], norm_ffn[layer], m_l[3], m_l[4], out_rows=True,
                                    router_w=router_w[layer])
        moe_l = _moe(hl, aff_l, w1b, w3b, w2b)
        if last:
            zero = jnp.zeros((B, 1, D), F32)
            return _resid_norm(xl, moe_l, m_l[5], final_norm, zero, zero, delta_rows=True, want_x=False)
        n_l, n_c = mods_l[layer + 1], mods_c[layer + 1]
        xl, hl = _resid_norm(xl, moe_l, m_l[5], norm_mix[layer + 1], n_l[0], n_l[1], delta_rows=True)
        xc, hc, aff_c = _resid_norm(xc, yc, m_c[2], norm_ffn[layer], m_c[3], m_c[4], out_rows=True,
                                    router_w=router_w[layer])
        moe_c = _moe(hc, aff_c, w1b, w3b, w2b)
        xc, hc = _resid_norm(xc, moe_c, m_c[5], norm_mix[layer + 1], n_c[0], n_c[1], delta_rows=True)
```

```python
import functools
import math

import jax
import jax.numpy as jnp
from jax import lax
from jax.experimental import pallas as pl
from jax.experimental.pallas import tpu as pltpu

F32 = jnp.float32
BF16 = jnp.bfloat16

GRID_W = 64
N_HEADS_ATTN = 8
N_KV = 2
HEAD_DIM = 64
AXIS_DIM = HEAD_DIM // 2
ATTN_W = N_HEADS_ATTN * HEAD_DIM
KV_W = N_KV * HEAD_DIM
WINDOW = 128
QBLOCK = 128
ROPE_BASE = 10000.0
LRU_W = 512
LRU_BLOCKS = 8
CONV_W = 4
CONV_LEFT = 2
LRU_C = 8.0
RWKV_H = 16
RWKV_HD = 64
GN_EPS = 64e-5
N_EXPERTS = 16
CAPACITY = 2
NORM_EPS = 1e-6
NEG_INF = -1e30

LANES = 128
VMEM_LIMIT_BYTES = 56 * 1024 * 1024
WKV_TC = 32
MOE_ROWS = 256

def _params(*sem):
    return pltpu.CompilerParams(dimension_semantics=sem, vmem_limit_bytes=VMEM_LIMIT_BYTES)


def _mm_kernel(x_ref, w_ref, o_ref, *, precise):
    if precise:
        o_ref[...] = jnp.dot(x_ref[...], w_ref[...], precision=lax.Precision.HIGHEST,
                             preferred_element_type=F32)
    else:
        o_ref[...] = jnp.dot(x_ref[...].astype(BF16), w_ref[...].astype(BF16),
                             preferred_element_type=F32)


def _mm(x, w, *, tm=512, tn=None, precise=False):
    M, K = x.shape
    N = w.shape[1]
    tm = min(tm, M)
    tn = N if tn is None else min(tn, N)
    assert M % tm == 0 and N % tn == 0, (M, tm, N, tn)
    return pl.pallas_call(
        functools.partial(_mm_kernel, precise=precise),
        grid=(M // tm, N // tn),
        in_specs=[pl.BlockSpec((tm, K), lambda i, j: (i, 0)),
                  pl.BlockSpec((K, tn), lambda i, j: (0, j))],
        out_specs=pl.BlockSpec((tm, tn), lambda i, j: (i, j)),
        out_shape=jax.ShapeDtypeStruct((M, N), F32),
        compiler_params=_params("parallel", "parallel"),
        name="mm",
    )(x, w)


def _resid_norm_kernel(*refs, has_delta, delta_rows, want_x, out_rows, route):
    refs = list(refs)
    x_ref = refs.pop(0)
    d_ref, gate_ref = (refs.pop(0), refs.pop(0)) if has_delta else (None, None)
    g_ref, sh_ref, sc_ref = refs.pop(0), refs.pop(0), refs.pop(0)
    wr_ref = refs.pop(0) if route else None
    xo_ref = refs.pop(0) if (has_delta and want_x) else None
    h_ref = refs.pop(0)
    aff_ref = refs.pop(0) if route else None
    x = x_ref[0]
    if has_delta:
        delta = _from_chunk_rows(d_ref.at[0], x.shape[0]) if delta_rows else d_ref[0]
        x = x + gate_ref[0] * delta
        if want_x:
            xo_ref[0] = x
    y = x * lax.rsqrt(jnp.mean(x * x, axis=-1, keepdims=True) + NORM_EPS) * g_ref[...]
    h = y * (1.0 + sc_ref[0]) + sh_ref[0]
    if out_rows:
        _to_chunk_rows(h_ref.at[0], h)
    else:
        h_ref[0] = h
    if route:
        logits = lax.dot_general(wr_ref[...], h, (((1,), (1,)), ((), ())),
                                 precision=lax.Precision.HIGHEST, preferred_element_type=F32)
        e = jnp.exp(logits - jnp.max(logits, axis=0, keepdims=True))
        aff_ref[0] = e / jnp.sum(e, axis=0, keepdims=True)


def _from_chunk_rows(ref, n):
    chunks = ref.shape[0] // n
    return jnp.concatenate([ref[pl.ds(j, n, stride=chunks), :] for j in range(chunks)], axis=-1)


def _to_chunk_rows(ref, x):
    n = x.shape[0]
    chunks = x.shape[1] // LANES
    for j in range(chunks):
        ref[pl.ds(j, n, stride=chunks), :] = x[:, j * LANES:(j + 1) * LANES]


def _resid_norm(x, delta, gate, g, shift, scale, *, delta_rows=False, want_x=True, out_rows=False,
                router_w=None):
    B, N, D = x.shape
    ts = min(512, N)
    has_delta = delta is not None
    chunks = D // LANES
    tok = pl.BlockSpec((1, ts, D), lambda b, i: (b, i, 0))
    tok_rows = pl.BlockSpec((1, ts * chunks, LANES), lambda b, i: (b, i, 0))
    per_b = pl.BlockSpec((1, 1, D), lambda b, i: (b, 0, 0))
    in_specs, args = [tok], [x]
    if has_delta:
        in_specs += [tok_rows if delta_rows else tok, per_b]
        args += [delta, gate]
    in_specs += [pl.BlockSpec((1, D), lambda b, i: (0, 0)), per_b, per_b]
    args += [g.reshape(1, D), shift, scale]
    route = router_w is not None
    if route:
        E = router_w.shape[1]
        in_specs.append(pl.BlockSpec((E, D), lambda b, i: (0, 0)))
        args.append(router_w.T)
    out_specs, out_shape = [], []
    if has_delta and want_x:
        out_specs.append(tok)
        out_shape.append(jax.ShapeDtypeStruct((B, N, D), F32))
    out_specs.append(tok_rows if out_rows else tok)
    out_shape.append(jax.ShapeDtypeStruct((B, N * chunks, LANES) if out_rows else (B, N, D), F32))
    if route:
        out_specs.append(pl.BlockSpec((1, E, ts), lambda b, i: (b, 0, i)))
        out_shape.append(jax.ShapeDtypeStruct((B, E, N), F32))
    outs = pl.pallas_call(
        functools.partial(_resid_norm_kernel, has_delta=has_delta, delta_rows=delta_rows, want_x=want_x,
                          out_rows=out_rows, route=route),
        grid=(B, N // ts),
        in_specs=in_specs,
        out_specs=out_specs,
        out_shape=out_shape,
        compiler_params=_params("parallel", "parallel"),
        name="resid_norm",
    )(*args)
    return outs if len(outs) > 1 else outs[0]


def _resid_norm_tm_kernel(*refs, delta_tm, h_tm, route):
    refs = list(refs)
    x_ref, d_ref, gate_ref, g_ref, sh_ref, sc_ref = (refs.pop(0) for _ in range(6))
    wr_ref = refs.pop(0) if route else None
    xo_ref, h_ref = refs.pop(0), refs.pop(0)
    aff_ref = refs.pop(0) if route else None
    nb, ts, _ = x_ref.shape
    for bi in range(nb):
        delta = d_ref[:, bi, :] if delta_tm else _from_chunk_rows(d_ref.at[bi], ts)
        x = x_ref[bi] + gate_ref[bi] * delta
        xo_ref[bi] = x
        y = x * lax.rsqrt(jnp.mean(x * x, axis=-1, keepdims=True) + NORM_EPS) * g_ref[...]
        h = y * (1.0 + sc_ref[bi]) + sh_ref[bi]
        if h_tm:
            h_ref[:, bi, :] = h
        else:
            _to_chunk_rows(h_ref.at[bi], h)
        if route:
            logits = lax.dot_general(wr_ref[...], h, (((1,), (1,)), ((), ())),
                                     precision=lax.Precision.HIGHEST, preferred_element_type=F32)
            e = jnp.exp(logits - jnp.max(logits, axis=0, keepdims=True))
            aff_ref[bi] = e / jnp.sum(e, axis=0, keepdims=True)


def _resid_norm_tm(x, delta, gate, g, shift, scale, *, delta_tm, h_tm, router_w=None):
    B, N, D = x.shape
    nb, ts = 8, min(LANES, N)
    chunks = D // LANES
    tok = pl.BlockSpec((nb, ts, D), lambda b, i: (b, i, 0))
    tok_rows = pl.BlockSpec((nb, ts * chunks, LANES), lambda b, i: (b, i, 0))
    tok_tm = pl.BlockSpec((ts, nb, D), lambda b, i: (i, b, 0))
    per_b = pl.BlockSpec((nb, 1, D), lambda b, i: (b, 0, 0))
    in_specs = [tok, tok_tm if delta_tm else tok_rows, per_b, pl.BlockSpec((1, D), lambda b, i: (0, 0)),
                per_b, per_b]
    args = [x, delta, gate, g.reshape(1, D), shift, scale]
    out_specs = [tok, tok_tm if h_tm else tok_rows]
    out_shape = [jax.ShapeDtypeStruct((B, N, D), F32),
                 jax.ShapeDtypeStruct((N, B, D) if h_tm else (B, N * chunks, LANES), F32)]
    route = router_w is not None
    if route:
        E = router_w.shape[1]
        in_specs.append(pl.BlockSpec((E, D), lambda b, i: (0, 0)))
        args.append(router_w.T)
        out_specs.append(pl.BlockSpec((nb, E, ts), lambda b, i: (b, 0, i)))
        out_shape.append(jax.ShapeDtypeStruct((B, E, N), F32))
    return pl.pallas_call(
        functools.partial(_resid_norm_tm_kernel, delta_tm=delta_tm, h_tm=h_tm, route=route),
        grid=(B // nb, N // ts),
        in_specs=in_specs,
        out_specs=out_specs,
        out_shape=out_shape,
        compiler_params=_params("parallel", "parallel"),
        name="resid_norm_tm",
    )(*args)


def _norm_mm_kernel(x_ref, g_ref, sh_ref, sc_ref, w_ref, o_ref):
    x = x_ref[0]
    y = x * lax.rsqrt(jnp.mean(x * x, axis=-1, keepdims=True) + NORM_EPS) * g_ref[...]
    h = (y * (1.0 + sc_ref[0]) + sh_ref[0]).astype(BF16)
    o_ref[0] = jnp.dot(h, w_ref[...], preferred_element_type=F32)


def _norm_mm(x, g, shift, scale, w):
    B, N, D = x.shape
    M = w.shape[1]
    ts = min(512, N)
    per_b = pl.BlockSpec((1, 1, D), lambda b, i: (b, 0, 0))
    return pl.pallas_call(
        _norm_mm_kernel,
        grid=(B, N // ts),
        in_specs=[pl.BlockSpec((1, ts, D), lambda b, i: (b, i, 0)),
                  pl.BlockSpec((1, D), lambda b, i: (0, 0)), per_b, per_b,
                  pl.BlockSpec((D, M), lambda b, i: (0, 0))],
        out_specs=pl.BlockSpec((1, ts, M), lambda b, i: (b, i, 0)),
        out_shape=jax.ShapeDtypeStruct((B, N, M), F32),
        compiler_params=_params("parallel", "parallel"),
        name="norm_mm",
    )(x, g.reshape(1, D), shift, scale, w)


def _rope_kernel(x_ref, cos_ref, sin_ref, o_ref):
    width = x_ref.shape[-1]
    for c in range(width // LANES):
        sl = slice(c * LANES, (c + 1) * LANES)
        x = x_ref[0, :, sl]
        lane = lax.broadcasted_iota(jnp.int32, x.shape, 1)
        first_half = (lane % AXIS_DIM) < (AXIS_DIM // 2)
        partner = jnp.where(first_half, pltpu.roll(x, LANES - AXIS_DIM // 2, 1),
                            pltpu.roll(x, AXIS_DIM // 2, 1))
        o_ref[0, :, sl] = x * cos_ref[:, sl] + partner * sin_ref[:, sl]


def _rope_tables(S):
    rows = S // GRID_W
    row = jnp.repeat(jnp.arange(rows), GRID_W).astype(F32)
    col = jnp.tile(jnp.arange(GRID_W), rows).astype(F32)
    inv = ROPE_BASE ** (-jnp.arange(0, AXIS_DIM, 2, dtype=F32) / AXIS_DIM)
    ang_row, ang_col = row[:, None] * inv, col[:, None] * inv

    def axis_tables(ang):
        c, s = jnp.cos(ang), jnp.sin(ang)
        return jnp.concatenate([c, c], -1), jnp.concatenate([-s, s], -1)

    cr, sr = axis_tables(ang_row)
    cc, sc = axis_tables(ang_col)
    cos_h = jnp.concatenate([cr, cc], -1)
    sin_h = jnp.concatenate([sr, sc], -1)
    q_scale = HEAD_DIM ** -0.5
    cos = jnp.concatenate([jnp.tile(cos_h, (1, N_HEADS_ATTN)) * q_scale, jnp.tile(cos_h, (1, N_KV))], -1)
    sin = jnp.concatenate([jnp.tile(sin_h, (1, N_HEADS_ATTN)) * q_scale, jnp.tile(sin_h, (1, N_KV))], -1)
    return cos, sin


def _rope(proj, cos, sin):
    B, S, _ = proj.shape
    width = ATTN_W + KV_W
    ts = min(512, S)
    return pl.pallas_call(
        _rope_kernel,
        grid=(B, S // ts),
        in_specs=[pl.BlockSpec((1, ts, width), lambda b, i: (b, i, 0)),
                  pl.BlockSpec((ts, width), lambda b, i: (i, 0)),
                  pl.BlockSpec((ts, width), lambda b, i: (i, 0))],
        out_specs=pl.BlockSpec((1, ts, width), lambda b, i: (b, i, 0)),
        out_shape=jax.ShapeDtypeStruct((B, S, width), F32),
        compiler_params=_params("parallel", "parallel"),
        name="rope",
    )(proj, cos, sin)


def _attn_heads(q, kcat, vcat, valid, sink_ref):
    outs = []
    G = N_HEADS_ATTN // N_KV
    Q = q.shape[0]
    for j in range(N_KV):
        kj = kcat[:, j * HEAD_DIM:(j + 1) * HEAD_DIM]
        vj = vcat[:, j * HEAD_DIM:(j + 1) * HEAD_DIM]
        heads = range(j * G, (j + 1) * G)
        qs = jnp.concatenate([q[:, h * HEAD_DIM:(h + 1) * HEAD_DIM] for h in heads], axis=0).astype(BF16)
        s = lax.dot_general(qs, kj, (((1,), (1,)), ((), ())), preferred_element_type=F32)
        if valid is not None:
            s = jnp.where(valid[None], s.reshape(G, Q, -1), NEG_INF).reshape(G * Q, -1)
        sink = jnp.concatenate([jnp.full((Q, 1), sink_ref[h], F32) for h in heads], axis=0)
        m = jnp.maximum(jnp.max(s, axis=-1, keepdims=True), sink)
        e = jnp.exp(s - m)
        den = jnp.sum(e, axis=-1, keepdims=True) + jnp.exp(sink - m)
        o = jnp.dot(e.astype(BF16), vj, preferred_element_type=F32) / den
        outs.extend(o[g * Q:(g + 1) * Q] for g in range(G))
    return jnp.concatenate(outs, axis=-1)


def _attn_kernel(sink_ref, q_ref, kp_ref, kc_ref, kn_ref, vp_ref, vc_ref, vn_ref, ck_ref, cv_ref,
                 o_ref, *, S, Lc):
    i = pl.program_id(1)
    kcat = jnp.concatenate([ck_ref[0], kp_ref[0], kc_ref[0], kn_ref[0]], axis=0).astype(BF16)
    vcat = jnp.concatenate([cv_ref[0], vp_ref[0], vc_ref[0], vn_ref[0]], axis=0).astype(BF16)
    L = Lc + 3 * QBLOCK
    row = lax.broadcasted_iota(jnp.int32, (QBLOCK, L), 0)
    col = lax.broadcasted_iota(jnp.int32, (QBLOCK, L), 1)
    rel = col - Lc - QBLOCK - row
    kpos = (i - 1) * QBLOCK + col - Lc
    valid = (col < Lc) | ((jnp.abs(rel) <= WINDOW) & (kpos >= 0) & (kpos < S))
    o_ref[0] = _attn_heads(q_ref[0], kcat, vcat, valid, sink_ref)


def _attention(sink, qk, proj_l, proj_c):
    B, S, _ = qk.shape
    Lc = proj_c.shape[1]
    nb = S // QBLOCK
    kcol, vcol = ATTN_W // KV_W, ATTN_W // KV_W + 1

    def blk(colblk, off):
        return pl.BlockSpec((1, QBLOCK, KV_W),
                            lambda b, i: (b, jnp.clip(i + off, 0, nb - 1), colblk))

    return pl.pallas_call(
        functools.partial(_attn_kernel, S=S, Lc=Lc),
        grid=(B, nb),
        in_specs=[pl.BlockSpec(memory_space=pltpu.SMEM),
                  pl.BlockSpec((1, QBLOCK, ATTN_W), lambda b, i: (b, i, 0)),
                  blk(kcol, -1), blk(kcol, 0), blk(kcol, 1),
                  blk(vcol, -1), blk(vcol, 0), blk(vcol, 1),
                  pl.BlockSpec((1, Lc, KV_W), lambda b, i: (b, 0, kcol)),
                  pl.BlockSpec((1, Lc, KV_W), lambda b, i: (b, 0, vcol))],
        out_specs=pl.BlockSpec((1, QBLOCK, ATTN_W), lambda b, i: (b, i, 0)),
        out_shape=jax.ShapeDtypeStruct((B, S, ATTN_W), F32),
        compiler_params=_params("parallel", "parallel"),
        name="attention",
    )(sink, qk, qk, qk, qk, proj_l, proj_l, proj_l, proj_c, proj_c)


def _ctx_attn_kernel(sink_ref, q_ref, ck_ref, cv_ref, o_ref):
    q = q_ref[0] * (HEAD_DIM ** -0.5)
    o_ref[0] = _attn_heads(q, ck_ref[0].astype(BF16), cv_ref[0].astype(BF16), None, sink_ref)


def _ctx_attention(sink, proj_c):
    B, Lc, _ = proj_c.shape
    kcol, vcol = ATTN_W // KV_W, ATTN_W // KV_W + 1
    return pl.pallas_call(
        _ctx_attn_kernel,
        grid=(B,),
        in_specs=[pl.BlockSpec(memory_space=pltpu.SMEM),
                  pl.BlockSpec((1, Lc, ATTN_W), lambda b: (b, 0, 0)),
                  pl.BlockSpec((1, Lc, KV_W), lambda b: (b, 0, kcol)),
                  pl.BlockSpec((1, Lc, KV_W), lambda b: (b, 0, vcol))],
        out_specs=pl.BlockSpec((1, Lc, ATTN_W), lambda b: (b, 0, 0)),
        out_shape=jax.ShapeDtypeStruct((B, Lc, ATTN_W), F32),
        compiler_params=_params("parallel"),
        name="ctx_attention",
    )(sink, proj_c, proj_c, proj_c)


def _seg_rev_block(j, nC, nL):
    return jnp.where(j < nC, nC - 1 - j, nC + nL - 1 - (j - nC))


def _lru_kernel(uf_ref, ur_ref, wg_ref, bg_ref, sp_ref, hf_ref, hr_ref, a_ref, b_ref, cf_ref, cr_ref, *, Tc):
    @pl.when(pl.program_id(1) == 0)
    def _():
        cf_ref[...] = jnp.zeros_like(cf_ref)
        cr_ref[...] = jnp.zeros_like(cr_ref)

    sub = 8
    row = lax.broadcasted_iota(jnp.int32, (sub, LRU_W), 0)
    for d, (u_ref, h_ref, c_ref) in enumerate(((uf_ref, hf_ref, cf_ref), (ur_ref, hr_ref, cr_ref))):
        u = u_ref[0]
        gates = jax.nn.sigmoid(jnp.dot(u.astype(BF16), wg_ref[d], preferred_element_type=F32) + bg_ref[d])
        a = jnp.exp(-(gates[:, :LRU_W] * sp_ref[d]))
        a_ref[...] = a
        b_ref[...] = jnp.sqrt(jnp.maximum(1.0 - a * a, 0.0)) * (gates[:, LRU_W:] * u)

        def tile(ti, carry, d=d, h_ref=h_ref):
            t0 = pl.multiple_of((ti if d == 0 else Tc // sub - 1 - ti) * sub, sub)
            av = a_ref[pl.ds(t0, sub), :]
            bv = b_ref[pl.ds(t0, sub), :]
            for s in (1, 2, 4):
                shift, known = (s, row >= s) if d == 0 else (sub - s, row < sub - s)
                a_prev = jnp.where(known, pltpu.roll(av, shift, 0), 1.0)
                b_prev = jnp.where(known, pltpu.roll(bv, shift, 0), 0.0)
                bv = bv + av * b_prev
                av = av * a_prev
            h = av * carry + bv
            h_ref[0, pl.ds(t0, sub), :] = h
            return h[sub - 1:sub] if d == 0 else h[0:1]

        c_ref[...] = lax.fori_loop(0, Tc // sub, tile, c_ref[...])


def _lru(u, Lc, wa, ba, wi, bi, lam):
    B, T, _ = u.shape
    S = T - Lc
    Tc = math.gcd(math.gcd(Lc, S), 256)
    nC, nL = Lc // Tc, S // Tc
    w_gates = jnp.stack([jnp.concatenate([_block_diag(wa[d]), _block_diag(wi[d])], axis=1)
                         for d in range(2)]).astype(BF16)
    b_gates = jnp.stack([jnp.concatenate([ba[d], bi[d]]) for d in range(2)])[:, None, :]
    decay_rate = (LRU_C * jax.nn.softplus(-lam))[:, None, :]
    fwd = pl.BlockSpec((1, Tc, LRU_W), lambda bi_, j: (bi_, j, 0))
    rev = pl.BlockSpec((1, Tc, LRU_W), lambda bi_, j: (bi_, _seg_rev_block(j, nC, nL), 0))

    def const(shape):
        return pl.BlockSpec(shape, lambda bi_, j: (0,) * len(shape))

    return pl.pallas_call(
        functools.partial(_lru_kernel, Tc=Tc),
        grid=(B, nC + nL),
        in_specs=[fwd, rev, const(w_gates.shape), const(b_gates.shape), const(decay_rate.shape)],
        out_specs=[fwd, rev],
        out_shape=[jax.ShapeDtypeStruct((B, T, LRU_W), F32)] * 2,
        scratch_shapes=[pltpu.VMEM((Tc, LRU_W), F32), pltpu.VMEM((Tc, LRU_W), F32),
                        pltpu.VMEM((1, LRU_W), F32), pltpu.VMEM((1, LRU_W), F32)],
        compiler_params=_params("parallel", "arbitrary"),
        name="lru",
    )(u, u, w_gates, b_gates, decay_rate)


def _mix_out_kernel(attn_ref, hf_ref, hr_ref, g0_ref, g1_ref, wo_ref, o_ref):
    gate = jnp.concatenate([g0_ref[0], g1_ref[0]], axis=-1)
    rec = ((hf_ref[0] + hr_ref[0]) * jax.nn.gelu(gate)).astype(BF16)
    o_ref[0] = (jnp.dot(attn_ref[0].astype(BF16), wo_ref[:ATTN_W, :], preferred_element_type=F32)
                + jnp.dot(rec, wo_ref[ATTN_W:, :], preferred_element_type=F32))


def _mix_out(attn, hf, hr, proj, w_out, t_off):
    B, N, _ = attn.shape
    D = w_out.shape[1]
    tm = math.gcd(math.gcd(N, 512), t_off) if t_off else min(512, N)
    half = LRU_W // 2
    gate_blk = (proj.shape[2] - LRU_W) // half
    tok = lambda b, i: (b, i, 0)
    return pl.pallas_call(
        _mix_out_kernel,
        grid=(B, N // tm),
        in_specs=[pl.BlockSpec((1, tm, ATTN_W), tok),
                  pl.BlockSpec((1, tm, LRU_W), lambda b, i: (b, i + t_off // tm, 0)),
                  pl.BlockSpec((1, tm, LRU_W), lambda b, i: (b, i + t_off // tm, 0)),
                  pl.BlockSpec((1, tm, half), lambda b, i: (b, i, gate_blk)),
                  pl.BlockSpec((1, tm, half), lambda b, i: (b, i, gate_blk + 1)),
                  pl.BlockSpec(w_out.shape, lambda b, i: (0, 0))],
        out_specs=pl.BlockSpec((1, tm, D), tok),
        out_shape=jax.ShapeDtypeStruct((B, N, D), F32),
        compiler_params=_params("parallel", "parallel"),
        name="mix_out",
    )(attn, hf, hr, proj, proj, w_out.astype(BF16))


def _softplus(z):
    return jnp.maximum(z, 0.0) + jnp.log(1.0 + jnp.exp(-jnp.abs(z)))


def _nt_dot(wt, x):
    return lax.dot_general(wt, x, (((1,), (1,)), ((), ())), preferred_element_type=F32)


def _rwkv_prep_kernel(cp_ref, c_ref, cn_ref, lp_ref, l_ref, ln_ref, mu_ref, wr_ref, wk_ref, wv_ref, w1_ref,
                      w2_ref, a1_ref, a2_ref, g1_ref, g2_ref, w0_ref, a0_ref, ka_ref, rk_ref,
                      dec_ref, aa_ref, k_ref, v_ref, r_ref, g_ref, bc_ref, *, B, ctx_tiles, tiles):
    i = pl.program_id(0)
    tm, D = l_ref.shape
    is_ctx = i < ctx_tiles
    h = jnp.where(is_ctx, c_ref[...], l_ref[...])
    seq_start = (i == 0) | (i == ctx_tiles)
    seq_end = (i == ctx_tiles - 1) | (i == tiles - 1)
    hp = jnp.where(seq_start, 0.0, jnp.where(is_ctx, cp_ref[...], lp_ref[...]))
    hn = jnp.where(seq_end, 0.0, jnp.where(is_ctx, cn_ref[...], ln_ref[...]))
    xx = 0.5 * (jnp.concatenate([hp, h[:tm - B]], axis=0) + jnp.concatenate([h[B:], hn], axis=0)) - h

    def mix(j):
        return (h + xx * mu_ref[j:j + 1, :]).astype(BF16)

    r = _nt_dot(wr_ref[...], mix(0))
    k = _nt_dot(wk_ref[...], mix(2))
    v = _nt_dot(wv_ref[...], mix(3))
    lw = jnp.tanh(_nt_dot(w1_ref[...], mix(1))).astype(BF16)
    wpre = jnp.dot(w2_ref[...], lw, preferred_element_type=F32)
    la = _nt_dot(a1_ref[...], mix(4)).astype(BF16)
    apre = jnp.dot(a2_ref[...], la, preferred_element_type=F32)
    gg = jax.nn.sigmoid(_nt_dot(g1_ref[...], mix(5))).astype(BF16)
    g_ref[...] = jnp.dot(g2_ref[...], gg, preferred_element_type=F32)
    hpg = LANES // B
    hd = RWKV_HD

    def to_scan_layout(x, ref):
        for g in range(RWKV_H // hpg):
            for c in range(tm // LANES):
                tiles = [x[(g * hpg + hh) * hd:(g * hpg + hh + 1) * hd, c * LANES:(c + 1) * LANES]
                         for hh in range(hpg)]
                for tl, tile in enumerate(_slab_transpose(tiles, B)):
                    ref[g, c * hpg + tl] = tile

    iclr = []
    for d in range(2):
        w_log = -_softplus(-(w0_ref[d] + wpre[d * D:(d + 1) * D])) - 0.5
        to_scan_layout(jnp.exp(-jnp.exp(w_log)), dec_ref.at[d])
        a = jax.nn.sigmoid(a0_ref[d] + apre[d * D:(d + 1) * D])
        iclr.append(a)
        to_scan_layout(a, aa_ref.at[d])
    kd_sum = k * (2.0 + (iclr[0] + iclr[1] - 2.0) * ka_ref[...])
    bc = jnp.sum((r * kd_sum * rk_ref[...]).reshape(RWKV_H, hd, tm), axis=1)
    for g in range(RWKV_H // hpg):
        for c in range(tm // LANES):
            rows = [jnp.broadcast_to(bc[g * hpg + hh:g * hpg + hh + 1, c * LANES:(c + 1) * LANES], (8, LANES))
                    for hh in range(hpg)]
            for tl, tile in enumerate(_slab_transpose(rows, B)):
                bc_ref[g, c * hpg + tl] = tile[0:1]
    to_scan_layout(k, k_ref)
    to_scan_layout(v, v_ref)
    to_scan_layout(r, r_ref)


def _slab_transpose(tiles, B):
    n = len(tiles)
    tiles = list(tiles)
    slab = lax.broadcasted_iota(jnp.int32, tiles[0].shape, 1) // B
    s = n // 2
    while s >= 1:
        upper = (slab & s) != 0
        for i in range(n):
            if i & s == 0:
                lo, hi = tiles[i], tiles[i + s]
                tiles[i] = jnp.where(upper, pltpu.roll(hi, s * B, 1), lo)
                tiles[i + s] = jnp.where(upper, hi, pltpu.roll(lo, LANES - s * B, 1))
        s //= 2
    return tiles


def _const_spec(shape):
    nd = len(shape)
    return pl.BlockSpec(shape, lambda i: (0,) * nd, pipeline_mode=pl.Buffered(1))


def _rwkv_prep(hc_tm, hl_tm, B, mu, w_rkv, w0, w1, w2, a0, a1, a2, g1, g2, k_a, r_k):
    D = hl_tm.shape[1]
    Lc, S = hc_tm.shape[0] // B, hl_tm.shape[0] // B
    T = Lc + S
    TB = T * B
    tm = 256
    tiles, ctx_tiles = TB // tm, Lc * B // tm
    hb = tm // B
    G = RWKV_H // (LANES // B)
    hd = RWKV_HD

    def t_bf16(w):
        return w.T.astype(BF16)

    zeros_w = jnp.zeros_like(w2[0].T)
    w2t = jnp.concatenate([jnp.concatenate([w2[0].T, zeros_w], 1),
                           jnp.concatenate([zeros_w, w2[1].T], 1)], 0).astype(BF16)
    zeros_a = jnp.zeros_like(a2[0].T)
    a2t = jnp.concatenate([jnp.concatenate([a2[0].T, zeros_a], 1),
                           jnp.concatenate([zeros_a, a2[1].T], 1)], 0).astype(BF16)
    consts = [mu, t_bf16(w_rkv[0]), t_bf16(w_rkv[1]), t_bf16(w_rkv[2]),
              t_bf16(jnp.concatenate([w1[0], w1[1]], 1)), w2t,
              t_bf16(jnp.concatenate([a1[0], a1[1]], 1)), a2t, t_bf16(g1), t_bf16(g2),
              w0.reshape(2, D, 1), a0.reshape(2, D, 1), k_a.reshape(D, 1), r_k.reshape(D, 1)]
    dir_spec = pl.BlockSpec((2, G, hb, hd, LANES), lambda i: (0, 0, i, 0, 0))
    all_spec = pl.BlockSpec((G, hb, hd, LANES), lambda i: (0, i, 0, 0))
    dir_shape = jax.ShapeDtypeStruct((2, G, T, hd, LANES), F32)
    all_shape = jax.ShapeDtypeStruct((G, T, hd, LANES), F32)

    def stream_specs(first, n_tiles, steps):
        def tile(i):
            return jnp.clip(i - first, 0, n_tiles - 1)
        return [pl.BlockSpec((B, D), lambda i: (jnp.maximum(tile(i) * hb - 1, 0), 0)),
                pl.BlockSpec((tm, D), lambda i: (tile(i), 0)),
                pl.BlockSpec((B, D), lambda i: (jnp.minimum((tile(i) + 1) * hb, steps - 1), 0))]

    return pl.pallas_call(
        functools.partial(_rwkv_prep_kernel, B=B, ctx_tiles=ctx_tiles, tiles=tiles),
        grid=(tiles,),
        in_specs=stream_specs(0, ctx_tiles, Lc) + stream_specs(ctx_tiles, tiles - ctx_tiles, S)
                 + [_const_spec(c.shape) for c in consts],
        out_specs=[dir_spec, dir_spec, all_spec, all_spec, all_spec,
                   pl.BlockSpec((D, tm), lambda i: (0, i)),
                   pl.BlockSpec((G, hb, 1, LANES), lambda i: (0, i, 0, 0))],
        out_shape=[dir_shape, dir_shape, all_shape, all_shape, all_shape,
                   jax.ShapeDtypeStruct((D, TB), F32),
                   jax.ShapeDtypeStruct((G, T, 1, LANES), F32)],
        compiler_params=_params("parallel"),
        name="rwkv_prep",
    )(hc_tm, hc_tm, hc_tm, hl_tm, hl_tm, hl_tm, *consts)


def _wkv_kernel(dec_ref, aa_ref, k_ref, v_ref, r_ref, kkc_ref, kac_ref, y_ref,
                s_ref, g_ref, p_ref, q_ref, sa_ref, *, Tc, nC):
    d = pl.program_id(0)
    j = pl.program_id(2)
    hd = RWKV_HD
    sub = 8

    @pl.when(j == 0)
    def _():
        s_ref[...] = jnp.zeros_like(s_ref)

    def partial_rows(x):
        return jnp.sum(x.reshape(hd // sub, sub, LANES), axis=0)

    pitch = hd + 1

    def put_partial(ref, v, x):
        ref[pl.ds(v, sub, stride=pitch), :] = partial_rows(x)

    def finish_rows(ref):
        acc = ref[pl.ds(0, hd), :]
        for q in range(1, sub):
            acc = acc + ref[pl.ds(q * pitch, hd), :]
        return acc

    def run(with_y):
        g_ref[...] = jnp.ones_like(g_ref)

        def step(s, _):
            tt = jnp.where(d == 0, s, Tc - 1 - s)
            aa = aa_ref[0, 0, tt]
            kk = k_ref[0, tt]
            kf = kk * kkc_ref[0]
            kn = kf * lax.rsqrt(jnp.maximum(jnp.sum(kf * kf, axis=0, keepdims=True), 1e-24))
            g_prev = g_ref[...]
            g = g_prev * dec_ref[0, 0, tt]
            g_inv = 1.0 / g
            g_ref[...] = g
            a_t = -(kn * g_prev)
            b_t = kn * aa * g_inv
            k_t = kk * (1.0 + (aa - 1.0) * kac_ref[0]) * g_inv
            r_t = r_ref[0, tt] * g if with_y else None

            def row_reduce(v, _):
                st = s_ref[v]
                put_partial(p_ref, v, st * a_t)
                if with_y:
                    put_partial(q_ref, v, st * r_t)
                return 0

            lax.fori_loop(0, hd, row_reduce, 0, unroll=8)
            sa = finish_rows(p_ref)
            sa_ref[...] = sa
            if with_y:
                b_r = jnp.sum(b_t * r_t, axis=0, keepdims=True)
                k_r = jnp.sum(k_t * r_t, axis=0, keepdims=True)
                y_ref[0, 0, tt] = finish_rows(q_ref) + sa * b_r + v_ref[0, tt] * k_r

            def row_update(v, _):
                s_ref[v] = s_ref[v] + sa_ref[pl.ds(v, 1), :] * b_t + v_ref[0, tt, pl.ds(v, 1), :] * k_t
                return 0

            lax.fori_loop(0, hd, row_update, 0, unroll=8)
            return 0

        lax.fori_loop(0, Tc, step, 0)

        def rescale(v, _):
            s_ref[v] = s_ref[v] * g_ref[...]
            return 0

        lax.fori_loop(0, hd, rescale, 0, unroll=8)

    @pl.when(j < nC)
    def _():
        run(False)

    @pl.when(j >= nC)
    def _():
        run(True)


def _wkv(dec, aa, k, v, r, k_k, k_a, B, Lc):
    _, G, T, hd, _ = dec.shape
    Tc = WKV_TC
    S = T - Lc
    nC, nL = Lc // Tc, S // Tc

    def tmap(d, j):
        return jnp.where(d == 0, j, _seg_rev_block(j, nC, nL))

    def lmap(d, j):
        jj = jnp.maximum(j - nC, 0)
        return jnp.where(d == 0, jj, nL - 1 - jj)

    dir_spec = pl.BlockSpec((1, 1, Tc, hd, LANES), lambda d, g, j: (d, g, tmap(d, j), 0, 0))
    all_spec = pl.BlockSpec((1, Tc, hd, LANES), lambda d, g, j: (g, tmap(d, j), 0, 0))
    const_spec = pl.BlockSpec((1, hd, LANES), lambda d, g, j: (g, 0, 0))
    return pl.pallas_call(
        functools.partial(_wkv_kernel, Tc=Tc, nC=nC),
        grid=(2, G, nC + nL),
        in_specs=[dir_spec, dir_spec, all_spec, all_spec, all_spec, const_spec, const_spec],
        out_specs=pl.BlockSpec((1, 1, Tc, hd, LANES), lambda d, g, j: (d, g, lmap(d, j), 0, 0)),
        out_shape=jax.ShapeDtypeStruct((2, G, S, hd, LANES), F32),
        scratch_shapes=[pltpu.VMEM((hd, hd, LANES), F32),
                        pltpu.VMEM((hd, LANES), F32),
                        pltpu.VMEM(((hd + 1) * 8, LANES), F32),
                        pltpu.VMEM(((hd + 1) * 8, LANES), F32),
                        pltpu.VMEM((hd, LANES), F32)],
        compiler_params=_params("parallel", "parallel", "arbitrary"),
        name="wkv",
    )(dec, aa, k, v, r, _lane_const(k_k, B), _lane_const(k_a, B))


def _lane_const(c, B):
    hpg = LANES // B
    return jnp.repeat(c.reshape(RWKV_H // hpg, hpg, RWKV_HD).transpose(0, 2, 1), B, axis=-1)


def _rwkv_finish_kernel(y_ref, v_ref, bc_ref, g_ref, gng_ref, gnb_ref, wo_ref, o_ref, *, B):
    G, steps, hd, _ = v_ref.shape
    hpg = LANES // B
    rows = []
    for g in range(G):
        head_cols = [[] for _ in range(hpg)]
        for c in range(steps // hpg):
            tiles = []
            for tl in range(hpg):
                t = c * hpg + tl
                y = y_ref[0, g, t] + y_ref[1, g, t]
                mean = jnp.mean(y, axis=0, keepdims=True)
                var = jnp.mean(jnp.square(y - mean), axis=0, keepdims=True)
                yn = (y - mean) * lax.rsqrt(var + GN_EPS)
                tiles.append(yn * gng_ref[g] + gnb_ref[g] + bc_ref[g, t] * v_ref[g, t])
            for hh, tile in enumerate(_slab_transpose(tiles, B)):
                head_cols[hh].append(tile)
        rows.extend(jnp.concatenate(cols, axis=1) for cols in head_cols)
    o = (jnp.concatenate(rows, axis=0) * g_ref[...]).astype(BF16)
    out_t = jnp.dot(wo_ref[...], o, preferred_element_type=F32)
    o_ref[...] = out_t.T


def _rwkv_finish(y2, v, bc, g, gn_g, gn_b, w_o, B, Lc):
    _, G, S, hd, _ = y2.shape
    D = g.shape[0]
    tm = 256
    steps = tm // B
    off = Lc // steps
    return pl.pallas_call(
        functools.partial(_rwkv_finish_kernel, B=B),
        grid=(S // steps,),
        in_specs=[pl.BlockSpec((2, G, steps, hd, LANES), lambda i: (0, 0, i, 0, 0)),
                  pl.BlockSpec((G, steps, hd, LANES), lambda i: (0, i + off, 0, 0)),
                  pl.BlockSpec((G, steps, 1, LANES), lambda i: (0, i + off, 0, 0)),
                  pl.BlockSpec((D, tm), lambda i: (0, i + off)),
                  _const_spec((G, hd, LANES)), _const_spec((G, hd, LANES)), _const_spec((D, D))],
        out_specs=pl.BlockSpec((tm, D), lambda i: (i, 0)),
        out_shape=jax.ShapeDtypeStruct((S * B, D), F32),
        compiler_params=_params("parallel"),
        name="rwkv_finish",
    )(y2, v, bc, g, _lane_const(gn_g, B), _lane_const(gn_b, B), w_o.T.astype(BF16))


def _rwkv7_mixer(hl, hc, mu, w_rkv, w0, w1, w2, a0, a1, a2, g1, g2, k_k, k_a, r_k, gn_g, gn_b, w_o):
    S, B, D = hl.shape
    Lc = hc.shape[0]
    dec, aa, k, v, r, g, bc = _rwkv_prep(hc.reshape(Lc * B, D), hl.reshape(S * B, D), B, mu, w_rkv, w0, w1,
                                         w2, a0, a1, a2, g1, g2, k_a, r_k.reshape(D))
    y2 = _wkv(dec, aa, k, v, r, k_k, k_a, B, Lc)
    return _rwkv_finish(y2, v, bc, g, gn_g, gn_b, w_o, B, Lc).reshape(S, B, D)


def _moe_kernel(idx_ref, gate_ref, h_ref, w1_ref, w3_ref, w2_ref, o_ref, xin_ref, y_ref, *, cap, chunks):
    bb = h_ref.shape[0]

    @pl.when(pl.program_id(1) == 0)
    def _():
        o_ref[...] = jnp.zeros_like(o_ref)

    def token_rows(t):
        return pl.ds(pl.multiple_of(t * chunks, chunks), chunks)

    for bi in range(bb):
        def gather(r, _, bi=bi):
            slot = bi * cap + r
            xin_ref[token_rows(slot), :] = h_ref[bi, token_rows(idx_ref[0, 0, slot]), :]
            return 0

        lax.fori_loop(0, cap, gather, 0, unroll=8)
    x = _from_chunk_rows(xin_ref, bb * cap).astype(BF16)
    h1 = jnp.dot(x, w1_ref[0], preferred_element_type=F32)
    h3 = jnp.dot(x, w3_ref[0], preferred_element_type=F32)
    hid = (h1 * jax.nn.sigmoid(h1) * h3).astype(BF16)
    _to_chunk_rows(y_ref, jnp.dot(hid, w2_ref[0], preferred_element_type=F32) * gate_ref[0])

    batch = 8
    for bi in range(bb):
        def scatter_add(i, _, bi=bi):
            slots = [bi * cap + i * batch + u for u in range(batch)]
            rows = [token_rows(idx_ref[0, 0, s]) for s in slots]
            sums = [o_ref[bi, rows[u], :] + y_ref[token_rows(slots[u]), :] for u in range(batch)]
            for u in range(batch):
                o_ref[bi, rows[u], :] = sums[u]
            return 0

        lax.fori_loop(0, cap // batch, scatter_add, 0)


def _moe(h, aff, w1, w3, w2):
    B, _, N = aff.shape
    E, D, FF = w1.shape
    chunks = D // LANES
    cap = CAPACITY * N // E
    bb = max(1, min(B, MOE_ROWS // cap))
    gate, idx = lax.top_k(aff, cap)

    def group(t):
        return t.reshape(B // bb, bb, E, cap).transpose(0, 2, 1, 3).reshape(B // bb * E, bb * cap)

    rows = bb * cap
    return pl.pallas_call(
        functools.partial(_moe_kernel, cap=cap, chunks=chunks),
        grid=(B // bb, E),
        in_specs=[pl.BlockSpec((1, 1, rows), lambda b, e: (b * E + e, 0, 0), memory_space=pltpu.SMEM),
                  pl.BlockSpec((1, rows, 1), lambda b, e: (b * E + e, 0, 0)),
                  pl.BlockSpec((bb, N * chunks, LANES), lambda b, e: (b, 0, 0), pipeline_mode=pl.Buffered(1)),
                  pl.BlockSpec((1, D, FF), lambda b, e: (e, 0, 0)),
                  pl.BlockSpec((1, D, FF), lambda b, e: (e, 0, 0)),
                  pl.BlockSpec((1, FF, D), lambda b, e: (e, 0, 0))],
        out_specs=pl.BlockSpec((bb, N * chunks, LANES), lambda b, e: (b, 0, 0)),
        out_shape=jax.ShapeDtypeStruct((B, N * chunks, LANES), F32),
        scratch_shapes=[pltpu.VMEM((rows * chunks, LANES), F32), pltpu.VMEM((rows * chunks, LANES), F32)],
        compiler_params=_params("parallel", "arbitrary"),
        name="moe",
    )(group(idx.astype(jnp.int32))[:, None, :], group(gate)[:, :, None], h, w1, w3, w2)


def _block_diag(w):
    H, bi, bj = w.shape
    eye = jnp.eye(H, dtype=w.dtype)
    return jnp.einsum('hij,hg->higj', w, eye).reshape(H * bi, H * bj)


def _dwconv(u, w, b):
    n = u.shape[1]
    up = jnp.pad(u, ((0, 0), (CONV_LEFT, CONV_W - 1 - CONV_LEFT), (0, 0)))
    out = up[:, 0:n] * w[0]
    for j in range(1, CONV_W):
        out = out + up[:, j:j + n] * w[j]
    return out + b


def _attn_lru_mixer(proj_l, proj_c, cos, sin, w_out, sink, conv_w, conv_b, wa, ba, wi, bi, lam):
    Lc = proj_c.shape[1]
    qk = _rope(proj_l, cos, sin)
    attn_l = _attention(sink, qk, proj_l, proj_c)
    attn_c = _ctx_attention(sink, proj_c)

    u0, g0 = ATTN_W + 2 * KV_W, ATTN_W + 2 * KV_W + LRU_W
    u = jnp.concatenate([_dwconv(proj_c[..., u0:g0], conv_w, conv_b),
                         _dwconv(proj_l[..., u0:g0], conv_w, conv_b)], axis=1)
    hf, hr = _lru(u, Lc, wa, ba, wi, bi, lam)
    yl = _mix_out(attn_l, hf, hr, proj_l, w_out, Lc)
    yc = _mix_out(attn_c, hf, hr, proj_c, w_out, 0)
    return yl, yc


def kernel(x, c, ctx, c_ctx, mod_w, mod_b, norm_mix, norm_ffn, router_w, exp_w1, exp_w3, exp_w2, mix_in, mix_out, attn_sink, lru_conv_w, lru_conv_b, lru_wa, lru_ba, lru_wi, lru_bi, lru_lam, rw_mu, rw_rkv, rw_w0, rw_w1, rw_w2, rw_a0, rw_a1, rw_a2, rw_g1, rw_g2, rw_kk, rw_ka, rw_rk, rw_gn_g, rw_gn_b, rw_wo, final_norm):
    B, S, D = x.shape
    Lc = ctx.shape[1]
    depth = mod_w.shape[0]
    assert depth == 2 and S % QBLOCK == 0
    assert LANES % B == 0 and RWKV_H % (LANES // B) == 0 and B % 8 == 0
    assert Lc % WKV_TC == 0 and S % WKV_TC == 0 and (Lc * B) % 256 == 0
    cos, sin = _rope_tables(S)

    n_rows = -(-(B + 1) // 8) * 8
    cond = jnp.concatenate([jax.nn.silu(c), jax.nn.silu(c_ctx)[None],
                            jnp.zeros((n_rows - B - 1, D), F32)], axis=0)

    mods_l, mods_c = [], []
    for layer in range(depth):
        mod = _mm(cond, mod_w[layer], tm=n_rows, tn=1024, precise=True) + mod_b[layer]
        mods_l.append([t[:, None, :] for t in jnp.split(mod[:B], 6, axis=-1)])
        mods_c.append([jnp.broadcast_to(t[None], (B, 1, D)) for t in jnp.split(mod[B:B + 1], 6, axis=-1)])

    xl, xc = x, ctx
    hl = hc = None
    for layer in range(depth):
        last = layer == depth - 1
        m_l, m_c = mods_l[layer], mods_c[layer]
        if layer % 2 == 0:
            i = layer // 2
            assert layer == 0
            w_in = mix_in[i].astype(BF16)
            proj_l = _norm_mm(xl, norm_mix[layer], m_l[0], m_l[1], w_in)
            proj_c = _norm_mm(xc, norm_mix[layer], m_c[0], m_c[1], w_in)
            yl, yc = _attn_lru_mixer(proj_l, proj_c, cos, sin, mix_out[i], attn_sink[i], lru_conv_w[i],
                                     lru_conv_b[i], lru_wa[i], lru_ba[i], lru_wi[i], lru_bi[i], lru_lam[i])
        else:
            i = layer // 2
            yl = _rwkv7_mixer(hl, hc, rw_mu[i], rw_rkv[i], rw_w0[i], rw_w1[i], rw_w2[i], rw_a0[i], rw_a1[i],
                              rw_a2[i], rw_g1[i], rw_g2[i], rw_kk[i], rw_ka[i], rw_rk[i], rw_gn_g[i],
                              rw_gn_b[i], rw_wo[i])
            yc = None
        w1b, w3b, w2b = exp_w1[layer].astype(BF16), exp_w3[layer].astype(BF16), exp_w2[layer].astype(BF16)
        if layer % 2 == 0:
            xl, hl, aff_l = _resid_norm(xl, yl, m_l[2], norm_ffn[layer], m_l[3], m_l[4], out_rows=True,
                                        router_w=router_w[layer])
        else:
            xl, hl, aff_l = _resid_norm_tm(xl, yl, m_l[2], norm_ffn[layer], m_l[3], m_l[4], delta_tm=True,
                                           h_tm=False, router_w=router_w[layer])
        moe_l = _moe(hl, aff_l, w1b, w3b, w2b)
        if last:
            zero = jnp.zeros((B, 1, D), F32)
            return _resid_norm(xl, moe_l, m_l[5], final_norm, zero, zero, delta_rows=True, want_x=False)
        assert (layer + 1) % 2 == 1
        n_l, n_c = mods_l[layer + 1], mods_c[layer + 1]
        xl, hl = _resid_norm_tm(xl, moe_l, m_l[5], norm_mix[layer + 1], n_l[0], n_l[1], delta_tm=False, h_tm=True)
        xc, hc, aff_c = _resid_norm(xc, yc, m_c[2], norm_ffn[layer], m_c[3], m_c[4], out_rows=True,
                                    router_w=router_w[layer])
        moe_c = _moe(hc, aff_c, w1b, w3b, w2b)
        xc, hc = _resid_norm_tm(xc, moe_c, m_c[5], norm_mix[layer + 1], n_c[0], n_c[1], delta_tm=False, h_tm=True)
```

```python
import functools
import math

import jax
import jax.numpy as jnp
from jax import lax
from jax.experimental import pallas as pl
from jax.experimental.pallas import tpu as pltpu

F32 = jnp.float32
BF16 = jnp.bfloat16

GRID_W = 64
N_HEADS_ATTN = 8
N_KV = 2
HEAD_DIM = 64
AXIS_DIM = HEAD_DIM // 2
ATTN_W = N_HEADS_ATTN * HEAD_DIM
KV_W = N_KV * HEAD_DIM
WINDOW = 128
QBLOCK = 128
ROPE_BASE = 10000.0
LRU_W = 512
LRU_BLOCKS = 8
CONV_W = 4
CONV_LEFT = 2
LRU_C = 8.0
RWKV_H = 16
RWKV_HD = 64
GN_EPS = 64e-5
N_EXPERTS = 16
CAPACITY = 2
NORM_EPS = 1e-6
NEG_INF = -1e30

LANES = 128
VMEM_LIMIT_BYTES = 56 * 1024 * 1024
WKV_TC = 32
MOE_ROWS = 256

def _params(*sem):
    return pltpu.CompilerParams(dimension_semantics=sem, vmem_limit_bytes=VMEM_LIMIT_BYTES)


def _mm_kernel(x_ref, w_ref, o_ref, *, precise):
    if precise:
        o_ref[...] = jnp.dot(x_ref[...], w_ref[...], precision=lax.Precision.HIGHEST,
                             preferred_element_type=F32)
    else:
        o_ref[...] = jnp.dot(x_ref[...].astype(BF16), w_ref[...].astype(BF16),
                             preferred_element_type=F32)


def _mm(x, w, *, tm=512, tn=None, precise=False):
    M, K = x.shape
    N = w.shape[1]
    tm = min(tm, M)
    tn = N if tn is None else min(tn, N)
    assert M % tm == 0 and N % tn == 0, (M, tm, N, tn)
    return pl.pallas_call(
        functools.partial(_mm_kernel, precise=precise),
        grid=(M // tm, N // tn),
        in_specs=[pl.BlockSpec((tm, K), lambda i, j: (i, 0)),
                  pl.BlockSpec((K, tn), lambda i, j: (0, j))],
        out_specs=pl.BlockSpec((tm, tn), lambda i, j: (i, j)),
        out_shape=jax.ShapeDtypeStruct((M, N), F32),
        compiler_params=_params("parallel", "parallel"),
        name="mm",
    )(x, w)


def _resid_norm_kernel(*refs, has_delta, delta_rows, want_x, out_rows, route):
    refs = list(refs)
    x_ref = refs.pop(0)
    d_ref, gate_ref = (refs.pop(0), refs.pop(0)) if has_delta else (None, None)
    g_ref, sh_ref, sc_ref = refs.pop(0), refs.pop(0), refs.pop(0)
    wr_ref = refs.pop(0) if route else None
    xo_ref = refs.pop(0) if (has_delta and want_x) else None
    h_ref = refs.pop(0)
    aff_ref = refs.pop(0) if route else None
    x = x_ref[0]
    if has_delta:
        delta = _from_chunk_rows(d_ref.at[0], x.shape[0]) if delta_rows else d_ref[0]
        x = x + gate_ref[0] * delta
        if want_x:
            xo_ref[0] = x
    y = x * lax.rsqrt(jnp.mean(x * x, axis=-1, keepdims=True) + NORM_EPS) * g_ref[...]
    h = y * (1.0 + sc_ref[0]) + sh_ref[0]
    if out_rows:
        _to_chunk_rows(h_ref.at[0], h)
    else:
        h_ref[0] = h
    if route:
        logits = lax.dot_general(wr_ref[...], h, (((1,), (1,)), ((), ())),
                                 precision=lax.Precision.HIGHEST, preferred_element_type=F32)
        e = jnp.exp(logits - jnp.max(logits, axis=0, keepdims=True))
        aff_ref[0] = e / jnp.sum(e, axis=0, keepdims=True)


def _from_chunk_rows(ref, n):
    chunks = ref.shape[0] // n
    return jnp.concatenate([ref[pl.ds(j, n, stride=chunks), :] for j in range(chunks)], axis=-1)


def _to_chunk_rows(ref, x):
    n = x.shape[0]
    chunks = x.shape[1] // LANES
    for j in range(chunks):
        ref[pl.ds(j, n, stride=chunks), :] = x[:, j * LANES:(j + 1) * LANES]


def _resid_norm(x, delta, gate, g, shift, scale, *, delta_rows=False, want_x=True, out_rows=False,
                router_w=None):
    B, N, D = x.shape
    ts = min(512, N)
    has_delta = delta is not None
    chunks = D // LANES
    tok = pl.BlockSpec((1, ts, D), lambda b, i: (b, i, 0))
    tok_rows = pl.BlockSpec((1, ts * chunks, LANES), lambda b, i: (b, i, 0))
    per_b = pl.BlockSpec((1, 1, D), lambda b, i: (b, 0, 0))
    in_specs, args = [tok], [x]
    if has_delta:
        in_specs += [tok_rows if delta_rows else tok, per_b]
        args += [delta, gate]
    in_specs += [pl.BlockSpec((1, D), lambda b, i: (0, 0)), per_b, per_b]
    args += [g.reshape(1, D), shift, scale]
    route = router_w is not None
    if route:
        E = router_w.shape[1]
        in_specs.append(pl.BlockSpec((E, D), lambda b, i: (0, 0)))
        args.append(router_w.T)
    out_specs, out_shape = [], []
    if has_delta and want_x:
        out_specs.append(tok)
        out_shape.append(jax.ShapeDtypeStruct((B, N, D), F32))
    out_specs.append(tok_rows if out_rows else tok)
    out_shape.append(jax.ShapeDtypeStruct((B, N * chunks, LANES) if out_rows else (B, N, D), F32))
    if route:
        out_specs.append(pl.BlockSpec((1, E, ts), lambda b, i: (b, 0, i)))
        out_shape.append(jax.ShapeDtypeStruct((B, E, N), F32))
    outs = pl.pallas_call(
        functools.partial(_resid_norm_kernel, has_delta=has_delta, delta_rows=delta_rows, want_x=want_x,
                          out_rows=out_rows, route=route),
        grid=(B, N // ts),
        in_specs=in_specs,
        out_specs=out_specs,
        out_shape=out_shape,
        compiler_params=_params("parallel", "parallel"),
        name="resid_norm",
    )(*args)
    return outs if len(outs) > 1 else outs[0]


def _resid_norm_tm_kernel(*refs, delta_tm, h_tm, route):
    refs = list(refs)
    x_ref, d_ref, gate_ref, g_ref, sh_ref, sc_ref = (refs.pop(0) for _ in range(6))
    wr_ref = refs.pop(0) if route else None
    xo_ref, h_ref = refs.pop(0), refs.pop(0)
    aff_ref = refs.pop(0) if route else None
    nb, ts, _ = x_ref.shape
    for bi in range(nb):
        delta = d_ref[:, bi, :] if delta_tm else _from_chunk_rows(d_ref.at[bi], ts)
        x = x_ref[bi] + gate_ref[bi] * delta
        xo_ref[bi] = x
        y = x * lax.rsqrt(jnp.mean(x * x, axis=-1, keepdims=True) + NORM_EPS) * g_ref[...]
        h = y * (1.0 + sc_ref[bi]) + sh_ref[bi]
        if h_tm:
            h_ref[:, bi, :] = h
        else:
            _to_chunk_rows(h_ref.at[bi], h)
        if route:
            logits = lax.dot_general(wr_ref[...], h, (((1,), (1,)), ((), ())),
                                     precision=lax.Precision.HIGHEST, preferred_element_type=F32)
            e = jnp.exp(logits - jnp.max(logits, axis=0, keepdims=True))
            aff_ref[bi] = e / jnp.sum(e, axis=0, keepdims=True)


def _resid_norm_tm(x, delta, gate, g, shift, scale, *, delta_tm, h_tm, router_w=None):
    B, N, D = x.shape
    nb, ts = 8, min(LANES, N)
    chunks = D // LANES
    tok = pl.BlockSpec((nb, ts, D), lambda b, i: (b, i, 0))
    tok_rows = pl.BlockSpec((nb, ts * chunks, LANES), lambda b, i: (b, i, 0))
    tok_tm = pl.BlockSpec((ts, nb, D), lambda b, i: (i, b, 0))
    per_b = pl.BlockSpec((nb, 1, D), lambda b, i: (b, 0, 0))
    in_specs = [tok, tok_tm if delta_tm else tok_rows, per_b, pl.BlockSpec((1, D), lambda b, i: (0, 0)),
                per_b, per_b]
    args = [x, delta, gate, g.reshape(1, D), shift, scale]
    out_specs = [tok, tok_tm if h_tm else tok_rows]
    out_shape = [jax.ShapeDtypeStruct((B, N, D), F32),
                 jax.ShapeDtypeStruct((N, B, D) if h_tm else (B, N * chunks, LANES), F32)]
    route = router_w is not None
    if route:
        E = router_w.shape[1]
        in_specs.append(pl.BlockSpec((E, D), lambda b, i: (0, 0)))
        args.append(router_w.T)
        out_specs.append(pl.BlockSpec((nb, E, ts), lambda b, i: (b, 0, i)))
        out_shape.append(jax.ShapeDtypeStruct((B, E, N), F32))
    return pl.pallas_call(
        functools.partial(_resid_norm_tm_kernel, delta_tm=delta_tm, h_tm=h_tm, route=route),
        grid=(B // nb, N // ts),
        in_specs=in_specs,
        out_specs=out_specs,
        out_shape=out_shape,
        compiler_params=_params("parallel", "parallel"),
        name="resid_norm_tm",
    )(*args)


def _norm_mm_kernel(x_ref, g_ref, sh_ref, sc_ref, w_ref, o_ref):
    x = x_ref[0]
    y = x * lax.rsqrt(jnp.mean(x * x, axis=-1, keepdims=True) + NORM_EPS) * g_ref[...]
    h = (y * (1.0 + sc_ref[0]) + sh_ref[0]).astype(BF16)
    o_ref[0] = jnp.dot(h, w_ref[...], preferred_element_type=F32)


def _norm_mm(x, g, shift, scale, w):
    B, N, D = x.shape
    M = w.shape[1]
    ts = min(512, N)
    per_b = pl.BlockSpec((1, 1, D), lambda b, i: (b, 0, 0))
    return pl.pallas_call(
        _norm_mm_kernel,
        grid=(B, N // ts),
        in_specs=[pl.BlockSpec((1, ts, D), lambda b, i: (b, i, 0)),
                  pl.BlockSpec((1, D), lambda b, i: (0, 0)), per_b, per_b,
                  pl.BlockSpec((D, M), lambda b, i: (0, 0))],
        out_specs=pl.BlockSpec((1, ts, M), lambda b, i: (b, i, 0)),
        out_shape=jax.ShapeDtypeStruct((B, N, M), F32),
        compiler_params=_params("parallel", "parallel"),
        name="norm_mm",
    )(x, g.reshape(1, D), shift, scale, w)


def _rope_kernel(x_ref, cos_ref, sin_ref, o_ref):
    width = x_ref.shape[-1]
    for c in range(width // LANES):
        sl = slice(c * LANES, (c + 1) * LANES)
        x = x_ref[0, :, sl]
        lane = lax.broadcasted_iota(jnp.int32, x.shape, 1)
        first_half = (lane % AXIS_DIM) < (AXIS_DIM // 2)
        partner = jnp.where(first_half, pltpu.roll(x, LANES - AXIS_DIM // 2, 1),
                            pltpu.roll(x, AXIS_DIM // 2, 1))
        o_ref[0, :, sl] = x * cos_ref[:, sl] + partner * sin_ref[:, sl]


def _rope_tables(S):
    rows = S // GRID_W
    row = jnp.repeat(jnp.arange(rows), GRID_W).astype(F32)
    col = jnp.tile(jnp.arange(GRID_W), rows).astype(F32)
    inv = ROPE_BASE ** (-jnp.arange(0, AXIS_DIM, 2, dtype=F32) / AXIS_DIM)
    ang_row, ang_col = row[:, None] * inv, col[:, None] * inv

    def axis_tables(ang):
        c, s = jnp.cos(ang), jnp.sin(ang)
        return jnp.concatenate([c, c], -1), jnp.concatenate([-s, s], -1)

    cr, sr = axis_tables(ang_row)
    cc, sc = axis_tables(ang_col)
    cos_h = jnp.concatenate([cr, cc], -1)
    sin_h = jnp.concatenate([sr, sc], -1)
    q_scale = HEAD_DIM ** -0.5
    cos = jnp.concatenate([jnp.tile(cos_h, (1, N_HEADS_ATTN)) * q_scale, jnp.tile(cos_h, (1, N_KV))], -1)
    sin = jnp.concatenate([jnp.tile(sin_h, (1, N_HEADS_ATTN)) * q_scale, jnp.tile(sin_h, (1, N_KV))], -1)
    return cos, sin


def _rope(proj, cos, sin):
    B, S, _ = proj.shape
    width = ATTN_W + KV_W
    ts = min(512, S)
    return pl.pallas_call(
        _rope_kernel,
        grid=(B, S // ts),
        in_specs=[pl.BlockSpec((1, ts, width), lambda b, i: (b, i, 0)),
                  pl.BlockSpec((ts, width), lambda b, i: (i, 0)),
                  pl.BlockSpec((ts, width), lambda b, i: (i, 0))],
        out_specs=pl.BlockSpec((1, ts, width), lambda b, i: (b, i, 0)),
        out_shape=jax.ShapeDtypeStruct((B, S, width), F32),
        compiler_params=_params("parallel", "parallel"),
        name="rope",
    )(proj, cos, sin)


def _attn_heads(q, kcat, vcat, valid, sink_ref):
    outs = []
    G = N_HEADS_ATTN // N_KV
    Q = q.shape[0]
    for j in range(N_KV):
        kj = kcat[:, j * HEAD_DIM:(j + 1) * HEAD_DIM]
        vj = vcat[:, j * HEAD_DIM:(j + 1) * HEAD_DIM]
        heads = range(j * G, (j + 1) * G)
        qs = jnp.concatenate([q[:, h * HEAD_DIM:(h + 1) * HEAD_DIM] for h in heads], axis=0).astype(BF16)
        s = lax.dot_general(qs, kj, (((1,), (1,)), ((), ())), preferred_element_type=F32)
        if valid is not None:
            s = jnp.where(valid[None], s.reshape(G, Q, -1), NEG_INF).reshape(G * Q, -1)
        sink = jnp.concatenate([jnp.full((Q, 1), sink_ref[h], F32) for h in heads], axis=0)
        m = jnp.maximum(jnp.max(s, axis=-1, keepdims=True), sink)
        e = jnp.exp(s - m)
        den = jnp.sum(e, axis=-1, keepdims=True) + jnp.exp(sink - m)
        o = jnp.dot(e.astype(BF16), vj, preferred_element_type=F32) / den
        outs.extend(o[g * Q:(g + 1) * Q] for g in range(G))
    return jnp.concatenate(outs, axis=-1)


def _attn_kernel(sink_ref, q_ref, kp_ref, kc_ref, kn_ref, vp_ref, vc_ref, vn_ref, ck_ref, cv_ref,
                 o_ref, *, S, Lc):
    i = pl.program_id(1)
    kcat = jnp.concatenate([ck_ref[0], kp_ref[0], kc_ref[0], kn_ref[0]], axis=0).astype(BF16)
    vcat = jnp.concatenate([cv_ref[0], vp_ref[0], vc_ref[0], vn_ref[0]], axis=0).astype(BF16)
    L = Lc + 3 * QBLOCK
    row = lax.broadcasted_iota(jnp.int32, (QBLOCK, L), 0)
    col = lax.broadcasted_iota(jnp.int32, (QBLOCK, L), 1)
    rel = col - Lc - QBLOCK - row
    kpos = (i - 1) * QBLOCK + col - Lc
    valid = (col < Lc) | ((jnp.abs(rel) <= WINDOW) & (kpos >= 0) & (kpos < S))
    o_ref[0] = _attn_heads(q_ref[0], kcat, vcat, valid, sink_ref)


def _attention(sink, qk, proj_l, proj_c):
    B, S, _ = qk.shape
    Lc = proj_c.shape[1]
    nb = S // QBLOCK
    kcol, vcol = ATTN_W // KV_W, ATTN_W // KV_W + 1

    def blk(colblk, off):
        return pl.BlockSpec((1, QBLOCK, KV_W),
                            lambda b, i: (b, jnp.clip(i + off, 0, nb - 1), colblk))

    return pl.pallas_call(
        functools.partial(_attn_kernel, S=S, Lc=Lc),
        grid=(B, nb),
        in_specs=[pl.BlockSpec(memory_space=pltpu.SMEM),
                  pl.BlockSpec((1, QBLOCK, ATTN_W), lambda b, i: (b, i, 0)),
                  blk(kcol, -1), blk(kcol, 0), blk(kcol, 1),
                  blk(vcol, -1), blk(vcol, 0), blk(vcol, 1),
                  pl.BlockSpec((1, Lc, KV_W), lambda b, i: (b, 0, kcol)),
                  pl.BlockSpec((1, Lc, KV_W), lambda b, i: (b, 0, vcol))],
        out_specs=pl.BlockSpec((1, QBLOCK, ATTN_W), lambda b, i: (b, i, 0)),
        out_shape=jax.ShapeDtypeStruct((B, S, ATTN_W), F32),
        compiler_params=_params("parallel", "parallel"),
        name="attention",
    )(sink, qk, qk, qk, qk, proj_l, proj_l, proj_l, proj_c, proj_c)


def _ctx_attn_kernel(sink_ref, q_ref, ck_ref, cv_ref, o_ref):
    q = q_ref[0] * (HEAD_DIM ** -0.5)
    o_ref[0] = _attn_heads(q, ck_ref[0].astype(BF16), cv_ref[0].astype(BF16), None, sink_ref)


def _ctx_attention(sink, proj_c):
    B, Lc, _ = proj_c.shape
    kcol, vcol = ATTN_W // KV_W, ATTN_W // KV_W + 1
    return pl.pallas_call(
        _ctx_attn_kernel,
        grid=(B,),
        in_specs=[pl.BlockSpec(memory_space=pltpu.SMEM),
                  pl.BlockSpec((1, Lc, ATTN_W), lambda b: (b, 0, 0)),
                  pl.BlockSpec((1, Lc, KV_W), lambda b: (b, 0, kcol)),
                  pl.BlockSpec((1, Lc, KV_W), lambda b: (b, 0, vcol))],
        out_specs=pl.BlockSpec((1, Lc, ATTN_W), lambda b: (b, 0, 0)),
        out_shape=jax.ShapeDtypeStruct((B, Lc, ATTN_W), F32),
        compiler_params=_params("parallel"),
        name="ctx_attention",
    )(sink, proj_c, proj_c, proj_c)


def _seg_rev_block(j, nC, nL):
    return jnp.where(j < nC, nC - 1 - j, nC + nL - 1 - (j - nC))


def _lru_kernel(uf_ref, ur_ref, wg_ref, bg_ref, sp_ref, hf_ref, hr_ref, a_ref, b_ref, cf_ref, cr_ref, *, Tc):
    @pl.when(pl.program_id(1) == 0)
    def _():
        cf_ref[...] = jnp.zeros_like(cf_ref)
        cr_ref[...] = jnp.zeros_like(cr_ref)

    sub = 8
    row = lax.broadcasted_iota(jnp.int32, (sub, LRU_W), 0)
    for d, (u_ref, h_ref, c_ref) in enumerate(((uf_ref, hf_ref, cf_ref), (ur_ref, hr_ref, cr_ref))):
        u = u_ref[0]
        gates = jax.nn.sigmoid(jnp.dot(u.astype(BF16), wg_ref[d], preferred_element_type=F32) + bg_ref[d])
        a = jnp.exp(-(gates[:, :LRU_W] * sp_ref[d]))
        a_ref[...] = a
        b_ref[...] = jnp.sqrt(jnp.maximum(1.0 - a * a, 0.0)) * (gates[:, LRU_W:] * u)

        def tile(ti, carry, d=d, h_ref=h_ref):
            t0 = pl.multiple_of((ti if d == 0 else Tc // sub - 1 - ti) * sub, sub)
            av = a_ref[pl.ds(t0, sub), :]
            bv = b_ref[pl.ds(t0, sub), :]
            for s in (1, 2, 4):
                shift, known = (s, row >= s) if d == 0 else (sub - s, row < sub - s)
                a_prev = jnp.where(known, pltpu.roll(av, shift, 0), 1.0)
                b_prev = jnp.where(known, pltpu.roll(bv, shift, 0), 0.0)
                bv = bv + av * b_prev
                av = av * a_prev
            h = av * carry + bv
            h_ref[0, pl.ds(t0, sub), :] = h
            return h[sub - 1:sub] if d == 0 else h[0:1]

        c_ref[...] = lax.fori_loop(0, Tc // sub, tile, c_ref[...])


def _lru(u, Lc, wa, ba, wi, bi, lam):
    B, T, _ = u.shape
    S = T - Lc
    Tc = math.gcd(math.gcd(Lc, S), 256)
    nC, nL = Lc // Tc, S // Tc
    w_gates = jnp.stack([jnp.concatenate([_block_diag(wa[d]), _block_diag(wi[d])], axis=1)
                         for d in range(2)]).astype(BF16)
    b_gates = jnp.stack([jnp.concatenate([ba[d], bi[d]]) for d in range(2)])[:, None, :]
    decay_rate = (LRU_C * jax.nn.softplus(-lam))[:, None, :]
    fwd = pl.BlockSpec((1, Tc, LRU_W), lambda bi_, j: (bi_, j, 0))
    rev = pl.BlockSpec((1, Tc, LRU_W), lambda bi_, j: (bi_, _seg_rev_block(j, nC, nL), 0))

    def const(shape):
        return pl.BlockSpec(shape, lambda bi_, j: (0,) * len(shape))

    return pl.pallas_call(
        functools.partial(_lru_kernel, Tc=Tc),
        grid=(B, nC + nL),
        in_specs=[fwd, rev, const(w_gates.shape), const(b_gates.shape), const(decay_rate.shape)],
        out_specs=[fwd, rev],
        out_shape=[jax.ShapeDtypeStruct((B, T, LRU_W), F32)] * 2,
        scratch_shapes=[pltpu.VMEM((Tc, LRU_W), F32), pltpu.VMEM((Tc, LRU_W), F32),
                        pltpu.VMEM((1, LRU_W), F32), pltpu.VMEM((1, LRU_W), F32)],
        compiler_params=_params("parallel", "arbitrary"),
        name="lru",
    )(u, u, w_gates, b_gates, decay_rate)


def _mix_out_kernel(attn_ref, hf_ref, hr_ref, g0_ref, g1_ref, wo_ref, o_ref):
    gate = jnp.concatenate([g0_ref[0], g1_ref[0]], axis=-1)
    rec = ((hf_ref[0] + hr_ref[0]) * jax.nn.gelu(gate)).astype(BF16)
    o_ref[0] = (jnp.dot(attn_ref[0].astype(BF16), wo_ref[:ATTN_W, :], preferred_element_type=F32)
                + jnp.dot(rec, wo_ref[ATTN_W:, :], preferred_element_type=F32))


def _mix_out(attn, hf, hr, proj, w_out, t_off):
    B, N, _ = attn.shape
    D = w_out.shape[1]
    tm = math.gcd(math.gcd(N, 512), t_off) if t_off else min(512, N)
    half = LRU_W // 2
    gate_blk = (proj.shape[2] - LRU_W) // half
    tok = lambda b, i: (b, i, 0)
    return pl.pallas_call(
        _mix_out_kernel,
        grid=(B, N // tm),
        in_specs=[pl.BlockSpec((1, tm, ATTN_W), tok),
                  pl.BlockSpec((1, tm, LRU_W), lambda b, i: (b, i + t_off // tm, 0)),
                  pl.BlockSpec((1, tm, LRU_W), lambda b, i: (b, i + t_off // tm, 0)),
                  pl.BlockSpec((1, tm, half), lambda b, i: (b, i, gate_blk)),
                  pl.BlockSpec((1, tm, half), lambda b, i: (b, i, gate_blk + 1)),
                  pl.BlockSpec(w_out.shape, lambda b, i: (0, 0))],
        out_specs=pl.BlockSpec((1, tm, D), tok),
        out_shape=jax.ShapeDtypeStruct((B, N, D), F32),
        compiler_params=_params("parallel", "parallel"),
        name="mix_out",
    )(attn, hf, hr, proj, proj, w_out.astype(BF16))


def _softplus(z):
    return jnp.maximum(z, 0.0) + jnp.log(1.0 + jnp.exp(-jnp.abs(z)))


def _nt_dot(wt, x):
    return lax.dot_general(wt, x, (((1,), (1,)), ((), ())), preferred_element_type=F32)


def _rwkv_prep_kernel(cp_ref, c_ref, cn_ref, lp_ref, l_ref, ln_ref, mu_ref, wr_ref, wk_ref, wv_ref, w1_ref,
                      w2_ref, a1_ref, a2_ref, g1_ref, g2_ref, w0_ref, a0_ref, ka_ref, rk_ref,
                      dec_ref, aa_ref, k_ref, v_ref, r_ref, g_ref, bc_ref, *, B, ctx_tiles, tiles):
    i = pl.program_id(0)
    tm, D = l_ref.shape
    is_ctx = i < ctx_tiles
    h = jnp.where(is_ctx, c_ref[...], l_ref[...])
    seq_start = (i == 0) | (i == ctx_tiles)
    seq_end = (i == ctx_tiles - 1) | (i == tiles - 1)
    hp = jnp.where(seq_start, 0.0, jnp.where(is_ctx, cp_ref[...], lp_ref[...]))
    hn = jnp.where(seq_end, 0.0, jnp.where(is_ctx, cn_ref[...], ln_ref[...]))
    xx = 0.5 * (jnp.concatenate([hp, h[:tm - B]], axis=0) + jnp.concatenate([h[B:], hn], axis=0)) - h

    def mix(j):
        return (h + xx * mu_ref[j:j + 1, :]).astype(BF16)

    r = _nt_dot(wr_ref[...], mix(0))
    k = _nt_dot(wk_ref[...], mix(2))
    v = _nt_dot(wv_ref[...], mix(3))
    lw = jnp.tanh(_nt_dot(w1_ref[...], mix(1))).astype(BF16)
    wpre = jnp.dot(w2_ref[...], lw, preferred_element_type=F32)
    la = _nt_dot(a1_ref[...], mix(4)).astype(BF16)
    apre = jnp.dot(a2_ref[...], la, preferred_element_type=F32)
    gg = jax.nn.sigmoid(_nt_dot(g1_ref[...], mix(5))).astype(BF16)
    g_ref[...] = jnp.dot(g2_ref[...], gg, preferred_element_type=F32)
    hpg = LANES // B
    hd = RWKV_HD

    def to_scan_layout(x, ref):
        for g in range(RWKV_H // hpg):
            for c in range(tm // LANES):
                tiles = [x[(g * hpg + hh) * hd:(g * hpg + hh + 1) * hd, c * LANES:(c + 1) * LANES]
                         for hh in range(hpg)]
                for tl, tile in enumerate(_slab_transpose(tiles, B)):
                    ref[g, c * hpg + tl] = tile

    iclr = []
    for d in range(2):
        w_log = -_softplus(-(w0_ref[d] + wpre[d * D:(d + 1) * D])) - 0.5
        to_scan_layout(jnp.exp(-jnp.exp(w_log)), dec_ref.at[d])
        a = jax.nn.sigmoid(a0_ref[d] + apre[d * D:(d + 1) * D])
        iclr.append(a)
        to_scan_layout(a, aa_ref.at[d])
    kd_sum = k * (2.0 + (iclr[0] + iclr[1] - 2.0) * ka_ref[...])
    bc = jnp.sum((r * kd_sum * rk_ref[...]).reshape(RWKV_H, hd, tm), axis=1)
    for g in range(RWKV_H // hpg):
        for c in range(tm // LANES):
            rows = [jnp.broadcast_to(bc[g * hpg + hh:g * hpg + hh + 1, c * LANES:(c + 1) * LANES], (8, LANES))
                    for hh in range(hpg)]
            for tl, tile in enumerate(_slab_transpose(rows, B)):
                bc_ref[g, c * hpg + tl] = tile[0:1]
    to_scan_layout(k, k_ref)
    to_scan_layout(v, v_ref)
    to_scan_layout(r, r_ref)


def _slab_transpose(tiles, B):
    n = len(tiles)
    tiles = list(tiles)
    slab = lax.broadcasted_iota(jnp.int32, tiles[0].shape, 1) // B
    s = n // 2
    while s >= 1:
        upper = (slab & s) != 0
        for i in range(n):
            if i & s == 0:
                lo, hi = tiles[i], tiles[i + s]
                tiles[i] = jnp.where(upper, pltpu.roll(hi, s * B, 1), lo)
                tiles[i + s] = jnp.where(upper, hi, pltpu.roll(lo, LANES - s * B, 1))
        s //= 2
    return tiles


def _const_spec(shape):
    nd = len(shape)
    return pl.BlockSpec(shape, lambda i: (0,) * nd, pipeline_mode=pl.Buffered(1))


def _rwkv_prep(hc_tm, hl_tm, B, mu, w_rkv, w0, w1, w2, a0, a1, a2, g1, g2, k_a, r_k):
    D = hl_tm.shape[1]
    Lc, S = hc_tm.shape[0] // B, hl_tm.shape[0] // B
    T = Lc + S
    TB = T * B
    tm = 256
    tiles, ctx_tiles = TB // tm, Lc * B // tm
    hb = tm // B
    G = RWKV_H // (LANES // B)
    hd = RWKV_HD

    def t_bf16(w):
        return w.T.astype(BF16)

    zeros_w = jnp.zeros_like(w2[0].T)
    w2t = jnp.concatenate([jnp.concatenate([w2[0].T, zeros_w], 1),
                           jnp.concatenate([zeros_w, w2[1].T], 1)], 0).astype(BF16)
    zeros_a = jnp.zeros_like(a2[0].T)
    a2t = jnp.concatenate([jnp.concatenate([a2[0].T, zeros_a], 1),
                           jnp.concatenate([zeros_a, a2[1].T], 1)], 0).astype(BF16)
    consts = [mu, t_bf16(w_rkv[0]), t_bf16(w_rkv[1]), t_bf16(w_rkv[2]),
              t_bf16(jnp.concatenate([w1[0], w1[1]], 1)), w2t,
              t_bf16(jnp.concatenate([a1[0], a1[1]], 1)), a2t, t_bf16(g1), t_bf16(g2),
              w0.reshape(2, D, 1), a0.reshape(2, D, 1), k_a.reshape(D, 1), r_k.reshape(D, 1)]
    dir_spec = pl.BlockSpec((2, G, hb, hd, LANES), lambda i: (0, 0, i, 0, 0))
    all_spec = pl.BlockSpec((G, hb, hd, LANES), lambda i: (0, i, 0, 0))
    dir_shape = jax.ShapeDtypeStruct((2, G, T, hd, LANES), F32)
    all_shape = jax.ShapeDtypeStruct((G, T, hd, LANES), F32)

    def stream_specs(first, n_tiles, steps):
        def tile(i):
            return jnp.clip(i - first, 0, n_tiles - 1)
        return [pl.BlockSpec((B, D), lambda i: (jnp.maximum(tile(i) * hb - 1, 0), 0)),
                pl.BlockSpec((tm, D), lambda i: (tile(i), 0)),
                pl.BlockSpec((B, D), lambda i: (jnp.minimum((tile(i) + 1) * hb, steps - 1), 0))]

    return pl.pallas_call(
        functools.partial(_rwkv_prep_kernel, B=B, ctx_tiles=ctx_tiles, tiles=tiles),
        grid=(tiles,),
        in_specs=stream_specs(0, ctx_tiles, Lc) + stream_specs(ctx_tiles, tiles - ctx_tiles, S)
                 + [_const_spec(c.shape) for c in consts],
        out_specs=[dir_spec, dir_spec, all_spec, all_spec, all_spec,
                   pl.BlockSpec((D, tm), lambda i: (0, i)),
                   pl.BlockSpec((G, hb, 1, LANES), lambda i: (0, i, 0, 0))],
        out_shape=[dir_shape, dir_shape, all_shape, all_shape, all_shape,
                   jax.ShapeDtypeStruct((D, TB), F32),
                   jax.ShapeDtypeStruct((G, T, 1, LANES), F32)],
        compiler_params=_params("parallel"),
        name="rwkv_prep",
    )(hc_tm, hc_tm, hc_tm, hl_tm, hl_tm, hl_tm, *consts)


def _wkv_kernel(dec_ref, aa_ref, k_ref, v_ref, r_ref, kkc_ref, kac_ref, y_ref,
                s_ref, g_ref, p_ref, q_ref, sa_ref, *, Tc, nC):
    d = pl.program_id(0)
    j = pl.program_id(2)
    hd = RWKV_HD
    sub = 8

    @pl.when(j == 0)
    def _():
        s_ref[...] = jnp.zeros_like(s_ref)

    def partial_rows(x):
        return jnp.sum(x.reshape(hd // sub, sub, LANES), axis=0)

    pitch = hd + 1

    def put_partial(ref, v, x):
        ref[pl.ds(v, sub, stride=pitch), :] = partial_rows(x)

    def finish_rows(ref):
        acc = ref[pl.ds(0, hd), :]
        for q in range(1, sub):
            acc = acc + ref[pl.ds(q * pitch, hd), :]
        return acc

    def run(with_y):
        g_ref[...] = jnp.ones_like(g_ref)

        def step(s, _):
            tt = jnp.where(d == 0, s, Tc - 1 - s)
            aa = aa_ref[0, 0, tt]
            kk = k_ref[0, tt]
            kf = kk * kkc_ref[0]
            kn = kf * lax.rsqrt(jnp.maximum(jnp.sum(kf * kf, axis=0, keepdims=True), 1e-24))
            g_prev = g_ref[...]
            g = g_prev * dec_ref[0, 0, tt]
            g_inv = 1.0 / g
            g_ref[...] = g
            a_t = -(kn * g_prev)
            b_t = kn * aa * g_inv
            k_t = kk * (1.0 + (aa - 1.0) * kac_ref[0]) * g_inv
            r_t = r_ref[0, tt] * g if with_y else None

            def row_reduce(v, _):
                st = s_ref[v]
                put_partial(p_ref, v, st * a_t)
                if with_y:
                    put_partial(q_ref, v, st * r_t)
                return 0

            lax.fori_loop(0, hd, row_reduce, 0, unroll=8)
            sa = finish_rows(p_ref)
            sa_ref[...] = sa
            if with_y:
                b_r = jnp.sum(b_t * r_t, axis=0, keepdims=True)
                k_r = jnp.sum(k_t * r_t, axis=0, keepdims=True)
                y_ref[0, 0, tt] = finish_rows(q_ref) + sa * b_r + v_ref[0, tt] * k_r

            def row_update(v, _):
                s_ref[v] = s_ref[v] + sa_ref[pl.ds(v, 1), :] * b_t + v_ref[0, tt, pl.ds(v, 1), :] * k_t
                return 0

            lax.fori_loop(0, hd, row_update, 0, unroll=8)
            return 0

        lax.fori_loop(0, Tc, step, 0)

        def rescale(v, _):
            s_ref[v] = s_ref[v] * g_ref[...]
            return 0

        lax.fori_loop(0, hd, rescale, 0, unroll=8)

    @pl.when(j < nC)
    def _():
        run(False)

    @pl.when(j >= nC)
    def _():
        run(True)


def _wkv(dec, aa, k, v, r, k_k, k_a, B, Lc):
    _, G, T, hd, _ = dec.shape
    Tc = WKV_TC
    S = T - Lc
    nC, nL = Lc // Tc, S // Tc

    def tmap(d, j):
        return jnp.where(d == 0, j, _seg_rev_block(j, nC, nL))

    def lmap(d, j):
        jj = jnp.maximum(j - nC, 0)
        return jnp.where(d == 0, jj, nL - 1 - jj)

    dir_spec = pl.BlockSpec((1, 1, Tc, hd, LANES), lambda d, g, j: (d, g, tmap(d, j), 0, 0))
    all_spec = pl.BlockSpec((1, Tc, hd, LANES), lambda d, g, j: (g, tmap(d, j), 0, 0))
    const_spec = pl.BlockSpec((1, hd, LANES), lambda d, g, j: (g, 0, 0))
    return pl.pallas_call(
        functools.partial(_wkv_kernel, Tc=Tc, nC=nC),
        grid=(2, G, nC + nL),
        in_specs=[dir_spec, dir_spec, all_spec, all_spec, all_spec, const_spec, const_spec],
        out_specs=pl.BlockSpec((1, 1, Tc, hd, LANES), lambda d, g, j: (d, g, lmap(d, j), 0, 0)),
        out_shape=jax.ShapeDtypeStruct((2, G, S, hd, LANES), F32),
        scratch_shapes=[pltpu.VMEM((hd, hd, LANES), F32),
                        pltpu.VMEM((hd, LANES), F32),
                        pltpu.VMEM(((hd + 1) * 8, LANES), F32),
                        pltpu.VMEM(((hd + 1) * 8, LANES), F32),
                        pltpu.VMEM((hd, LANES), F32)],
        compiler_params=_params("parallel", "parallel", "arbitrary"),
        name="wkv",
    )(dec, aa, k, v, r, _lane_const(k_k, B), _lane_const(k_a, B))


def _lane_const(c, B):
    hpg = LANES // B
    return jnp.repeat(c.reshape(RWKV_H // hpg, hpg, RWKV_HD).transpose(0, 2, 1), B, axis=-1)


def _rwkv_finish_kernel(y_ref, v_ref, bc_ref, g_ref, gng_ref, gnb_ref, wo_ref, o_ref, *, B):
    G, steps, hd, _ = v_ref.shape
    hpg = LANES // B
    rows = []
    for g in range(G):
        head_cols = [[] for _ in range(hpg)]
        for c in range(steps // hpg):
            tiles = []
            for tl in range(hpg):
                t = c * hpg + tl
                y = y_ref[0, g, t] + y_ref[1, g, t]
                mean = jnp.mean(y, axis=0, keepdims=True)
                var = jnp.mean(jnp.square(y - mean), axis=0, keepdims=True)
                yn = (y - mean) * lax.rsqrt(var + GN_EPS)
                tiles.append(yn * gng_ref[g] + gnb_ref[g] + bc_ref[g, t] * v_ref[g, t])
            for hh, tile in enumerate(_slab_transpose(tiles, B)):
                head_cols[hh].append(tile)
        rows.extend(jnp.concatenate(cols, axis=1) for cols in head_cols)
    o = (jnp.concatenate(rows, axis=0) * g_ref[...]).astype(BF16)
    out_t = jnp.dot(wo_ref[...], o, preferred_element_type=F32)
    o_ref[...] = out_t.T


def _rwkv_finish(y2, v, bc, g, gn_g, gn_b, w_o, B, Lc):
    _, G, S, hd, _ = y2.shape
    D = g.shape[0]
    tm = 256
    steps = tm // B
    off = Lc // steps
    return pl.pallas_call(
        functools.partial(_rwkv_finish_kernel, B=B),
        grid=(S // steps,),
        in_specs=[pl.BlockSpec((2, G, steps, hd, LANES), lambda i: (0, 0, i, 0, 0)),
                  pl.BlockSpec((G, steps, hd, LANES), lambda i: (0, i + off, 0, 0)),
                  pl.BlockSpec((G, steps, 1, LANES), lambda i: (0, i + off, 0, 0)),
                  pl.BlockSpec((D, tm), lambda i: (0, i + off)),
                  _const_spec((G, hd, LANES)), _const_spec((G, hd, LANES)), _const_spec((D, D))],
        out_specs=pl.BlockSpec((tm, D), lambda i: (i, 0)),
        out_shape=jax.ShapeDtypeStruct((S * B, D), F32),
        compiler_params=_params("parallel"),
        name="rwkv_finish",
    )(y2, v, bc, g, _lane_const(gn_g, B), _lane_const(gn_b, B), w_o.T.astype(BF16))


def _rwkv7_mixer(hl, hc, mu, w_rkv, w0, w1, w2, a0, a1, a2, g1, g2, k_k, k_a, r_k, gn_g, gn_b, w_o):
    S, B, D = hl.shape
    Lc = hc.shape[0]
    dec, aa, k, v, r, g, bc = _rwkv_prep(hc.reshape(Lc * B, D), hl.reshape(S * B, D), B, mu, w_rkv, w0, w1,
                                         w2, a0, a1, a2, g1, g2, k_a, r_k.reshape(D))
    y2 = _wkv(dec, aa, k, v, r, k_k, k_a, B, Lc)
    return _rwkv_finish(y2, v, bc, g, gn_g, gn_b, w_o, B, Lc).reshape(S, B, D)


def _moe_kernel(idx_prev_ref, idx_ref, idx_next_ref, gate_ref, h_ref, w1_ref, w3_ref, w2_ref, o_ref,
                xa_ref, xb_ref, ya_ref, yb_ref, *, cap, chunks):
    bb = h_ref.shape[0]
    rows = bb * cap
    e = pl.program_id(1)
    n_e = pl.num_programs(1)
    batch = 8

    def token_rows(t):
        return pl.ds(pl.multiple_of(t * chunks, chunks), chunks)

    def scatter_batch(ids_ref, y_ref, first, keep):
        bi = first // cap
        slots = [first + u for u in range(batch)]
        dst = [token_rows(ids_ref[0, 0, s]) for s in slots]
        vals = [y_ref[token_rows(s), :] for s in slots]
        if keep is not None:
            vals = [jnp.where(keep, v, 0.0) for v in vals]
        sums = [o_ref[bi, dst[u], :] + vals[u] for u in range(batch)]
        for u in range(batch):
            o_ref[bi, dst[u], :] = sums[u]

    @pl.when(e == 0)
    def _():
        o_ref[...] = jnp.zeros_like(o_ref)
        yb_ref[...] = jnp.zeros_like(yb_ref)
        for bi in range(bb):
            def gather(r, _, bi=bi):
                slot = bi * cap + r
                xa_ref[token_rows(slot), :] = h_ref[bi, token_rows(idx_ref[0, 0, slot]), :]
                return 0

            lax.fori_loop(0, cap, gather, 0, unroll=8)

    def stage(x_cur, x_next, y_cur, y_prev):
        x = _from_chunk_rows(x_cur, rows).astype(BF16)
        h1 = jnp.dot(x, w1_ref[0, 0], preferred_element_type=F32)
        h3 = jnp.dot(x, w3_ref[0, 0], preferred_element_type=F32)
        hid = (h1 * jax.nn.sigmoid(h1) * h3).astype(BF16)
        _to_chunk_rows(y_cur, jnp.dot(hid, w2_ref[0, 0], preferred_element_type=F32) * gate_ref[0])
        for slot in range(rows):
            x_next[token_rows(slot), :] = h_ref[slot // cap, token_rows(idx_next_ref[0, 0, slot]), :]
        for first in range(0, rows, batch):
            scatter_batch(idx_prev_ref, y_prev, first, e > 0)

    @pl.when(e % 2 == 0)
    def _():
        stage(xa_ref, xb_ref, ya_ref, yb_ref)

    @pl.when(e % 2 == 1)
    def _():
        stage(xb_ref, xa_ref, yb_ref, ya_ref)

    def drain(y_ref):
        for first in range(0, rows, batch):
            scatter_batch(idx_ref, y_ref, first, None)

    @pl.when((e == n_e - 1) & (e % 2 == 0))
    def _():
        drain(ya_ref)

    @pl.when((e == n_e - 1) & (e % 2 == 1))
    def _():
        drain(yb_ref)


def _moe(h, aff, layer, w1, w3, w2):
    B, _, N = aff.shape
    _, E, D, FF = w1.shape
    chunks = D // LANES
    cap = CAPACITY * N // E
    bb = max(1, min(B, MOE_ROWS // cap))
    gate, idx = lax.top_k(aff, cap)

    def group(t):
        return t.reshape(B // bb, bb, E, cap).transpose(0, 2, 1, 3).reshape(B // bb * E, bb * cap)

    rows = bb * cap
    idx_g = group(idx.astype(jnp.int32))[:, None, :]

    def idx_spec(off):
        return pl.BlockSpec((1, 1, rows), lambda b, e: (b * E + jnp.clip(e + off, 0, E - 1), 0, 0),
                            memory_space=pltpu.SMEM)

    scratch = pltpu.VMEM((rows * chunks, LANES), F32)
    return pl.pallas_call(
        functools.partial(_moe_kernel, cap=cap, chunks=chunks),
        grid=(B // bb, E),
        in_specs=[idx_spec(-1), idx_spec(0), idx_spec(1),
                  pl.BlockSpec((1, rows, 1), lambda b, e: (b * E + e, 0, 0)),
                  pl.BlockSpec((bb, N * chunks, LANES), lambda b, e: (b, 0, 0), pipeline_mode=pl.Buffered(1)),
                  pl.BlockSpec((1, 1, D, FF), lambda b, e: (layer, e, 0, 0)),
                  pl.BlockSpec((1, 1, D, FF), lambda b, e: (layer, e, 0, 0)),
                  pl.BlockSpec((1, 1, FF, D), lambda b, e: (layer, e, 0, 0))],
        out_specs=pl.BlockSpec((bb, N * chunks, LANES), lambda b, e: (b, 0, 0)),
        out_shape=jax.ShapeDtypeStruct((B, N * chunks, LANES), F32),
        scratch_shapes=[scratch] * 4,
        compiler_params=_params("parallel", "arbitrary"),
        name="moe",
    )(idx_g, idx_g, idx_g, group(gate)[:, :, None], h, w1, w3, w2)


def _block_diag(w):
    H, bi, bj = w.shape
    eye = jnp.eye(H, dtype=w.dtype)
    return jnp.einsum('hij,hg->higj', w, eye).reshape(H * bi, H * bj)


def _dwconv(u, w, b):
    n = u.shape[1]
    up = jnp.pad(u, ((0, 0), (CONV_LEFT, CONV_W - 1 - CONV_LEFT), (0, 0)))
    out = up[:, 0:n] * w[0]
    for j in range(1, CONV_W):
        out = out + up[:, j:j + n] * w[j]
    return out + b


def _attn_lru_mixer(proj_l, proj_c, cos, sin, w_out, sink, conv_w, conv_b, wa, ba, wi, bi, lam):
    Lc = proj_c.shape[1]
    qk = _rope(proj_l, cos, sin)
    attn_l = _attention(sink, qk, proj_l, proj_c)
    attn_c = _ctx_attention(sink, proj_c)

    u0, g0 = ATTN_W + 2 * KV_W, ATTN_W + 2 * KV_W + LRU_W
    u = jnp.concatenate([_dwconv(proj_c[..., u0:g0], conv_w, conv_b),
                         _dwconv(proj_l[..., u0:g0], conv_w, conv_b)], axis=1)
    hf, hr = _lru(u, Lc, wa, ba, wi, bi, lam)
    yl = _mix_out(attn_l, hf, hr, proj_l, w_out, Lc)
    yc = _mix_out(attn_c, hf, hr, proj_c, w_out, 0)
    return yl, yc


def kernel(x, c, ctx, c_ctx, mod_w, mod_b, norm_mix, norm_ffn, router_w, exp_w1, exp_w3, exp_w2, mix_in, mix_out, attn_sink, lru_conv_w, lru_conv_b, lru_wa, lru_ba, lru_wi, lru_bi, lru_lam, rw_mu, rw_rkv, rw_w0, rw_w1, rw_w2, rw_a0, rw_a1, rw_a2, rw_g1, rw_g2, rw_kk, rw_ka, rw_rk, rw_gn_g, rw_gn_b, rw_wo, final_norm):
    B, S, D = x.shape
    Lc = ctx.shape[1]
    depth = mod_w.shape[0]
    assert depth == 2 and S % QBLOCK == 0
    assert LANES % B == 0 and RWKV_H % (LANES // B) == 0 and B % 8 == 0
    assert Lc % WKV_TC == 0 and S % WKV_TC == 0 and (Lc * B) % 256 == 0
    cos, sin = _rope_tables(S)

    n_rows = -(-(B + 1) // 8) * 8
    cond = jnp.concatenate([jax.nn.silu(c), jax.nn.silu(c_ctx)[None],
                            jnp.zeros((n_rows - B - 1, D), F32)], axis=0)

    mods_l, mods_c = [], []
    for layer in range(depth):
        mod = _mm(cond, mod_w[layer], tm=n_rows, tn=1024, precise=True) + mod_b[layer]
        mods_l.append([t[:, None, :] for t in jnp.split(mod[:B], 6, axis=-1)])
        mods_c.append([jnp.broadcast_to(t[None], (B, 1, D)) for t in jnp.split(mod[B:B + 1], 6, axis=-1)])

    w1b, w3b, w2b = exp_w1.astype(BF16), exp_w3.astype(BF16), exp_w2.astype(BF16)
    xl, xc = x, ctx
    hl = hc = None
    for layer in range(depth):
        last = layer == depth - 1
        m_l, m_c = mods_l[layer], mods_c[layer]
        if layer % 2 == 0:
            i = layer // 2
            assert layer == 0
            w_in = mix_in[i].astype(BF16)
            proj_l = _norm_mm(xl, norm_mix[layer], m_l[0], m_l[1], w_in)
            proj_c = _norm_mm(xc, norm_mix[layer], m_c[0], m_c[1], w_in)
            yl, yc = _attn_lru_mixer(proj_l, proj_c, cos, sin, mix_out[i], attn_sink[i], lru_conv_w[i],
                                     lru_conv_b[i], lru_wa[i], lru_ba[i], lru_wi[i], lru_bi[i], lru_lam[i])
        else:
            i = layer // 2
            yl = _rwkv7_mixer(hl, hc, rw_mu[i], rw_rkv[i], rw_w0[i], rw_w1[i], rw_w2[i], rw_a0[i], rw_a1[i],
                              rw_a2[i], rw_g1[i], rw_g2[i], rw_kk[i], rw_ka[i], rw_rk[i], rw_gn_g[i],
                              rw_gn_b[i], rw_wo[i])
            yc = None
        if layer % 2 == 0:
            xl, hl, aff_l = _resid_norm(xl, yl, m_l[2], norm_ffn[layer], m_l[3], m_l[4], out_rows=True,
                                        router_w=router_w[layer])
        else:
            xl, hl, aff_l = _resid_norm_tm(xl, yl, m_l[2], norm_ffn[layer], m_l[3], m_l[4], delta_tm=True,
                                           h_tm=False, router_w=router_w[layer])
        moe_l = _moe(hl, aff_l, layer, w1b, w3b, w2b)
        if last:
            zero = jnp.zeros((B, 1, D), F32)
            return _resid_norm(xl, moe_l, m_l[5], final_norm, zero, zero, delta_rows=True, want_x=False)
        assert (layer + 1) % 2 == 1
        n_l, n_c = mods_l[layer + 1], mods_c[layer + 1]
        xl, hl = _resid_norm_tm(xl, moe_l, m_l[5], norm_mix[layer + 1], n_l[0], n_l[1], delta_tm=False, h_tm=True)
        xc, hc, aff_c = _resid_norm(xc, yc, m_c[2], norm_ffn[layer], m_c[3], m_c[4], out_rows=True,
                                    router_w=router_w[layer])
        moe_c = _moe(hc, aff_c, layer, w1b, w3b, w2b)
        xc, hc = _resid_norm_tm(xc, moe_c, m_c[5], norm_mix[layer + 1], n_c[0], n_c[1], delta_tm=False, h_tm=True)
```

```python
import functools
import math

import jax
import jax.numpy as jnp
from jax import lax
from jax.experimental import pallas as pl
from jax.experimental.pallas import tpu as pltpu

F32 = jnp.float32
BF16 = jnp.bfloat16

GRID_W = 64
N_HEADS_ATTN = 8
N_KV = 2
HEAD_DIM = 64
AXIS_DIM = HEAD_DIM // 2
ATTN_W = N_HEADS_ATTN * HEAD_DIM
KV_W = N_KV * HEAD_DIM
WINDOW = 128
QBLOCK = 128
ROPE_BASE = 10000.0
LRU_W = 512
LRU_BLOCKS = 8
CONV_W = 4
CONV_LEFT = 2
LRU_C = 8.0
RWKV_H = 16
RWKV_HD = 64
GN_EPS = 64e-5
N_EXPERTS = 16
CAPACITY = 2
NORM_EPS = 1e-6
NEG_INF = -1e30

LANES = 128
VMEM_LIMIT_BYTES = 56 * 1024 * 1024
WKV_TC = 32
MOE_ROWS = 256

def _params(*sem):
    return pltpu.CompilerParams(dimension_semantics=sem, vmem_limit_bytes=VMEM_LIMIT_BYTES)


def _mm_kernel(x_ref, w_ref, o_ref, *, precise):
    if precise:
        o_ref[...] = jnp.dot(x_ref[...], w_ref[...], precision=lax.Precision.HIGHEST,
                             preferred_element_type=F32)
    else:
        o_ref[...] = jnp.dot(x_ref[...].astype(BF16), w_ref[...].astype(BF16),
                             preferred_element_type=F32)


def _mm(x, w, *, tm=512, tn=None, precise=False):
    M, K = x.shape
    N = w.shape[1]
    tm = min(tm, M)
    tn = N if tn is None else min(tn, N)
    assert M % tm == 0 and N % tn == 0, (M, tm, N, tn)
    return pl.pallas_call(
        functools.partial(_mm_kernel, precise=precise),
        grid=(M // tm, N // tn),
        in_specs=[pl.BlockSpec((tm, K), lambda i, j: (i, 0)),
                  pl.BlockSpec((K, tn), lambda i, j: (0, j))],
        out_specs=pl.BlockSpec((tm, tn), lambda i, j: (i, j)),
        out_shape=jax.ShapeDtypeStruct((M, N), F32),
        compiler_params=_params("parallel", "parallel"),
        name="mm",
    )(x, w)


def _resid_norm_kernel(*refs, has_delta, delta_rows, want_x, out_rows, route):
    refs = list(refs)
    x_ref = refs.pop(0)
    d_ref, gate_ref = (refs.pop(0), refs.pop(0)) if has_delta else (None, None)
    g_ref, sh_ref, sc_ref = refs.pop(0), refs.pop(0), refs.pop(0)
    wr_ref = refs.pop(0) if route else None
    xo_ref = refs.pop(0) if (has_delta and want_x) else None
    h_ref = refs.pop(0)
    aff_ref = refs.pop(0) if route else None
    x = x_ref[0]
    if has_delta:
        delta = _from_chunk_rows(d_ref.at[0], x.shape[0]) if delta_rows else d_ref[0]
        x = x + gate_ref[0] * delta
        if want_x:
            xo_ref[0] = x
    y = x * lax.rsqrt(jnp.mean(x * x, axis=-1, keepdims=True) + NORM_EPS) * g_ref[...]
    h = y * (1.0 + sc_ref[0]) + sh_ref[0]
    if out_rows:
        _to_chunk_rows(h_ref.at[0], h)
    else:
        h_ref[0] = h
    if route:
        logits = lax.dot_general(wr_ref[...], h, (((1,), (1,)), ((), ())),
                                 precision=lax.Precision.HIGHEST, preferred_element_type=F32)
        e = jnp.exp(logits - jnp.max(logits, axis=0, keepdims=True))
        aff_ref[0] = e / jnp.sum(e, axis=0, keepdims=True)


def _from_chunk_rows(ref, n):
    chunks = ref.shape[0] // n
    return jnp.concatenate([ref[pl.ds(j, n, stride=chunks), :] for j in range(chunks)], axis=-1)


def _to_chunk_rows(ref, x):
    n = x.shape[0]
    chunks = x.shape[1] // LANES
    for j in range(chunks):
        ref[pl.ds(j, n, stride=chunks), :] = x[:, j * LANES:(j + 1) * LANES]


def _resid_norm(x, delta, gate, g, shift, scale, *, delta_rows=False, want_x=True, out_rows=False,
                router_w=None):
    B, N, D = x.shape
    ts = min(512, N)
    has_delta = delta is not None
    chunks = D // LANES
    tok = pl.BlockSpec((1, ts, D), lambda b, i: (b, i, 0))
    tok_rows = pl.BlockSpec((1, ts * chunks, LANES), lambda b, i: (b, i, 0))
    per_b = pl.BlockSpec((1, 1, D), lambda b, i: (b, 0, 0))
    in_specs, args = [tok], [x]
    if has_delta:
        in_specs += [tok_rows if delta_rows else tok, per_b]
        args += [delta, gate]
    in_specs += [pl.BlockSpec((1, D), lambda b, i: (0, 0)), per_b, per_b]
    args += [g.reshape(1, D), shift, scale]
    route = router_w is not None
    if route:
        E = router_w.shape[1]
        in_specs.append(pl.BlockSpec((E, D), lambda b, i: (0, 0)))
        args.append(router_w.T)
    out_specs, out_shape = [], []
    if has_delta and want_x:
        out_specs.append(tok)
        out_shape.append(jax.ShapeDtypeStruct((B, N, D), F32))
    out_specs.append(tok_rows if out_rows else tok)
    out_shape.append(jax.ShapeDtypeStruct((B, N * chunks, LANES) if out_rows else (B, N, D), F32))
    if route:
        out_specs.append(pl.BlockSpec((1, E, ts), lambda b, i: (b, 0, i)))
        out_shape.append(jax.ShapeDtypeStruct((B, E, N), F32))
    outs = pl.pallas_call(
        functools.partial(_resid_norm_kernel, has_delta=has_delta, delta_rows=delta_rows, want_x=want_x,
                          out_rows=out_rows, route=route),
        grid=(B, N // ts),
        in_specs=in_specs,
        out_specs=out_specs,
        out_shape=out_shape,
        compiler_params=_params("parallel", "parallel"),
        name="resid_norm",
    )(*args)
    return outs if len(outs) > 1 else outs[0]


def _resid_norm_tm_kernel(*refs, delta_tm, h_tm, route):
    refs = list(refs)
    x_ref, d_ref, gate_ref, g_ref, sh_ref, sc_ref = (refs.pop(0) for _ in range(6))
    wr_ref = refs.pop(0) if route else None
    xo_ref, h_ref = refs.pop(0), refs.pop(0)
    aff_ref = refs.pop(0) if route else None
    nb, ts, _ = x_ref.shape
    for bi in range(nb):
        delta = d_ref[:, bi, :] if delta_tm else _from_chunk_rows(d_ref.at[bi], ts)
        x = x_ref[bi] + gate_ref[bi] * delta
        xo_ref[bi] = x
        y = x * lax.rsqrt(jnp.mean(x * x, axis=-1, keepdims=True) + NORM_EPS) * g_ref[...]
        h = y * (1.0 + sc_ref[bi]) + sh_ref[bi]
        if h_tm:
            h_ref[:, bi, :] = h
        else:
            _to_chunk_rows(h_ref.at[bi], h)
        if route:
            logits = lax.dot_general(wr_ref[...], h, (((1,), (1,)), ((), ())),
                                     precision=lax.Precision.HIGHEST, preferred_element_type=F32)
            e = jnp.exp(logits - jnp.max(logits, axis=0, keepdims=True))
            aff_ref[bi] = e / jnp.sum(e, axis=0, keepdims=True)


def _resid_norm_tm(x, delta, gate, g, shift, scale, *, delta_tm, h_tm, router_w=None):
    B, N, D = x.shape
    nb, ts = 8, min(LANES, N)
    chunks = D // LANES
    tok = pl.BlockSpec((nb, ts, D), lambda b, i: (b, i, 0))
    tok_rows = pl.BlockSpec((nb, ts * chunks, LANES), lambda b, i: (b, i, 0))
    tok_tm = pl.BlockSpec((ts, nb, D), lambda b, i: (i, b, 0))
    per_b = pl.BlockSpec((nb, 1, D), lambda b, i: (b, 0, 0))
    in_specs = [tok, tok_tm if delta_tm else tok_rows, per_b, pl.BlockSpec((1, D), lambda b, i: (0, 0)),
                per_b, per_b]
    args = [x, delta, gate, g.reshape(1, D), shift, scale]
    out_specs = [tok, tok_tm if h_tm else tok_rows]
    out_shape = [jax.ShapeDtypeStruct((B, N, D), F32),
                 jax.ShapeDtypeStruct((N, B, D) if h_tm else (B, N * chunks, LANES), F32)]
    route = router_w is not None
    if route:
        E = router_w.shape[1]
        in_specs.append(pl.BlockSpec((E, D), lambda b, i: (0, 0)))
        args.append(router_w.T)
        out_specs.append(pl.BlockSpec((nb, E, ts), lambda b, i: (b, 0, i)))
        out_shape.append(jax.ShapeDtypeStruct((B, E, N), F32))
    return pl.pallas_call(
        functools.partial(_resid_norm_tm_kernel, delta_tm=delta_tm, h_tm=h_tm, route=route),
        grid=(B // nb, N // ts),
        in_specs=in_specs,
        out_specs=out_specs,
        out_shape=out_shape,
        compiler_params=_params("parallel", "parallel"),
        name="resid_norm_tm",
    )(*args)


def _norm_mm_kernel(x_ref, g_ref, sh_ref, sc_ref, w_ref, o_ref):
    x = x_ref[0]
    y = x * lax.rsqrt(jnp.mean(x * x, axis=-1, keepdims=True) + NORM_EPS) * g_ref[...]
    h = (y * (1.0 + sc_ref[0]) + sh_ref[0]).astype(BF16)
    o_ref[0] = jnp.dot(h, w_ref[...], preferred_element_type=F32)


def _norm_mm(x, g, shift, scale, w):
    B, N, D = x.shape
    M = w.shape[1]
    ts = min(512, N)
    per_b = pl.BlockSpec((1, 1, D), lambda b, i: (b, 0, 0))
    return pl.pallas_call(
        _norm_mm_kernel,
        grid=(B, N // ts),
        in_specs=[pl.BlockSpec((1, ts, D), lambda b, i: (b, i, 0)),
                  pl.BlockSpec((1, D), lambda b, i: (0, 0)), per_b, per_b,
                  pl.BlockSpec((D, M), lambda b, i: (0, 0))],
        out_specs=pl.BlockSpec((1, ts, M), lambda b, i: (b, i, 0)),
        out_shape=jax.ShapeDtypeStruct((B, N, M), F32),
        compiler_params=_params("parallel", "parallel"),
        name="norm_mm",
    )(x, g.reshape(1, D), shift, scale, w)


def _rope_kernel(x_ref, cos_ref, sin_ref, o_ref):
    width = x_ref.shape[-1]
    for c in range(width // LANES):
        sl = slice(c * LANES, (c + 1) * LANES)
        x = x_ref[0, :, sl]
        lane = lax.broadcasted_iota(jnp.int32, x.shape, 1)
        first_half = (lane % AXIS_DIM) < (AXIS_DIM // 2)
        partner = jnp.where(first_half, pltpu.roll(x, LANES - AXIS_DIM // 2, 1),
                            pltpu.roll(x, AXIS_DIM // 2, 1))
        o_ref[0, :, sl] = x * cos_ref[:, sl] + partner * sin_ref[:, sl]


def _rope_tables(S):
    rows = S // GRID_W
    row = jnp.repeat(jnp.arange(rows), GRID_W).astype(F32)
    col = jnp.tile(jnp.arange(GRID_W), rows).astype(F32)
    inv = ROPE_BASE ** (-jnp.arange(0, AXIS_DIM, 2, dtype=F32) / AXIS_DIM)
    ang_row, ang_col = row[:, None] * inv, col[:, None] * inv

    def axis_tables(ang):
        c, s = jnp.cos(ang), jnp.sin(ang)
        return jnp.concatenate([c, c], -1), jnp.concatenate([-s, s], -1)

    cr, sr = axis_tables(ang_row)
    cc, sc = axis_tables(ang_col)
    cos_h = jnp.concatenate([cr, cc], -1)
    sin_h = jnp.concatenate([sr, sc], -1)
    q_scale = HEAD_DIM ** -0.5
    cos = jnp.concatenate([jnp.tile(cos_h, (1, N_HEADS_ATTN)) * q_scale, jnp.tile(cos_h, (1, N_KV))], -1)
    sin = jnp.concatenate([jnp.tile(sin_h, (1, N_HEADS_ATTN)) * q_scale, jnp.tile(sin_h, (1, N_KV))], -1)
    return cos, sin


def _rope(proj, cos, sin):
    B, S, _ = proj.shape
    width = ATTN_W + KV_W
    ts = min(512, S)
    return pl.pallas_call(
        _rope_kernel,
        grid=(B, S // ts),
        in_specs=[pl.BlockSpec((1, ts, width), lambda b, i: (b, i, 0)),
                  pl.BlockSpec((ts, width), lambda b, i: (i, 0)),
                  pl.BlockSpec((ts, width), lambda b, i: (i, 0))],
        out_specs=pl.BlockSpec((1, ts, width), lambda b, i: (b, i, 0)),
        out_shape=jax.ShapeDtypeStruct((B, S, width), F32),
        compiler_params=_params("parallel", "parallel"),
        name="rope",
    )(proj, cos, sin)


def _attn_heads(q, kcat, vcat, valid, sink_ref):
    outs = []
    G = N_HEADS_ATTN // N_KV
    Q = q.shape[0]
    for j in range(N_KV):
        kj = kcat[:, j * HEAD_DIM:(j + 1) * HEAD_DIM]
        vj = vcat[:, j * HEAD_DIM:(j + 1) * HEAD_DIM]
        heads = range(j * G, (j + 1) * G)
        qs = jnp.concatenate([q[:, h * HEAD_DIM:(h + 1) * HEAD_DIM] for h in heads], axis=0).astype(BF16)
        s = lax.dot_general(qs, kj, (((1,), (1,)), ((), ())), preferred_element_type=F32)
        if valid is not None:
            s = jnp.where(valid[None], s.reshape(G, Q, -1), NEG_INF).reshape(G * Q, -1)
        sink = jnp.concatenate([jnp.full((Q, 1), sink_ref[h], F32) for h in heads], axis=0)
        m = jnp.maximum(jnp.max(s, axis=-1, keepdims=True), sink)
        e = jnp.exp(s - m)
        den = jnp.sum(e, axis=-1, keepdims=True) + jnp.exp(sink - m)
        o = jnp.dot(e.astype(BF16), vj, preferred_element_type=F32) / den
        outs.extend(o[g * Q:(g + 1) * Q] for g in range(G))
    return jnp.concatenate(outs, axis=-1)


def _attn_kernel(sink_ref, q_ref, kp_ref, kc_ref, kn_ref, vp_ref, vc_ref, vn_ref, ck_ref, cv_ref,
                 o_ref, *, S, Lc):
    i = pl.program_id(1)
    kcat = jnp.concatenate([ck_ref[0], kp_ref[0], kc_ref[0], kn_ref[0]], axis=0).astype(BF16)
    vcat = jnp.concatenate([cv_ref[0], vp_ref[0], vc_ref[0], vn_ref[0]], axis=0).astype(BF16)
    L = Lc + 3 * QBLOCK
    row = lax.broadcasted_iota(jnp.int32, (QBLOCK, L), 0)
    col = lax.broadcasted_iota(jnp.int32, (QBLOCK, L), 1)
    rel = col - Lc - QBLOCK - row
    kpos = (i - 1) * QBLOCK + col - Lc
    valid = (col < Lc) | ((jnp.abs(rel) <= WINDOW) & (kpos >= 0) & (kpos < S))
    o_ref[0] = _attn_heads(q_ref[0], kcat, vcat, valid, sink_ref)


def _attention(sink, qk, proj_l, proj_c):
    B, S, _ = qk.shape
    Lc = proj_c.shape[1]
    nb = S // QBLOCK
    kcol, vcol = ATTN_W // KV_W, ATTN_W // KV_W + 1

    def blk(colblk, off):
        return pl.BlockSpec((1, QBLOCK, KV_W),
                            lambda b, i: (b, jnp.clip(i + off, 0, nb - 1), colblk))

    return pl.pallas_call(
        functools.partial(_attn_kernel, S=S, Lc=Lc),
        grid=(B, nb),
        in_specs=[pl.BlockSpec(memory_space=pltpu.SMEM),
                  pl.BlockSpec((1, QBLOCK, ATTN_W), lambda b, i: (b, i, 0)),
                  blk(kcol, -1), blk(kcol, 0), blk(kcol, 1),
                  blk(vcol, -1), blk(vcol, 0), blk(vcol, 1),
                  pl.BlockSpec((1, Lc, KV_W), lambda b, i: (b, 0, kcol)),
                  pl.BlockSpec((1, Lc, KV_W), lambda b, i: (b, 0, vcol))],
        out_specs=pl.BlockSpec((1, QBLOCK, ATTN_W), lambda b, i: (b, i, 0)),
        out_shape=jax.ShapeDtypeStruct((B, S, ATTN_W), F32),
        compiler_params=_params("parallel", "parallel"),
        name="attention",
    )(sink, qk, qk, qk, qk, proj_l, proj_l, proj_l, proj_c, proj_c)


def _ctx_attn_kernel(sink_ref, q_ref, ck_ref, cv_ref, o_ref):
    q = q_ref[0] * (HEAD_DIM ** -0.5)
    o_ref[0] = _attn_heads(q, ck_ref[0].astype(BF16), cv_ref[0].astype(BF16), None, sink_ref)


def _ctx_attention(sink, proj_c):
    B, Lc, _ = proj_c.shape
    kcol, vcol = ATTN_W // KV_W, ATTN_W // KV_W + 1
    return pl.pallas_call(
        _ctx_attn_kernel,
        grid=(B,),
        in_specs=[pl.BlockSpec(memory_space=pltpu.SMEM),
                  pl.BlockSpec((1, Lc, ATTN_W), lambda b: (b, 0, 0)),
                  pl.BlockSpec((1, Lc, KV_W), lambda b: (b, 0, kcol)),
                  pl.BlockSpec((1, Lc, KV_W), lambda b: (b, 0, vcol))],
        out_specs=pl.BlockSpec((1, Lc, ATTN_W), lambda b: (b, 0, 0)),
        out_shape=jax.ShapeDtypeStruct((B, Lc, ATTN_W), F32),
        compiler_params=_params("parallel"),
        name="ctx_attention",
    )(sink, proj_c, proj_c, proj_c)


def _seg_rev_block(j, nC, nL):
    return jnp.where(j < nC, nC - 1 - j, nC + nL - 1 - (j - nC))


def _lru_kernel(uf_ref, ur_ref, wg_ref, bg_ref, sp_ref, hf_ref, hr_ref, a_ref, b_ref, cf_ref, cr_ref, *, Tc):
    @pl.when(pl.program_id(1) == 0)
    def _():
        cf_ref[...] = jnp.zeros_like(cf_ref)
        cr_ref[...] = jnp.zeros_like(cr_ref)

    sub = 8
    row = lax.broadcasted_iota(jnp.int32, (sub, LRU_W), 0)
    for d, (u_ref, h_ref, c_ref) in enumerate(((uf_ref, hf_ref, cf_ref), (ur_ref, hr_ref, cr_ref))):
        u = u_ref[0]
        gates = jax.nn.sigmoid(jnp.dot(u.astype(BF16), wg_ref[d], preferred_element_type=F32) + bg_ref[d])
        a = jnp.exp(-(gates[:, :LRU_W] * sp_ref[d]))
        a_ref[...] = a
        b_ref[...] = jnp.sqrt(jnp.maximum(1.0 - a * a, 0.0)) * (gates[:, LRU_W:] * u)

        def tile(ti, carry, d=d, h_ref=h_ref):
            t0 = pl.multiple_of((ti if d == 0 else Tc // sub - 1 - ti) * sub, sub)
            av = a_ref[pl.ds(t0, sub), :]
            bv = b_ref[pl.ds(t0, sub), :]
            for s in (1, 2, 4):
                shift, known = (s, row >= s) if d == 0 else (sub - s, row < sub - s)
                a_prev = jnp.where(known, pltpu.roll(av, shift, 0), 1.0)
                b_prev = jnp.where(known, pltpu.roll(bv, shift, 0), 0.0)
                bv = bv + av * b_prev
                av = av * a_prev
            h = av * carry + bv
            h_ref[0, pl.ds(t0, sub), :] = h
            return h[sub - 1:sub] if d == 0 else h[0:1]

        c_ref[...] = lax.fori_loop(0, Tc // sub, tile, c_ref[...])


def _lru(u, Lc, wa, ba, wi, bi, lam):
    B, T, _ = u.shape
    S = T - Lc
    Tc = math.gcd(math.gcd(Lc, S), 256)
    nC, nL = Lc // Tc, S // Tc
    w_gates = jnp.stack([jnp.concatenate([_block_diag(wa[d]), _block_diag(wi[d])], axis=1)
                         for d in range(2)]).astype(BF16)
    b_gates = jnp.stack([jnp.concatenate([ba[d], bi[d]]) for d in range(2)])[:, None, :]
    decay_rate = (LRU_C * jax.nn.softplus(-lam))[:, None, :]
    fwd = pl.BlockSpec((1, Tc, LRU_W), lambda bi_, j: (bi_, j, 0))
    rev = pl.BlockSpec((1, Tc, LRU_W), lambda bi_, j: (bi_, _seg_rev_block(j, nC, nL), 0))

    def const(shape):
        return pl.BlockSpec(shape, lambda bi_, j: (0,) * len(shape))

    return pl.pallas_call(
        functools.partial(_lru_kernel, Tc=Tc),
        grid=(B, nC + nL),
        in_specs=[fwd, rev, const(w_gates.shape), const(b_gates.shape), const(decay_rate.shape)],
        out_specs=[fwd, rev],
        out_shape=[jax.ShapeDtypeStruct((B, T, LRU_W), F32)] * 2,
        scratch_shapes=[pltpu.VMEM((Tc, LRU_W), F32), pltpu.VMEM((Tc, LRU_W), F32),
                        pltpu.VMEM((1, LRU_W), F32), pltpu.VMEM((1, LRU_W), F32)],
        compiler_params=_params("parallel", "arbitrary"),
        name="lru",
    )(u, u, w_gates, b_gates, decay_rate)


def _mix_out_kernel(attn_ref, hf_ref, hr_ref, g0_ref, g1_ref, wo_ref, o_ref):
    gate = jnp.concatenate([g0_ref[0], g1_ref[0]], axis=-1)
    rec = ((hf_ref[0] + hr_ref[0]) * jax.nn.gelu(gate)).astype(BF16)
    o_ref[0] = (jnp.dot(attn_ref[0].astype(BF16), wo_ref[:ATTN_W, :], preferred_element_type=F32)
                + jnp.dot(rec, wo_ref[ATTN_W:, :], preferred_element_type=F32))


def _mix_out(attn, hf, hr, proj, w_out, t_off):
    B, N, _ = attn.shape
    D = w_out.shape[1]
    tm = math.gcd(math.gcd(N, 512), t_off) if t_off else min(512, N)
    half = LRU_W // 2
    gate_blk = (proj.shape[2] - LRU_W) // half
    tok = lambda b, i: (b, i, 0)
    return pl.pallas_call(
        _mix_out_kernel,
        grid=(B, N // tm),
        in_specs=[pl.BlockSpec((1, tm, ATTN_W), tok),
                  pl.BlockSpec((1, tm, LRU_W), lambda b, i: (b, i + t_off // tm, 0)),
                  pl.BlockSpec((1, tm, LRU_W), lambda b, i: (b, i + t_off // tm, 0)),
                  pl.BlockSpec((1, tm, half), lambda b, i: (b, i, gate_blk)),
                  pl.BlockSpec((1, tm, half), lambda b, i: (b, i, gate_blk + 1)),
                  pl.BlockSpec(w_out.shape, lambda b, i: (0, 0))],
        out_specs=pl.BlockSpec((1, tm, D), tok),
        out_shape=jax.ShapeDtypeStruct((B, N, D), F32),
        compiler_params=_params("parallel", "parallel"),
        name="mix_out",
    )(attn, hf, hr, proj, proj, w_out.astype(BF16))


def _softplus(z):
    return jnp.maximum(z, 0.0) + jnp.log(1.0 + jnp.exp(-jnp.abs(z)))


def _nt_dot(wt, x):
    return lax.dot_general(wt, x, (((1,), (1,)), ((), ())), preferred_element_type=F32)


def _rwkv_prep_kernel(cp_ref, c_ref, cn_ref, lp_ref, l_ref, ln_ref, mu_ref, wr_ref, wk_ref, wv_ref, w1_ref,
                      w2_ref, a1_ref, a2_ref, g1_ref, g2_ref, w0_ref, a0_ref, ka_ref, rk_ref, perm_ref,
                      dec_ref, aa_ref, k_ref, v_ref, r_ref, g_ref, bc_ref, *, B, ctx_tiles, tiles):
    i = pl.program_id(0)
    tm, D = l_ref.shape
    is_ctx = i < ctx_tiles
    h = jnp.where(is_ctx, c_ref[...], l_ref[...])
    seq_start = (i == 0) | (i == ctx_tiles)
    seq_end = (i == ctx_tiles - 1) | (i == tiles - 1)
    hp = jnp.where(seq_start, 0.0, jnp.where(is_ctx, cp_ref[...], lp_ref[...]))
    hn = jnp.where(seq_end, 0.0, jnp.where(is_ctx, cn_ref[...], ln_ref[...]))
    xx = 0.5 * (jnp.concatenate([hp, h[:tm - B]], axis=0) + jnp.concatenate([h[B:], hn], axis=0)) - h

    def mix(j):
        return (h + xx * mu_ref[j:j + 1, :]).astype(BF16)

    hpg = LANES // B
    hd = RWKV_HD

    def to_scan_layout(x, ref):
        for g in range(RWKV_H // hpg):
            for c in range(tm // LANES):
                tiles = [x[(g * hpg + hh) * hd:(g * hpg + hh + 1) * hd, c * LANES:(c + 1) * LANES]
                         for hh in range(hpg)]
                for tl, tile in enumerate(_slab_transpose(tiles, B)):
                    ref[g, c * hpg + tl] = tile

    v = _nt_dot(wv_ref[...], mix(3))
    to_scan_layout(v, v_ref)
    r = _nt_dot(wr_ref[...], mix(0))
    to_scan_layout(r, r_ref)
    k = _nt_dot(wk_ref[...], mix(2))
    to_scan_layout(k, k_ref)
    lw = jnp.tanh(_nt_dot(w1_ref[...], mix(1))).astype(BF16)
    wpre = jnp.dot(w2_ref[...], lw, preferred_element_type=F32)
    la = _nt_dot(a1_ref[...], mix(4)).astype(BF16)
    apre = jnp.dot(a2_ref[...], la, preferred_element_type=F32)
    gg = jax.nn.sigmoid(_nt_dot(g1_ref[...], mix(5))).astype(BF16)
    g_ref[...] = jnp.dot(g2_ref[...], gg, preferred_element_type=F32)

    def to_scan_layout_mxu(xs, refs):
        groups = RWKV_H // hpg
        for c in range(tm // LANES):
            lhs = jnp.concatenate(
                [jnp.concatenate([x[(g * hpg + hh) * hd:(g * hpg + hh + 1) * hd, c * LANES:(c + 1) * LANES]
                                  for hh in range(hpg)], axis=1)
                 for x in xs for g in range(groups)], axis=0)
            hi = lhs.astype(BF16)
            rest = lhs - hi.astype(F32)
            mid = rest.astype(BF16)
            lo = (rest - mid.astype(F32)).astype(BF16)
            out = (jnp.dot(hi, perm_ref[...], preferred_element_type=F32)
                   + jnp.dot(mid, perm_ref[...], preferred_element_type=F32)
                   + jnp.dot(lo, perm_ref[...], preferred_element_type=F32))
            for n, ref in enumerate(refs):
                for g in range(groups):
                    blk = out[(n * groups + g) * hd:(n * groups + g + 1) * hd]
                    for tl in range(hpg):
                        ref[g, c * hpg + tl] = blk[:, tl * LANES:(tl + 1) * LANES]

    iclr, decay = [], []
    for d in range(2):
        w_log = -_softplus(-(w0_ref[d] + wpre[d * D:(d + 1) * D])) - 0.5
        decay.append(jnp.exp(-jnp.exp(w_log)))
        a = jax.nn.sigmoid(a0_ref[d] + apre[d * D:(d + 1) * D])
        iclr.append(a)
        to_scan_layout(a, aa_ref.at[d])
    to_scan_layout_mxu(decay, [dec_ref.at[0], dec_ref.at[1]])
    kd_sum = k * (2.0 + (iclr[0] + iclr[1] - 2.0) * ka_ref[...])
    bc = jnp.sum((r * kd_sum * rk_ref[...]).reshape(RWKV_H, hd, tm), axis=1)
    for g in range(RWKV_H // hpg):
        for c in range(tm // LANES):
            rows = [jnp.broadcast_to(bc[g * hpg + hh:g * hpg + hh + 1, c * LANES:(c + 1) * LANES], (8, LANES))
                    for hh in range(hpg)]
            for tl, tile in enumerate(_slab_transpose(rows, B)):
                bc_ref[g, c * hpg + tl] = tile[0:1]


def _slab_permutation(B):
    n = LANES // B
    src = jnp.arange(n * LANES).reshape(n, n, B).transpose(1, 0, 2).reshape(-1)
    return (jnp.arange(n * LANES)[:, None] == src[None, :]).astype(BF16)


def _slab_transpose(tiles, B):
    n = len(tiles)
    tiles = list(tiles)
    slab = lax.broadcasted_iota(jnp.int32, tiles[0].shape, 1) // B
    s = n // 2
    while s >= 1:
        upper = (slab & s) != 0
        for i in range(n):
            if i & s == 0:
                lo, hi = tiles[i], tiles[i + s]
                tiles[i] = jnp.where(upper, pltpu.roll(hi, s * B, 1), lo)
                tiles[i + s] = jnp.where(upper, hi, pltpu.roll(lo, LANES - s * B, 1))
        s //= 2
    return tiles


def _const_spec(shape):
    nd = len(shape)
    return pl.BlockSpec(shape, lambda i: (0,) * nd, pipeline_mode=pl.Buffered(1))


def _rwkv_prep(hc_tm, hl_tm, B, mu, w_rkv, w0, w1, w2, a0, a1, a2, g1, g2, k_a, r_k):
    D = hl_tm.shape[1]
    Lc, S = hc_tm.shape[0] // B, hl_tm.shape[0] // B
    T = Lc + S
    TB = T * B
    tm = 256
    tiles, ctx_tiles = TB // tm, Lc * B // tm
    hb = tm // B
    G = RWKV_H // (LANES // B)
    hd = RWKV_HD

    def t_bf16(w):
        return w.T.astype(BF16)

    zeros_w = jnp.zeros_like(w2[0].T)
    w2t = jnp.concatenate([jnp.concatenate([w2[0].T, zeros_w], 1),
                           jnp.concatenate([zeros_w, w2[1].T], 1)], 0).astype(BF16)
    zeros_a = jnp.zeros_like(a2[0].T)
    a2t = jnp.concatenate([jnp.concatenate([a2[0].T, zeros_a], 1),
                           jnp.concatenate([zeros_a, a2[1].T], 1)], 0).astype(BF16)
    consts = [mu, t_bf16(w_rkv[0]), t_bf16(w_rkv[1]), t_bf16(w_rkv[2]),
              t_bf16(jnp.concatenate([w1[0], w1[1]], 1)), w2t,
              t_bf16(jnp.concatenate([a1[0], a1[1]], 1)), a2t, t_bf16(g1), t_bf16(g2),
              w0.reshape(2, D, 1), a0.reshape(2, D, 1), k_a.reshape(D, 1), r_k.reshape(D, 1),
              _slab_permutation(B)]
    dir_spec = pl.BlockSpec((2, G, hb, hd, LANES), lambda i: (0, 0, i, 0, 0))
    all_spec = pl.BlockSpec((G, hb, hd, LANES), lambda i: (0, i, 0, 0))
    dir_shape = jax.ShapeDtypeStruct((2, G, T, hd, LANES), F32)
    all_shape = jax.ShapeDtypeStruct((G, T, hd, LANES), F32)

    def stream_specs(first, n_tiles, steps):
        def tile(i):
            return jnp.clip(i - first, 0, n_tiles - 1)
        return [pl.BlockSpec((B, D), lambda i: (jnp.maximum(tile(i) * hb - 1, 0), 0)),
                pl.BlockSpec((tm, D), lambda i: (tile(i), 0)),
                pl.BlockSpec((B, D), lambda i: (jnp.minimum((tile(i) + 1) * hb, steps - 1), 0))]

    return pl.pallas_call(
        functools.partial(_rwkv_prep_kernel, B=B, ctx_tiles=ctx_tiles, tiles=tiles),
        grid=(tiles,),
        in_specs=stream_specs(0, ctx_tiles, Lc) + stream_specs(ctx_tiles, tiles - ctx_tiles, S)
                 + [_const_spec(c.shape) for c in consts],
        out_specs=[dir_spec, dir_spec, all_spec, all_spec, all_spec,
                   pl.BlockSpec((D, tm), lambda i: (0, i)),
                   pl.BlockSpec((G, hb, 1, LANES), lambda i: (0, i, 0, 0))],
        out_shape=[dir_shape, dir_shape, all_shape, all_shape, all_shape,
                   jax.ShapeDtypeStruct((D, TB), F32),
                   jax.ShapeDtypeStruct((G, T, 1, LANES), F32)],
        compiler_params=_params("parallel"),
        name="rwkv_prep",
    )(hc_tm, hc_tm, hc_tm, hl_tm, hl_tm, hl_tm, *consts)


def _wkv_kernel(dec_ref, aa_ref, k_ref, v_ref, r_ref, kkc_ref, kac_ref, y_ref,
                s_ref, g_ref, p_ref, q_ref, sa_ref, *, Tc, nC):
    d = pl.program_id(0)
    j = pl.program_id(2)
    hd = RWKV_HD
    sub = 8

    @pl.when(j == 0)
    def _():
        s_ref[...] = jnp.zeros_like(s_ref)

    def partial_rows(x):
        return jnp.sum(x.reshape(hd // sub, sub, LANES), axis=0)

    pitch = hd + 1

    def put_partial(ref, v, x):
        ref[pl.ds(v, sub, stride=pitch), :] = partial_rows(x)

    def finish_rows(ref):
        acc = ref[pl.ds(0, hd), :]
        for q in range(1, sub):
            acc = acc + ref[pl.ds(q * pitch, hd), :]
        return acc

    def run(with_y):
        g_ref[...] = jnp.ones_like(g_ref)

        def step(s, _):
            tt = jnp.where(d == 0, s, Tc - 1 - s)
            aa = aa_ref[0, 0, tt]
            kk = k_ref[0, tt]
            kf = kk * kkc_ref[0]
            kn = kf * lax.rsqrt(jnp.maximum(jnp.sum(kf * kf, axis=0, keepdims=True), 1e-24))
            g_prev = g_ref[...]
            g = g_prev * dec_ref[0, 0, tt]
            g_inv = 1.0 / g
            g_ref[...] = g
            a_t = -(kn * g_prev)
            b_t = kn * aa * g_inv
            k_t = kk * (1.0 + (aa - 1.0) * kac_ref[0]) * g_inv
            r_t = r_ref[0, tt] * g if with_y else None

            def row_reduce(v, _):
                st = s_ref[v]
                put_partial(p_ref, v, st * a_t)
                if with_y:
                    put_partial(q_ref, v, st * r_t)
                return 0

            lax.fori_loop(0, hd, row_reduce, 0, unroll=8)
            sa = finish_rows(p_ref)
            sa_ref[...] = sa
            if with_y:
                b_r = jnp.sum(b_t * r_t, axis=0, keepdims=True)
                k_r = jnp.sum(k_t * r_t, axis=0, keepdims=True)
                y_ref[0, 0, tt] = finish_rows(q_ref) + sa * b_r + v_ref[0, tt] * k_r

            def row_update(v, _):
                s_ref[v] = s_ref[v] + sa_ref[pl.ds(v, 1), :] * b_t + v_ref[0, tt, pl.ds(v, 1), :] * k_t
                return 0

            lax.fori_loop(0, hd, row_update, 0, unroll=8)
            return 0

        lax.fori_loop(0, Tc, step, 0)

        def rescale(v, _):
            s_ref[v] = s_ref[v] * g_ref[...]
            return 0

        lax.fori_loop(0, hd, rescale, 0, unroll=8)

    @pl.when(j < nC)
    def _():
        run(False)

    @pl.when(j >= nC)
    def _():
        run(True)


def _wkv(dec, aa, k, v, r, k_k, k_a, B, Lc):
    _, G, T, hd, _ = dec.shape
    Tc = WKV_TC
    S = T - Lc
    nC, nL = Lc // Tc, S // Tc

    def tmap(d, j):
        return jnp.where(d == 0, j, _seg_rev_block(j, nC, nL))

    def lmap(d, j):
        jj = jnp.maximum(j - nC, 0)
        return jnp.where(d == 0, jj, nL - 1 - jj)

    dir_spec = pl.BlockSpec((1, 1, Tc, hd, LANES), lambda d, g, j: (d, g, tmap(d, j), 0, 0))
    all_spec = pl.BlockSpec((1, Tc, hd, LANES), lambda d, g, j: (g, tmap(d, j), 0, 0))
    const_spec = pl.BlockSpec((1, hd, LANES), lambda d, g, j: (g, 0, 0))
    return pl.pallas_call(
        functools.partial(_wkv_kernel, Tc=Tc, nC=nC),
        grid=(2, G, nC + nL),
        in_specs=[dir_spec, dir_spec, all_spec, all_spec, all_spec, const_spec, const_spec],
        out_specs=pl.BlockSpec((1, 1, Tc, hd, LANES), lambda d, g, j: (d, g, lmap(d, j), 0, 0)),
        out_shape=jax.ShapeDtypeStruct((2, G, S, hd, LANES), F32),
        scratch_shapes=[pltpu.VMEM((hd, hd, LANES), F32),
                        pltpu.VMEM((hd, LANES), F32),
                        pltpu.VMEM(((hd + 1) * 8, LANES), F32),
                        pltpu.VMEM(((hd + 1) * 8, LANES), F32),
                        pltpu.VMEM((hd, LANES), F32)],
        compiler_params=_params("parallel", "parallel", "arbitrary"),
        name="wkv",
    )(dec, aa, k, v, r, _lane_const(k_k, B), _lane_const(k_a, B))


def _lane_const(c, B):
    hpg = LANES // B
    return jnp.repeat(c.reshape(RWKV_H // hpg, hpg, RWKV_HD).transpose(0, 2, 1), B, axis=-1)


def _rwkv_finish_kernel(y_ref, v_ref, bc_ref, g_ref, gng_ref, gnb_ref, wo_ref, o_ref, *, B):
    G, steps, hd, _ = v_ref.shape
    hpg = LANES // B
    rows = []
    for g in range(G):
        head_cols = [[] for _ in range(hpg)]
        for c in range(steps // hpg):
            tiles = []
            for tl in range(hpg):
                t = c * hpg + tl
                y = y_ref[0, g, t] + y_ref[1, g, t]
                mean = jnp.mean(y, axis=0, keepdims=True)
                var = jnp.mean(jnp.square(y - mean), axis=0, keepdims=True)
                yn = (y - mean) * lax.rsqrt(var + GN_EPS)
                tiles.append(yn * gng_ref[g] + gnb_ref[g] + bc_ref[g, t] * v_ref[g, t])
            for hh, tile in enumerate(_slab_transpose(tiles, B)):
                head_cols[hh].append(tile)
        rows.extend(jnp.concatenate(cols, axis=1) for cols in head_cols)
    o = (jnp.concatenate(rows, axis=0) * g_ref[...]).astype(BF16)
    out_t = jnp.dot(wo_ref[...], o, preferred_element_type=F32)
    o_ref[...] = out_t.T


def _rwkv_finish(y2, v, bc, g, gn_g, gn_b, w_o, B, Lc):
    _, G, S, hd, _ = y2.shape
    D = g.shape[0]
    tm = 256
    steps = tm // B
    off = Lc // steps
    return pl.pallas_call(
        functools.partial(_rwkv_finish_kernel, B=B),
        grid=(S // steps,),
        in_specs=[pl.BlockSpec((2, G, steps, hd, LANES), lambda i: (0, 0, i, 0, 0)),
                  pl.BlockSpec((G, steps, hd, LANES), lambda i: (0, i + off, 0, 0)),
                  pl.BlockSpec((G, steps, 1, LANES), lambda i: (0, i + off, 0, 0)),
                  pl.BlockSpec((D, tm), lambda i: (0, i + off)),
                  _const_spec((G, hd, LANES)), _const_spec((G, hd, LANES)), _const_spec((D, D))],
        out_specs=pl.BlockSpec((tm, D), lambda i: (i, 0)),
        out_shape=jax.ShapeDtypeStruct((S * B, D), F32),
        compiler_params=_params("parallel"),
        name="rwkv_finish",
    )(y2, v, bc, g, _lane_const(gn_g, B), _lane_const(gn_b, B), w_o.T.astype(BF16))


def _rwkv7_mixer(hl, hc, mu, w_rkv, w0, w1, w2, a0, a1, a2, g1, g2, k_k, k_a, r_k, gn_g, gn_b, w_o):
    S, B, D = hl.shape
    Lc = hc.shape[0]
    dec, aa, k, v, r, g, bc = _rwkv_prep(hc.reshape(Lc * B, D), hl.reshape(S * B, D), B, mu, w_rkv, w0, w1,
                                         w2, a0, a1, a2, g1, g2, k_a, r_k.reshape(D))
    y2 = _wkv(dec, aa, k, v, r, k_k, k_a, B, Lc)
    return _rwkv_finish(y2, v, bc, g, gn_g, gn_b, w_o, B, Lc).reshape(S, B, D)


def _moe_kernel(idx_prev_ref, idx_ref, idx_next_ref, gate_ref, h_ref, w1_ref, w3_ref, w2_ref, o_ref,
                xa_ref, xb_ref, ya_ref, yb_ref, *, cap, chunks):
    bb = h_ref.shape[0]
    rows = bb * cap
    e = pl.program_id(1)
    n_e = pl.num_programs(1)
    batch = 8

    def token_rows(t):
        return pl.ds(pl.multiple_of(t * chunks, chunks), chunks)

    def scatter_batch(ids_ref, y_ref, first, keep):
        bi = first // cap
        slots = [first + u for u in range(batch)]
        dst = [token_rows(ids_ref[0, 0, s]) for s in slots]
        vals = [y_ref[token_rows(s), :] for s in slots]
        if keep is not None:
            vals = [jnp.where(keep, v, 0.0) for v in vals]
        sums = [o_ref[bi, dst[u], :] + vals[u] for u in range(batch)]
        for u in range(batch):
            o_ref[bi, dst[u], :] = sums[u]

    @pl.when(e == 0)
    def _():
        o_ref[...] = jnp.zeros_like(o_ref)
        yb_ref[...] = jnp.zeros_like(yb_ref)
        for bi in range(bb):
            def gather(r, _, bi=bi):
                slot = bi * cap + r
                xa_ref[token_rows(slot), :] = h_ref[bi, token_rows(idx_ref[0, 0, slot]), :]
                return 0

            lax.fori_loop(0, cap, gather, 0, unroll=8)

    def stage(x_cur, x_next, y_cur, y_prev):
        x = _from_chunk_rows(x_cur, rows).astype(BF16)
        h1 = jnp.dot(x, w1_ref[0, 0], preferred_element_type=F32)
        h3 = jnp.dot(x, w3_ref[0, 0], preferred_element_type=F32)
        hid = (h1 * jax.nn.sigmoid(h1) * h3).astype(BF16)
        _to_chunk_rows(y_cur, jnp.dot(hid, w2_ref[0, 0], preferred_element_type=F32) * gate_ref[0])
        for slot in range(rows):
            x_next[token_rows(slot), :] = h_ref[slot // cap, token_rows(idx_next_ref[0, 0, slot]), :]
        for first in range(0, rows, batch):
            scatter_batch(idx_prev_ref, y_prev, first, e > 0)

    @pl.when(e % 2 == 0)
    def _():
        stage(xa_ref, xb_ref, ya_ref, yb_ref)

    @pl.when(e % 2 == 1)
    def _():
        stage(xb_ref, xa_ref, yb_ref, ya_ref)

    def drain(y_ref):
        for first in range(0, rows, batch):
            scatter_batch(idx_ref, y_ref, first, None)

    @pl.when((e == n_e - 1) & (e % 2 == 0))
    def _():
        drain(ya_ref)

    @pl.when((e == n_e - 1) & (e % 2 == 1))
    def _():
        drain(yb_ref)


def _moe(h, aff, layer, w1, w3, w2):
    B, _, N = aff.shape
    _, E, D, FF = w1.shape
    chunks = D // LANES
    cap = CAPACITY * N // E
    bb = max(1, min(B, MOE_ROWS // cap))
    gate, idx = lax.top_k(aff, cap)

    def group(t):
        return t.reshape(B // bb, bb, E, cap).transpose(0, 2, 1, 3).reshape(B // bb * E, bb * cap)

    rows = bb * cap
    idx_g = group(idx.astype(jnp.int32))[:, None, :]

    def idx_spec(off):
        return pl.BlockSpec((1, 1, rows), lambda b, e: (b * E + jnp.clip(e + off, 0, E - 1), 0, 0),
                            memory_space=pltpu.SMEM)

    scratch = pltpu.VMEM((rows * chunks, LANES), F32)
    return pl.pallas_call(
        functools.partial(_moe_kernel, cap=cap, chunks=chunks),
        grid=(B // bb, E),
        in_specs=[idx_spec(-1), idx_spec(0), idx_spec(1),
                  pl.BlockSpec((1, rows, 1), lambda b, e: (b * E + e, 0, 0)),
                  pl.BlockSpec((bb, N * chunks, LANES), lambda b, e: (b, 0, 0), pipeline_mode=pl.Buffered(1)),
                  pl.BlockSpec((1, 1, D, FF), lambda b, e: (layer, e, 0, 0)),
                  pl.BlockSpec((1, 1, D, FF), lambda b, e: (layer, e, 0, 0)),
                  pl.BlockSpec((1, 1, FF, D), lambda b, e: (layer, e, 0, 0))],
        out_specs=pl.BlockSpec((bb, N * chunks, LANES), lambda b, e: (b, 0, 0)),
        out_shape=jax.ShapeDtypeStruct((B, N * chunks, LANES), F32),
        scratch_shapes=[scratch] * 4,
        compiler_params=_params("parallel", "arbitrary"),
        name="moe",
    )(idx_g, idx_g, idx_g, group(gate)[:, :, None], h, w1, w3, w2)


def _block_diag(w):
    H, bi, bj = w.shape
    eye = jnp.eye(H, dtype=w.dtype)
    return jnp.einsum('hij,hg->higj', w, eye).reshape(H * bi, H * bj)


def _dwconv(u, w, b):
    n = u.shape[1]
    up = jnp.pad(u, ((0, 0), (CONV_LEFT, CONV_W - 1 - CONV_LEFT), (0, 0)))
    out = up[:, 0:n] * w[0]
    for j in range(1, CONV_W):
        out = out + up[:, j:j + n] * w[j]
    return out + b


def _attn_lru_mixer(proj_l, proj_c, cos, sin, w_out, sink, conv_w, conv_b, wa, ba, wi, bi, lam):
    Lc = proj_c.shape[1]
    qk = _rope(proj_l, cos, sin)
    attn_l = _attention(sink, qk, proj_l, proj_c)
    attn_c = _ctx_attention(sink, proj_c)

    u0, g0 = ATTN_W + 2 * KV_W, ATTN_W + 2 * KV_W + LRU_W
    u = jnp.concatenate([_dwconv(proj_c[..., u0:g0], conv_w, conv_b),
                         _dwconv(proj_l[..., u0:g0], conv_w, conv_b)], axis=1)
    hf, hr = _lru(u, Lc, wa, ba, wi, bi, lam)
    yl = _mix_out(attn_l, hf, hr, proj_l, w_out, Lc)
    yc = _mix_out(attn_c, hf, hr, proj_c, w_out, 0)
    return yl, yc


def kernel(x, c, ctx, c_ctx, mod_w, mod_b, norm_mix, norm_ffn, router_w, exp_w1, exp_w3, exp_w2, mix_in, mix_out, attn_sink, lru_conv_w, lru_conv_b, lru_wa, lru_ba, lru_wi, lru_bi, lru_lam, rw_mu, rw_rkv, rw_w0, rw_w1, rw_w2, rw_a0, rw_a1, rw_a2, rw_g1, rw_g2, rw_kk, rw_ka, rw_rk, rw_gn_g, rw_gn_b, rw_wo, final_norm):
    B, S, D = x.shape
    Lc = ctx.shape[1]
    depth = mod_w.shape[0]
    assert depth == 2 and S % QBLOCK == 0
    assert LANES % B == 0 and RWKV_H % (LANES // B) == 0 and B % 8 == 0
    assert Lc % WKV_TC == 0 and S % WKV_TC == 0 and (Lc * B) % 256 == 0
    cos, sin = _rope_tables(S)

    n_rows = -(-(B + 1) // 8) * 8
    cond = jnp.concatenate([jax.nn.silu(c), jax.nn.silu(c_ctx)[None],
                            jnp.zeros((n_rows - B - 1, D), F32)], axis=0)

    mods_l, mods_c = [], []
    for layer in range(depth):
        mod = _mm(cond, mod_w[layer], tm=n_rows, tn=1024, precise=True) + mod_b[layer]
        mods_l.append([t[:, None, :] for t in jnp.split(mod[:B], 6, axis=-1)])
        mods_c.append([jnp.broadcast_to(t[None], (B, 1, D)) for t in jnp.split(mod[B:B + 1], 6, axis=-1)])

    w1b, w3b, w2b = exp_w1.astype(BF16), exp_w3.astype(BF16), exp_w2.astype(BF16)
    xl, xc = x, ctx
    hl = hc = None
    for layer in range(depth):
        last = layer == depth - 1
        m_l, m_c = mods_l[layer], mods_c[layer]
        if layer % 2 == 0:
            i = layer // 2
            assert layer == 0
            w_in = mix_in[i].astype(BF16)
            proj_l = _norm_mm(xl, norm_mix[layer], m_l[0], m_l[1], w_in)
            proj_c = _norm_mm(xc, norm_mix[layer], m_c[0], m_c[1], w_in)
            yl, yc = _attn_lru_mixer(proj_l, proj_c, cos, sin, mix_out[i], attn_sink[i], lru_conv_w[i],
                                     lru_conv_b[i], lru_wa[i], lru_ba[i], lru_wi[i], lru_bi[i], lru_lam[i])
        else:
            i = layer // 2
            yl = _rwkv7_mixer(hl, hc, rw_mu[i], rw_rkv[i], rw_w0[i], rw_w1[i], rw_w2[i], rw_a0[i], rw_a1[i],
                              rw_a2[i], rw_g1[i], rw_g2[i], rw_kk[i], rw_ka[i], rw_rk[i], rw_gn_g[i],
                              rw_gn_b[i], rw_wo[i])
            yc = None
        if layer % 2 == 0:
            xl, hl, aff_l = _resid_norm(xl, yl, m_l[2], norm_ffn[layer], m_l[3], m_l[4], out_rows=True,
                                        router_w=router_w[layer])
        else:
            xl, hl, aff_l = _resid_norm_tm(xl, yl, m_l[2], norm_ffn[layer], m_l[3], m_l[4], delta_tm=True,
                                           h_tm=False, router_w=router_w[layer])
        moe_l = _moe(hl, aff_l, layer, w1b, w3b, w2b)
        if last:
            zero = jnp.zeros((B, 1, D), F32)
            return _resid_norm(xl, moe_l, m_l[5], final_norm, zero, zero, delta_rows=True, want_x=False)
        assert (layer + 1) % 2 == 1
        n_l, n_c = mods_l[layer + 1], mods_c[layer + 1]
        xl, hl = _resid_norm_tm(xl, moe_l, m_l[5], norm_mix[layer + 1], n_l[0], n_l[1], delta_tm=False, h_tm=True)
        xc, hc, aff_c = _resid_norm(xc, yc, m_c[2], norm_ffn[layer], m_c[3], m_c[4], out_rows=True,
                                    router_w=router_w[layer])
        moe_c = _moe(hc, aff_c, layer, w1b, w3b, w2b)
        xc, hc = _resid_norm_tm(xc, moe_c, m_c[5], norm_mix[layer + 1], n_c[0], n_c[1], delta_tm=False, h_tm=True)
```

```python
import functools
import math

import jax
import jax.numpy as jnp
from jax import lax
from jax.experimental import pallas as pl
from jax.experimental.pallas import tpu as pltpu

F32 = jnp.float32
BF16 = jnp.bfloat16

GRID_W = 64
N_HEADS_ATTN = 8
N_KV = 2
HEAD_DIM = 64
AXIS_DIM = HEAD_DIM // 2
ATTN_W = N_HEADS_ATTN * HEAD_DIM
KV_W = N_KV * HEAD_DIM
WINDOW = 128
QBLOCK = 128
ROPE_BASE = 10000.0
LRU_W = 512
LRU_BLOCKS = 8
CONV_W = 4
CONV_LEFT = 2
LRU_C = 8.0
RWKV_H = 16
RWKV_HD = 64
GN_EPS = 64e-5
N_EXPERTS = 16
CAPACITY = 2
NORM_EPS = 1e-6
NEG_INF = -1e30

LANES = 128
VMEM_LIMIT_BYTES = 56 * 1024 * 1024
WKV_TC = 32
MOE_ROWS = 256

def _params(*sem):
    return pltpu.CompilerParams(dimension_semantics=sem, vmem_limit_bytes=VMEM_LIMIT_BYTES)


def _mm_kernel(x_ref, w_ref, o_ref, *, precise):
    if precise:
        o_ref[...] = jnp.dot(x_ref[...], w_ref[...], precision=lax.Precision.HIGHEST,
                             preferred_element_type=F32)
    else:
        o_ref[...] = jnp.dot(x_ref[...].astype(BF16), w_ref[...].astype(BF16),
                             preferred_element_type=F32)


def _mm(x, w, *, tm=512, tn=None, precise=False):
    M, K = x.shape
    N = w.shape[1]
    tm = min(tm, M)
    tn = N if tn is None else min(tn, N)
    assert M % tm == 0 and N % tn == 0, (M, tm, N, tn)
    return pl.pallas_call(
        functools.partial(_mm_kernel, precise=precise),
        grid=(M // tm, N // tn),
        in_specs=[pl.BlockSpec((tm, K), lambda i, j: (i, 0)),
                  pl.BlockSpec((K, tn), lambda i, j: (0, j))],
        out_specs=pl.BlockSpec((tm, tn), lambda i, j: (i, j)),
        out_shape=jax.ShapeDtypeStruct((M, N), F32),
        compiler_params=_params("parallel", "parallel"),
        name="mm",
    )(x, w)


def _resid_norm_kernel(*refs, has_delta, delta_rows, want_x, out_rows, route):
    refs = list(refs)
    x_ref = refs.pop(0)
    d_ref, gate_ref = (refs.pop(0), refs.pop(0)) if has_delta else (None, None)
    g_ref, sh_ref, sc_ref = refs.pop(0), refs.pop(0), refs.pop(0)
    wr_ref = refs.pop(0) if route else None
    xo_ref = refs.pop(0) if (has_delta and want_x) else None
    h_ref = refs.pop(0)
    aff_ref = refs.pop(0) if route else None
    x = x_ref[0]
    if has_delta:
        delta = _from_chunk_rows(d_ref.at[0], x.shape[0]) if delta_rows else d_ref[0]
        x = x + gate_ref[0] * delta
        if want_x:
            xo_ref[0] = x
    y = x * lax.rsqrt(jnp.mean(x * x, axis=-1, keepdims=True) + NORM_EPS) * g_ref[...]
    h = y * (1.0 + sc_ref[0]) + sh_ref[0]
    if out_rows:
        _to_chunk_rows(h_ref.at[0], h)
    else:
        h_ref[0] = h
    if route:
        logits = lax.dot_general(wr_ref[...], h, (((1,), (1,)), ((), ())),
                                 precision=lax.Precision.HIGHEST, preferred_element_type=F32)
        e = jnp.exp(logits - jnp.max(logits, axis=0, keepdims=True))
        aff_ref[0] = e / jnp.sum(e, axis=0, keepdims=True)


def _from_chunk_rows(ref, n):
    chunks = ref.shape[0] // n
    return jnp.concatenate([ref[pl.ds(j, n, stride=chunks), :] for j in range(chunks)], axis=-1)


def _to_chunk_rows(ref, x):
    n = x.shape[0]
    chunks = x.shape[1] // LANES
    for j in range(chunks):
        ref[pl.ds(j, n, stride=chunks), :] = x[:, j * LANES:(j + 1) * LANES]


def _resid_norm(x, delta, gate, g, shift, scale, *, delta_rows=False, want_x=True, out_rows=False,
                router_w=None):
    B, N, D = x.shape
    ts = min(512, N)
    has_delta = delta is not None
    chunks = D // LANES
    tok = pl.BlockSpec((1, ts, D), lambda b, i: (b, i, 0))
    tok_rows = pl.BlockSpec((1, ts * chunks, LANES), lambda b, i: (b, i, 0))
    per_b = pl.BlockSpec((1, 1, D), lambda b, i: (b, 0, 0))
    in_specs, args = [tok], [x]
    if has_delta:
        in_specs += [tok_rows if delta_rows else tok, per_b]
        args += [delta, gate]
    in_specs += [pl.BlockSpec((1, D), lambda b, i: (0, 0)), per_b, per_b]
    args += [g.reshape(1, D), shift, scale]
    route = router_w is not None
    if route:
        E = router_w.shape[1]
        in_specs.append(pl.BlockSpec((E, D), lambda b, i: (0, 0)))
        args.append(router_w.T)
    out_specs, out_shape = [], []
    if has_delta and want_x:
        out_specs.append(tok)
        out_shape.append(jax.ShapeDtypeStruct((B, N, D), F32))
    out_specs.append(tok_rows if out_rows else tok)
    out_shape.append(jax.ShapeDtypeStruct((B, N * chunks, LANES) if out_rows else (B, N, D), F32))
    if route:
        out_specs.append(pl.BlockSpec((1, E, ts), lambda b, i: (b, 0, i)))
        out_shape.append(jax.ShapeDtypeStruct((B, E, N), F32))
    outs = pl.pallas_call(
        functools.partial(_resid_norm_kernel, has_delta=has_delta, delta_rows=delta_rows, want_x=want_x,
                          out_rows=out_rows, route=route),
        grid=(B, N // ts),
        in_specs=in_specs,
        out_specs=out_specs,
        out_shape=out_shape,
        compiler_params=_params("parallel", "parallel"),
        name="resid_norm",
    )(*args)
    return outs if len(outs) > 1 else outs[0]


def _resid_norm_tm_kernel(*refs, delta_tm, h_tm, route):
    refs = list(refs)
    x_ref, d_ref, gate_ref, g_ref, sh_ref, sc_ref = (refs.pop(0) for _ in range(6))
    wr_ref = refs.pop(0) if route else None
    xo_ref, h_ref = refs.pop(0), refs.pop(0)
    aff_ref = refs.pop(0) if route else None
    nb, ts, _ = x_ref.shape
    for bi in range(nb):
        delta = d_ref[:, bi, :] if delta_tm else _from_chunk_rows(d_ref.at[bi], ts)
        x = x_ref[bi] + gate_ref[bi] * delta
        xo_ref[bi] = x
        y = x * lax.rsqrt(jnp.mean(x * x, axis=-1, keepdims=True) + NORM_EPS) * g_ref[...]
        h = y * (1.0 + sc_ref[bi]) + sh_ref[bi]
        if h_tm:
            h_ref[:, bi, :] = h
        else:
            _to_chunk_rows(h_ref.at[bi], h)
        if route:
            logits = lax.dot_general(wr_ref[...], h, (((1,), (1,)), ((), ())),
                                     precision=lax.Precision.HIGHEST, preferred_element_type=F32)
            e = jnp.exp(logits - jnp.max(logits, axis=0, keepdims=True))
            aff_ref[bi] = e / jnp.sum(e, axis=0, keepdims=True)


def _resid_norm_tm(x, delta, gate, g, shift, scale, *, delta_tm, h_tm, router_w=None):
    B, N, D = x.shape
    nb, ts = 8, min(LANES, N)
    chunks = D // LANES
    tok = pl.BlockSpec((nb, ts, D), lambda b, i: (b, i, 0))
    tok_rows = pl.BlockSpec((nb, ts * chunks, LANES), lambda b, i: (b, i, 0))
    tok_tm = pl.BlockSpec((ts, nb, D), lambda b, i: (i, b, 0))
    per_b = pl.BlockSpec((nb, 1, D), lambda b, i: (b, 0, 0))
    in_specs = [tok, tok_tm if delta_tm else tok_rows, per_b, pl.BlockSpec((1, D), lambda b, i: (0, 0)),
                per_b, per_b]
    args = [x, delta, gate, g.reshape(1, D), shift, scale]
    out_specs = [tok, tok_tm if h_tm else tok_rows]
    out_shape = [jax.ShapeDtypeStruct((B, N, D), F32),
                 jax.ShapeDtypeStruct((N, B, D) if h_tm else (B, N * chunks, LANES), F32)]
    route = router_w is not None
    if route:
        E = router_w.shape[1]
        in_specs.append(pl.BlockSpec((E, D), lambda b, i: (0, 0)))
        args.append(router_w.T)
        out_specs.append(pl.BlockSpec((nb, E, ts), lambda b, i: (b, 0, i)))
        out_shape.append(jax.ShapeDtypeStruct((B, E, N), F32))
    return pl.pallas_call(
        functools.partial(_resid_norm_tm_kernel, delta_tm=delta_tm, h_tm=h_tm, route=route),
        grid=(B // nb, N // ts),
        in_specs=in_specs,
        out_specs=out_specs,
        out_shape=out_shape,
        compiler_params=_params("parallel", "parallel"),
        name="resid_norm_tm",
    )(*args)


def _norm_mm_kernel(x_ref, g_ref, sh_ref, sc_ref, w_ref, o_ref):
    x = x_ref[0]
    y = x * lax.rsqrt(jnp.mean(x * x, axis=-1, keepdims=True) + NORM_EPS) * g_ref[...]
    h = (y * (1.0 + sc_ref[0]) + sh_ref[0]).astype(BF16)
    o_ref[0] = jnp.dot(h, w_ref[...], preferred_element_type=F32)


def _norm_mm(x, g, shift, scale, w):
    B, N, D = x.shape
    M = w.shape[1]
    ts = min(512, N)
    per_b = pl.BlockSpec((1, 1, D), lambda b, i: (b, 0, 0))
    return pl.pallas_call(
        _norm_mm_kernel,
        grid=(B, N // ts),
        in_specs=[pl.BlockSpec((1, ts, D), lambda b, i: (b, i, 0)),
                  pl.BlockSpec((1, D), lambda b, i: (0, 0)), per_b, per_b,
                  pl.BlockSpec((D, M), lambda b, i: (0, 0))],
        out_specs=pl.BlockSpec((1, ts, M), lambda b, i: (b, i, 0)),
        out_shape=jax.ShapeDtypeStruct((B, N, M), F32),
        compiler_params=_params("parallel", "parallel"),
        name="norm_mm",
    )(x, g.reshape(1, D), shift, scale, w)


def _rotate(x, cos, sin):
    out = []
    for c in range(x.shape[1] // LANES):
        sl = slice(c * LANES, (c + 1) * LANES)
        xc = x[:, sl]
        lane = lax.broadcasted_iota(jnp.int32, xc.shape, 1)
        first_half = (lane % AXIS_DIM) < (AXIS_DIM // 2)
        partner = jnp.where(first_half, pltpu.roll(xc, LANES - AXIS_DIM // 2, 1),
                            pltpu.roll(xc, AXIS_DIM // 2, 1))
        out.append(xc * cos[:, sl] + partner * sin[:, sl])
    return jnp.concatenate(out, axis=1)


def _rope_tables(S):
    rows = S // GRID_W
    row = jnp.repeat(jnp.arange(rows), GRID_W).astype(F32)
    col = jnp.tile(jnp.arange(GRID_W), rows).astype(F32)
    inv = ROPE_BASE ** (-jnp.arange(0, AXIS_DIM, 2, dtype=F32) / AXIS_DIM)
    ang_row, ang_col = row[:, None] * inv, col[:, None] * inv

    def axis_tables(ang):
        c, s = jnp.cos(ang), jnp.sin(ang)
        return jnp.concatenate([c, c], -1), jnp.concatenate([-s, s], -1)

    cr, sr = axis_tables(ang_row)
    cc, sc = axis_tables(ang_col)
    cos_h = jnp.concatenate([cr, cc], -1)
    sin_h = jnp.concatenate([sr, sc], -1)
    q_scale = HEAD_DIM ** -0.5
    cos = jnp.concatenate([jnp.tile(cos_h, (1, N_HEADS_ATTN)) * q_scale, jnp.tile(cos_h, (1, N_KV))], -1)
    sin = jnp.concatenate([jnp.tile(sin_h, (1, N_HEADS_ATTN)) * q_scale, jnp.tile(sin_h, (1, N_KV))], -1)
    return cos, sin


def _attn_heads(q, kcat, vcat, valid, sink_ref):
    outs = []
    G = N_HEADS_ATTN // N_KV
    Q = q.shape[0]
    for j in range(N_KV):
        kj = kcat[:, j * HEAD_DIM:(j + 1) * HEAD_DIM]
        vj = vcat[:, j * HEAD_DIM:(j + 1) * HEAD_DIM]
        heads = range(j * G, (j + 1) * G)
        qs = jnp.concatenate([q[:, h * HEAD_DIM:(h + 1) * HEAD_DIM] for h in heads], axis=0).astype(BF16)
        s = lax.dot_general(qs, kj, (((1,), (1,)), ((), ())), preferred_element_type=F32)
        if valid is not None:
            s = jnp.where(valid[None], s.reshape(G, Q, -1), NEG_INF).reshape(G * Q, -1)
        sink = jnp.concatenate([jnp.full((Q, 1), sink_ref[h], F32) for h in heads], axis=0)
        m = jnp.maximum(jnp.max(s, axis=-1, keepdims=True), sink)
        e = jnp.exp(s - m)
        den = jnp.sum(e, axis=-1, keepdims=True) + jnp.exp(sink - m)
        o = jnp.dot(e.astype(BF16), vj, preferred_element_type=F32) / den
        outs.extend(o[g * Q:(g + 1) * Q] for g in range(G))
    return jnp.concatenate(outs, axis=-1)


def _attn_kernel(sink_ref, q_ref, kp_ref, kc_ref, kn_ref, vp_ref, vc_ref, vn_ref, ck_ref, cv_ref,
                 cq_ref, sq_ref, ckp_ref, skp_ref, ckc_ref, skc_ref, ckn_ref, skn_ref, o_ref, *, S, Lc):
    i = pl.program_id(1)
    q = _rotate(q_ref[0], cq_ref[...], sq_ref[...])
    k_win = [_rotate(k_ref[0], c_ref[...], s_ref[...]) for k_ref, c_ref, s_ref in
             ((kp_ref, ckp_ref, skp_ref), (kc_ref, ckc_ref, skc_ref), (kn_ref, ckn_ref, skn_ref))]
    kcat = jnp.concatenate([ck_ref[0]] + k_win, axis=0).astype(BF16)
    vcat = jnp.concatenate([cv_ref[0], vp_ref[0], vc_ref[0], vn_ref[0]], axis=0).astype(BF16)
    L = Lc + 3 * QBLOCK
    row = lax.broadcasted_iota(jnp.int32, (QBLOCK, L), 0)
    col = lax.broadcasted_iota(jnp.int32, (QBLOCK, L), 1)
    rel = col - Lc - QBLOCK - row
    kpos = (i - 1) * QBLOCK + col - Lc
    valid = (col < Lc) | ((jnp.abs(rel) <= WINDOW) & (kpos >= 0) & (kpos < S))
    o_ref[0] = _attn_heads(q, kcat, vcat, valid, sink_ref)


def _attention(sink, proj_l, proj_c, cos, sin):
    B, S, _ = proj_l.shape
    Lc = proj_c.shape[1]
    nb = S // QBLOCK
    kcol, vcol = ATTN_W // KV_W, ATTN_W // KV_W + 1

    def blk(colblk, off):
        return pl.BlockSpec((1, QBLOCK, KV_W),
                            lambda b, i: (b, jnp.clip(i + off, 0, nb - 1), colblk))

    def k_table(off):
        return pl.BlockSpec((QBLOCK, KV_W), lambda b, i: (jnp.clip(i + off, 0, nb - 1), kcol))

    q_table = pl.BlockSpec((QBLOCK, ATTN_W), lambda b, i: (i, 0))
    return pl.pallas_call(
        functools.partial(_attn_kernel, S=S, Lc=Lc),
        grid=(B, nb),
        in_specs=[pl.BlockSpec(memory_space=pltpu.SMEM),
                  pl.BlockSpec((1, QBLOCK, ATTN_W), lambda b, i: (b, i, 0)),
                  blk(kcol, -1), blk(kcol, 0), blk(kcol, 1),
                  blk(vcol, -1), blk(vcol, 0), blk(vcol, 1),
                  pl.BlockSpec((1, Lc, KV_W), lambda b, i: (b, 0, kcol)),
                  pl.BlockSpec((1, Lc, KV_W), lambda b, i: (b, 0, vcol)),
                  q_table, q_table, k_table(-1), k_table(-1), k_table(0), k_table(0), k_table(1), k_table(1)],
        out_specs=pl.BlockSpec((1, QBLOCK, ATTN_W), lambda b, i: (b, i, 0)),
        out_shape=jax.ShapeDtypeStruct((B, S, ATTN_W), F32),
        compiler_params=_params("parallel", "parallel"),
        name="attention",
    )(sink, proj_l, proj_l, proj_l, proj_l, proj_l, proj_l, proj_l, proj_c, proj_c,
      cos, sin, cos, sin, cos, sin, cos, sin)


def _ctx_attn_kernel(sink_ref, q_ref, ck_ref, cv_ref, o_ref):
    q = q_ref[0] * (HEAD_DIM ** -0.5)
    o_ref[0] = _attn_heads(q, ck_ref[0].astype(BF16), cv_ref[0].astype(BF16), None, sink_ref)


def _ctx_attention(sink, proj_c):
    B, Lc, _ = proj_c.shape
    kcol, vcol = ATTN_W // KV_W, ATTN_W // KV_W + 1
    return pl.pallas_call(
        _ctx_attn_kernel,
        grid=(B,),
        in_specs=[pl.BlockSpec(memory_space=pltpu.SMEM),
                  pl.BlockSpec((1, Lc, ATTN_W), lambda b: (b, 0, 0)),
                  pl.BlockSpec((1, Lc, KV_W), lambda b: (b, 0, kcol)),
                  pl.BlockSpec((1, Lc, KV_W), lambda b: (b, 0, vcol))],
        out_specs=pl.BlockSpec((1, Lc, ATTN_W), lambda b: (b, 0, 0)),
        out_shape=jax.ShapeDtypeStruct((B, Lc, ATTN_W), F32),
        compiler_params=_params("parallel"),
        name="ctx_attention",
    )(sink, proj_c, proj_c, proj_c)


def _seg_rev_block(j, nC, nL):
    return jnp.where(j < nC, nC - 1 - j, nC + nL - 1 - (j - nC))


def _lru_kernel(uf_ref, ur_ref, wg_ref, bg_ref, sp_ref, hf_ref, hr_ref, a_ref, b_ref, cf_ref, cr_ref, *, Tc):
    @pl.when(pl.program_id(1) == 0)
    def _():
        cf_ref[...] = jnp.zeros_like(cf_ref)
        cr_ref[...] = jnp.zeros_like(cr_ref)

    sub = 8
    row = lax.broadcasted_iota(jnp.int32, (sub, LRU_W), 0)
    for d, (u_ref, h_ref, c_ref) in enumerate(((uf_ref, hf_ref, cf_ref), (ur_ref, hr_ref, cr_ref))):
        u = u_ref[0]
        gates = jax.nn.sigmoid(jnp.dot(u.astype(BF16), wg_ref[d], preferred_element_type=F32) + bg_ref[d])
        a = jnp.exp(-(gates[:, :LRU_W] * sp_ref[d]))
        a_ref[...] = a
        b_ref[...] = jnp.sqrt(jnp.maximum(1.0 - a * a, 0.0)) * (gates[:, LRU_W:] * u)

        def tile(ti, carry, d=d, h_ref=h_ref):
            t0 = pl.multiple_of((ti if d == 0 else Tc // sub - 1 - ti) * sub, sub)
            av = a_ref[pl.ds(t0, sub), :]
            bv = b_ref[pl.ds(t0, sub), :]
            for s in (1, 2, 4):
                shift, known = (s, row >= s) if d == 0 else (sub - s, row < sub - s)
                a_prev = jnp.where(known, pltpu.roll(av, shift, 0), 1.0)
                b_prev = jnp.where(known, pltpu.roll(bv, shift, 0), 0.0)
                bv = bv + av * b_prev
                av = av * a_prev
            h = av * carry + bv
            h_ref[0, pl.ds(t0, sub), :] = h
            return h[sub - 1:sub] if d == 0 else h[0:1]

        c_ref[...] = lax.fori_loop(0, Tc // sub, tile, c_ref[...])


def _lru(u, Lc, wa, ba, wi, bi, lam):
    B, T, _ = u.shape
    S = T - Lc
    Tc = math.gcd(math.gcd(Lc, S), 256)
    nC, nL = Lc // Tc, S // Tc
    w_gates = jnp.stack([jnp.concatenate([_block_diag(wa[d]), _block_diag(wi[d])], axis=1)
                         for d in range(2)]).astype(BF16)
    b_gates = jnp.stack([jnp.concatenate([ba[d], bi[d]]) for d in range(2)])[:, None, :]
    decay_rate = (LRU_C * jax.nn.softplus(-lam))[:, None, :]
    fwd = pl.BlockSpec((1, Tc, LRU_W), lambda bi_, j: (bi_, j, 0))
    rev = pl.BlockSpec((1, Tc, LRU_W), lambda bi_, j: (bi_, _seg_rev_block(j, nC, nL), 0))

    def const(shape):
        return pl.BlockSpec(shape, lambda bi_, j: (0,) * len(shape))

    return pl.pallas_call(
        functools.partial(_lru_kernel, Tc=Tc),
        grid=(B, nC + nL),
        in_specs=[fwd, rev, const(w_gates.shape), const(b_gates.shape), const(decay_rate.shape)],
        out_specs=[fwd, rev],
        out_shape=[jax.ShapeDtypeStruct((B, T, LRU_W), F32)] * 2,
        scratch_shapes=[pltpu.VMEM((Tc, LRU_W), F32), pltpu.VMEM((Tc, LRU_W), F32),
                        pltpu.VMEM((1, LRU_W), F32), pltpu.VMEM((1, LRU_W), F32)],
        compiler_params=_params("parallel", "arbitrary"),
        name="lru",
    )(u, u, w_gates, b_gates, decay_rate)


def _mix_out_kernel(attn_ref, hf_ref, hr_ref, g0_ref, g1_ref, wo_ref, x_ref, gate_ref, g_ref, sh_ref, sc_ref,
                    wr_ref, xo_ref, h_ref, aff_ref):
    gate = jnp.concatenate([g0_ref[0], g1_ref[0]], axis=-1)
    rec = ((hf_ref[0] + hr_ref[0]) * jax.nn.gelu(gate)).astype(BF16)
    y = (jnp.dot(attn_ref[0].astype(BF16), wo_ref[:ATTN_W, :], preferred_element_type=F32)
         + jnp.dot(rec, wo_ref[ATTN_W:, :], preferred_element_type=F32))
    x = x_ref[0] + gate_ref[0] * y
    xo_ref[0] = x
    xn = x * lax.rsqrt(jnp.mean(x * x, axis=-1, keepdims=True) + NORM_EPS) * g_ref[...]
    h = xn * (1.0 + sc_ref[0]) + sh_ref[0]
    _to_chunk_rows(h_ref.at[0], h)
    logits = lax.dot_general(wr_ref[...], h, (((1,), (1,)), ((), ())),
                             precision=lax.Precision.HIGHEST, preferred_element_type=F32)
    e = jnp.exp(logits - jnp.max(logits, axis=0, keepdims=True))
    aff_ref[0] = e / jnp.sum(e, axis=0, keepdims=True)


def _mix_out(attn, hf, hr, proj, w_out, t_off, x, gate, g, shift, scale, router_w):
    B, N, _ = attn.shape
    D = w_out.shape[1]
    E = router_w.shape[1]
    chunks = D // LANES
    tm = math.gcd(math.gcd(N, 256), t_off) if t_off else min(256, N)
    half = LRU_W // 2
    gate_blk = (proj.shape[2] - LRU_W) // half
    tok = lambda b, i: (b, i, 0)
    per_b = pl.BlockSpec((1, 1, D), lambda b, i: (b, 0, 0))
    return pl.pallas_call(
        _mix_out_kernel,
        grid=(B, N // tm),
        in_specs=[pl.BlockSpec((1, tm, ATTN_W), tok),
                  pl.BlockSpec((1, tm, LRU_W), lambda b, i: (b, i + t_off // tm, 0)),
                  pl.BlockSpec((1, tm, LRU_W), lambda b, i: (b, i + t_off // tm, 0)),
                  pl.BlockSpec((1, tm, half), lambda b, i: (b, i, gate_blk)),
                  pl.BlockSpec((1, tm, half), lambda b, i: (b, i, gate_blk + 1)),
                  pl.BlockSpec(w_out.shape, lambda b, i: (0, 0)),
                  pl.BlockSpec((1, tm, D), tok), per_b,
                  pl.BlockSpec((1, D), lambda b, i: (0, 0)), per_b, per_b,
                  pl.BlockSpec((E, D), lambda b, i: (0, 0))],
        out_specs=[pl.BlockSpec((1, tm, D), tok),
                   pl.BlockSpec((1, tm * chunks, LANES), tok),
                   pl.BlockSpec((1, E, tm), lambda b, i: (b, 0, i))],
        out_shape=[jax.ShapeDtypeStruct((B, N, D), F32),
                   jax.ShapeDtypeStruct((B, N * chunks, LANES), F32),
                   jax.ShapeDtypeStruct((B, E, N), F32)],
        compiler_params=_params("parallel", "parallel"),
        name="mix_out",
    )(attn, hf, hr, proj, proj, w_out.astype(BF16), x, gate, g.reshape(1, D), shift, scale, router_w.T)


def _softplus(z):
    return jnp.maximum(z, 0.0) + jnp.log(1.0 + jnp.exp(-jnp.abs(z)))


def _nt_dot(wt, x):
    return lax.dot_general(wt, x, (((1,), (1,)), ((), ())), preferred_element_type=F32)


def _rwkv_prep_kernel(cp_ref, c_ref, cn_ref, lp_ref, l_ref, ln_ref, mu_ref, wr_ref, wk_ref, wv_ref, w1_ref,
                      w2_ref, a1_ref, a2_ref, g1_ref, g2_ref, w0_ref, a0_ref, ka_ref, rk_ref, perm_ref,
                      dec_ref, aa_ref, k_ref, v_ref, r_ref, g_ref, bc_ref, *, B, ctx_tiles, tiles):
    i = pl.program_id(0)
    tm, D = l_ref.shape
    is_ctx = i < ctx_tiles
    h = jnp.where(is_ctx, c_ref[...], l_ref[...])
    seq_start = (i == 0) | (i == ctx_tiles)
    seq_end = (i == ctx_tiles - 1) | (i == tiles - 1)
    hp = jnp.where(seq_start, 0.0, jnp.where(is_ctx, cp_ref[...], lp_ref[...]))
    hn = jnp.where(seq_end, 0.0, jnp.where(is_ctx, cn_ref[...], ln_ref[...]))
    xx = 0.5 * (jnp.concatenate([hp, h[:tm - B]], axis=0) + jnp.concatenate([h[B:], hn], axis=0)) - h

    def mix(j):
        return (h + xx * mu_ref[j:j + 1, :]).astype(BF16)

    hpg = LANES // B
    hd = RWKV_HD

    def to_scan_layout(x, ref):
        for g in range(RWKV_H // hpg):
            for c in range(tm // LANES):
                tiles = [x[(g * hpg + hh) * hd:(g * hpg + hh + 1) * hd, c * LANES:(c + 1) * LANES]
                         for hh in range(hpg)]
                for tl, tile in enumerate(_slab_transpose(tiles, B)):
                    ref[g, c * hpg + tl] = tile

    v = _nt_dot(wv_ref[...], mix(3))
    to_scan_layout(v, v_ref)
    r = _nt_dot(wr_ref[...], mix(0))
    to_scan_layout(r, r_ref)
    k = _nt_dot(wk_ref[...], mix(2))
    to_scan_layout(k, k_ref)
    lw = jnp.tanh(_nt_dot(w1_ref[...], mix(1))).astype(BF16)
    wpre = jnp.dot(w2_ref[...], lw, preferred_element_type=F32)
    la = _nt_dot(a1_ref[...], mix(4)).astype(BF16)
    apre = jnp.dot(a2_ref[...], la, preferred_element_type=F32)
    gg = jax.nn.sigmoid(_nt_dot(g1_ref[...], mix(5))).astype(BF16)
    g_ref[...] = jnp.dot(g2_ref[...], gg, preferred_element_type=F32)

    def to_scan_layout_mxu(xs, refs):
        groups = RWKV_H // hpg
        for c in range(tm // LANES):
            lhs = jnp.concatenate(
                [jnp.concatenate([x[(g * hpg + hh) * hd:(g * hpg + hh + 1) * hd, c * LANES:(c + 1) * LANES]
                                  for hh in range(hpg)], axis=1)
                 for x in xs for g in range(groups)], axis=0)
            hi = lhs.astype(BF16)
            rest = lhs - hi.astype(F32)
            mid = rest.astype(BF16)
            lo = (rest - mid.astype(F32)).astype(BF16)
            out = (jnp.dot(hi, perm_ref[...], preferred_element_type=F32)
                   + jnp.dot(mid, perm_ref[...], preferred_element_type=F32)
                   + jnp.dot(lo, perm_ref[...], preferred_element_type=F32))
            for n, ref in enumerate(refs):
                for g in range(groups):
                    blk = out[(n * groups + g) * hd:(n * groups + g + 1) * hd]
                    for tl in range(hpg):
                        ref[g, c * hpg + tl] = blk[:, tl * LANES:(tl + 1) * LANES]

    iclr, decay = [], []
    for d in range(2):
        w_log = -_softplus(-(w0_ref[d] + wpre[d * D:(d + 1) * D])) - 0.5
        decay.append(jnp.exp(-jnp.exp(w_log)))
        a = jax.nn.sigmoid(a0_ref[d] + apre[d * D:(d + 1) * D])
        iclr.append(a)
        to_scan_layout(a, aa_ref.at[d])
    to_scan_layout_mxu(decay, [dec_ref.at[0], dec_ref.at[1]])
    kd_sum = k * (2.0 + (iclr[0] + iclr[1] - 2.0) * ka_ref[...])
    bc = jnp.sum((r * kd_sum * rk_ref[...]).reshape(RWKV_H, hd, tm), axis=1)
    for g in range(RWKV_H // hpg):
        for c in range(tm // LANES):
            rows = [jnp.broadcast_to(bc[g * hpg + hh:g * hpg + hh + 1, c * LANES:(c + 1) * LANES], (8, LANES))
                    for hh in range(hpg)]
            for tl, tile in enumerate(_slab_transpose(rows, B)):
                bc_ref[g, c * hpg + tl] = tile[0:1]


def _slab_permutation(B):
    n = LANES // B
    src = jnp.arange(n * LANES).reshape(n, n, B).transpose(1, 0, 2).reshape(-1)
    return (jnp.arange(n * LANES)[:, None] == src[None, :]).astype(BF16)


def _slab_transpose(tiles, B):
    n = len(tiles)
    tiles = list(tiles)
    slab = lax.broadcasted_iota(jnp.int32, tiles[0].shape, 1) // B
    s = n // 2
    while s >= 1:
        upper = (slab & s) != 0
        for i in range(n):
            if i & s == 0:
                lo, hi = tiles[i], tiles[i + s]
                tiles[i] = jnp.where(upper, pltpu.roll(hi, s * B, 1), lo)
                tiles[i + s] = jnp.where(upper, hi, pltpu.roll(lo, LANES - s * B, 1))
        s //= 2
    return tiles


def _const_spec(shape):
    nd = len(shape)
    return pl.BlockSpec(shape, lambda i: (0,) * nd, pipeline_mode=pl.Buffered(1))


def _rwkv_prep(hc_tm, hl_tm, B, mu, w_rkv, w0, w1, w2, a0, a1, a2, g1, g2, k_a, r_k):
    D = hl_tm.shape[1]
    Lc, S = hc_tm.shape[0] // B, hl_tm.shape[0] // B
    T = Lc + S
    TB = T * B
    tm = 256
    tiles, ctx_tiles = TB // tm, Lc * B // tm
    hb = tm // B
    G = RWKV_H // (LANES // B)
    hd = RWKV_HD

    def t_bf16(w):
        return w.T.astype(BF16)

    zeros_w = jnp.zeros_like(w2[0].T)
    w2t = jnp.concatenate([jnp.concatenate([w2[0].T, zeros_w], 1),
                           jnp.concatenate([zeros_w, w2[1].T], 1)], 0).astype(BF16)
    zeros_a = jnp.zeros_like(a2[0].T)
    a2t = jnp.concatenate([jnp.concatenate([a2[0].T, zeros_a], 1),
                           jnp.concatenate([zeros_a, a2[1].T], 1)], 0).astype(BF16)
    consts = [mu, t_bf16(w_rkv[0]), t_bf16(w_rkv[1]), t_bf16(w_rkv[2]),
              t_bf16(jnp.concatenate([w1[0], w1[1]], 1)), w2t,
              t_bf16(jnp.concatenate([a1[0], a1[1]], 1)), a2t, t_bf16(g1), t_bf16(g2),
              w0.reshape(2, D, 1), a0.reshape(2, D, 1), k_a.reshape(D, 1), r_k.reshape(D, 1),
              _slab_permutation(B)]
    dir_spec = pl.BlockSpec((2, G, hb, hd, LANES), lambda i: (0, 0, i, 0, 0))
    all_spec = pl.BlockSpec((G, hb, hd, LANES), lambda i: (0, i, 0, 0))
    dir_shape = jax.ShapeDtypeStruct((2, G, T, hd, LANES), F32)
    all_shape = jax.ShapeDtypeStruct((G, T, hd, LANES), F32)

    def stream_specs(first, n_tiles, steps):
        def tile(i):
            return jnp.clip(i - first, 0, n_tiles - 1)
        return [pl.BlockSpec((B, D), lambda i: (jnp.maximum(tile(i) * hb - 1, 0), 0)),
                pl.BlockSpec((tm, D), lambda i: (tile(i), 0)),
                pl.BlockSpec((B, D), lambda i: (jnp.minimum((tile(i) + 1) * hb, steps - 1), 0))]

    return pl.pallas_call(
        functools.partial(_rwkv_prep_kernel, B=B, ctx_tiles=ctx_tiles, tiles=tiles),
        grid=(tiles,),
        in_specs=stream_specs(0, ctx_tiles, Lc) + stream_specs(ctx_tiles, tiles - ctx_tiles, S)
                 + [_const_spec(c.shape) for c in consts],
        out_specs=[dir_spec, dir_spec, all_spec, all_spec, all_spec,
                   pl.BlockSpec((D, tm), lambda i: (0, i)),
                   pl.BlockSpec((G, hb, 1, LANES), lambda i: (0, i, 0, 0))],
        out_shape=[dir_shape, dir_shape, all_shape, all_shape, all_shape,
                   jax.ShapeDtypeStruct((D, TB), F32),
                   jax.ShapeDtypeStruct((G, T, 1, LANES), F32)],
        compiler_params=_params("parallel"),
        name="rwkv_prep",
    )(hc_tm, hc_tm, hc_tm, hl_tm, hl_tm, hl_tm, *consts)


def _wkv_kernel(dec_ref, aa_ref, k_ref, v_ref, r_ref, kkc_ref, kac_ref, y_ref,
                s_ref, g_ref, p_ref, q_ref, sa_ref, *, Tc, nC):
    d = pl.program_id(0)
    j = pl.program_id(2)
    hd = RWKV_HD
    sub = 8

    @pl.when(j == 0)
    def _():
        s_ref[...] = jnp.zeros_like(s_ref)

    def partial_rows(x):
        return jnp.sum(x.reshape(hd // sub, sub, LANES), axis=0)

    pitch = hd + 1

    def put_partial(ref, v, x):
        ref[pl.ds(v, sub, stride=pitch), :] = partial_rows(x)

    def finish_rows(ref):
        acc = ref[pl.ds(0, hd), :]
        for q in range(1, sub):
            acc = acc + ref[pl.ds(q * pitch, hd), :]
        return acc

    def run(with_y):
        g_ref[...] = jnp.ones_like(g_ref)

        def step(s, _):
            tt = jnp.where(d == 0, s, Tc - 1 - s)
            aa = aa_ref[0, 0, tt]
            kk = k_ref[0, tt]
            kf = kk * kkc_ref[0]
            kn = kf * lax.rsqrt(jnp.maximum(jnp.sum(kf * kf, axis=0, keepdims=True), 1e-24))
            g_prev = g_ref[...]
            g = g_prev * dec_ref[0, 0, tt]
            g_inv = 1.0 / g
            g_ref[...] = g
            a_t = -(kn * g_prev)
            b_t = kn * aa * g_inv
            k_t = kk * (1.0 + (aa - 1.0) * kac_ref[0]) * g_inv
            r_t = r_ref[0, tt] * g if with_y else None

            def row_reduce(v, _):
                st = s_ref[v]
                put_partial(p_ref, v, st * a_t)
                if with_y:
                    put_partial(q_ref, v, st * r_t)
                return 0

            lax.fori_loop(0, hd, row_reduce, 0, unroll=8)
            sa = finish_rows(p_ref)
            sa_ref[...] = sa
            if with_y:
                b_r = jnp.sum(b_t * r_t, axis=0, keepdims=True)
                k_r = jnp.sum(k_t * r_t, axis=0, keepdims=True)
                y_ref[0, 0, tt] = finish_rows(q_ref) + sa * b_r + v_ref[0, tt] * k_r

            def row_update(v, _):
                s_ref[v] = s_ref[v] + sa_ref[pl.ds(v, 1), :] * b_t + v_ref[0, tt, pl.ds(v, 1), :] * k_t
                return 0

            lax.fori_loop(0, hd, row_update, 0, unroll=8)
            return 0

        lax.fori_loop(0, Tc, step, 0)

        def rescale(v, _):
            s_ref[v] = s_ref[v] * g_ref[...]
            return 0

        lax.fori_loop(0, hd, rescale, 0, unroll=8)

    @pl.when(j < nC)
    def _():
        run(False)

    @pl.when(j >= nC)
    def _():
        run(True)


def _wkv(dec, aa, k, v, r, k_k, k_a, B, Lc):
    _, G, T, hd, _ = dec.shape
    Tc = WKV_TC
    S = T - Lc
    nC, nL = Lc // Tc, S // Tc

    def tmap(d, j):
        return jnp.where(d == 0, j, _seg_rev_block(j, nC, nL))

    def lmap(d, j):
        jj = jnp.maximum(j - nC, 0)
        return jnp.where(d == 0, jj, nL - 1 - jj)

    dir_spec = pl.BlockSpec((1, 1, Tc, hd, LANES), lambda d, g, j: (d, g, tmap(d, j), 0, 0))
    all_spec = pl.BlockSpec((1, Tc, hd, LANES), lambda d, g, j: (g, tmap(d, j), 0, 0))
    const_spec = pl.BlockSpec((1, hd, LANES), lambda d, g, j: (g, 0, 0))
    return pl.pallas_call(
        functools.partial(_wkv_kernel, Tc=Tc, nC=nC),
        grid=(2, G, nC + nL),
        in_specs=[dir_spec, dir_spec, all_spec, all_spec, all_spec, const_spec, const_spec],
        out_specs=pl.BlockSpec((1, 1, Tc, hd, LANES), lambda d, g, j: (d, g, lmap(d, j), 0, 0)),
        out_shape=jax.ShapeDtypeStruct((2, G, S, hd, LANES), F32),
        scratch_shapes=[pltpu.VMEM((hd, hd, LANES), F32),
                        pltpu.VMEM((hd, LANES), F32),
                        pltpu.VMEM(((hd + 1) * 8, LANES), F32),
                        pltpu.VMEM(((hd + 1) * 8, LANES), F32),
                        pltpu.VMEM((hd, LANES), F32)],
        compiler_params=_params("parallel", "parallel", "arbitrary"),
        name="wkv",
    )(dec, aa, k, v, r, _lane_const(k_k, B), _lane_const(k_a, B))


def _lane_const(c, B):
    hpg = LANES // B
    return jnp.repeat(c.reshape(RWKV_H // hpg, hpg, RWKV_HD).transpose(0, 2, 1), B, axis=-1)


def _rwkv_finish_kernel(y_ref, v_ref, bc_ref, g_ref, gng_ref, gnb_ref, wo_ref, o_ref, *, B):
    G, steps, hd, _ = v_ref.shape
    hpg = LANES // B
    rows = []
    for g in range(G):
        head_cols = [[] for _ in range(hpg)]
        for c in range(steps // hpg):
            tiles = []
            for tl in range(hpg):
                t = c * hpg + tl
                y = y_ref[0, g, t] + y_ref[1, g, t]
                mean = jnp.mean(y, axis=0, keepdims=True)
                var = jnp.mean(jnp.square(y - mean), axis=0, keepdims=True)
                yn = (y - mean) * lax.rsqrt(var + GN_EPS)
                tiles.append(yn * gng_ref[g] + gnb_ref[g] + bc_ref[g, t] * v_ref[g, t])
            for hh, tile in enumerate(_slab_transpose(tiles, B)):
                head_cols[hh].append(tile)
        rows.extend(jnp.concatenate(cols, axis=1) for cols in head_cols)
    o = (jnp.concatenate(rows, axis=0) * g_ref[...]).astype(BF16)
    out_t = jnp.dot(wo_ref[...], o, preferred_element_type=F32)
    o_ref[...] = out_t.T


def _rwkv_finish(y2, v, bc, g, gn_g, gn_b, w_o, B, Lc):
    _, G, S, hd, _ = y2.shape
    D = g.shape[0]
    tm = 256
    steps = tm // B
    off = Lc // steps
    return pl.pallas_call(
        functools.partial(_rwkv_finish_kernel, B=B),
        grid=(S // steps,),
        in_specs=[pl.BlockSpec((2, G, steps, hd, LANES), lambda i: (0, 0, i, 0, 0)),
                  pl.BlockSpec((G, steps, hd, LANES), lambda i: (0, i + off, 0, 0)),
                  pl.BlockSpec((G, steps, 1, LANES), lambda i: (0, i + off, 0, 0)),
                  pl.BlockSpec((D, tm), lambda i: (0, i + off)),
                  _const_spec((G, hd, LANES)), _const_spec((G, hd, LANES)), _const_spec((D, D))],
        out_specs=pl.BlockSpec((tm, D), lambda i: (i, 0)),
        out_shape=jax.ShapeDtypeStruct((S * B, D), F32),
        compiler_params=_params("parallel"),
        name="rwkv_finish",
    )(y2, v, bc, g, _lane_const(gn_g, B), _lane_const(gn_b, B), w_o.T.astype(BF16))


def _rwkv7_mixer(hl, hc, mu, w_rkv, w0, w1, w2, a0, a1, a2, g1, g2, k_k, k_a, r_k, gn_g, gn_b, w_o):
    S, B, D = hl.shape
    Lc = hc.shape[0]
    dec, aa, k, v, r, g, bc = _rwkv_prep(hc.reshape(Lc * B, D), hl.reshape(S * B, D), B, mu, w_rkv, w0, w1,
                                         w2, a0, a1, a2, g1, g2, k_a, r_k.reshape(D))
    y2 = _wkv(dec, aa, k, v, r, k_k, k_a, B, Lc)
    return _rwkv_finish(y2, v, bc, g, gn_g, gn_b, w_o, B, Lc).reshape(S, B, D)


def _moe_kernel(idx_prev_ref, idx_ref, idx_next_ref, gate_ref, h_ref, w1_ref, w3_ref, w2_ref, o_ref,
                xa_ref, xb_ref, ya_ref, yb_ref, *, cap, chunks):
    bb = h_ref.shape[0]
    rows = bb * cap
    e = pl.program_id(1)
    n_e = pl.num_programs(1)
    batch = 8

    def token_rows(t):
        return pl.ds(pl.multiple_of(t * chunks, chunks), chunks)

    def scatter_batch(ids_ref, y_ref, first, keep):
        bi = first // cap
        slots = [first + u for u in range(batch)]
        dst = [token_rows(ids_ref[0, 0, s]) for s in slots]
        vals = [y_ref[token_rows(s), :] for s in slots]
        if keep is not None:
            vals = [jnp.where(keep, v, 0.0) for v in vals]
        sums = [o_ref[bi, dst[u], :] + vals[u] for u in range(batch)]
        for u in range(batch):
            o_ref[bi, dst[u], :] = sums[u]

    @pl.when(e == 0)
    def _():
        o_ref[...] = jnp.zeros_like(o_ref)
        yb_ref[...] = jnp.zeros_like(yb_ref)
        for bi in range(bb):
            def gather(r, _, bi=bi):
                slot = bi * cap + r
                xa_ref[token_rows(slot), :] = h_ref[bi, token_rows(idx_ref[0, 0, slot]), :]
                return 0

            lax.fori_loop(0, cap, gather, 0, unroll=8)

    def stage(x_cur, x_next, y_cur, y_prev):
        x = _from_chunk_rows(x_cur, rows).astype(BF16)
        h1 = jnp.dot(x, w1_ref[0, 0], preferred_element_type=F32)
        h3 = jnp.dot(x, w3_ref[0, 0], preferred_element_type=F32)
        hid = (h1 * jax.nn.sigmoid(h1) * h3).astype(BF16)
        _to_chunk_rows(y_cur, jnp.dot(hid, w2_ref[0, 0], preferred_element_type=F32) * gate_ref[0])
        for slot in range(rows):
            x_next[token_rows(slot), :] = h_ref[slot // cap, token_rows(idx_next_ref[0, 0, slot]), :]
        for first in range(0, rows, batch):
            scatter_batch(idx_prev_ref, y_prev, first, e > 0)

    @pl.when(e % 2 == 0)
    def _():
        stage(xa_ref, xb_ref, ya_ref, yb_ref)

    @pl.when(e % 2 == 1)
    def _():
        stage(xb_ref, xa_ref, yb_ref, ya_ref)

    def drain(y_ref):
        for first in range(0, rows, batch):
            scatter_batch(idx_ref, y_ref, first, None)

    @pl.when((e == n_e - 1) & (e % 2 == 0))
    def _():
        drain(ya_ref)

    @pl.when((e == n_e - 1) & (e % 2 == 1))
    def _():
        drain(yb_ref)


def _moe(h, aff, layer, w1, w3, w2):
    B, _, N = aff.shape
    _, E, D, FF = w1.shape
    chunks = D // LANES
    cap = CAPACITY * N // E
    bb = max(1, min(B, MOE_ROWS // cap))
    gate, idx = lax.top_k(aff, cap)

    def group(t):
        return t.reshape(B // bb, bb, E, cap).transpose(0, 2, 1, 3).reshape(B // bb * E, bb * cap)

    rows = bb * cap
    idx_g = group(idx.astype(jnp.int32))[:, None, :]

    def idx_spec(off):
        return pl.BlockSpec((1, 1, rows), lambda b, e: (b * E + jnp.clip(e + off, 0, E - 1), 0, 0),
                            memory_space=pltpu.SMEM)

    scratch = pltpu.VMEM((rows * chunks, LANES), F32)
    return pl.pallas_call(
        functools.partial(_moe_kernel, cap=cap, chunks=chunks),
        grid=(B // bb, E),
        in_specs=[idx_spec(-1), idx_spec(0), idx_spec(1),
                  pl.BlockSpec((1, rows, 1), lambda b, e: (b * E + e, 0, 0)),
                  pl.BlockSpec((bb, N * chunks, LANES), lambda b, e: (b, 0, 0), pipeline_mode=pl.Buffered(1)),
                  pl.BlockSpec((1, 1, D, FF), lambda b, e: (layer, e, 0, 0)),
                  pl.BlockSpec((1, 1, D, FF), lambda b, e: (layer, e, 0, 0)),
                  pl.BlockSpec((1, 1, FF, D), lambda b, e: (layer, e, 0, 0))],
        out_specs=pl.BlockSpec((bb, N * chunks, LANES), lambda b, e: (b, 0, 0)),
        out_shape=jax.ShapeDtypeStruct((B, N * chunks, LANES), F32),
        scratch_shapes=[scratch] * 4,
        compiler_params=_params("parallel", "arbitrary"),
        name="moe",
    )(idx_g, idx_g, idx_g, group(gate)[:, :, None], h, w1, w3, w2)


def _block_diag(w):
    H, bi, bj = w.shape
    eye = jnp.eye(H, dtype=w.dtype)
    return jnp.einsum('hij,hg->higj', w, eye).reshape(H * bi, H * bj)


def _dwconv(u, w, b):
    n = u.shape[1]
    up = jnp.pad(u, ((0, 0), (CONV_LEFT, CONV_W - 1 - CONV_LEFT), (0, 0)))
    out = up[:, 0:n] * w[0]
    for j in range(1, CONV_W):
        out = out + up[:, j:j + n] * w[j]
    return out + b


def _attn_lru_mixer(proj_l, proj_c, cos, sin, w_out, sink, conv_w, conv_b, wa, ba, wi, bi, lam, tail_l, tail_c):
    Lc = proj_c.shape[1]
    attn_l = _attention(sink, proj_l, proj_c, cos, sin)
    attn_c = _ctx_attention(sink, proj_c)

    u0, g0 = ATTN_W + 2 * KV_W, ATTN_W + 2 * KV_W + LRU_W
    u = jnp.concatenate([_dwconv(proj_c[..., u0:g0], conv_w, conv_b),
                         _dwconv(proj_l[..., u0:g0], conv_w, conv_b)], axis=1)
    hf, hr = _lru(u, Lc, wa, ba, wi, bi, lam)
    return (_mix_out(attn_l, hf, hr, proj_l, w_out, Lc, *tail_l),
            _mix_out(attn_c, hf, hr, proj_c, w_out, 0, *tail_c))


def kernel(x, c, ctx, c_ctx, mod_w, mod_b, norm_mix, norm_ffn, router_w, exp_w1, exp_w3, exp_w2, mix_in, mix_out, attn_sink, lru_conv_w, lru_conv_b, lru_wa, lru_ba, lru_wi, lru_bi, lru_lam, rw_mu, rw_rkv, rw_w0, rw_w1, rw_w2, rw_a0, rw_a1, rw_a2, rw_g1, rw_g2, rw_kk, rw_ka, rw_rk, rw_gn_g, rw_gn_b, rw_wo, final_norm):
    B, S, D = x.shape
    Lc = ctx.shape[1]
    depth = mod_w.shape[0]
    assert depth == 2 and S % QBLOCK == 0
    assert LANES % B == 0 and RWKV_H % (LANES // B) == 0 and B % 8 == 0
    assert Lc % WKV_TC == 0 and S % WKV_TC == 0 and (Lc * B) % 256 == 0
    cos, sin = _rope_tables(S)

    n_rows = -(-(B + 1) // 8) * 8
    cond = jnp.concatenate([jax.nn.silu(c), jax.nn.silu(c_ctx)[None],
                            jnp.zeros((n_rows - B - 1, D), F32)], axis=0)

    mods_l, mods_c = [], []
    for layer in range(depth):
        mod = _mm(cond, mod_w[layer], tm=n_rows, tn=1024, precise=True) + mod_b[layer]
        mods_l.append([t[:, None, :] for t in jnp.split(mod[:B], 6, axis=-1)])
        mods_c.append([jnp.broadcast_to(t[None], (B, 1, D)) for t in jnp.split(mod[B:B + 1], 6, axis=-1)])

    w1b, w3b, w2b = exp_w1.astype(BF16), exp_w3.astype(BF16), exp_w2.astype(BF16)
    xl, xc = x, ctx
    hl = hc = None
    for layer in range(depth):
        last = layer == depth - 1
        m_l, m_c = mods_l[layer], mods_c[layer]
        if layer % 2 == 0:
            i = layer // 2
            assert layer == 0
            w_in = mix_in[i].astype(BF16)
            proj_l = _norm_mm(xl, norm_mix[layer], m_l[0], m_l[1], w_in)
            proj_c = _norm_mm(xc, norm_mix[layer], m_c[0], m_c[1], w_in)
            tail_l = (xl, m_l[2], norm_ffn[layer], m_l[3], m_l[4], router_w[layer])
            tail_c = (xc, m_c[2], norm_ffn[layer], m_c[3], m_c[4], router_w[layer])
            (xl, hl, aff_l), (xc, hc, aff_c) = _attn_lru_mixer(
                proj_l, proj_c, cos, sin, mix_out[i], attn_sink[i], lru_conv_w[i], lru_conv_b[i], lru_wa[i],
                lru_ba[i], lru_wi[i], lru_bi[i], lru_lam[i], tail_l, tail_c)
        else:
            i = layer // 2
            yl = _rwkv7_mixer(hl, hc, rw_mu[i], rw_rkv[i], rw_w0[i], rw_w1[i], rw_w2[i], rw_a0[i], rw_a1[i],
                              rw_a2[i], rw_g1[i], rw_g2[i], rw_kk[i], rw_ka[i], rw_rk[i], rw_gn_g[i],
                              rw_gn_b[i], rw_wo[i])
            xl, hl, aff_l = _resid_norm_tm(xl, yl, m_l[2], norm_ffn[layer], m_l[3], m_l[4], delta_tm=True,
                                           h_tm=False, router_w=router_w[layer])
        moe_l = _moe(hl, aff_l, layer, w1b, w3b, w2b)
        if last:
            zero = jnp.zeros((B, 1, D), F32)
            return _resid_norm(xl, moe_l, m_l[5], final_norm, zero, zero, delta_rows=True, want_x=False)
        assert (layer + 1) % 2 == 1
        n_l, n_c = mods_l[layer + 1], mods_c[layer + 1]
        xl, hl = _resid_norm_tm(xl, moe_l, m_l[5], norm_mix[layer + 1], n_l[0], n_l[1], delta_tm=False, h_tm=True)
        moe_c = _moe(hc, aff_c, layer, w1b, w3b, w2b)
        xc, hc = _resid_norm_tm(xc, moe_c, m_c[5], norm_mix[layer + 1], n_c[0], n_c[1], delta_tm=False, h_tm=True)
```

```python
import functools
import math

import jax
import jax.numpy as jnp
from jax import lax
from jax.experimental import pallas as pl
from jax.experimental.pallas import tpu as pltpu

F32 = jnp.float32
BF16 = jnp.bfloat16

GRID_W = 64
N_HEADS_ATTN = 8
N_KV = 2
HEAD_DIM = 64
AXIS_DIM = HEAD_DIM // 2
ATTN_W = N_HEADS_ATTN * HEAD_DIM
KV_W = N_KV * HEAD_DIM
WINDOW = 128
QBLOCK = 128
ROPE_BASE = 10000.0
LRU_W = 512
LRU_BLOCKS = 8
CONV_W = 4
CONV_LEFT = 2
LRU_C = 8.0
RWKV_H = 16
RWKV_HD = 64
GN_EPS = 64e-5
N_EXPERTS = 16
CAPACITY = 2
NORM_EPS = 1e-6
NEG_INF = -1e30

LANES = 128
VMEM_LIMIT_BYTES = 56 * 1024 * 1024
WKV_TC = 32
MOE_ROWS = 256

def _params(*sem):
    return pltpu.CompilerParams(dimension_semantics=sem, vmem_limit_bytes=VMEM_LIMIT_BYTES)


def _mm_kernel(x_ref, w_ref, o_ref, *, precise):
    if precise:
        o_ref[...] = jnp.dot(x_ref[...], w_ref[...], precision=lax.Precision.HIGHEST,
                             preferred_element_type=F32)
    else:
        o_ref[...] = jnp.dot(x_ref[...].astype(BF16), w_ref[...].astype(BF16),
                             preferred_element_type=F32)


def _mm(x, w, *, tm=512, tn=None, precise=False):
    M, K = x.shape
    N = w.shape[1]
    tm = min(tm, M)
    tn = N if tn is None else min(tn, N)
    assert M % tm == 0 and N % tn == 0, (M, tm, N, tn)
    return pl.pallas_call(
        functools.partial(_mm_kernel, precise=precise),
        grid=(M // tm, N // tn),
        in_specs=[pl.BlockSpec((tm, K), lambda i, j: (i, 0)),
                  pl.BlockSpec((K, tn), lambda i, j: (0, j))],
        out_specs=pl.BlockSpec((tm, tn), lambda i, j: (i, j)),
        out_shape=jax.ShapeDtypeStruct((M, N), F32),
        compiler_params=_params("parallel", "parallel"),
        name="mm",
    )(x, w)


def _final_norm_kernel(x_ref, d_ref, gate_ref, g_ref, o_ref):
    x = x_ref[0] + gate_ref[0] * _from_chunk_rows(d_ref.at[0], x_ref.shape[1])
    o_ref[0] = x * lax.rsqrt(jnp.mean(x * x, axis=-1, keepdims=True) + NORM_EPS) * g_ref[...]


def _from_chunk_rows(ref, n):
    chunks = ref.shape[0] // n
    return jnp.concatenate([ref[pl.ds(j, n, stride=chunks), :] for j in range(chunks)], axis=-1)


def _to_chunk_rows(ref, x):
    n = x.shape[0]
    chunks = x.shape[1] // LANES
    for j in range(chunks):
        ref[pl.ds(j, n, stride=chunks), :] = x[:, j * LANES:(j + 1) * LANES]


def _final_norm(x, delta, gate, g):
    B, N, D = x.shape
    ts = min(512, N)
    chunks = D // LANES
    tok = pl.BlockSpec((1, ts, D), lambda b, i: (b, i, 0))
    return pl.pallas_call(
        _final_norm_kernel,
        grid=(B, N // ts),
        in_specs=[tok, pl.BlockSpec((1, ts * chunks, LANES), lambda b, i: (b, i, 0)),
                  pl.BlockSpec((1, 1, D), lambda b, i: (b, 0, 0)),
                  pl.BlockSpec((1, D), lambda b, i: (0, 0))],
        out_specs=tok,
        out_shape=jax.ShapeDtypeStruct((B, N, D), F32),
        compiler_params=_params("parallel", "parallel"),
        name="final_norm",
    )(x, delta, gate, g.reshape(1, D))


def _resid_norm_tm_kernel(*refs, delta_tm, h_tm, route):
    refs = list(refs)
    x_ref, d_ref, gate_ref, g_ref, sh_ref, sc_ref = (refs.pop(0) for _ in range(6))
    wr_ref = refs.pop(0) if route else None
    xo_ref, h_ref = refs.pop(0), refs.pop(0)
    aff_ref = refs.pop(0) if route else None
    nb, ts, _ = x_ref.shape
    for bi in range(nb):
        delta = d_ref[:, bi, :] if delta_tm else _from_chunk_rows(d_ref.at[bi], ts)
        x = x_ref[bi] + gate_ref[bi] * delta
        xo_ref[bi] = x
        y = x * lax.rsqrt(jnp.mean(x * x, axis=-1, keepdims=True) + NORM_EPS) * g_ref[...]
        h = y * (1.0 + sc_ref[bi]) + sh_ref[bi]
        if h_tm:
            h_ref[:, bi, :] = h
        else:
            _to_chunk_rows(h_ref.at[bi], h)
        if route:
            logits = lax.dot_general(wr_ref[...], h, (((1,), (1,)), ((), ())),
                                     precision=lax.Precision.HIGHEST, preferred_element_type=F32)
            e = jnp.exp(logits - jnp.max(logits, axis=0, keepdims=True))
            aff_ref[bi] = e / jnp.sum(e, axis=0, keepdims=True)


def _resid_norm_tm(x, delta, gate, g, shift, scale, *, delta_tm, h_tm, router_w=None):
    B, N, D = x.shape
    nb, ts = 8, min(LANES, N)
    chunks = D // LANES
    tok = pl.BlockSpec((nb, ts, D), lambda b, i: (b, i, 0))
    tok_rows = pl.BlockSpec((nb, ts * chunks, LANES), lambda b, i: (b, i, 0))
    tok_tm = pl.BlockSpec((ts, nb, D), lambda b, i: (i, b, 0))
    per_b = pl.BlockSpec((nb, 1, D), lambda b, i: (b, 0, 0))
    in_specs = [tok, tok_tm if delta_tm else tok_rows, per_b, pl.BlockSpec((1, D), lambda b, i: (0, 0)),
                per_b, per_b]
    args = [x, delta, gate, g.reshape(1, D), shift, scale]
    out_specs = [tok, tok_tm if h_tm else tok_rows]
    out_shape = [jax.ShapeDtypeStruct((B, N, D), F32),
                 jax.ShapeDtypeStruct((N, B, D) if h_tm else (B, N * chunks, LANES), F32)]
    route = router_w is not None
    if route:
        E = router_w.shape[1]
        in_specs.append(pl.BlockSpec((E, D), lambda b, i: (0, 0)))
        args.append(router_w.T)
        out_specs.append(pl.BlockSpec((nb, E, ts), lambda b, i: (b, 0, i)))
        out_shape.append(jax.ShapeDtypeStruct((B, E, N), F32))
    return pl.pallas_call(
        functools.partial(_resid_norm_tm_kernel, delta_tm=delta_tm, h_tm=h_tm, route=route),
        grid=(B // nb, N // ts),
        in_specs=in_specs,
        out_specs=out_specs,
        out_shape=out_shape,
        compiler_params=_params("parallel", "parallel"),
        name="resid_norm_tm",
    )(*args)


def _norm_mm_kernel(x_ref, g_ref, sh_ref, sc_ref, w_ref, *rest):
    o_ref = rest[-1]
    x = x_ref[0]
    y = x * lax.rsqrt(jnp.mean(x * x, axis=-1, keepdims=True) + NORM_EPS) * g_ref[...]
    h = (y * (1.0 + sc_ref[0]) + sh_ref[0]).astype(BF16)
    p = jnp.dot(h, w_ref[...], preferred_element_type=F32)
    if len(rest) == 3:
        cos_ref, sin_ref = rest[:2]
        width = cos_ref.shape[1]
        o_ref[0, :, :width] = _rotate(p[:, :width], cos_ref[...], sin_ref[...])
        o_ref[0, :, width:] = p[:, width:]
    else:
        o_ref[0] = p


def _norm_mm(x, g, shift, scale, w, rope=None):
    B, N, D = x.shape
    M = w.shape[1]
    ts = min(512, N)
    per_b = pl.BlockSpec((1, 1, D), lambda b, i: (b, 0, 0))
    in_specs = [pl.BlockSpec((1, ts, D), lambda b, i: (b, i, 0)),
                pl.BlockSpec((1, D), lambda b, i: (0, 0)), per_b, per_b,
                pl.BlockSpec((D, M), lambda b, i: (0, 0))]
    args = [x, g.reshape(1, D), shift, scale, w]
    if rope is not None:
        in_specs += [pl.BlockSpec((ts, rope[0].shape[1]), lambda b, i: (i, 0))] * 2
        args += list(rope)
    return pl.pallas_call(
        _norm_mm_kernel,
        grid=(B, N // ts),
        in_specs=in_specs,
        out_specs=pl.BlockSpec((1, ts, M), lambda b, i: (b, i, 0)),
        out_shape=jax.ShapeDtypeStruct((B, N, M), F32),
        compiler_params=_params("parallel", "parallel"),
        name="norm_mm",
    )(*args)


def _rotate(x, cos, sin):
    out = []
    for c in range(x.shape[1] // LANES):
        sl = slice(c * LANES, (c + 1) * LANES)
        xc = x[:, sl]
        lane = lax.broadcasted_iota(jnp.int32, xc.shape, 1)
        first_half = (lane % AXIS_DIM) < (AXIS_DIM // 2)
        partner = jnp.where(first_half, pltpu.roll(xc, LANES - AXIS_DIM // 2, 1),
                            pltpu.roll(xc, AXIS_DIM // 2, 1))
        out.append(xc * cos[:, sl] + partner * sin[:, sl])
    return jnp.concatenate(out, axis=1)


def _rope_tables(S):
    rows = S // GRID_W
    row = jnp.repeat(jnp.arange(rows), GRID_W).astype(F32)
    col = jnp.tile(jnp.arange(GRID_W), rows).astype(F32)
    inv = ROPE_BASE ** (-jnp.arange(0, AXIS_DIM, 2, dtype=F32) / AXIS_DIM)
    ang_row, ang_col = row[:, None] * inv, col[:, None] * inv

    def axis_tables(ang):
        c, s = jnp.cos(ang), jnp.sin(ang)
        return jnp.concatenate([c, c], -1), jnp.concatenate([-s, s], -1)

    cr, sr = axis_tables(ang_row)
    cc, sc = axis_tables(ang_col)
    cos_h = jnp.concatenate([cr, cc], -1)
    sin_h = jnp.concatenate([sr, sc], -1)
    q_scale = HEAD_DIM ** -0.5
    cos = jnp.concatenate([jnp.tile(cos_h, (1, N_HEADS_ATTN)) * q_scale, jnp.tile(cos_h, (1, N_KV))], -1)
    sin = jnp.concatenate([jnp.tile(sin_h, (1, N_HEADS_ATTN)) * q_scale, jnp.tile(sin_h, (1, N_KV))], -1)
    return cos, sin


def _attn_heads(q, kcat, vcat, valid, sink_ref):
    outs = []
    G = N_HEADS_ATTN // N_KV
    Q = q.shape[0]
    for j in range(N_KV):
        kj = kcat[:, j * HEAD_DIM:(j + 1) * HEAD_DIM]
        vj = vcat[:, j * HEAD_DIM:(j + 1) * HEAD_DIM]
        heads = range(j * G, (j + 1) * G)
        qs = jnp.concatenate([q[:, h * HEAD_DIM:(h + 1) * HEAD_DIM] for h in heads], axis=0).astype(BF16)
        s = lax.dot_general(qs, kj, (((1,), (1,)), ((), ())), preferred_element_type=F32)
        if valid is not None:
            s = jnp.where(valid[None], s.reshape(G, Q, -1), NEG_INF).reshape(G * Q, -1)
        sink = jnp.concatenate([jnp.full((Q, 1), sink_ref[h], F32) for h in heads], axis=0)
        m = jnp.maximum(jnp.max(s, axis=-1, keepdims=True), sink)
        e = jnp.exp(s - m)
        den = jnp.sum(e, axis=-1, keepdims=True) + jnp.exp(sink - m)
        o = jnp.dot(e.astype(BF16), vj, preferred_element_type=F32) / den
        outs.extend(o[g * Q:(g + 1) * Q] for g in range(G))
    return jnp.concatenate(outs, axis=-1)


def _attn_kernel(sink_ref, q_ref, kp_ref, kc_ref, kn_ref, vp_ref, vc_ref, vn_ref, ck_ref, cv_ref,
                 o_ref, *, S, Lc):
    i = pl.program_id(1)
    kcat = jnp.concatenate([ck_ref[0], kp_ref[0], kc_ref[0], kn_ref[0]], axis=0).astype(BF16)
    vcat = jnp.concatenate([cv_ref[0], vp_ref[0], vc_ref[0], vn_ref[0]], axis=0).astype(BF16)
    L = Lc + 3 * QBLOCK
    row = lax.broadcasted_iota(jnp.int32, (QBLOCK, L), 0)
    col = lax.broadcasted_iota(jnp.int32, (QBLOCK, L), 1)
    rel = col - Lc - QBLOCK - row
    kpos = (i - 1) * QBLOCK + col - Lc
    valid = (col < Lc) | ((jnp.abs(rel) <= WINDOW) & (kpos >= 0) & (kpos < S))
    o_ref[0] = _attn_heads(q_ref[0], kcat, vcat, valid, sink_ref)


def _attention(sink, proj_l, proj_c):
    B, S, _ = proj_l.shape
    Lc = proj_c.shape[1]
    nb = S // QBLOCK
    kcol, vcol = ATTN_W // KV_W, ATTN_W // KV_W + 1

    def blk(colblk, off):
        return pl.BlockSpec((1, QBLOCK, KV_W),
                            lambda b, i: (b, jnp.clip(i + off, 0, nb - 1), colblk))

    return pl.pallas_call(
        functools.partial(_attn_kernel, S=S, Lc=Lc),
        grid=(B, nb),
        in_specs=[pl.BlockSpec(memory_space=pltpu.SMEM),
                  pl.BlockSpec((1, QBLOCK, ATTN_W), lambda b, i: (b, i, 0)),
                  blk(kcol, -1), blk(kcol, 0), blk(kcol, 1),
                  blk(vcol, -1), blk(vcol, 0), blk(vcol, 1),
                  pl.BlockSpec((1, Lc, KV_W), lambda b, i: (b, 0, kcol)),
                  pl.BlockSpec((1, Lc, KV_W), lambda b, i: (b, 0, vcol))],
        out_specs=pl.BlockSpec((1, QBLOCK, ATTN_W), lambda b, i: (b, i, 0)),
        out_shape=jax.ShapeDtypeStruct((B, S, ATTN_W), F32),
        compiler_params=_params("parallel", "parallel"),
        name="attention",
    )(sink, proj_l, proj_l, proj_l, proj_l, proj_l, proj_l, proj_l, proj_c, proj_c)


def _ctx_attn_kernel(sink_ref, q_ref, ck_ref, cv_ref, o_ref):
    q = q_ref[0] * (HEAD_DIM ** -0.5)
    o_ref[0] = _attn_heads(q, ck_ref[0].astype(BF16), cv_ref[0].astype(BF16), None, sink_ref)


def _ctx_attention(sink, proj_c):
    B, Lc, _ = proj_c.shape
    kcol, vcol = ATTN_W // KV_W, ATTN_W // KV_W + 1
    return pl.pallas_call(
        _ctx_attn_kernel,
        grid=(B,),
        in_specs=[pl.BlockSpec(memory_space=pltpu.SMEM),
                  pl.BlockSpec((1, Lc, ATTN_W), lambda b: (b, 0, 0)),
                  pl.BlockSpec((1, Lc, KV_W), lambda b: (b, 0, kcol)),
                  pl.BlockSpec((1, Lc, KV_W), lambda b: (b, 0, vcol))],
        out_specs=pl.BlockSpec((1, Lc, ATTN_W), lambda b: (b, 0, 0)),
        out_shape=jax.ShapeDtypeStruct((B, Lc, ATTN_W), F32),
        compiler_params=_params("parallel"),
        name="ctx_attention",
    )(sink, proj_c, proj_c, proj_c)


def _seg_rev_block(j, nC, nL):
    return jnp.where(j < nC, nC - 1 - j, nC + nL - 1 - (j - nC))


def _lru_kernel(uf_ref, ur_ref, wg_ref, bg_ref, sp_ref, hf_ref, hr_ref, a_ref, b_ref, cf_ref, cr_ref, *, Tc):
    @pl.when(pl.program_id(1) == 0)
    def _():
        cf_ref[...] = jnp.zeros_like(cf_ref)
        cr_ref[...] = jnp.zeros_like(cr_ref)

    sub = 8
    row = lax.broadcasted_iota(jnp.int32, (sub, LRU_W), 0)
    for d, (u_ref, h_ref, c_ref) in enumerate(((uf_ref, hf_ref, cf_ref), (ur_ref, hr_ref, cr_ref))):
        u = u_ref[0]
        gates = jax.nn.sigmoid(jnp.dot(u.astype(BF16), wg_ref[d], preferred_element_type=F32) + bg_ref[d])
        a = jnp.exp(-(gates[:, :LRU_W] * sp_ref[d]))
        a_ref[...] = a
        b_ref[...] = jnp.sqrt(jnp.maximum(1.0 - a * a, 0.0)) * (gates[:, LRU_W:] * u)

        def tile(ti, carry, d=d, h_ref=h_ref):
            t0 = pl.multiple_of((ti if d == 0 else Tc // sub - 1 - ti) * sub, sub)
            av = a_ref[pl.ds(t0, sub), :]
            bv = b_ref[pl.ds(t0, sub), :]
            for s in (1, 2, 4):
                shift, known = (s, row >= s) if d == 0 else (sub - s, row < sub - s)
                a_prev = jnp.where(known, pltpu.roll(av, shift, 0), 1.0)
                b_prev = jnp.where(known, pltpu.roll(bv, shift, 0), 0.0)
                bv = bv + av * b_prev
                av = av * a_prev
            h = av * carry + bv
            h_ref[0, pl.ds(t0, sub), :] = h
            return h[sub - 1:sub] if d == 0 else h[0:1]

        c_ref[...] = lax.fori_loop(0, Tc // sub, tile, c_ref[...])


def _lru(u, Lc, wa, ba, wi, bi, lam):
    B, T, _ = u.shape
    S = T - Lc
    Tc = math.gcd(math.gcd(Lc, S), 256)
    nC, nL = Lc // Tc, S // Tc
    w_gates = jnp.stack([jnp.concatenate([_block_diag(wa[d]), _block_diag(wi[d])], axis=1)
                         for d in range(2)]).astype(BF16)
    b_gates = jnp.stack([jnp.concatenate([ba[d], bi[d]]) for d in range(2)])[:, None, :]
    decay_rate = (LRU_C * jax.nn.softplus(-lam))[:, None, :]
    fwd = pl.BlockSpec((1, Tc, LRU_W), lambda bi_, j: (bi_, j, 0))
    rev = pl.BlockSpec((1, Tc, LRU_W), lambda bi_, j: (bi_, _seg_rev_block(j, nC, nL), 0))

    def const(shape):
        return pl.BlockSpec(shape, lambda bi_, j: (0,) * len(shape))

    return pl.pallas_call(
        functools.partial(_lru_kernel, Tc=Tc),
        grid=(B, nC + nL),
        in_specs=[fwd, rev, const(w_gates.shape), const(b_gates.shape), const(decay_rate.shape)],
        out_specs=[fwd, rev],
        out_shape=[jax.ShapeDtypeStruct((B, T, LRU_W), F32)] * 2,
        scratch_shapes=[pltpu.VMEM((Tc, LRU_W), F32), pltpu.VMEM((Tc, LRU_W), F32),
                        pltpu.VMEM((1, LRU_W), F32), pltpu.VMEM((1, LRU_W), F32)],
        compiler_params=_params("parallel", "arbitrary"),
        name="lru",
    )(u, u, w_gates, b_gates, decay_rate)


def _mix_out_kernel(attn_ref, hf_ref, hr_ref, g0_ref, g1_ref, wo_ref, x_ref, gate_ref, g_ref, sh_ref, sc_ref,
                    wr_ref, xo_ref, h_ref, aff_ref):
    gate = jnp.concatenate([g0_ref[0], g1_ref[0]], axis=-1)
    rec = ((hf_ref[0] + hr_ref[0]) * jax.nn.gelu(gate)).astype(BF16)
    y = (jnp.dot(attn_ref[0].astype(BF16), wo_ref[:ATTN_W, :], preferred_element_type=F32)
         + jnp.dot(rec, wo_ref[ATTN_W:, :], preferred_element_type=F32))
    x = x_ref[0] + gate_ref[0] * y
    xo_ref[0] = x
    xn = x * lax.rsqrt(jnp.mean(x * x, axis=-1, keepdims=True) + NORM_EPS) * g_ref[...]
    h = xn * (1.0 + sc_ref[0]) + sh_ref[0]
    _to_chunk_rows(h_ref.at[0], h)
    logits = lax.dot_general(wr_ref[...], h, (((1,), (1,)), ((), ())),
                             precision=lax.Precision.HIGHEST, preferred_element_type=F32)
    e = jnp.exp(logits - jnp.max(logits, axis=0, keepdims=True))
    aff_ref[0] = e / jnp.sum(e, axis=0, keepdims=True)


def _mix_out(attn, hf, hr, proj, w_out, t_off, x, gate, g, shift, scale, router_w):
    B, N, _ = attn.shape
    D = w_out.shape[1]
    E = router_w.shape[1]
    chunks = D // LANES
    tm = math.gcd(math.gcd(N, 256), t_off) if t_off else min(256, N)
    half = LRU_W // 2
    gate_blk = (proj.shape[2] - LRU_W) // half
    tok = lambda b, i: (b, i, 0)
    per_b = pl.BlockSpec((1, 1, D), lambda b, i: (b, 0, 0))
    return pl.pallas_call(
        _mix_out_kernel,
        grid=(B, N // tm),
        in_specs=[pl.BlockSpec((1, tm, ATTN_W), tok),
                  pl.BlockSpec((1, tm, LRU_W), lambda b, i: (b, i + t_off // tm, 0)),
                  pl.BlockSpec((1, tm, LRU_W), lambda b, i: (b, i + t_off // tm, 0)),
                  pl.BlockSpec((1, tm, half), lambda b, i: (b, i, gate_blk)),
                  pl.BlockSpec((1, tm, half), lambda b, i: (b, i, gate_blk + 1)),
                  pl.BlockSpec(w_out.shape, lambda b, i: (0, 0)),
                  pl.BlockSpec((1, tm, D), tok), per_b,
                  pl.BlockSpec((1, D), lambda b, i: (0, 0)), per_b, per_b,
                  pl.BlockSpec((E, D), lambda b, i: (0, 0))],
        out_specs=[pl.BlockSpec((1, tm, D), tok),
                   pl.BlockSpec((1, tm * chunks, LANES), tok),
                   pl.BlockSpec((1, E, tm), lambda b, i: (b, 0, i))],
        out_shape=[jax.ShapeDtypeStruct((B, N, D), F32),
                   jax.ShapeDtypeStruct((B, N * chunks, LANES), F32),
                   jax.ShapeDtypeStruct((B, E, N), F32)],
        compiler_params=_params("parallel", "parallel"),
        name="mix_out",
    )(attn, hf, hr, proj, proj, w_out.astype(BF16), x, gate, g.reshape(1, D), shift, scale, router_w.T)


def _softplus(z):
    return jnp.maximum(z, 0.0) + jnp.log(1.0 + jnp.exp(-jnp.abs(z)))


def _nt_dot(wt, x):
    return lax.dot_general(wt, x, (((1,), (1,)), ((), ())), preferred_element_type=F32)


def _rwkv_prep_kernel(cp_ref, c_ref, cn_ref, lp_ref, l_ref, ln_ref, mu_ref, wr_ref, wk_ref, wv_ref, w1_ref,
                      w2_ref, a1_ref, a2_ref, g1_ref, g2_ref, w0_ref, a0_ref, ka_ref, rk_ref, perm_ref,
                      dec_ref, aa_ref, k_ref, v_ref, r_ref, g_ref, bc_ref, *, B, ctx_tiles, tiles):
    i = pl.program_id(0)
    tm, D = l_ref.shape
    is_ctx = i < ctx_tiles
    h = jnp.where(is_ctx, c_ref[...], l_ref[...])
    seq_start = (i == 0) | (i == ctx_tiles)
    seq_end = (i == ctx_tiles - 1) | (i == tiles - 1)
    hp = jnp.where(seq_start, 0.0, jnp.where(is_ctx, cp_ref[...], lp_ref[...]))
    hn = jnp.where(seq_end, 0.0, jnp.where(is_ctx, cn_ref[...], ln_ref[...]))
    xx = 0.5 * (jnp.concatenate([hp, h[:tm - B]], axis=0) + jnp.concatenate([h[B:], hn], axis=0)) - h

    def mix(j):
        return (h + xx * mu_ref[j:j + 1, :]).astype(BF16)

    hpg = LANES // B
    hd = RWKV_HD

    def to_scan_layout(x, ref):
        for g in range(RWKV_H // hpg):
            for c in range(tm // LANES):
                tiles = [x[(g * hpg + hh) * hd:(g * hpg + hh + 1) * hd, c * LANES:(c + 1) * LANES]
                         for hh in range(hpg)]
                for tl, tile in enumerate(_slab_transpose(tiles, B)):
                    ref[g, c * hpg + tl] = tile

    v = _nt_dot(wv_ref[...], mix(3))
    to_scan_layout(v, v_ref)
    r = _nt_dot(wr_ref[...], mix(0))
    to_scan_layout(r, r_ref)
    k = _nt_dot(wk_ref[...], mix(2))
    to_scan_layout(k, k_ref)
    lw = jnp.tanh(_nt_dot(w1_ref[...], mix(1))).astype(BF16)
    wpre = jnp.dot(w2_ref[...], lw, preferred_element_type=F32)
    la = _nt_dot(a1_ref[...], mix(4)).astype(BF16)
    apre = jnp.dot(a2_ref[...], la, preferred_element_type=F32)
    gg = jax.nn.sigmoid(_nt_dot(g1_ref[...], mix(5))).astype(BF16)
    g_ref[...] = jnp.dot(g2_ref[...], gg, preferred_element_type=F32)

    def to_scan_layout_mxu(xs, refs):
        groups = RWKV_H // hpg
        for c in range(tm // LANES):
            lhs = jnp.concatenate(
                [jnp.concatenate([x[(g * hpg + hh) * hd:(g * hpg + hh + 1) * hd, c * LANES:(c + 1) * LANES]
                                  for hh in range(hpg)], axis=1)
                 for x in xs for g in range(groups)], axis=0)
            hi = lhs.astype(BF16)
            rest = lhs - hi.astype(F32)
            mid = rest.astype(BF16)
            lo = (rest - mid.astype(F32)).astype(BF16)
            out = (jnp.dot(hi, perm_ref[...], preferred_element_type=F32)
                   + jnp.dot(mid, perm_ref[...], preferred_element_type=F32)
                   + jnp.dot(lo, perm_ref[...], preferred_element_type=F32))
            for n, ref in enumerate(refs):
                for g in range(groups):
                    blk = out[(n * groups + g) * hd:(n * groups + g + 1) * hd]
                    for tl in range(hpg):
                        ref[g, c * hpg + tl] = blk[:, tl * LANES:(tl + 1) * LANES]

    iclr, decay = [], []
    for d in range(2):
        w_log = -_softplus(-(w0_ref[d] + wpre[d * D:(d + 1) * D])) - 0.5
        decay.append(jnp.exp(-jnp.exp(w_log)))
        a = jax.nn.sigmoid(a0_ref[d] + apre[d * D:(d + 1) * D])
        iclr.append(a)
        to_scan_layout(a, aa_ref.at[d])
    to_scan_layout_mxu(decay, [dec_ref.at[0], dec_ref.at[1]])
    kd_sum = k * (2.0 + (iclr[0] + iclr[1] - 2.0) * ka_ref[...])
    bc = jnp.sum((r * kd_sum * rk_ref[...]).reshape(RWKV_H, hd, tm), axis=1)
    for g in range(RWKV_H // hpg):
        for c in range(tm // LANES):
            rows = [jnp.broadcast_to(bc[g * hpg + hh:g * hpg + hh + 1, c * LANES:(c + 1) * LANES], (8, LANES))
                    for hh in range(hpg)]
            for tl, tile in enumerate(_slab_transpose(rows, B)):
                bc_ref[g, c * hpg + tl] = tile[0:1]


def _slab_permutation(B):
    n = LANES // B
    src = jnp.arange(n * LANES).reshape(n, n, B).transpose(1, 0, 2).reshape(-1)
    return (jnp.arange(n * LANES)[:, None] == src[None, :]).astype(BF16)


def _slab_transpose(tiles, B):
    n = len(tiles)
    tiles = list(tiles)
    slab = lax.broadcasted_iota(jnp.int32, tiles[0].shape, 1) // B
    s = n // 2
    while s >= 1:
        upper = (slab & s) != 0
        for i in range(n):
            if i & s == 0:
                lo, hi = tiles[i], tiles[i + s]
                tiles[i] = jnp.where(upper, pltpu.roll(hi, s * B, 1), lo)
                tiles[i + s] = jnp.where(upper, hi, pltpu.roll(lo, LANES - s * B, 1))
        s //= 2
    return tiles


def _const_spec(shape):
    nd = len(shape)
    return pl.BlockSpec(shape, lambda i: (0,) * nd, pipeline_mode=pl.Buffered(1))


def _rwkv_prep(hc_tm, hl_tm, B, mu, w_rkv, w0, w1, w2, a0, a1, a2, g1, g2, k_a, r_k):
    D = hl_tm.shape[1]
    Lc, S = hc_tm.shape[0] // B, hl_tm.shape[0] // B
    T = Lc + S
    TB = T * B
    tm = 256
    tiles, ctx_tiles = TB // tm, Lc * B // tm
    hb = tm // B
    G = RWKV_H // (LANES // B)
    hd = RWKV_HD

    def t_bf16(w):
        return w.T.astype(BF16)

    zeros_w = jnp.zeros_like(w2[0].T)
    w2t = jnp.concatenate([jnp.concatenate([w2[0].T, zeros_w], 1),
                           jnp.concatenate([zeros_w, w2[1].T], 1)], 0).astype(BF16)
    zeros_a = jnp.zeros_like(a2[0].T)
    a2t = jnp.concatenate([jnp.concatenate([a2[0].T, zeros_a], 1),
                           jnp.concatenate([zeros_a, a2[1].T], 1)], 0).astype(BF16)
    consts = [mu, t_bf16(w_rkv[0]), t_bf16(w_rkv[1]), t_bf16(w_rkv[2]),
              t_bf16(jnp.concatenate([w1[0], w1[1]], 1)), w2t,
              t_bf16(jnp.concatenate([a1[0], a1[1]], 1)), a2t, t_bf16(g1), t_bf16(g2),
              w0.reshape(2, D, 1), a0.reshape(2, D, 1), k_a.reshape(D, 1), r_k.reshape(D, 1),
              _slab_permutation(B)]
    dir_spec = pl.BlockSpec((2, G, hb, hd, LANES), lambda i: (0, 0, i, 0, 0))
    all_spec = pl.BlockSpec((G, hb, hd, LANES), lambda i: (0, i, 0, 0))
    dir_shape = jax.ShapeDtypeStruct((2, G, T, hd, LANES), F32)
    all_shape = jax.ShapeDtypeStruct((G, T, hd, LANES), F32)

    def stream_specs(first, n_tiles, steps):
        def tile(i):
            return jnp.clip(i - first, 0, n_tiles - 1)
        return [pl.BlockSpec((B, D), lambda i: (jnp.maximum(tile(i) * hb - 1, 0), 0)),
                pl.BlockSpec((tm, D), lambda i: (tile(i), 0)),
                pl.BlockSpec((B, D), lambda i: (jnp.minimum((tile(i) + 1) * hb, steps - 1), 0))]

    return pl.pallas_call(
        functools.partial(_rwkv_prep_kernel, B=B, ctx_tiles=ctx_tiles, tiles=tiles),
        grid=(tiles,),
        in_specs=stream_specs(0, ctx_tiles, Lc) + stream_specs(ctx_tiles, tiles - ctx_tiles, S)
                 + [_const_spec(c.shape) for c in consts],
        out_specs=[dir_spec, dir_spec, all_spec, all_spec, all_spec,
                   pl.BlockSpec((D, tm), lambda i: (0, i)),
                   pl.BlockSpec((G, hb, 1, LANES), lambda i: (0, i, 0, 0))],
        out_shape=[dir_shape, dir_shape, all_shape, all_shape, all_shape,
                   jax.ShapeDtypeStruct((D, TB), F32),
                   jax.ShapeDtypeStruct((G, T, 1, LANES), F32)],
        compiler_params=_params("parallel"),
        name="rwkv_prep",
    )(hc_tm, hc_tm, hc_tm, hl_tm, hl_tm, hl_tm, *consts)


def _wkv_kernel(dec_ref, aa_ref, k_ref, v_ref, r_ref, kkc_ref, kac_ref, y_ref,
                s_ref, g_ref, p_ref, q_ref, sa_ref, *, Tc, nC):
    d = pl.program_id(0)
    j = pl.program_id(2)
    hd = RWKV_HD
    sub = 8

    @pl.when(j == 0)
    def _():
        s_ref[...] = jnp.zeros_like(s_ref)

    def partial_rows(x):
        return jnp.sum(x.reshape(hd // sub, sub, LANES), axis=0)

    pitch = hd + 1

    def put_partial(ref, v, x):
        ref[pl.ds(v, sub, stride=pitch), :] = partial_rows(x)

    def finish_rows(ref):
        acc = ref[pl.ds(0, hd), :]
        for q in range(1, sub):
            acc = acc + ref[pl.ds(q * pitch, hd), :]
        return acc

    def run(with_y):
        g_ref[...] = jnp.ones_like(g_ref)

        def step(s, _):
            tt = jnp.where(d == 0, s, Tc - 1 - s)
            aa = aa_ref[0, 0, tt]
            kk = k_ref[0, tt]
            kf = kk * kkc_ref[0]
            kn = kf * lax.rsqrt(jnp.maximum(jnp.sum(kf * kf, axis=0, keepdims=True), 1e-24))
            g_prev = g_ref[...]
            g = g_prev * dec_ref[0, 0, tt]
            g_inv = 1.0 / g
            g_ref[...] = g
            a_t = -(kn * g_prev)
            b_t = kn * aa * g_inv
            k_t = kk * (1.0 + (aa - 1.0) * kac_ref[0]) * g_inv
            r_t = r_ref[0, tt] * g if with_y else None

            def row_reduce(v, _):
                st = s_ref[v]
                put_partial(p_ref, v, st * a_t)
                if with_y:
                    put_partial(q_ref, v, st * r_t)
                return 0

            lax.fori_loop(0, hd, row_reduce, 0, unroll=8)
            sa = finish_rows(p_ref)
            sa_ref[...] = sa
            if with_y:
                b_r = jnp.sum(b_t * r_t, axis=0, keepdims=True)
                k_r = jnp.sum(k_t * r_t, axis=0, keepdims=True)
                y_ref[0, 0, tt] = finish_rows(q_ref) + sa * b_r + v_ref[0, tt] * k_r

            def row_update(v, _):
                s_ref[v] = s_ref[v] + sa_ref[pl.ds(v, 1), :] * b_t + v_ref[0, tt, pl.ds(v, 1), :] * k_t
                return 0

            lax.fori_loop(0, hd, row_update, 0, unroll=8)
            return 0

        lax.fori_loop(0, Tc, step, 0)

        def rescale(v, _):
            s_ref[v] = s_ref[v] * g_ref[...]
            return 0

        lax.fori_loop(0, hd, rescale, 0, unroll=8)

    @pl.when(j < nC)
    def _():
        run(False)

    @pl.when(j >= nC)
    def _():
        run(True)


def _wkv(dec, aa, k, v, r, k_k, k_a, B, Lc):
    _, G, T, hd, _ = dec.shape
    Tc = WKV_TC
    S = T - Lc
    nC, nL = Lc // Tc, S // Tc

    def tmap(d, j):
        return jnp.where(d == 0, j, _seg_rev_block(j, nC, nL))

    def lmap(d, j):
        jj = jnp.maximum(j - nC, 0)
        return jnp.where(d == 0, jj, nL - 1 - jj)

    dir_spec = pl.BlockSpec((1, 1, Tc, hd, LANES), lambda d, g, j: (d, g, tmap(d, j), 0, 0))
    all_spec = pl.BlockSpec((1, Tc, hd, LANES), lambda d, g, j: (g, tmap(d, j), 0, 0))
    const_spec = pl.BlockSpec((1, hd, LANES), lambda d, g, j: (g, 0, 0))
    return pl.pallas_call(
        functools.partial(_wkv_kernel, Tc=Tc, nC=nC),
        grid=(2, G, nC + nL),
        in_specs=[dir_spec, dir_spec, all_spec, all_spec, all_spec, const_spec, const_spec],
        out_specs=pl.BlockSpec((1, 1, Tc, hd, LANES), lambda d, g, j: (d, g, lmap(d, j), 0, 0)),
        out_shape=jax.ShapeDtypeStruct((2, G, S, hd, LANES), F32),
        scratch_shapes=[pltpu.VMEM((hd, hd, LANES), F32),
                        pltpu.VMEM((hd, LANES), F32),
                        pltpu.VMEM(((hd + 1) * 8, LANES), F32),
                        pltpu.VMEM(((hd + 1) * 8, LANES), F32),
                        pltpu.VMEM((hd, LANES), F32)],
        compiler_params=_params("parallel", "parallel", "arbitrary"),
        name="wkv",
    )(dec, aa, k, v, r, _lane_const(k_k, B), _lane_const(k_a, B))


def _lane_const(c, B):
    hpg = LANES // B
    return jnp.repeat(c.reshape(RWKV_H // hpg, hpg, RWKV_HD).transpose(0, 2, 1), B, axis=-1)


def _rwkv_finish_kernel(y_ref, v_ref, bc_ref, g_ref, gng_ref, gnb_ref, wo_ref, o_ref, *, B):
    G, steps, hd, _ = v_ref.shape
    hpg = LANES // B
    rows = []
    for g in range(G):
        head_cols = [[] for _ in range(hpg)]
        for c in range(steps // hpg):
            tiles = []
            for tl in range(hpg):
                t = c * hpg + tl
                y = y_ref[0, g, t] + y_ref[1, g, t]
                mean = jnp.mean(y, axis=0, keepdims=True)
                var = jnp.mean(jnp.square(y - mean), axis=0, keepdims=True)
                yn = (y - mean) * lax.rsqrt(var + GN_EPS)
                tiles.append(yn * gng_ref[g] + gnb_ref[g] + bc_ref[g, t] * v_ref[g, t])
            for hh, tile in enumerate(_slab_transpose(tiles, B)):
                head_cols[hh].append(tile)
        rows.extend(jnp.concatenate(cols, axis=1) for cols in head_cols)
    o = (jnp.concatenate(rows, axis=0) * g_ref[...]).astype(BF16)
    out_t = jnp.dot(wo_ref[...], o, preferred_element_type=F32)
    o_ref[...] = out_t.T


def _rwkv_finish(y2, v, bc, g, gn_g, gn_b, w_o, B, Lc):
    _, G, S, hd, _ = y2.shape
    D = g.shape[0]
    tm = 256
    steps = tm // B
    off = Lc // steps
    return pl.pallas_call(
        functools.partial(_rwkv_finish_kernel, B=B),
        grid=(S // steps,),
        in_specs=[pl.BlockSpec((2, G, steps, hd, LANES), lambda i: (0, 0, i, 0, 0)),
                  pl.BlockSpec((G, steps, hd, LANES), lambda i: (0, i + off, 0, 0)),
                  pl.BlockSpec((G, steps, 1, LANES), lambda i: (0, i + off, 0, 0)),
                  pl.BlockSpec((D, tm), lambda i: (0, i + off)),
                  _const_spec((G, hd, LANES)), _const_spec((G, hd, LANES)), _const_spec((D, D))],
        out_specs=pl.BlockSpec((tm, D), lambda i: (i, 0)),
        out_shape=jax.ShapeDtypeStruct((S * B, D), F32),
        compiler_params=_params("parallel"),
        name="rwkv_finish",
    )(y2, v, bc, g, _lane_const(gn_g, B), _lane_const(gn_b, B), w_o.T.astype(BF16))


def _rwkv7_mixer(hl, hc, mu, w_rkv, w0, w1, w2, a0, a1, a2, g1, g2, k_k, k_a, r_k, gn_g, gn_b, w_o):
    S, B, D = hl.shape
    Lc = hc.shape[0]
    dec, aa, k, v, r, g, bc = _rwkv_prep(hc.reshape(Lc * B, D), hl.reshape(S * B, D), B, mu, w_rkv, w0, w1,
                                         w2, a0, a1, a2, g1, g2, k_a, r_k.reshape(D))
    y2 = _wkv(dec, aa, k, v, r, k_k, k_a, B, Lc)
    return _rwkv_finish(y2, v, bc, g, gn_g, gn_b, w_o, B, Lc).reshape(S, B, D)


def _moe_kernel(idx_prev_ref, idx_ref, idx_next_ref, gate_ref, h_ref, w1_ref, w3_ref, w2_ref, o_ref,
                xa_ref, xb_ref, ya_ref, yb_ref, *, cap, chunks):
    bb = h_ref.shape[0]
    rows = bb * cap
    e = pl.program_id(1)
    n_e = pl.num_programs(1)
    batch = 8

    def token_rows(t):
        return pl.ds(pl.multiple_of(t * chunks, chunks), chunks)

    def scatter_batch(ids_ref, y_ref, first, keep):
        bi = first // cap
        slots = [first + u for u in range(batch)]
        dst = [token_rows(ids_ref[0, 0, s]) for s in slots]
        vals = [y_ref[token_rows(s), :] for s in slots]
        if keep is not None:
            vals = [jnp.where(keep, v, 0.0) for v in vals]
        sums = [o_ref[bi, dst[u], :] + vals[u] for u in range(batch)]
        for u in range(batch):
            o_ref[bi, dst[u], :] = sums[u]

    @pl.when(e == 0)
    def _():
        o_ref[...] = jnp.zeros_like(o_ref)
        yb_ref[...] = jnp.zeros_like(yb_ref)
        for bi in range(bb):
            def gather(r, _, bi=bi):
                slot = bi * cap + r
                xa_ref[token_rows(slot), :] = h_ref[bi, token_rows(idx_ref[0, 0, slot]), :]
                return 0

            lax.fori_loop(0, cap, gather, 0, unroll=8)

    def stage(x_cur, x_next, y_cur, y_prev):
        x = _from_chunk_rows(x_cur, rows).astype(BF16)
        h1 = jnp.dot(x, w1_ref[0, 0], preferred_element_type=F32)
        h3 = jnp.dot(x, w3_ref[0, 0], preferred_element_type=F32)
        hid = (h1 * jax.nn.sigmoid(h1) * h3).astype(BF16)
        _to_chunk_rows(y_cur, jnp.dot(hid, w2_ref[0, 0], preferred_element_type=F32) * gate_ref[0])
        for slot in range(rows):
            x_next[token_rows(slot), :] = h_ref[slot // cap, token_rows(idx_next_ref[0, 0, slot]), :]
        for first in range(0, rows, batch):
            scatter_batch(idx_prev_ref, y_prev, first, e > 0)

    @pl.when(e % 2 == 0)
    def _():
        stage(xa_ref, xb_ref, ya_ref, yb_ref)

    @pl.when(e % 2 == 1)
    def _():
        stage(xb_ref, xa_ref, yb_ref, ya_ref)

    def drain(y_ref):
        for first in range(0, rows, batch):
            scatter_batch(idx_ref, y_ref, first, None)

    @pl.when((e == n_e - 1) & (e % 2 == 0))
    def _():
        drain(ya_ref)

    @pl.when((e == n_e - 1) & (e % 2 == 1))
    def _():
        drain(yb_ref)


def _moe(h, aff, layer, w1, w3, w2):
    B, _, N = aff.shape
    _, E, D, FF = w1.shape
    chunks = D // LANES
    cap = CAPACITY * N // E
    bb = max(1, min(B, MOE_ROWS // cap))
    gate, idx = lax.top_k(aff, cap)

    def group(t):
        return t.reshape(B // bb, bb, E, cap).transpose(0, 2, 1, 3).reshape(B // bb * E, bb * cap)

    rows = bb * cap
    idx_g = group(idx.astype(jnp.int32))[:, None, :]

    def idx_spec(off):
        return pl.BlockSpec((1, 1, rows), lambda b, e: (b * E + jnp.clip(e + off, 0, E - 1), 0, 0),
                            memory_space=pltpu.SMEM)

    scratch = pltpu.VMEM((rows * chunks, LANES), F32)
    return pl.pallas_call(
        functools.partial(_moe_kernel, cap=cap, chunks=chunks),
        grid=(B // bb, E),
        in_specs=[idx_spec(-1), idx_spec(0), idx_spec(1),
                  pl.BlockSpec((1, rows, 1), lambda b, e: (b * E + e, 0, 0)),
                  pl.BlockSpec((bb, N * chunks, LANES), lambda b, e: (b, 0, 0), pipeline_mode=pl.Buffered(1)),
                  pl.BlockSpec((1, 1, D, FF), lambda b, e: (layer, e, 0, 0)),
                  pl.BlockSpec((1, 1, D, FF), lambda b, e: (layer, e, 0, 0)),
                  pl.BlockSpec((1, 1, FF, D), lambda b, e: (layer, e, 0, 0))],
        out_specs=pl.BlockSpec((bb, N * chunks, LANES), lambda b, e: (b, 0, 0)),
        out_shape=jax.ShapeDtypeStruct((B, N * chunks, LANES), F32),
        scratch_shapes=[scratch] * 4,
        compiler_params=_params("parallel", "arbitrary"),
        name="moe",
    )(idx_g, idx_g, idx_g, group(gate)[:, :, None], h, w1, w3, w2)


def _block_diag(w):
    H, bi, bj = w.shape
    eye = jnp.eye(H, dtype=w.dtype)
    return jnp.einsum('hij,hg->higj', w, eye).reshape(H * bi, H * bj)


def _dwconv(u, w, b):
    n = u.shape[1]
    up = jnp.pad(u, ((0, 0), (CONV_LEFT, CONV_W - 1 - CONV_LEFT), (0, 0)))
    out = up[:, 0:n] * w[0]
    for j in range(1, CONV_W):
        out = out + up[:, j:j + n] * w[j]
    return out + b


def _attn_lru_mixer(proj_l, proj_c, w_out, sink, conv_w, conv_b, wa, ba, wi, bi, lam, tail_l, tail_c):
    Lc = proj_c.shape[1]
    attn_l = _attention(sink, proj_l, proj_c)
    attn_c = _ctx_attention(sink, proj_c)

    u0, g0 = ATTN_W + 2 * KV_W, ATTN_W + 2 * KV_W + LRU_W
    u = jnp.concatenate([_dwconv(proj_c[..., u0:g0], conv_w, conv_b),
                         _dwconv(proj_l[..., u0:g0], conv_w, conv_b)], axis=1)
    hf, hr = _lru(u, Lc, wa, ba, wi, bi, lam)
    return (_mix_out(attn_l, hf, hr, proj_l, w_out, Lc, *tail_l),
            _mix_out(attn_c, hf, hr, proj_c, w_out, 0, *tail_c))


def kernel(x, c, ctx, c_ctx, mod_w, mod_b, norm_mix, norm_ffn, router_w, exp_w1, exp_w3, exp_w2, mix_in, mix_out, attn_sink, lru_conv_w, lru_conv_b, lru_wa, lru_ba, lru_wi, lru_bi, lru_lam, rw_mu, rw_rkv, rw_w0, rw_w1, rw_w2, rw_a0, rw_a1, rw_a2, rw_g1, rw_g2, rw_kk, rw_ka, rw_rk, rw_gn_g, rw_gn_b, rw_wo, final_norm):
    B, S, D = x.shape
    Lc = ctx.shape[1]
    depth = mod_w.shape[0]
    assert depth == 2 and S % QBLOCK == 0
    assert LANES % B == 0 and RWKV_H % (LANES // B) == 0 and B % 8 == 0
    assert Lc % WKV_TC == 0 and S % WKV_TC == 0 and (Lc * B) % 256 == 0

    n_rows = -(-(B + 1) // 8) * 8
    cond = jnp.concatenate([jax.nn.silu(c), jax.nn.silu(c_ctx)[None],
                            jnp.zeros((n_rows - B - 1, D), F32)], axis=0)

    mods_l, mods_c = [], []
    for layer in range(depth):
        mod = _mm(cond, mod_w[layer], tm=n_rows, tn=1024, precise=True) + mod_b[layer]
        mods_l.append([t[:, None, :] for t in jnp.split(mod[:B], 6, axis=-1)])
        mods_c.append([jnp.broadcast_to(t[None], (B, 1, D)) for t in jnp.split(mod[B:B + 1], 6, axis=-1)])

    w1b, w3b, w2b = exp_w1.astype(BF16), exp_w3.astype(BF16), exp_w2.astype(BF16)
    xl, xc = x, ctx
    hl = hc = None
    for layer in range(depth):
        last = layer == depth - 1
        m_l, m_c = mods_l[layer], mods_c[layer]
        if layer % 2 == 0:
            i = layer // 2
            assert layer == 0
            w_in = mix_in[i].astype(BF16)
            proj_l = _norm_mm(xl, norm_mix[layer], m_l[0], m_l[1], w_in, rope=_rope_tables(S))
            proj_c = _norm_mm(xc, norm_mix[layer], m_c[0], m_c[1], w_in)
            tail_l = (xl, m_l[2], norm_ffn[layer], m_l[3], m_l[4], router_w[layer])
            tail_c = (xc, m_c[2], norm_ffn[layer], m_c[3], m_c[4], router_w[layer])
            (xl, hl, aff_l), (xc, hc, aff_c) = _attn_lru_mixer(
                proj_l, proj_c, mix_out[i], attn_sink[i], lru_conv_w[i], lru_conv_b[i], lru_wa[i],
                lru_ba[i], lru_wi[i], lru_bi[i], lru_lam[i], tail_l, tail_c)
        else:
            i = layer // 2
            yl = _rwkv7_mixer(hl, hc, rw_mu[i], rw_rkv[i], rw_w0[i], rw_w1[i], rw_w2[i], rw_a0[i], rw_a1[i],
                              rw_a2[i], rw_g1[i], rw_g2[i], rw_kk[i], rw_ka[i], rw_rk[i], rw_gn_g[i],
                              rw_gn_b[i], rw_wo[i])
            xl, hl, aff_l = _resid_norm_tm(xl, yl, m_l[2], norm_ffn[layer], m_l[3], m_l[4], delta_tm=True,
                                           h_tm=False, router_w=router_w[layer])
        moe_l = _moe(hl, aff_l, layer, w1b, w3b, w2b)
        if last:
            return _final_norm(xl, moe_l, m_l[5], final_norm)
        assert (layer + 1) % 2 == 1
        n_l, n_c = mods_l[layer + 1], mods_c[layer + 1]
        xl, hl = _resid_norm_tm(xl, moe_l, m_l[5], norm_mix[layer + 1], n_l[0], n_l[1], delta_tm=False, h_tm=True)
        moe_c = _moe(hc, aff_c, layer, w1b, w3b, w2b)
        xc, hc = _resid_norm_tm(xc, moe_c, m_c[5], norm_mix[layer + 1], n_c[0], n_c[1], delta_tm=False, h_tm=True)
```

```python
import functools
import math

import jax
import jax.numpy as jnp
from jax import lax
from jax.experimental import pallas as pl
from jax.experimental.pallas import tpu as pltpu

F32 = jnp.float32
BF16 = jnp.bfloat16

GRID_W = 64
N_HEADS_ATTN = 8
N_KV = 2
HEAD_DIM = 64
AXIS_DIM = HEAD_DIM // 2
ATTN_W = N_HEADS_ATTN * HEAD_DIM
KV_W = N_KV * HEAD_DIM
WINDOW = 128
QBLOCK = 128
ROPE_BASE = 10000.0
LRU_W = 512
LRU_BLOCKS = 8
CONV_W = 4
CONV_LEFT = 2
LRU_C = 8.0
RWKV_H = 16
RWKV_HD = 64
GN_EPS = 64e-5
N_EXPERTS = 16
CAPACITY = 2
NORM_EPS = 1e-6
NEG_INF = -1e30

LANES = 128
VMEM_LIMIT_BYTES = 56 * 1024 * 1024
WKV_TC = 32
MOE_ROWS = 256
MIX_OUT_ROWS = 128
def _params(*sem):
    return pltpu.CompilerParams(dimension_semantics=sem, vmem_limit_bytes=VMEM_LIMIT_BYTES)


def _mm_kernel(x_ref, w_ref, o_ref, *, precise):
    if precise:
        o_ref[...] = jnp.dot(x_ref[...], w_ref[...], precision=lax.Precision.HIGHEST,
                             preferred_element_type=F32)
    else:
        o_ref[...] = jnp.dot(x_ref[...].astype(BF16), w_ref[...].astype(BF16),
                             preferred_element_type=F32)


def _mm(x, w, *, tm=512, tn=None, precise=False):
    M, K = x.shape
    N = w.shape[1]
    tm = min(tm, M)
    tn = N if tn is None else min(tn, N)
    assert M % tm == 0 and N % tn == 0, (M, tm, N, tn)
    return pl.pallas_call(
        functools.partial(_mm_kernel, precise=precise),
        grid=(M // tm, N // tn),
        in_specs=[pl.BlockSpec((tm, K), lambda i, j: (i, 0)),
                  pl.BlockSpec((K, tn), lambda i, j: (0, j))],
        out_specs=pl.BlockSpec((tm, tn), lambda i, j: (i, j)),
        out_shape=jax.ShapeDtypeStruct((M, N), F32),
        compiler_params=_params("parallel", "parallel"),
        name="mm",
    )(x, w)


def _final_norm_kernel(x_ref, d_ref, gate_ref, g_ref, o_ref):
    x = x_ref[0] + gate_ref[0] * _from_chunk_rows(d_ref.at[0], x_ref.shape[1])
    o_ref[0] = x * lax.rsqrt(jnp.mean(x * x, axis=-1, keepdims=True) + NORM_EPS) * g_ref[...]


def _from_chunk_rows(ref, n):
    chunks = ref.shape[0] // n
    return jnp.concatenate([ref[pl.ds(j, n, stride=chunks), :] for j in range(chunks)], axis=-1)


def _to_chunk_rows(ref, x):
    n = x.shape[0]
    chunks = x.shape[1] // LANES
    for j in range(chunks):
        ref[pl.ds(j, n, stride=chunks), :] = x[:, j * LANES:(j + 1) * LANES]


def _final_norm(x, delta, gate, g):
    B, N, D = x.shape
    ts = min(512, N)
    chunks = D // LANES
    tok = pl.BlockSpec((1, ts, D), lambda b, i: (b, i, 0))
    return pl.pallas_call(
        _final_norm_kernel,
        grid=(B, N // ts),
        in_specs=[tok, pl.BlockSpec((1, ts * chunks, LANES), lambda b, i: (b, i, 0)),
                  pl.BlockSpec((1, 1, D), lambda b, i: (b, 0, 0)),
                  pl.BlockSpec((1, D), lambda b, i: (0, 0))],
        out_specs=tok,
        out_shape=jax.ShapeDtypeStruct((B, N, D), F32),
        compiler_params=_params("parallel", "parallel"),
        name="final_norm",
    )(x, delta, gate, g.reshape(1, D))


def _resid_norm_tm_kernel(*refs, delta_tm, h_tm, route):
    refs = list(refs)
    x_ref, d_ref, gate_ref, g_ref, sh_ref, sc_ref = (refs.pop(0) for _ in range(6))
    wr_ref = refs.pop(0) if route else None
    xo_ref, h_ref = refs.pop(0), refs.pop(0)
    aff_ref = refs.pop(0) if route else None
    nb, ts, _ = x_ref.shape
    for bi in range(nb):
        delta = d_ref[:, bi, :] if delta_tm else _from_chunk_rows(d_ref.at[bi], ts)
        x = x_ref[bi] + gate_ref[bi] * delta
        xo_ref[bi] = x
        y = x * lax.rsqrt(jnp.mean(x * x, axis=-1, keepdims=True) + NORM_EPS) * g_ref[...]
        h = y * (1.0 + sc_ref[bi]) + sh_ref[bi]
        if h_tm:
            h_ref[:, bi, :] = h
        else:
            _to_chunk_rows(h_ref.at[bi], h)
        if route:
            logits = lax.dot_general(wr_ref[...], h, (((1,), (1,)), ((), ())),
                                     precision=lax.Precision.HIGHEST, preferred_element_type=F32)
            e = jnp.exp(logits - jnp.max(logits, axis=0, keepdims=True))
            aff_ref[bi] = e / jnp.sum(e, axis=0, keepdims=True)


def _resid_norm_tm(x, delta, gate, g, shift, scale, *, delta_tm, h_tm, router_w=None):
    B, N, D = x.shape
    nb, ts = 8, min(LANES, N)
    chunks = D // LANES
    tok = pl.BlockSpec((nb, ts, D), lambda b, i: (b, i, 0))
    tok_rows = pl.BlockSpec((nb, ts * chunks, LANES), lambda b, i: (b, i, 0))
    tok_tm = pl.BlockSpec((ts, nb, D), lambda b, i: (i, b, 0))
    per_b = pl.BlockSpec((nb, 1, D), lambda b, i: (b, 0, 0))
    in_specs = [tok, tok_tm if delta_tm else tok_rows, per_b, pl.BlockSpec((1, D), lambda b, i: (0, 0)),
                per_b, per_b]
    args = [x, delta, gate, g.reshape(1, D), shift, scale]
    out_specs = [tok, tok_tm if h_tm else tok_rows]
    out_shape = [jax.ShapeDtypeStruct((B, N, D), F32),
                 jax.ShapeDtypeStruct((N, B, D) if h_tm else (B, N * chunks, LANES), F32)]
    route = router_w is not None
    if route:
        E = router_w.shape[1]
        in_specs.append(pl.BlockSpec((E, D), lambda b, i: (0, 0)))
        args.append(router_w.T)
        out_specs.append(pl.BlockSpec((nb, E, ts), lambda b, i: (b, 0, i)))
        out_shape.append(jax.ShapeDtypeStruct((B, E, N), F32))
    return pl.pallas_call(
        functools.partial(_resid_norm_tm_kernel, delta_tm=delta_tm, h_tm=h_tm, route=route),
        grid=(B // nb, N // ts),
        in_specs=in_specs,
        out_specs=out_specs,
        out_shape=out_shape,
        compiler_params=_params("parallel", "parallel"),
        name="resid_norm_tm",
    )(*args)


def _norm_mm_kernel(x_ref, g_ref, sh_ref, sc_ref, w_ref, *rest):
    o_ref = rest[-1]
    x = x_ref[0]
    y = x * lax.rsqrt(jnp.mean(x * x, axis=-1, keepdims=True) + NORM_EPS) * g_ref[...]
    h = (y * (1.0 + sc_ref[0]) + sh_ref[0]).astype(BF16)
    p = jnp.dot(h, w_ref[...], preferred_element_type=F32)
    if len(rest) == 3:
        cos_ref, sin_ref = rest[:2]
        width = cos_ref.shape[1]
        o_ref[0, :, :width] = _rotate(p[:, :width], cos_ref[...], sin_ref[...])
        o_ref[0, :, width:] = p[:, width:]
    else:
        o_ref[0] = p


def _norm_mm(x, g, shift, scale, w, rope=None):
    B, N, D = x.shape
    M = w.shape[1]
    ts = min(512, N)
    per_b = pl.BlockSpec((1, 1, D), lambda b, i: (b, 0, 0))
    in_specs = [pl.BlockSpec((1, ts, D), lambda b, i: (b, i, 0)),
                pl.BlockSpec((1, D), lambda b, i: (0, 0)), per_b, per_b,
                pl.BlockSpec((D, M), lambda b, i: (0, 0))]
    args = [x, g.reshape(1, D), shift, scale, w]
    if rope is not None:
        in_specs += [pl.BlockSpec((ts, rope[0].shape[1]), lambda b, i: (i, 0))] * 2
        args += list(rope)
    return pl.pallas_call(
        _norm_mm_kernel,
        grid=(B, N // ts),
        in_specs=in_specs,
        out_specs=pl.BlockSpec((1, ts, M), lambda b, i: (b, i, 0)),
        out_shape=jax.ShapeDtypeStruct((B, N, M), F32),
        compiler_params=_params("parallel", "parallel"),
        name="norm_mm",
    )(*args)


def _rotate(x, cos, sin):
    out = []
    for c in range(x.shape[1] // LANES):
        sl = slice(c * LANES, (c + 1) * LANES)
        xc = x[:, sl]
        lane = lax.broadcasted_iota(jnp.int32, xc.shape, 1)
        first_half = (lane % AXIS_DIM) < (AXIS_DIM // 2)
        partner = jnp.where(first_half, pltpu.roll(xc, LANES - AXIS_DIM // 2, 1),
                            pltpu.roll(xc, AXIS_DIM // 2, 1))
        out.append(xc * cos[:, sl] + partner * sin[:, sl])
    return jnp.concatenate(out, axis=1)


def _rope_tables(S):
    rows = S // GRID_W
    row = jnp.repeat(jnp.arange(rows), GRID_W).astype(F32)
    col = jnp.tile(jnp.arange(GRID_W), rows).astype(F32)
    inv = ROPE_BASE ** (-jnp.arange(0, AXIS_DIM, 2, dtype=F32) / AXIS_DIM)
    ang_row, ang_col = row[:, None] * inv, col[:, None] * inv

    def axis_tables(ang):
        c, s = jnp.cos(ang), jnp.sin(ang)
        return jnp.concatenate([c, c], -1), jnp.concatenate([-s, s], -1)

    cr, sr = axis_tables(ang_row)
    cc, sc = axis_tables(ang_col)
    cos_h = jnp.concatenate([cr, cc], -1)
    sin_h = jnp.concatenate([sr, sc], -1)
    q_scale = HEAD_DIM ** -0.5
    cos = jnp.concatenate([jnp.tile(cos_h, (1, N_HEADS_ATTN)) * q_scale, jnp.tile(cos_h, (1, N_KV))], -1)
    sin = jnp.concatenate([jnp.tile(sin_h, (1, N_HEADS_ATTN)) * q_scale, jnp.tile(sin_h, (1, N_KV))], -1)
    return cos, sin


def _attn_heads(q, kcat, vcat, valid, sink_ref):
    outs = []
    G = N_HEADS_ATTN // N_KV
    Q = q.shape[0]
    for j in range(N_KV):
        kj = kcat[:, j * HEAD_DIM:(j + 1) * HEAD_DIM]
        vj = vcat[:, j * HEAD_DIM:(j + 1) * HEAD_DIM]
        heads = range(j * G, (j + 1) * G)
        qs = jnp.concatenate([q[:, h * HEAD_DIM:(h + 1) * HEAD_DIM] for h in heads], axis=0).astype(BF16)
        s = lax.dot_general(qs, kj, (((1,), (1,)), ((), ())), preferred_element_type=F32)
        if valid is not None:
            s = jnp.where(valid[None], s.reshape(G, Q, -1), NEG_INF).reshape(G * Q, -1)
        sink = jnp.concatenate([jnp.full((Q, 1), sink_ref[h], F32) for h in heads], axis=0)
        m = jnp.maximum(jnp.max(s, axis=-1, keepdims=True), sink)
        e = jnp.exp(s - m)
        den = jnp.sum(e, axis=-1, keepdims=True) + jnp.exp(sink - m)
        o = jnp.dot(e.astype(BF16), vj, preferred_element_type=F32) / den
        outs.extend(o[g * Q:(g + 1) * Q] for g in range(G))
    return jnp.concatenate(outs, axis=-1)


def _attn_kernel(sink_ref, q_ref, kp_ref, kc_ref, kn_ref, vp_ref, vc_ref, vn_ref, ck_ref, cv_ref,
                 o_ref, *, S, Lc):
    i = pl.program_id(1)
    kcat = jnp.concatenate([ck_ref[0], kp_ref[0], kc_ref[0], kn_ref[0]], axis=0).astype(BF16)
    vcat = jnp.concatenate([cv_ref[0], vp_ref[0], vc_ref[0], vn_ref[0]], axis=0).astype(BF16)
    L = Lc + 3 * QBLOCK
    row = lax.broadcasted_iota(jnp.int32, (QBLOCK, L), 0)
    col = lax.broadcasted_iota(jnp.int32, (QBLOCK, L), 1)
    rel = col - Lc - QBLOCK - row
    kpos = (i - 1) * QBLOCK + col - Lc
    valid = (col < Lc) | ((jnp.abs(rel) <= WINDOW) & (kpos >= 0) & (kpos < S))
    o_ref[0] = _attn_heads(q_ref[0], kcat, vcat, valid, sink_ref)


def _attention(sink, proj_l, proj_c):
    B, S, _ = proj_l.shape
    Lc = proj_c.shape[1]
    nb = S // QBLOCK
    kcol, vcol = ATTN_W // KV_W, ATTN_W // KV_W + 1

    def blk(colblk, off):
        return pl.BlockSpec((1, QBLOCK, KV_W),
                            lambda b, i: (b, jnp.clip(i + off, 0, nb - 1), colblk))

    return pl.pallas_call(
        functools.partial(_attn_kernel, S=S, Lc=Lc),
        grid=(B, nb),
        in_specs=[pl.BlockSpec(memory_space=pltpu.SMEM),
                  pl.BlockSpec((1, QBLOCK, ATTN_W), lambda b, i: (b, i, 0)),
                  blk(kcol, -1), blk(kcol, 0), blk(kcol, 1),
                  blk(vcol, -1), blk(vcol, 0), blk(vcol, 1),
                  pl.BlockSpec((1, Lc, KV_W), lambda b, i: (b, 0, kcol)),
                  pl.BlockSpec((1, Lc, KV_W), lambda b, i: (b, 0, vcol))],
        out_specs=pl.BlockSpec((1, QBLOCK, ATTN_W), lambda b, i: (b, i, 0)),
        out_shape=jax.ShapeDtypeStruct((B, S, ATTN_W), F32),
        compiler_params=_params("parallel", "parallel"),
        name="attention",
    )(sink, proj_l, proj_l, proj_l, proj_l, proj_l, proj_l, proj_l, proj_c, proj_c)


def _ctx_attn_kernel(sink_ref, q_ref, ck_ref, cv_ref, o_ref):
    q = q_ref[0] * (HEAD_DIM ** -0.5)
    o_ref[0] = _attn_heads(q, ck_ref[0].astype(BF16), cv_ref[0].astype(BF16), None, sink_ref)


def _ctx_attention(sink, proj_c):
    B, Lc, _ = proj_c.shape
    kcol, vcol = ATTN_W // KV_W, ATTN_W // KV_W + 1
    return pl.pallas_call(
        _ctx_attn_kernel,
        grid=(B,),
        in_specs=[pl.BlockSpec(memory_space=pltpu.SMEM),
                  pl.BlockSpec((1, Lc, ATTN_W), lambda b: (b, 0, 0)),
                  pl.BlockSpec((1, Lc, KV_W), lambda b: (b, 0, kcol)),
                  pl.BlockSpec((1, Lc, KV_W), lambda b: (b, 0, vcol))],
        out_specs=pl.BlockSpec((1, Lc, ATTN_W), lambda b: (b, 0, 0)),
        out_shape=jax.ShapeDtypeStruct((B, Lc, ATTN_W), F32),
        compiler_params=_params("parallel"),
        name="ctx_attention",
    )(sink, proj_c, proj_c, proj_c)


def _seg_rev_block(j, nC, nL):
    return jnp.where(j < nC, nC - 1 - j, nC + nL - 1 - (j - nC))


def _lru_kernel(uf_ref, ur_ref, wg_ref, bg_ref, sp_ref, hf_ref, hr_ref, a_ref, b_ref, cf_ref, cr_ref, *, Tc):
    @pl.when(pl.program_id(1) == 0)
    def _():
        cf_ref[...] = jnp.zeros_like(cf_ref)
        cr_ref[...] = jnp.zeros_like(cr_ref)

    sub = 8
    row = lax.broadcasted_iota(jnp.int32, (sub, LRU_W), 0)
    for d, (u_ref, h_ref, c_ref) in enumerate(((uf_ref, hf_ref, cf_ref), (ur_ref, hr_ref, cr_ref))):
        u = u_ref[0]
        gates = jax.nn.sigmoid(jnp.dot(u.astype(BF16), wg_ref[d], preferred_element_type=F32) + bg_ref[d])
        a = jnp.exp(-(gates[:, :LRU_W] * sp_ref[d]))
        a_ref[...] = a
        b_ref[...] = jnp.sqrt(jnp.maximum(1.0 - a * a, 0.0)) * (gates[:, LRU_W:] * u)

        def tile(ti, carry, d=d, h_ref=h_ref):
            t0 = pl.multiple_of((ti if d == 0 else Tc // sub - 1 - ti) * sub, sub)
            av = a_ref[pl.ds(t0, sub), :]
            bv = b_ref[pl.ds(t0, sub), :]
            for s in (1, 2, 4):
                shift, known = (s, row >= s) if d == 0 else (sub - s, row < sub - s)
                a_prev = jnp.where(known, pltpu.roll(av, shift, 0), 1.0)
                b_prev = jnp.where(known, pltpu.roll(bv, shift, 0), 0.0)
                bv = bv + av * b_prev
                av = av * a_prev
            h = av * carry + bv
            h_ref[0, pl.ds(t0, sub), :] = h
            return h[sub - 1:sub] if d == 0 else h[0:1]

        c_ref[...] = lax.fori_loop(0, Tc // sub, tile, c_ref[...])


def _lru(u, Lc, wa, ba, wi, bi, lam):
    B, T, _ = u.shape
    S = T - Lc
    Tc = math.gcd(math.gcd(Lc, S), 256)
    nC, nL = Lc // Tc, S // Tc
    w_gates = jnp.stack([jnp.concatenate([_block_diag(wa[d]), _block_diag(wi[d])], axis=1)
                         for d in range(2)]).astype(BF16)
    b_gates = jnp.stack([jnp.concatenate([ba[d], bi[d]]) for d in range(2)])[:, None, :]
    decay_rate = (LRU_C * jax.nn.softplus(-lam))[:, None, :]
    fwd = pl.BlockSpec((1, Tc, LRU_W), lambda bi_, j: (bi_, j, 0))
    rev = pl.BlockSpec((1, Tc, LRU_W), lambda bi_, j: (bi_, _seg_rev_block(j, nC, nL), 0))

    def const(shape):
        return pl.BlockSpec(shape, lambda bi_, j: (0,) * len(shape))

    return pl.pallas_call(
        functools.partial(_lru_kernel, Tc=Tc),
        grid=(B, nC + nL),
        in_specs=[fwd, rev, const(w_gates.shape), const(b_gates.shape), const(decay_rate.shape)],
        out_specs=[fwd, rev],
        out_shape=[jax.ShapeDtypeStruct((B, T, LRU_W), F32)] * 2,
        scratch_shapes=[pltpu.VMEM((Tc, LRU_W), F32), pltpu.VMEM((Tc, LRU_W), F32),
                        pltpu.VMEM((1, LRU_W), F32), pltpu.VMEM((1, LRU_W), F32)],
        compiler_params=_params("parallel", "arbitrary"),
        name="lru",
    )(u, u, w_gates, b_gates, decay_rate)


def _mix_out_kernel(attn_ref, hf_ref, hr_ref, g0_ref, g1_ref, wo_ref, x_ref, gate_ref, g_ref, sh_ref, sc_ref,
                    wr_ref, xo_ref, h_ref, aff_ref):
    tm = x_ref.shape[1]
    chunks = x_ref.shape[2] // LANES
    part = min(tm, MIX_OUT_ROWS)
    for p in range(tm // part):
        rows = slice(p * part, (p + 1) * part)
        gate = jnp.concatenate([g0_ref[0, rows], g1_ref[0, rows]], axis=-1)
        rec = ((hf_ref[0, rows] + hr_ref[0, rows]) * jax.nn.gelu(gate)).astype(BF16)
        y = (jnp.dot(attn_ref[0, rows].astype(BF16), wo_ref[:ATTN_W, :], preferred_element_type=F32)
             + jnp.dot(rec, wo_ref[ATTN_W:, :], preferred_element_type=F32))
        x = x_ref[0, rows] + gate_ref[0] * y
        xo_ref[0, rows] = x
        xn = x * lax.rsqrt(jnp.mean(x * x, axis=-1, keepdims=True) + NORM_EPS) * g_ref[...]
        h = xn * (1.0 + sc_ref[0]) + sh_ref[0]
        _to_chunk_rows(h_ref.at[0, pl.ds(p * part * chunks, part * chunks)], h)
        logits = lax.dot_general(wr_ref[...], h, (((1,), (1,)), ((), ())),
                                 precision=lax.Precision.HIGHEST, preferred_element_type=F32)
        e = jnp.exp(logits - jnp.max(logits, axis=0, keepdims=True))
        aff_ref[0, :, rows] = e / jnp.sum(e, axis=0, keepdims=True)


def _mix_out(attn, hf, hr, proj, w_out, t_off, x, gate, g, shift, scale, router_w):
    B, N, _ = attn.shape
    D = w_out.shape[1]
    E = router_w.shape[1]
    chunks = D // LANES
    tm = math.gcd(math.gcd(N, 256), t_off) if t_off else min(256, N)
    half = LRU_W // 2
    gate_blk = (proj.shape[2] - LRU_W) // half
    tok = lambda b, i: (b, i, 0)
    per_b = pl.BlockSpec((1, 1, D), lambda b, i: (b, 0, 0))
    return pl.pallas_call(
        _mix_out_kernel,
        grid=(B, N // tm),
        in_specs=[pl.BlockSpec((1, tm, ATTN_W), tok),
                  pl.BlockSpec((1, tm, LRU_W), lambda b, i: (b, i + t_off // tm, 0)),
                  pl.BlockSpec((1, tm, LRU_W), lambda b, i: (b, i + t_off // tm, 0)),
                  pl.BlockSpec((1, tm, half), lambda b, i: (b, i, gate_blk)),
                  pl.BlockSpec((1, tm, half), lambda b, i: (b, i, gate_blk + 1)),
                  pl.BlockSpec(w_out.shape, lambda b, i: (0, 0)),
                  pl.BlockSpec((1, tm, D), tok), per_b,
                  pl.BlockSpec((1, D), lambda b, i: (0, 0)), per_b, per_b,
                  pl.BlockSpec((E, D), lambda b, i: (0, 0))],
        out_specs=[pl.BlockSpec((1, tm, D), tok),
                   pl.BlockSpec((1, tm * chunks, LANES), tok),
                   pl.BlockSpec((1, E, tm), lambda b, i: (b, 0, i))],
        out_shape=[jax.ShapeDtypeStruct((B, N, D), F32),
                   jax.ShapeDtypeStruct((B, N * chunks, LANES), F32),
                   jax.ShapeDtypeStruct((B, E, N), F32)],
        compiler_params=_params("parallel", "parallel"),
        name="mix_out",
    )(attn, hf, hr, proj, proj, w_out.astype(BF16), x, gate, g.reshape(1, D), shift, scale, router_w.T)


def _softplus(z):
    return jnp.maximum(z, 0.0) + jnp.log(1.0 + jnp.exp(-jnp.abs(z)))


def _nt_dot(wt, x):
    return lax.dot_general(wt, x, (((1,), (1,)), ((), ())), preferred_element_type=F32)


def _rwkv_prep_kernel(cp_ref, c_ref, cn_ref, lp_ref, l_ref, ln_ref, mu_ref, wr_ref, wk_ref, wv_ref, w1_ref,
                      w2_ref, a1_ref, a2_ref, g1_ref, g2_ref, w0_ref, a0_ref, ka_ref, rk_ref, perm_ref,
                      dec_ref, aa_ref, k_ref, v_ref, r_ref, g_ref, bc_ref, *, B, ctx_tiles, tiles):
    i = pl.program_id(0)
    tm, D = l_ref.shape
    is_ctx = i < ctx_tiles
    h = jnp.where(is_ctx, c_ref[...], l_ref[...])
    seq_start = (i == 0) | (i == ctx_tiles)
    seq_end = (i == ctx_tiles - 1) | (i == tiles - 1)
    hp = jnp.where(seq_start, 0.0, jnp.where(is_ctx, cp_ref[...], lp_ref[...]))
    hn = jnp.where(seq_end, 0.0, jnp.where(is_ctx, cn_ref[...], ln_ref[...]))
    xx = 0.5 * (jnp.concatenate([hp, h[:tm - B]], axis=0) + jnp.concatenate([h[B:], hn], axis=0)) - h

    def mix(j):
        return (h + xx * mu_ref[j:j + 1, :]).astype(BF16)

    hpg = LANES // B
    hd = RWKV_HD

    def to_scan_layout(x, ref):
        for g in range(RWKV_H // hpg):
            for c in range(tm // LANES):
                tiles = [x[(g * hpg + hh) * hd:(g * hpg + hh + 1) * hd, c * LANES:(c + 1) * LANES]
                         for hh in range(hpg)]
                for tl, tile in enumerate(_slab_transpose(tiles, B)):
                    ref[g, c * hpg + tl] = tile

    v = _nt_dot(wv_ref[...], mix(3))
    to_scan_layout(v, v_ref)
    r = _nt_dot(wr_ref[...], mix(0))
    to_scan_layout(r, r_ref)
    k = _nt_dot(wk_ref[...], mix(2))
    to_scan_layout(k, k_ref)
    lw = jnp.tanh(_nt_dot(w1_ref[...], mix(1))).astype(BF16)
    wpre = jnp.dot(w2_ref[...], lw, preferred_element_type=F32)
    la = _nt_dot(a1_ref[...], mix(4)).astype(BF16)
    apre = jnp.dot(a2_ref[...], la, preferred_element_type=F32)
    gg = jax.nn.sigmoid(_nt_dot(g1_ref[...], mix(5))).astype(BF16)
    g_ref[...] = jnp.dot(g2_ref[...], gg, preferred_element_type=F32)

    def to_scan_layout_mxu(xs, refs):
        groups = RWKV_H // hpg
        for c in range(tm // LANES):
            lhs = jnp.concatenate(
                [jnp.concatenate([x[(g * hpg + hh) * hd:(g * hpg + hh + 1) * hd, c * LANES:(c + 1) * LANES]
                                  for hh in range(hpg)], axis=1)
                 for x in xs for g in range(groups)], axis=0)
            hi = lhs.astype(BF16)
            rest = lhs - hi.astype(F32)
            mid = rest.astype(BF16)
            lo = (rest - mid.astype(F32)).astype(BF16)
            out = (jnp.dot(hi, perm_ref[...], preferred_element_type=F32)
                   + jnp.dot(mid, perm_ref[...], preferred_element_type=F32)
                   + jnp.dot(lo, perm_ref[...], preferred_element_type=F32))
            for n, ref in enumerate(refs):
                for g in range(groups):
                    blk = out[(n * groups + g) * hd:(n * groups + g + 1) * hd]
                    for tl in range(hpg):
                        ref[g, c * hpg + tl] = blk[:, tl * LANES:(tl + 1) * LANES]

    iclr, decay = [], []
    for d in range(2):
        w_log = -_softplus(-(w0_ref[d] + wpre[d * D:(d + 1) * D])) - 0.5
        decay.append(jnp.exp(-jnp.exp(w_log)))
        a = jax.nn.sigmoid(a0_ref[d] + apre[d * D:(d + 1) * D])
        iclr.append(a)
        to_scan_layout(a, aa_ref.at[d])
    to_scan_layout_mxu(decay, [dec_ref.at[0], dec_ref.at[1]])
    kd_sum = k * (2.0 + (iclr[0] + iclr[1] - 2.0) * ka_ref[...])
    bc = jnp.sum((r * kd_sum * rk_ref[...]).reshape(RWKV_H, hd, tm), axis=1)
    for g in range(RWKV_H // hpg):
        for c in range(tm // LANES):
            rows = [jnp.broadcast_to(bc[g * hpg + hh:g * hpg + hh + 1, c * LANES:(c + 1) * LANES], (8, LANES))
                    for hh in range(hpg)]
            for tl, tile in enumerate(_slab_transpose(rows, B)):
                bc_ref[g, c * hpg + tl] = tile[0:1]


def _slab_permutation(B):
    n = LANES // B
    src = jnp.arange(n * LANES).reshape(n, n, B).transpose(1, 0, 2).reshape(-1)
    return (jnp.arange(n * LANES)[:, None] == src[None, :]).astype(BF16)


def _slab_transpose(tiles, B):
    n = len(tiles)
    tiles = list(tiles)
    slab = lax.broadcasted_iota(jnp.int32, tiles[0].shape, 1) // B
    s = n // 2
    while s >= 1:
        upper = (slab & s) != 0
        for i in range(n):
            if i & s == 0:
                lo, hi = tiles[i], tiles[i + s]
                tiles[i] = jnp.where(upper, pltpu.roll(hi, s * B, 1), lo)
                tiles[i + s] = jnp.where(upper, hi, pltpu.roll(lo, LANES - s * B, 1))
        s //= 2
    return tiles


def _const_spec(shape):
    nd = len(shape)
    return pl.BlockSpec(shape, lambda i: (0,) * nd, pipeline_mode=pl.Buffered(1))


def _rwkv_prep(hc_tm, hl_tm, B, mu, w_rkv, w0, w1, w2, a0, a1, a2, g1, g2, k_a, r_k):
    D = hl_tm.shape[1]
    Lc, S = hc_tm.shape[0] // B, hl_tm.shape[0] // B
    T = Lc + S
    TB = T * B
    tm = 256
    tiles, ctx_tiles = TB // tm, Lc * B // tm
    hb = tm // B
    G = RWKV_H // (LANES // B)
    hd = RWKV_HD

    def t_bf16(w):
        return w.T.astype(BF16)

    zeros_w = jnp.zeros_like(w2[0].T)
    w2t = jnp.concatenate([jnp.concatenate([w2[0].T, zeros_w], 1),
                           jnp.concatenate([zeros_w, w2[1].T], 1)], 0).astype(BF16)
    zeros_a = jnp.zeros_like(a2[0].T)
    a2t = jnp.concatenate([jnp.concatenate([a2[0].T, zeros_a], 1),
                           jnp.concatenate([zeros_a, a2[1].T], 1)], 0).astype(BF16)
    consts = [mu, t_bf16(w_rkv[0]), t_bf16(w_rkv[1]), t_bf16(w_rkv[2]),
              t_bf16(jnp.concatenate([w1[0], w1[1]], 1)), w2t,
              t_bf16(jnp.concatenate([a1[0], a1[1]], 1)), a2t, t_bf16(g1), t_bf16(g2),
              w0.reshape(2, D, 1), a0.reshape(2, D, 1), k_a.reshape(D, 1), r_k.reshape(D, 1),
              _slab_permutation(B)]
    dir_spec = pl.BlockSpec((2, G, hb, hd, LANES), lambda i: (0, 0, i, 0, 0))
    all_spec = pl.BlockSpec((G, hb, hd, LANES), lambda i: (0, i, 0, 0))
    dir_shape = jax.ShapeDtypeStruct((2, G, T, hd, LANES), F32)
    all_shape = jax.ShapeDtypeStruct((G, T, hd, LANES), F32)

    def stream_specs(first, n_tiles, steps):
        def tile(i):
            return jnp.clip(i - first, 0, n_tiles - 1)
        return [pl.BlockSpec((B, D), lambda i: (jnp.maximum(tile(i) * hb - 1, 0), 0)),
                pl.BlockSpec((tm, D), lambda i: (tile(i), 0)),
                pl.BlockSpec((B, D), lambda i: (jnp.minimum((tile(i) + 1) * hb, steps - 1), 0))]

    return pl.pallas_call(
        functools.partial(_rwkv_prep_kernel, B=B, ctx_tiles=ctx_tiles, tiles=tiles),
        grid=(tiles,),
        in_specs=stream_specs(0, ctx_tiles, Lc) + stream_specs(ctx_tiles, tiles - ctx_tiles, S)
                 + [_const_spec(c.shape) for c in consts],
        out_specs=[dir_spec, dir_spec, all_spec, all_spec, all_spec,
                   pl.BlockSpec((D, tm), lambda i: (0, i)),
                   pl.BlockSpec((G, hb, 1, LANES), lambda i: (0, i, 0, 0))],
        out_shape=[dir_shape, dir_shape, all_shape, all_shape, all_shape,
                   jax.ShapeDtypeStruct((D, TB), F32),
                   jax.ShapeDtypeStruct((G, T, 1, LANES), F32)],
        compiler_params=_params("parallel"),
        name="rwkv_prep",
    )(hc_tm, hc_tm, hc_tm, hl_tm, hl_tm, hl_tm, *consts)


def _wkv_kernel(dec_ref, aa_ref, k_ref, v_ref, r_ref, kkc_ref, kac_ref, y_ref,
                s_ref, g_ref, p_ref, q_ref, sa_ref, *, Tc, nC):
    d = pl.program_id(0)
    j = pl.program_id(2)
    hd = RWKV_HD
    sub = 8

    @pl.when(j == 0)
    def _():
        s_ref[...] = jnp.zeros_like(s_ref)

    def partial_rows(x):
        return jnp.sum(x.reshape(hd // sub, sub, LANES), axis=0)

    pitch = hd + 1

    def put_partial(ref, v, x):
        ref[pl.ds(v, sub, stride=pitch), :] = partial_rows(x)

    def finish_rows(ref):
        acc = ref[pl.ds(0, hd), :]
        for q in range(1, sub):
            acc = acc + ref[pl.ds(q * pitch, hd), :]
        return acc

    def run(with_y):
        g_ref[...] = jnp.ones_like(g_ref)

        def step(s, _):
            tt = jnp.where(d == 0, s, Tc - 1 - s)
            aa = aa_ref[0, 0, tt]
            kk = k_ref[0, tt]
            kf = kk * kkc_ref[0]
            kn = kf * lax.rsqrt(jnp.maximum(jnp.sum(kf * kf, axis=0, keepdims=True), 1e-24))
            g_prev = g_ref[...]
            g = g_prev * dec_ref[0, 0, tt]
            g_inv = 1.0 / g
            g_ref[...] = g
            a_t = -(kn * g_prev)
            b_t = kn * aa * g_inv
            k_t = kk * (1.0 + (aa - 1.0) * kac_ref[0]) * g_inv
            r_t = r_ref[0, tt] * g if with_y else None

            def row_reduce(v, _):
                st = s_ref[v]
                put_partial(p_ref, v, st * a_t)
                if with_y:
                    put_partial(q_ref, v, st * r_t)
                return 0

            lax.fori_loop(0, hd, row_reduce, 0, unroll=8)
            sa = finish_rows(p_ref)
            sa_ref[...] = sa
            if with_y:
                b_r = jnp.sum(b_t * r_t, axis=0, keepdims=True)
                k_r = jnp.sum(k_t * r_t, axis=0, keepdims=True)
                y_ref[0, 0, tt] = finish_rows(q_ref) + sa * b_r + v_ref[0, tt] * k_r

            def row_update(v, _):
                s_ref[v] = s_ref[v] + sa_ref[pl.ds(v, 1), :] * b_t + v_ref[0, tt, pl.ds(v, 1), :] * k_t
                return 0

            lax.fori_loop(0, hd, row_update, 0, unroll=8)
            return 0

        lax.fori_loop(0, Tc, step, 0)

        def rescale(v, _):
            s_ref[v] = s_ref[v] * g_ref[...]
            return 0

        lax.fori_loop(0, hd, rescale, 0, unroll=8)

    @pl.when(j < nC)
    def _():
        run(False)

    @pl.when(j >= nC)
    def _():
        run(True)


def _wkv(dec, aa, k, v, r, k_k, k_a, B, Lc):
    _, G, T, hd, _ = dec.shape
    Tc = WKV_TC
    S = T - Lc
    nC, nL = Lc // Tc, S // Tc

    def tmap(d, j):
        return jnp.where(d == 0, j, _seg_rev_block(j, nC, nL))

    def lmap(d, j):
        jj = jnp.maximum(j - nC, 0)
        return jnp.where(d == 0, jj, nL - 1 - jj)

    dir_spec = pl.BlockSpec((1, 1, Tc, hd, LANES), lambda d, g, j: (d, g, tmap(d, j), 0, 0))
    all_spec = pl.BlockSpec((1, Tc, hd, LANES), lambda d, g, j: (g, tmap(d, j), 0, 0))
    const_spec = pl.BlockSpec((1, hd, LANES), lambda d, g, j: (g, 0, 0))
    return pl.pallas_call(
        functools.partial(_wkv_kernel, Tc=Tc, nC=nC),
        grid=(2, G, nC + nL),
        in_specs=[dir_spec, dir_spec, all_spec, all_spec, all_spec, const_spec, const_spec],
        out_specs=pl.BlockSpec((1, 1, Tc, hd, LANES), lambda d, g, j: (d, g, lmap(d, j), 0, 0)),
        out_shape=jax.ShapeDtypeStruct((2, G, S, hd, LANES), F32),
        scratch_shapes=[pltpu.VMEM((hd, hd, LANES), F32),
                        pltpu.VMEM((hd, LANES), F32),
                        pltpu.VMEM(((hd + 1) * 8, LANES), F32),
                        pltpu.VMEM(((hd + 1) * 8, LANES), F32),
                        pltpu.VMEM((hd, LANES), F32)],
        compiler_params=_params("parallel", "parallel", "arbitrary"),
        name="wkv",
    )(dec, aa, k, v, r, _lane_const(k_k, B), _lane_const(k_a, B))


def _lane_const(c, B):
    hpg = LANES // B
    return jnp.repeat(c.reshape(RWKV_H // hpg, hpg, RWKV_HD).transpose(0, 2, 1), B, axis=-1)


def _rwkv_finish_kernel(y_ref, v_ref, bc_ref, g_ref, gng_ref, gnb_ref, wo_ref, o_ref, *, B):
    G, steps, hd, _ = v_ref.shape
    hpg = LANES // B
    rows = []
    for g in range(G):
        head_cols = [[] for _ in range(hpg)]
        for c in range(steps // hpg):
            tiles = []
            for tl in range(hpg):
                t = c * hpg + tl
                y = y_ref[0, g, t] + y_ref[1, g, t]
                mean = jnp.mean(y, axis=0, keepdims=True)
                var = jnp.mean(jnp.square(y - mean), axis=0, keepdims=True)
                yn = (y - mean) * lax.rsqrt(var + GN_EPS)
                tiles.append(yn * gng_ref[g] + gnb_ref[g] + bc_ref[g, t] * v_ref[g, t])
            for hh, tile in enumerate(_slab_transpose(tiles, B)):
                head_cols[hh].append(tile)
        rows.extend(jnp.concatenate(cols, axis=1) for cols in head_cols)
    o = (jnp.concatenate(rows, axis=0) * g_ref[...]).astype(BF16)
    out_t = jnp.dot(wo_ref[...], o, preferred_element_type=F32)
    o_ref[...] = out_t.T


def _rwkv_finish(y2, v, bc, g, gn_g, gn_b, w_o, B, Lc):
    _, G, S, hd, _ = y2.shape
    D = g.shape[0]
    tm = 256
    steps = tm // B
    off = Lc // steps
    return pl.pallas_call(
        functools.partial(_rwkv_finish_kernel, B=B),
        grid=(S // steps,),
        in_specs=[pl.BlockSpec((2, G, steps, hd, LANES), lambda i: (0, 0, i, 0, 0)),
                  pl.BlockSpec((G, steps, hd, LANES), lambda i: (0, i + off, 0, 0)),
                  pl.BlockSpec((G, steps, 1, LANES), lambda i: (0, i + off, 0, 0)),
                  pl.BlockSpec((D, tm), lambda i: (0, i + off)),
                  _const_spec((G, hd, LANES)), _const_spec((G, hd, LANES)), _const_spec((D, D))],
        out_specs=pl.BlockSpec((tm, D), lambda i: (i, 0)),
        out_shape=jax.ShapeDtypeStruct((S * B, D), F32),
        compiler_params=_params("parallel"),
        name="rwkv_finish",
    )(y2, v, bc, g, _lane_const(gn_g, B), _lane_const(gn_b, B), w_o.T.astype(BF16))


def _rwkv7_mixer(hl, hc, mu, w_rkv, w0, w1, w2, a0, a1, a2, g1, g2, k_k, k_a, r_k, gn_g, gn_b, w_o):
    S, B, D = hl.shape
    Lc = hc.shape[0]
    dec, aa, k, v, r, g, bc = _rwkv_prep(hc.reshape(Lc * B, D), hl.reshape(S * B, D), B, mu, w_rkv, w0, w1,
                                         w2, a0, a1, a2, g1, g2, k_a, r_k.reshape(D))
    y2 = _wkv(dec, aa, k, v, r, k_k, k_a, B, Lc)
    return _rwkv_finish(y2, v, bc, g, gn_g, gn_b, w_o, B, Lc).reshape(S, B, D)


def _moe_kernel(idx_prev_ref, idx_ref, idx_next_ref, gate_ref, h_ref, w1_ref, w3_ref, w2_ref, o_ref,
                xa_ref, xb_ref, ya_ref, yb_ref, *, cap, chunks):
    bb = h_ref.shape[0]
    rows = bb * cap
    e = pl.program_id(1)
    n_e = pl.num_programs(1)
    batch = 8

    def token_rows(t):
        return pl.ds(pl.multiple_of(t * chunks, chunks), chunks)

    def scatter_batch(ids_ref, y_ref, first, keep):
        bi = first // cap
        slots = [first + u for u in range(batch)]
        dst = [token_rows(ids_ref[0, 0, s]) for s in slots]
        vals = [y_ref[token_rows(s), :] for s in slots]
        if keep is not None:
            vals = [jnp.where(keep, v, 0.0) for v in vals]
        sums = [o_ref[bi, dst[u], :] + vals[u] for u in range(batch)]
        for u in range(batch):
            o_ref[bi, dst[u], :] = sums[u]

    @pl.when(e == 0)
    def _():
        o_ref[...] = jnp.zeros_like(o_ref)
        yb_ref[...] = jnp.zeros_like(yb_ref)
        for bi in range(bb):
            def gather(r, _, bi=bi):
                slot = bi * cap + r
                xa_ref[token_rows(slot), :] = h_ref[bi, token_rows(idx_ref[0, 0, slot]), :]
                return 0

            lax.fori_loop(0, cap, gather, 0, unroll=8)

    def stage(x_cur, x_next, y_cur, y_prev):
        x = _from_chunk_rows(x_cur, rows).astype(BF16)
        h1 = jnp.dot(x, w1_ref[0, 0], preferred_element_type=F32)
        h3 = jnp.dot(x, w3_ref[0, 0], preferred_element_type=F32)
        hid = (h1 * jax.nn.sigmoid(h1) * h3).astype(BF16)
        _to_chunk_rows(y_cur, jnp.dot(hid, w2_ref[0, 0], preferred_element_type=F32) * gate_ref[0])
        for slot in range(rows):
            x_next[token_rows(slot), :] = h_ref[slot // cap, token_rows(idx_next_ref[0, 0, slot]), :]
        for first in range(0, rows, batch):
            scatter_batch(idx_prev_ref, y_prev, first, e > 0)

    @pl.when(e % 2 == 0)
    def _():
        stage(xa_ref, xb_ref, ya_ref, yb_ref)

    @pl.when(e % 2 == 1)
    def _():
        stage(xb_ref, xa_ref, yb_ref, ya_ref)

    def drain(y_ref):
        for first in range(0, rows, batch):
            scatter_batch(idx_ref, y_ref, first, None)

    @pl.when((e == n_e - 1) & (e % 2 == 0))
    def _():
        drain(ya_ref)

    @pl.when((e == n_e - 1) & (e % 2 == 1))
    def _():
        drain(yb_ref)


def _moe(h, aff, layer, w1, w3, w2):
    B, _, N = aff.shape
    _, E, D, FF = w1.shape
    chunks = D // LANES
    cap = CAPACITY * N // E
    bb = max(1, min(B, MOE_ROWS // cap))
    gate, idx = lax.top_k(aff, cap)

    def group(t):
        return t.reshape(B // bb, bb, E, cap).transpose(0, 2, 1, 3).reshape(B // bb * E, bb * cap)

    rows = bb * cap
    idx_g = group(idx.astype(jnp.int32))[:, None, :]

    def idx_spec(off):
        return pl.BlockSpec((1, 1, rows), lambda b, e: (b * E + jnp.clip(e + off, 0, E - 1), 0, 0),
                            memory_space=pltpu.SMEM)

    scratch = pltpu.VMEM((rows * chunks, LANES), F32)
    return pl.pallas_call(
        functools.partial(_moe_kernel, cap=cap, chunks=chunks),
        grid=(B // bb, E),
        in_specs=[idx_spec(-1), idx_spec(0), idx_spec(1),
                  pl.BlockSpec((1, rows, 1), lambda b, e: (b * E + e, 0, 0)),
                  pl.BlockSpec((bb, N * chunks, LANES), lambda b, e: (b, 0, 0), pipeline_mode=pl.Buffered(1)),
                  pl.BlockSpec((1, 1, D, FF), lambda b, e: (layer, e, 0, 0)),
                  pl.BlockSpec((1, 1, D, FF), lambda b, e: (layer, e, 0, 0)),
                  pl.BlockSpec((1, 1, FF, D), lambda b, e: (layer, e, 0, 0))],
        out_specs=pl.BlockSpec((bb, N * chunks, LANES), lambda b, e: (b, 0, 0)),
        out_shape=jax.ShapeDtypeStruct((B, N * chunks, LANES), F32),
        scratch_shapes=[scratch] * 4,
        compiler_params=_params("parallel", "arbitrary"),
        name="moe",
    )(idx_g, idx_g, idx_g, group(gate)[:, :, None], h, w1, w3, w2)


def _block_diag(w):
    H, bi, bj = w.shape
    eye = jnp.eye(H, dtype=w.dtype)
    return jnp.einsum('hij,hg->higj', w, eye).reshape(H * bi, H * bj)


def _dwconv(u, w, b):
    n = u.shape[1]
    up = jnp.pad(u, ((0, 0), (CONV_LEFT, CONV_W - 1 - CONV_LEFT), (0, 0)))
    out = up[:, 0:n] * w[0]
    for j in range(1, CONV_W):
        out = out + up[:, j:j + n] * w[j]
    return out + b


def _attn_lru_mixer(proj_l, proj_c, w_out, sink, conv_w, conv_b, wa, ba, wi, bi, lam, tail_l, tail_c):
    Lc = proj_c.shape[1]
    attn_l = _attention(sink, proj_l, proj_c)
    attn_c = _ctx_attention(sink, proj_c)

    u0, g0 = ATTN_W + 2 * KV_W, ATTN_W + 2 * KV_W + LRU_W
    u = jnp.concatenate([_dwconv(proj_c[..., u0:g0], conv_w, conv_b),
                         _dwconv(proj_l[..., u0:g0], conv_w, conv_b)], axis=1)
    hf, hr = _lru(u, Lc, wa, ba, wi, bi, lam)
    return (_mix_out(attn_l, hf, hr, proj_l, w_out, Lc, *tail_l),
            _mix_out(attn_c, hf, hr, proj_c, w_out, 0, *tail_c))


def kernel(x, c, ctx, c_ctx, mod_w, mod_b, norm_mix, norm_ffn, router_w, exp_w1, exp_w3, exp_w2, mix_in, mix_out, attn_sink, lru_conv_w, lru_conv_b, lru_wa, lru_ba, lru_wi, lru_bi, lru_lam, rw_mu, rw_rkv, rw_w0, rw_w1, rw_w2, rw_a0, rw_a1, rw_a2, rw_g1, rw_g2, rw_kk, rw_ka, rw_rk, rw_gn_g, rw_gn_b, rw_wo, final_norm):
    B, S, D = x.shape
    Lc = ctx.shape[1]
    depth = mod_w.shape[0]
    assert depth == 2 and S % QBLOCK == 0
    assert LANES % B == 0 and RWKV_H % (LANES // B) == 0 and B % 8 == 0
    assert Lc % WKV_TC == 0 and S % WKV_TC == 0 and (Lc * B) % 256 == 0

    n_rows = -(-(B + 1) // 8) * 8
    cond = jnp.concatenate([jax.nn.silu(c), jax.nn.silu(c_ctx)[None],
                            jnp.zeros((n_rows - B - 1, D), F32)], axis=0)

    mods_l, mods_c = [], []
    for layer in range(depth):
        mod = _mm(cond, mod_w[layer], tm=n_rows, tn=1024, precise=True) + mod_b[layer]
        mods_l.append([t[:, None, :] for t in jnp.split(mod[:B], 6, axis=-1)])
        mods_c.append([jnp.broadcast_to(t[None], (B, 1, D)) for t in jnp.split(mod[B:B + 1], 6, axis=-1)])

    w1b, w3b, w2b = exp_w1.astype(BF16), exp_w3.astype(BF16), exp_w2.astype(BF16)
    xl, xc = x, ctx
    hl = hc = None
    for layer in range(depth):
        last = layer == depth - 1
        m_l, m_c = mods_l[layer], mods_c[layer]
        if layer % 2 == 0:
            i = layer // 2
            assert layer == 0
            w_in = mix_in[i].astype(BF16)
            proj_l = _norm_mm(xl, norm_mix[layer], m_l[0], m_l[1], w_in, rope=_rope_tables(S))
            proj_c = _norm_mm(xc, norm_mix[layer], m_c[0], m_c[1], w_in)
            tail_l = (xl, m_l[2], norm_ffn[layer], m_l[3], m_l[4], router_w[layer])
            tail_c = (xc, m_c[2], norm_ffn[layer], m_c[3], m_c[4], router_w[layer])
            (xl, hl, aff_l), (xc, hc, aff_c) = _attn_lru_mixer(
                proj_l, proj_c, mix_out[i], attn_sink[i], lru_conv_w[i], lru_conv_b[i], lru_wa[i],
                lru_ba[i], lru_wi[i], lru_bi[i], lru_lam[i], tail_l, tail_c)
        else:
            i = layer // 2
            yl = _rwkv7_mixer(hl, hc, rw_mu[i], rw_rkv[i], rw_w0[i], rw_w1[i], rw_w2[i], rw_a0[i], rw_a1[i],
                              rw_a2[i], rw_g1[i], rw_g2[i], rw_kk[i], rw_ka[i], rw_rk[i], rw_gn_g[i],
                              rw_gn_b[i], rw_wo[i])
            xl, hl, aff_l = _resid_norm_tm(xl, yl, m_l[2], norm_ffn[layer], m_l[3], m_l[4], delta_tm=True,
                                           h_tm=False, router_w=router_w[layer])
        moe_l = _moe(hl, aff_l, layer, w1b, w3b, w2b)
        if last:
            return _final_norm(xl, moe_l, m_l[5], final_norm)
        assert (layer + 1) % 2 == 1
        n_l, n_c = mods_l[layer + 1], mods_c[layer + 1]
        xl, hl = _resid_norm_tm(xl, moe_l, m_l[5], norm_mix[layer + 1], n_l[0], n_l[1], delta_tm=False, h_tm=True)
        moe_c = _moe(hc, aff_c, layer, w1b, w3b, w2b)
        xc, hc = _resid_norm_tm(xc, moe_c, m_c[5], norm_mix[layer + 1], n_c[0], n_c[1], delta_tm=False, h_tm=True)
```

```python
import functools
import math

import jax
import jax.numpy as jnp
from jax import lax
from jax.experimental import pallas as pl
from jax.experimental.pallas import tpu as pltpu

F32 = jnp.float32
BF16 = jnp.bfloat16

GRID_W = 64
N_HEADS_ATTN = 8
N_KV = 2
HEAD_DIM = 64
AXIS_DIM = HEAD_DIM // 2
ATTN_W = N_HEADS_ATTN * HEAD_DIM
KV_W = N_KV * HEAD_DIM
WINDOW = 128
QBLOCK = 128
ROPE_BASE = 10000.0
LRU_W = 512
LRU_BLOCKS = 8
CONV_W = 4
CONV_LEFT = 2
LRU_C = 8.0
RWKV_H = 16
RWKV_HD = 64
GN_EPS = 64e-5
N_EXPERTS = 16
CAPACITY = 2
NORM_EPS = 1e-6
NEG_INF = -1e30

LANES = 128
VMEM_LIMIT_BYTES = 56 * 1024 * 1024
WKV_TC = 32
MOE_ROWS = 256
MIX_OUT_ROWS = 128
def _params(*sem):
    return pltpu.CompilerParams(dimension_semantics=sem, vmem_limit_bytes=VMEM_LIMIT_BYTES)


def _mm_kernel(x_ref, w_ref, o_ref, *, precise):
    if precise:
        o_ref[...] = jnp.dot(x_ref[...], w_ref[...], precision=lax.Precision.HIGHEST,
                             preferred_element_type=F32)
    else:
        o_ref[...] = jnp.dot(x_ref[...].astype(BF16), w_ref[...].astype(BF16),
                             preferred_element_type=F32)


def _mm(x, w, *, tm=512, tn=None, precise=False):
    M, K = x.shape
    N = w.shape[1]
    tm = min(tm, M)
    tn = N if tn is None else min(tn, N)
    assert M % tm == 0 and N % tn == 0, (M, tm, N, tn)
    return pl.pallas_call(
        functools.partial(_mm_kernel, precise=precise),
        grid=(M // tm, N // tn),
        in_specs=[pl.BlockSpec((tm, K), lambda i, j: (i, 0)),
                  pl.BlockSpec((K, tn), lambda i, j: (0, j))],
        out_specs=pl.BlockSpec((tm, tn), lambda i, j: (i, j)),
        out_shape=jax.ShapeDtypeStruct((M, N), F32),
        compiler_params=_params("parallel", "parallel"),
        name="mm",
    )(x, w)


def _final_norm_kernel(x_ref, d_ref, gate_ref, g_ref, o_ref):
    x = x_ref[0] + gate_ref[0] * _from_chunk_rows(d_ref.at[0], x_ref.shape[1])
    o_ref[0] = x * lax.rsqrt(jnp.mean(x * x, axis=-1, keepdims=True) + NORM_EPS) * g_ref[...]


def _from_chunk_rows(ref, n):
    chunks = ref.shape[0] // n
    return jnp.concatenate([ref[pl.ds(j, n, stride=chunks), :] for j in range(chunks)], axis=-1)


def _to_chunk_rows(ref, x):
    n = x.shape[0]
    chunks = x.shape[1] // LANES
    for j in range(chunks):
        ref[pl.ds(j, n, stride=chunks), :] = x[:, j * LANES:(j + 1) * LANES]


def _final_norm(x, delta, gate, g):
    B, N, D = x.shape
    ts = min(512, N)
    chunks = D // LANES
    tok = pl.BlockSpec((1, ts, D), lambda b, i: (b, i, 0))
    return pl.pallas_call(
        _final_norm_kernel,
        grid=(B, N // ts),
        in_specs=[tok, pl.BlockSpec((1, ts * chunks, LANES), lambda b, i: (b, i, 0)),
                  pl.BlockSpec((1, 1, D), lambda b, i: (b, 0, 0)),
                  pl.BlockSpec((1, D), lambda b, i: (0, 0))],
        out_specs=tok,
        out_shape=jax.ShapeDtypeStruct((B, N, D), F32),
        compiler_params=_params("parallel", "parallel"),
        name="final_norm",
    )(x, delta, gate, g.reshape(1, D))


def _resid_norm_tm_kernel(*refs, delta_tm, h_tm, route):
    refs = list(refs)
    x_ref, d_ref, gate_ref, g_ref, sh_ref, sc_ref = (refs.pop(0) for _ in range(6))
    wr_ref = refs.pop(0) if route else None
    xo_ref, h_ref = refs.pop(0), refs.pop(0)
    aff_ref = refs.pop(0) if route else None
    nb, ts, _ = x_ref.shape
    for bi in range(nb):
        delta = d_ref[:, bi, :] if delta_tm else _from_chunk_rows(d_ref.at[bi], ts)
        x = x_ref[bi] + gate_ref[bi] * delta
        xo_ref[bi] = x
        y = x * lax.rsqrt(jnp.mean(x * x, axis=-1, keepdims=True) + NORM_EPS) * g_ref[...]
        h = y * (1.0 + sc_ref[bi]) + sh_ref[bi]
        if h_tm:
            h_ref[:, bi, :] = h
        else:
            _to_chunk_rows(h_ref.at[bi], h)
        if route:
            logits = lax.dot_general(wr_ref[...], h, (((1,), (1,)), ((), ())),
                                     precision=lax.Precision.HIGHEST, preferred_element_type=F32)
            e = jnp.exp(logits - jnp.max(logits, axis=0, keepdims=True))
            aff_ref[bi] = e / jnp.sum(e, axis=0, keepdims=True)


def _resid_norm_tm(x, delta, gate, g, shift, scale, *, delta_tm, h_tm, router_w=None):
    B, N, D = x.shape
    nb, ts = 8, min(LANES, N)
    chunks = D // LANES
    tok = pl.BlockSpec((nb, ts, D), lambda b, i: (b, i, 0))
    tok_rows = pl.BlockSpec((nb, ts * chunks, LANES), lambda b, i: (b, i, 0))
    tok_tm = pl.BlockSpec((ts, nb, D), lambda b, i: (i, b, 0))
    per_b = pl.BlockSpec((nb, 1, D), lambda b, i: (b, 0, 0))
    in_specs = [tok, tok_tm if delta_tm else tok_rows, per_b, pl.BlockSpec((1, D), lambda b, i: (0, 0)),
                per_b, per_b]
    args = [x, delta, gate, g.reshape(1, D), shift, scale]
    out_specs = [tok, tok_tm if h_tm else tok_rows]
    out_shape = [jax.ShapeDtypeStruct((B, N, D), F32),
                 jax.ShapeDtypeStruct((N, B, D) if h_tm else (B, N * chunks, LANES), F32)]
    route = router_w is not None
    if route:
        E = router_w.shape[1]
        in_specs.append(pl.BlockSpec((E, D), lambda b, i: (0, 0)))
        args.append(router_w.T)
        out_specs.append(pl.BlockSpec((nb, E, ts), lambda b, i: (b, 0, i)))
        out_shape.append(jax.ShapeDtypeStruct((B, E, N), F32))
    return pl.pallas_call(
        functools.partial(_resid_norm_tm_kernel, delta_tm=delta_tm, h_tm=h_tm, route=route),
        grid=(B // nb, N // ts),
        in_specs=in_specs,
        out_specs=out_specs,
        out_shape=out_shape,
        compiler_params=_params("parallel", "parallel"),
        name="resid_norm_tm",
    )(*args)


def _norm_mm_kernel(x_ref, g_ref, sh_ref, sc_ref, w_ref, *rest):
    o_ref = rest[-1]
    x = x_ref[0]
    y = x * lax.rsqrt(jnp.mean(x * x, axis=-1, keepdims=True) + NORM_EPS) * g_ref[...]
    h = (y * (1.0 + sc_ref[0]) + sh_ref[0]).astype(BF16)
    p = jnp.dot(h, w_ref[...], preferred_element_type=F32)
    if len(rest) == 3:
        cos_ref, sin_ref = rest[:2]
        width = cos_ref.shape[1]
        o_ref[0, :, :width] = _rotate(p[:, :width], cos_ref[...], sin_ref[...])
        o_ref[0, :, width:] = p[:, width:]
    else:
        o_ref[0] = p


def _norm_mm(x, g, shift, scale, w, rope=None):
    B, N, D = x.shape
    M = w.shape[1]
    ts = min(512, N)
    per_b = pl.BlockSpec((1, 1, D), lambda b, i: (b, 0, 0))
    in_specs = [pl.BlockSpec((1, ts, D), lambda b, i: (b, i, 0)),
                pl.BlockSpec((1, D), lambda b, i: (0, 0)), per_b, per_b,
                pl.BlockSpec((D, M), lambda b, i: (0, 0))]
    args = [x, g.reshape(1, D), shift, scale, w]
    if rope is not None:
        in_specs += [pl.BlockSpec((ts, rope[0].shape[1]), lambda b, i: (i, 0))] * 2
        args += list(rope)
    return pl.pallas_call(
        _norm_mm_kernel,
        grid=(B, N // ts),
        in_specs=in_specs,
        out_specs=pl.BlockSpec((1, ts, M), lambda b, i: (b, i, 0)),
        out_shape=jax.ShapeDtypeStruct((B, N, M), F32),
        compiler_params=_params("parallel", "parallel"),
        name="norm_mm",
    )(*args)


def _rotate(x, cos, sin):
    out = []
    for c in range(x.shape[1] // LANES):
        sl = slice(c * LANES, (c + 1) * LANES)
        xc = x[:, sl]
        lane = lax.broadcasted_iota(jnp.int32, xc.shape, 1)
        first_half = (lane % AXIS_DIM) < (AXIS_DIM // 2)
        partner = jnp.where(first_half, pltpu.roll(xc, LANES - AXIS_DIM // 2, 1),
                            pltpu.roll(xc, AXIS_DIM // 2, 1))
        out.append(xc * cos[:, sl] + partner * sin[:, sl])
    return jnp.concatenate(out, axis=1)


def _rope_tables(S):
    rows = S // GRID_W
    row = jnp.repeat(jnp.arange(rows), GRID_W).astype(F32)
    col = jnp.tile(jnp.arange(GRID_W), rows).astype(F32)
    inv = ROPE_BASE ** (-jnp.arange(0, AXIS_DIM, 2, dtype=F32) / AXIS_DIM)
    ang_row, ang_col = row[:, None] * inv, col[:, None] * inv

    def axis_tables(ang):
        c, s = jnp.cos(ang), jnp.sin(ang)
        return jnp.concatenate([c, c], -1), jnp.concatenate([-s, s], -1)

    cr, sr = axis_tables(ang_row)
    cc, sc = axis_tables(ang_col)
    cos_h = jnp.concatenate([cr, cc], -1)
    sin_h = jnp.concatenate([sr, sc], -1)
    q_scale = HEAD_DIM ** -0.5
    cos = jnp.concatenate([jnp.tile(cos_h, (1, N_HEADS_ATTN)) * q_scale, jnp.tile(cos_h, (1, N_KV))], -1)
    sin = jnp.concatenate([jnp.tile(sin_h, (1, N_HEADS_ATTN)) * q_scale, jnp.tile(sin_h, (1, N_KV))], -1)
    return cos, sin


def _attn_heads(q, kcat, vcat, valid, sink_ref):
    outs = []
    G = N_HEADS_ATTN // N_KV
    Q = q.shape[0]
    for j in range(N_KV):
        kj = kcat[:, j * HEAD_DIM:(j + 1) * HEAD_DIM]
        vj = vcat[:, j * HEAD_DIM:(j + 1) * HEAD_DIM]
        heads = range(j * G, (j + 1) * G)
        qs = jnp.concatenate([q[:, h * HEAD_DIM:(h + 1) * HEAD_DIM] for h in heads], axis=0).astype(BF16)
        s = lax.dot_general(qs, kj, (((1,), (1,)), ((), ())), preferred_element_type=F32)
        if valid is not None:
            s = jnp.where(valid[None], s.reshape(G, Q, -1), NEG_INF).reshape(G * Q, -1)
        sink = jnp.concatenate([jnp.full((Q, 1), sink_ref[h], F32) for h in heads], axis=0)
        m = jnp.maximum(jnp.max(s, axis=-1, keepdims=True), sink)
        e = jnp.exp(s - m)
        den = jnp.sum(e, axis=-1, keepdims=True) + jnp.exp(sink - m)
        o = jnp.dot(e.astype(BF16), vj, preferred_element_type=F32) / den
        outs.extend(o[g * Q:(g + 1) * Q] for g in range(G))
    return jnp.concatenate(outs, axis=-1)


def _attn_kernel(sink_ref, q_ref, kp_ref, kc_ref, kn_ref, vp_ref, vc_ref, vn_ref, ck_ref, cv_ref,
                 o_ref, *, S, Lc):
    i = pl.program_id(1)
    kcat = jnp.concatenate([ck_ref[0], kp_ref[0], kc_ref[0], kn_ref[0]], axis=0).astype(BF16)
    vcat = jnp.concatenate([cv_ref[0], vp_ref[0], vc_ref[0], vn_ref[0]], axis=0).astype(BF16)
    L = Lc + 3 * QBLOCK
    row = lax.broadcasted_iota(jnp.int32, (QBLOCK, L), 0)
    col = lax.broadcasted_iota(jnp.int32, (QBLOCK, L), 1)
    rel = col - Lc - QBLOCK - row
    kpos = (i - 1) * QBLOCK + col - Lc
    valid = (col < Lc) | ((jnp.abs(rel) <= WINDOW) & (kpos >= 0) & (kpos < S))
    o_ref[0] = _attn_heads(q_ref[0], kcat, vcat, valid, sink_ref)


def _attention(sink, proj_l, proj_c):
    B, S, _ = proj_l.shape
    Lc = proj_c.shape[1]
    nb = S // QBLOCK
    kcol, vcol = ATTN_W // KV_W, ATTN_W // KV_W + 1

    def blk(colblk, off):
        return pl.BlockSpec((1, QBLOCK, KV_W),
                            lambda b, i: (b, jnp.clip(i + off, 0, nb - 1), colblk))

    return pl.pallas_call(
        functools.partial(_attn_kernel, S=S, Lc=Lc),
        grid=(B, nb),
        in_specs=[pl.BlockSpec(memory_space=pltpu.SMEM),
                  pl.BlockSpec((1, QBLOCK, ATTN_W), lambda b, i: (b, i, 0)),
                  blk(kcol, -1), blk(kcol, 0), blk(kcol, 1),
                  blk(vcol, -1), blk(vcol, 0), blk(vcol, 1),
                  pl.BlockSpec((1, Lc, KV_W), lambda b, i: (b, 0, kcol)),
                  pl.BlockSpec((1, Lc, KV_W), lambda b, i: (b, 0, vcol))],
        out_specs=pl.BlockSpec((1, QBLOCK, ATTN_W), lambda b, i: (b, i, 0)),
        out_shape=jax.ShapeDtypeStruct((B, S, ATTN_W), F32),
        compiler_params=_params("parallel", "parallel"),
        name="attention",
    )(sink, proj_l, proj_l, proj_l, proj_l, proj_l, proj_l, proj_l, proj_c, proj_c)


def _ctx_attn_kernel(sink_ref, q_ref, ck_ref, cv_ref, o_ref):
    q = q_ref[0] * (HEAD_DIM ** -0.5)
    o_ref[0] = _attn_heads(q, ck_ref[0].astype(BF16), cv_ref[0].astype(BF16), None, sink_ref)


def _ctx_attention(sink, proj_c):
    B, Lc, _ = proj_c.shape
    kcol, vcol = ATTN_W // KV_W, ATTN_W // KV_W + 1
    return pl.pallas_call(
        _ctx_attn_kernel,
        grid=(B,),
        in_specs=[pl.BlockSpec(memory_space=pltpu.SMEM),
                  pl.BlockSpec((1, Lc, ATTN_W), lambda b: (b, 0, 0)),
                  pl.BlockSpec((1, Lc, KV_W), lambda b: (b, 0, kcol)),
                  pl.BlockSpec((1, Lc, KV_W), lambda b: (b, 0, vcol))],
        out_specs=pl.BlockSpec((1, Lc, ATTN_W), lambda b: (b, 0, 0)),
        out_shape=jax.ShapeDtypeStruct((B, Lc, ATTN_W), F32),
        compiler_params=_params("parallel"),
        name="ctx_attention",
    )(sink, proj_c, proj_c, proj_c)


def _seg_rev_block(j, nC, nL):
    return jnp.where(j < nC, nC - 1 - j, nC + nL - 1 - (j - nC))


def _lru_kernel(uf_ref, ur_ref, wg_ref, bg_ref, sp_ref, hf_ref, hr_ref, a_ref, b_ref, cf_ref, cr_ref, *, Tc):
    @pl.when(pl.program_id(1) == 0)
    def _():
        cf_ref[...] = jnp.zeros_like(cf_ref)
        cr_ref[...] = jnp.zeros_like(cr_ref)

    sub = 8
    row = lax.broadcasted_iota(jnp.int32, (sub, LRU_W), 0)
    for d, (u_ref, h_ref, c_ref) in enumerate(((uf_ref, hf_ref, cf_ref), (ur_ref, hr_ref, cr_ref))):
        u = u_ref[0]
        gates = jax.nn.sigmoid(jnp.dot(u.astype(BF16), wg_ref[d], preferred_element_type=F32) + bg_ref[d])
        a = jnp.exp(-(gates[:, :LRU_W] * sp_ref[d]))
        a_ref[...] = a
        b_ref[...] = jnp.sqrt(jnp.maximum(1.0 - a * a, 0.0)) * (gates[:, LRU_W:] * u)

        def tile(ti, carry, d=d, h_ref=h_ref):
            t0 = pl.multiple_of((ti if d == 0 else Tc // sub - 1 - ti) * sub, sub)
            av = a_ref[pl.ds(t0, sub), :]
            bv = b_ref[pl.ds(t0, sub), :]
            for s in (1, 2, 4):
                shift, known = (s, row >= s) if d == 0 else (sub - s, row < sub - s)
                a_prev = jnp.where(known, pltpu.roll(av, shift, 0), 1.0)
                b_prev = jnp.where(known, pltpu.roll(bv, shift, 0), 0.0)
                bv = bv + av * b_prev
                av = av * a_prev
            h = av * carry + bv
            h_ref[0, pl.ds(t0, sub), :] = h
            return h[sub - 1:sub] if d == 0 else h[0:1]

        c_ref[...] = lax.fori_loop(0, Tc // sub, tile, c_ref[...], unroll=4)


def _lru(u, Lc, wa, ba, wi, bi, lam):
    B, T, _ = u.shape
    S = T - Lc
    Tc = math.gcd(math.gcd(Lc, S), 256)
    nC, nL = Lc // Tc, S // Tc
    w_gates = jnp.stack([jnp.concatenate([_block_diag(wa[d]), _block_diag(wi[d])], axis=1)
                         for d in range(2)]).astype(BF16)
    b_gates = jnp.stack([jnp.concatenate([ba[d], bi[d]]) for d in range(2)])[:, None, :]
    decay_rate = (LRU_C * jax.nn.softplus(-lam))[:, None, :]
    fwd = pl.BlockSpec((1, Tc, LRU_W), lambda bi_, j: (bi_, j, 0))
    rev = pl.BlockSpec((1, Tc, LRU_W), lambda bi_, j: (bi_, _seg_rev_block(j, nC, nL), 0))

    def const(shape):
        return pl.BlockSpec(shape, lambda bi_, j: (0,) * len(shape))

    return pl.pallas_call(
        functools.partial(_lru_kernel, Tc=Tc),
        grid=(B, nC + nL),
        in_specs=[fwd, rev, const(w_gates.shape), const(b_gates.shape), const(decay_rate.shape)],
        out_specs=[fwd, rev],
        out_shape=[jax.ShapeDtypeStruct((B, T, LRU_W), F32)] * 2,
        scratch_shapes=[pltpu.VMEM((Tc, LRU_W), F32), pltpu.VMEM((Tc, LRU_W), F32),
                        pltpu.VMEM((1, LRU_W), F32), pltpu.VMEM((1, LRU_W), F32)],
        compiler_params=_params("parallel", "arbitrary"),
        name="lru",
    )(u, u, w_gates, b_gates, decay_rate)


def _mix_out_kernel(attn_ref, hf_ref, hr_ref, g0_ref, g1_ref, wo_ref, x_ref, gate_ref, g_ref, sh_ref, sc_ref,
                    wr_ref, xo_ref, h_ref, aff_ref):
    tm = x_ref.shape[1]
    chunks = x_ref.shape[2] // LANES
    part = min(tm, MIX_OUT_ROWS)
    for p in range(tm // part):
        rows = slice(p * part, (p + 1) * part)
        gate = jnp.concatenate([g0_ref[0, rows], g1_ref[0, rows]], axis=-1)
        rec = ((hf_ref[0, rows] + hr_ref[0, rows]) * jax.nn.gelu(gate)).astype(BF16)
        y = (jnp.dot(attn_ref[0, rows].astype(BF16), wo_ref[:ATTN_W, :], preferred_element_type=F32)
             + jnp.dot(rec, wo_ref[ATTN_W:, :], preferred_element_type=F32))
        x = x_ref[0, rows] + gate_ref[0] * y
        xo_ref[0, rows] = x
        xn = x * lax.rsqrt(jnp.mean(x * x, axis=-1, keepdims=True) + NORM_EPS) * g_ref[...]
        h = xn * (1.0 + sc_ref[0]) + sh_ref[0]
        _to_chunk_rows(h_ref.at[0, pl.ds(p * part * chunks, part * chunks)], h)
        logits = lax.dot_general(wr_ref[...], h, (((1,), (1,)), ((), ())),
                                 precision=lax.Precision.HIGHEST, preferred_element_type=F32)
        e = jnp.exp(logits - jnp.max(logits, axis=0, keepdims=True))
        aff_ref[0, :, rows] = e / jnp.sum(e, axis=0, keepdims=True)


def _mix_out(attn, hf, hr, proj, w_out, t_off, x, gate, g, shift, scale, router_w):
    B, N, _ = attn.shape
    D = w_out.shape[1]
    E = router_w.shape[1]
    chunks = D // LANES
    tm = math.gcd(math.gcd(N, 256), t_off) if t_off else min(256, N)
    half = LRU_W // 2
    gate_blk = (proj.shape[2] - LRU_W) // half
    tok = lambda b, i: (b, i, 0)
    per_b = pl.BlockSpec((1, 1, D), lambda b, i: (b, 0, 0))
    return pl.pallas_call(
        _mix_out_kernel,
        grid=(B, N // tm),
        in_specs=[pl.BlockSpec((1, tm, ATTN_W), tok),
                  pl.BlockSpec((1, tm, LRU_W), lambda b, i: (b, i + t_off // tm, 0)),
                  pl.BlockSpec((1, tm, LRU_W), lambda b, i: (b, i + t_off // tm, 0)),
                  pl.BlockSpec((1, tm, half), lambda b, i: (b, i, gate_blk)),
                  pl.BlockSpec((1, tm, half), lambda b, i: (b, i, gate_blk + 1)),
                  pl.BlockSpec(w_out.shape, lambda b, i: (0, 0)),
                  pl.BlockSpec((1, tm, D), tok), per_b,
                  pl.BlockSpec((1, D), lambda b, i: (0, 0)), per_b, per_b,
                  pl.BlockSpec((E, D), lambda b, i: (0, 0))],
        out_specs=[pl.BlockSpec((1, tm, D), tok),
                   pl.BlockSpec((1, tm * chunks, LANES), tok),
                   pl.BlockSpec((1, E, tm), lambda b, i: (b, 0, i))],
        out_shape=[jax.ShapeDtypeStruct((B, N, D), F32),
                   jax.ShapeDtypeStruct((B, N * chunks, LANES), F32),
                   jax.ShapeDtypeStruct((B, E, N), F32)],
        compiler_params=_params("parallel", "parallel"),
        name="mix_out",
    )(attn, hf, hr, proj, proj, w_out.astype(BF16), x, gate, g.reshape(1, D), shift, scale, router_w.T)


def _softplus(z):
    return jnp.maximum(z, 0.0) + jnp.log(1.0 + jnp.exp(-jnp.abs(z)))


def _nt_dot(wt, x):
    return lax.dot_general(wt, x, (((1,), (1,)), ((), ())), preferred_element_type=F32)


def _rwkv_prep_kernel(cp_ref, c_ref, cn_ref, lp_ref, l_ref, ln_ref, mu_ref, wr_ref, wk_ref, wv_ref, w1_ref,
                      w2_ref, a1_ref, a2_ref, g1_ref, g2_ref, w0_ref, a0_ref, ka_ref, rk_ref, perm_ref,
                      dec_ref, aa_ref, k_ref, v_ref, r_ref, g_ref, bc_ref, *, B, ctx_tiles, tiles):
    i = pl.program_id(0)
    tm, D = l_ref.shape
    is_ctx = i < ctx_tiles
    h = jnp.where(is_ctx, c_ref[...], l_ref[...])
    seq_start = (i == 0) | (i == ctx_tiles)
    seq_end = (i == ctx_tiles - 1) | (i == tiles - 1)
    hp = jnp.where(seq_start, 0.0, jnp.where(is_ctx, cp_ref[...], lp_ref[...]))
    hn = jnp.where(seq_end, 0.0, jnp.where(is_ctx, cn_ref[...], ln_ref[...]))
    xx = 0.5 * (jnp.concatenate([hp, h[:tm - B]], axis=0) + jnp.concatenate([h[B:], hn], axis=0)) - h

    def mix(j):
        return (h + xx * mu_ref[j:j + 1, :]).astype(BF16)

    hpg = LANES // B
    hd = RWKV_HD

    def to_scan_layout(x, ref):
        for g in range(RWKV_H // hpg):
            for c in range(tm // LANES):
                tiles = [x[(g * hpg + hh) * hd:(g * hpg + hh + 1) * hd, c * LANES:(c + 1) * LANES]
                         for hh in range(hpg)]
                for tl, tile in enumerate(_slab_transpose(tiles, B)):
                    ref[g, c * hpg + tl] = tile

    v = _nt_dot(wv_ref[...], mix(3))
    to_scan_layout(v, v_ref)
    r = _nt_dot(wr_ref[...], mix(0))
    to_scan_layout(r, r_ref)
    k = _nt_dot(wk_ref[...], mix(2))
    to_scan_layout(k, k_ref)
    lw = jnp.tanh(_nt_dot(w1_ref[...], mix(1))).astype(BF16)
    wpre = jnp.dot(w2_ref[...], lw, preferred_element_type=F32)
    la = _nt_dot(a1_ref[...], mix(4)).astype(BF16)
    apre = jnp.dot(a2_ref[...], la, preferred_element_type=F32)
    gg = jax.nn.sigmoid(_nt_dot(g1_ref[...], mix(5))).astype(BF16)
    g_ref[...] = jnp.dot(g2_ref[...], gg, preferred_element_type=F32)

    def to_scan_layout_mxu(xs, refs):
        groups = RWKV_H // hpg
        for c in range(tm // LANES):
            lhs = jnp.concatenate(
                [jnp.concatenate([x[(g * hpg + hh) * hd:(g * hpg + hh + 1) * hd, c * LANES:(c + 1) * LANES]
                                  for hh in range(hpg)], axis=1)
                 for x in xs for g in range(groups)], axis=0)
            hi = lhs.astype(BF16)
            rest = lhs - hi.astype(F32)
            mid = rest.astype(BF16)
            lo = (rest - mid.astype(F32)).astype(BF16)
            out = (jnp.dot(hi, perm_ref[...], preferred_element_type=F32)
                   + jnp.dot(mid, perm_ref[...], preferred_element_type=F32)
                   + jnp.dot(lo, perm_ref[...], preferred_element_type=F32))
            for n, ref in enumerate(refs):
                for g in range(groups):
                    blk = out[(n * groups + g) * hd:(n * groups + g + 1) * hd]
                    for tl in range(hpg):
                        ref[g, c * hpg + tl] = blk[:, tl * LANES:(tl + 1) * LANES]

    iclr, decay = [], []
    for d in range(2):
        w_log = -_softplus(-(w0_ref[d] + wpre[d * D:(d + 1) * D])) - 0.5
        decay.append(jnp.exp(-jnp.exp(w_log)))
        a = jax.nn.sigmoid(a0_ref[d] + apre[d * D:(d + 1) * D])
        iclr.append(a)
        to_scan_layout(a, aa_ref.at[d])
    to_scan_layout_mxu(decay, [dec_ref.at[0], dec_ref.at[1]])
    kd_sum = k * (2.0 + (iclr[0] + iclr[1] - 2.0) * ka_ref[...])
    bc = jnp.sum((r * kd_sum * rk_ref[...]).reshape(RWKV_H, hd, tm), axis=1)
    for g in range(RWKV_H // hpg):
        for c in range(tm // LANES):
            rows = [jnp.broadcast_to(bc[g * hpg + hh:g * hpg + hh + 1, c * LANES:(c + 1) * LANES], (8, LANES))
                    for hh in range(hpg)]
            for tl, tile in enumerate(_slab_transpose(rows, B)):
                bc_ref[g, c * hpg + tl] = tile[0:1]


def _slab_permutation(B):
    n = LANES // B
    src = jnp.arange(n * LANES).reshape(n, n, B).transpose(1, 0, 2).reshape(-1)
    return (jnp.arange(n * LANES)[:, None] == src[None, :]).astype(BF16)


def _slab_transpose(tiles, B):
    n = len(tiles)
    tiles = list(tiles)
    slab = lax.broadcasted_iota(jnp.int32, tiles[0].shape, 1) // B
    s = n // 2
    while s >= 1:
        upper = (slab & s) != 0
        for i in range(n):
            if i & s == 0:
                lo, hi = tiles[i], tiles[i + s]
                tiles[i] = jnp.where(upper, pltpu.roll(hi, s * B, 1), lo)
                tiles[i + s] = jnp.where(upper, hi, pltpu.roll(lo, LANES - s * B, 1))
        s //= 2
    return tiles


def _const_spec(shape):
    nd = len(shape)
    return pl.BlockSpec(shape, lambda i: (0,) * nd, pipeline_mode=pl.Buffered(1))


def _rwkv_prep(hc_tm, hl_tm, B, mu, w_rkv, w0, w1, w2, a0, a1, a2, g1, g2, k_a, r_k):
    D = hl_tm.shape[1]
    Lc, S = hc_tm.shape[0] // B, hl_tm.shape[0] // B
    T = Lc + S
    TB = T * B
    tm = 256
    tiles, ctx_tiles = TB // tm, Lc * B // tm
    hb = tm // B
    G = RWKV_H // (LANES // B)
    hd = RWKV_HD

    def t_bf16(w):
        return w.T.astype(BF16)

    zeros_w = jnp.zeros_like(w2[0].T)
    w2t = jnp.concatenate([jnp.concatenate([w2[0].T, zeros_w], 1),
                           jnp.concatenate([zeros_w, w2[1].T], 1)], 0).astype(BF16)
    zeros_a = jnp.zeros_like(a2[0].T)
    a2t = jnp.concatenate([jnp.concatenate([a2[0].T, zeros_a], 1),
                           jnp.concatenate([zeros_a, a2[1].T], 1)], 0).astype(BF16)
    consts = [mu, t_bf16(w_rkv[0]), t_bf16(w_rkv[1]), t_bf16(w_rkv[2]),
              t_bf16(jnp.concatenate([w1[0], w1[1]], 1)), w2t,
              t_bf16(jnp.concatenate([a1[0], a1[1]], 1)), a2t, t_bf16(g1), t_bf16(g2),
              w0.reshape(2, D, 1), a0.reshape(2, D, 1), k_a.reshape(D, 1), r_k.reshape(D, 1),
              _slab_permutation(B)]
    dir_spec = pl.BlockSpec((2, G, hb, hd, LANES), lambda i: (0, 0, i, 0, 0))
    all_spec = pl.BlockSpec((G, hb, hd, LANES), lambda i: (0, i, 0, 0))
    dir_shape = jax.ShapeDtypeStruct((2, G, T, hd, LANES), F32)
    all_shape = jax.ShapeDtypeStruct((G, T, hd, LANES), F32)

    def stream_specs(first, n_tiles, steps):
        def tile(i):
            return jnp.clip(i - first, 0, n_tiles - 1)
        return [pl.BlockSpec((B, D), lambda i: (jnp.maximum(tile(i) * hb - 1, 0), 0)),
                pl.BlockSpec((tm, D), lambda i: (tile(i), 0)),
                pl.BlockSpec((B, D), lambda i: (jnp.minimum((tile(i) + 1) * hb, steps - 1), 0))]

    return pl.pallas_call(
        functools.partial(_rwkv_prep_kernel, B=B, ctx_tiles=ctx_tiles, tiles=tiles),
        grid=(tiles,),
        in_specs=stream_specs(0, ctx_tiles, Lc) + stream_specs(ctx_tiles, tiles - ctx_tiles, S)
                 + [_const_spec(c.shape) for c in consts],
        out_specs=[dir_spec, dir_spec, all_spec, all_spec, all_spec,
                   pl.BlockSpec((D, tm), lambda i: (0, i)),
                   pl.BlockSpec((G, hb, 1, LANES), lambda i: (0, i, 0, 0))],
        out_shape=[dir_shape, dir_shape, all_shape, all_shape, all_shape,
                   jax.ShapeDtypeStruct((D, TB), F32),
                   jax.ShapeDtypeStruct((G, T, 1, LANES), F32)],
        compiler_params=_params("parallel"),
        name="rwkv_prep",
    )(hc_tm, hc_tm, hc_tm, hl_tm, hl_tm, hl_tm, *consts)


def _wkv_kernel(dec_ref, aa_ref, k_ref, v_ref, r_ref, kkc_ref, kac_ref, y_ref,
                s_ref, g_ref, p_ref, q_ref, sa_ref, *, Tc, nC):
    d = pl.program_id(0)
    j = pl.program_id(2)
    hd = RWKV_HD
    sub = 8

    @pl.when(j == 0)
    def _():
        s_ref[...] = jnp.zeros_like(s_ref)

    def partial_rows(x):
        return jnp.sum(x.reshape(hd // sub, sub, LANES), axis=0)

    pitch = hd + 1

    def put_partial(ref, v, x):
        ref[pl.ds(v, sub, stride=pitch), :] = partial_rows(x)

    def finish_rows(ref):
        acc = ref[pl.ds(0, hd), :]
        for q in range(1, sub):
            acc = acc + ref[pl.ds(q * pitch, hd), :]
        return acc

    def run(with_y):
        g_ref[...] = jnp.ones_like(g_ref)

        def step(s, _):
            tt = jnp.where(d == 0, s, Tc - 1 - s)
            aa = aa_ref[0, 0, tt]
            kk = k_ref[0, tt]
            kf = kk * kkc_ref[0]
            kn = kf * lax.rsqrt(jnp.maximum(jnp.sum(kf * kf, axis=0, keepdims=True), 1e-24))
            g_prev = g_ref[...]
            g = g_prev * dec_ref[0, 0, tt]
            g_inv = 1.0 / g
            g_ref[...] = g
            a_t = -(kn * g_prev)
            b_t = kn * aa * g_inv
            k_t = kk * (1.0 + (aa - 1.0) * kac_ref[0]) * g_inv
            r_t = r_ref[0, tt] * g if with_y else None

            def row_reduce(v, _):
                st = s_ref[v]
                put_partial(p_ref, v, st * a_t)
                if with_y:
                    put_partial(q_ref, v, st * r_t)
                return 0

            lax.fori_loop(0, hd, row_reduce, 0, unroll=64)
            sa = finish_rows(p_ref)
            sa_ref[...] = sa
            if with_y:
                b_r = jnp.sum(b_t * r_t, axis=0, keepdims=True)
                k_r = jnp.sum(k_t * r_t, axis=0, keepdims=True)
                y_ref[0, 0, tt] = finish_rows(q_ref) + sa * b_r + v_ref[0, tt] * k_r

            def row_update(v, _):
                s_ref[v] = s_ref[v] + sa_ref[pl.ds(v, 1), :] * b_t + v_ref[0, tt, pl.ds(v, 1), :] * k_t
                return 0

            lax.fori_loop(0, hd, row_update, 0, unroll=64)
            return 0

        lax.fori_loop(0, Tc, step, 0)

        def rescale(v, _):
            s_ref[v] = s_ref[v] * g_ref[...]
            return 0

        lax.fori_loop(0, hd, rescale, 0, unroll=8)

    @pl.when(j < nC)
    def _():
        run(False)

    @pl.when(j >= nC)
    def _():
        run(True)


def _wkv(dec, aa, k, v, r, k_k, k_a, B, Lc):
    _, G, T, hd, _ = dec.shape
    Tc = WKV_TC
    S = T - Lc
    nC, nL = Lc // Tc, S // Tc

    def tmap(d, j):
        return jnp.where(d == 0, j, _seg_rev_block(j, nC, nL))

    def lmap(d, j):
        jj = jnp.maximum(j - nC, 0)
        return jnp.where(d == 0, jj, nL - 1 - jj)

    dir_spec = pl.BlockSpec((1, 1, Tc, hd, LANES), lambda d, g, j: (d, g, tmap(d, j), 0, 0))
    all_spec = pl.BlockSpec((1, Tc, hd, LANES), lambda d, g, j: (g, tmap(d, j), 0, 0))
    const_spec = pl.BlockSpec((1, hd, LANES), lambda d, g, j: (g, 0, 0))
    return pl.pallas_call(
        functools.partial(_wkv_kernel, Tc=Tc, nC=nC),
        grid=(2, G, nC + nL),
        in_specs=[dir_spec, dir_spec, all_spec, all_spec, all_spec, const_spec, const_spec],
        out_specs=pl.BlockSpec((1, 1, Tc, hd, LANES), lambda d, g, j: (d, g, lmap(d, j), 0, 0)),
        out_shape=jax.ShapeDtypeStruct((2, G, S, hd, LANES), F32),
        scratch_shapes=[pltpu.VMEM((hd, hd, LANES), F32),
                        pltpu.VMEM((hd, LANES), F32),
                        pltpu.VMEM(((hd + 1) * 8, LANES), F32),
                        pltpu.VMEM(((hd + 1) * 8, LANES), F32),
                        pltpu.VMEM((hd, LANES), F32)],
        compiler_params=_params("parallel", "parallel", "arbitrary"),
        name="wkv",
    )(dec, aa, k, v, r, _lane_const(k_k, B), _lane_const(k_a, B))


def _lane_const(c, B):
    hpg = LANES // B
    return jnp.repeat(c.reshape(RWKV_H // hpg, hpg, RWKV_HD).transpose(0, 2, 1), B, axis=-1)


def _rwkv_finish_kernel(y_ref, v_ref, bc_ref, g_ref, gng_ref, gnb_ref, wo_ref, o_ref, *, B):
    G, steps, hd, _ = v_ref.shape
    hpg = LANES // B
    rows = []
    for g in range(G):
        head_cols = [[] for _ in range(hpg)]
        for c in range(steps // hpg):
            tiles = []
            for tl in range(hpg):
                t = c * hpg + tl
                y = y_ref[0, g, t] + y_ref[1, g, t]
                mean = jnp.mean(y, axis=0, keepdims=True)
                var = jnp.mean(jnp.square(y - mean), axis=0, keepdims=True)
                yn = (y - mean) * lax.rsqrt(var + GN_EPS)
                tiles.append(yn * gng_ref[g] + gnb_ref[g] + bc_ref[g, t] * v_ref[g, t])
            for hh, tile in enumerate(_slab_transpose(tiles, B)):
                head_cols[hh].append(tile)
        rows.extend(jnp.concatenate(cols, axis=1) for cols in head_cols)
    o = (jnp.concatenate(rows, axis=0) * g_ref[...]).astype(BF16)
    out_t = jnp.dot(wo_ref[...], o, preferred_element_type=F32)
    o_ref[...] = out_t.T


def _rwkv_finish(y2, v, bc, g, gn_g, gn_b, w_o, B, Lc):
    _, G, S, hd, _ = y2.shape
    D = g.shape[0]
    tm = 256
    steps = tm // B
    off = Lc // steps
    return pl.pallas_call(
        functools.partial(_rwkv_finish_kernel, B=B),
        grid=(S // steps,),
        in_specs=[pl.BlockSpec((2, G, steps, hd, LANES), lambda i: (0, 0, i, 0, 0)),
                  pl.BlockSpec((G, steps, hd, LANES), lambda i: (0, i + off, 0, 0)),
                  pl.BlockSpec((G, steps, 1, LANES), lambda i: (0, i + off, 0, 0)),
                  pl.BlockSpec((D, tm), lambda i: (0, i + off)),
                  _const_spec((G, hd, LANES)), _const_spec((G, hd, LANES)), _const_spec((D, D))],
        out_specs=pl.BlockSpec((tm, D), lambda i: (i, 0)),
        out_shape=jax.ShapeDtypeStruct((S * B, D), F32),
        compiler_params=_params("parallel"),
        name="rwkv_finish",
    )(y2, v, bc, g, _lane_const(gn_g, B), _lane_const(gn_b, B), w_o.T.astype(BF16))


def _rwkv7_mixer(hl, hc, mu, w_rkv, w0, w1, w2, a0, a1, a2, g1, g2, k_k, k_a, r_k, gn_g, gn_b, w_o):
    S, B, D = hl.shape
    Lc = hc.shape[0]
    dec, aa, k, v, r, g, bc = _rwkv_prep(hc.reshape(Lc * B, D), hl.reshape(S * B, D), B, mu, w_rkv, w0, w1,
                                         w2, a0, a1, a2, g1, g2, k_a, r_k.reshape(D))
    y2 = _wkv(dec, aa, k, v, r, k_k, k_a, B, Lc)
    return _rwkv_finish(y2, v, bc, g, gn_g, gn_b, w_o, B, Lc).reshape(S, B, D)


def _moe_kernel(idx_prev_ref, idx_ref, idx_next_ref, gate_ref, h_ref, w1_ref, w3_ref, w2_ref, o_ref,
                xa_ref, xb_ref, ya_ref, yb_ref, *, cap, chunks):
    bb = h_ref.shape[0]
    rows = bb * cap
    e = pl.program_id(1)
    n_e = pl.num_programs(1)
    batch = 8

    def token_rows(t):
        return pl.ds(pl.multiple_of(t * chunks, chunks), chunks)

    def scatter_batch(ids_ref, y_ref, first, keep):
        bi = first // cap
        slots = [first + u for u in range(batch)]
        dst = [token_rows(ids_ref[0, 0, s]) for s in slots]
        vals = [y_ref[token_rows(s), :] for s in slots]
        if keep is not None:
            vals = [jnp.where(keep, v, 0.0) for v in vals]
        sums = [o_ref[bi, dst[u], :] + vals[u] for u in range(batch)]
        for u in range(batch):
            o_ref[bi, dst[u], :] = sums[u]

    @pl.when(e == 0)
    def _():
        o_ref[...] = jnp.zeros_like(o_ref)
        yb_ref[...] = jnp.zeros_like(yb_ref)
        for bi in range(bb):
            def gather(r, _, bi=bi):
                slot = bi * cap + r
                xa_ref[token_rows(slot), :] = h_ref[bi, token_rows(idx_ref[0, 0, slot]), :]
                return 0

            lax.fori_loop(0, cap, gather, 0, unroll=8)

    def stage(x_cur, x_next, y_cur, y_prev):
        x = _from_chunk_rows(x_cur, rows).astype(BF16)
        h1 = jnp.dot(x, w1_ref[0, 0], preferred_element_type=F32)
        h3 = jnp.dot(x, w3_ref[0, 0], preferred_element_type=F32)
        hid = (h1 * jax.nn.sigmoid(h1) * h3).astype(BF16)
        _to_chunk_rows(y_cur, jnp.dot(hid, w2_ref[0, 0], preferred_element_type=F32) * gate_ref[0])
        for slot in range(rows):
            x_next[token_rows(slot), :] = h_ref[slot // cap, token_rows(idx_next_ref[0, 0, slot]), :]
        for first in range(0, rows, batch):
            scatter_batch(idx_prev_ref, y_prev, first, e > 0)

    @pl.when(e % 2 == 0)
    def _():
        stage(xa_ref, xb_ref, ya_ref, yb_ref)

    @pl.when(e % 2 == 1)
    def _():
        stage(xb_ref, xa_ref, yb_ref, ya_ref)

    def drain(y_ref):
        for first in range(0, rows, batch):
            scatter_batch(idx_ref, y_ref, first, None)

    @pl.when((e == n_e - 1) & (e % 2 == 0))
    def _():
        drain(ya_ref)

    @pl.when((e == n_e - 1) & (e % 2 == 1))
    def _():
        drain(yb_ref)


def _moe(h, aff, layer, w1, w3, w2):
    B, _, N = aff.shape
    _, E, D, FF = w1.shape
    chunks = D // LANES
    cap = CAPACITY * N // E
    bb = max(1, min(B, MOE_ROWS // cap))
    gate, idx = lax.top_k(aff, cap)

    def group(t):
        return t.reshape(B // bb, bb, E, cap).transpose(0, 2, 1, 3).reshape(B // bb * E, bb * cap)

    rows = bb * cap
    idx_g = group(idx.astype(jnp.int32))[:, None, :]

    def idx_spec(off):
        return pl.BlockSpec((1, 1, rows), lambda b, e: (b * E + jnp.clip(e + off, 0, E - 1), 0, 0),
                            memory_space=pltpu.SMEM)

    scratch = pltpu.VMEM((rows * chunks, LANES), F32)
    return pl.pallas_call(
        functools.partial(_moe_kernel, cap=cap, chunks=chunks),
        grid=(B // bb, E),
        in_specs=[idx_spec(-1), idx_spec(0), idx_spec(1),
                  pl.BlockSpec((1, rows, 1), lambda b, e: (b * E + e, 0, 0)),
                  pl.BlockSpec((bb, N * chunks, LANES), lambda b, e: (b, 0, 0), pipeline_mode=pl.Buffered(1)),
                  pl.BlockSpec((1, 1, D, FF), lambda b, e: (layer, e, 0, 0)),
                  pl.BlockSpec((1, 1, D, FF), lambda b, e: (layer, e, 0, 0)),
                  pl.BlockSpec((1, 1, FF, D), lambda b, e: (layer, e, 0, 0))],
        out_specs=pl.BlockSpec((bb, N * chunks, LANES), lambda b, e: (b, 0, 0)),
        out_shape=jax.ShapeDtypeStruct((B, N * chunks, LANES), F32),
        scratch_shapes=[scratch] * 4,
        compiler_params=_params("parallel", "arbitrary"),
        name="moe",
    )(idx_g, idx_g, idx_g, group(gate)[:, :, None], h, w1, w3, w2)


def _block_diag(w):
    H, bi, bj = w.shape
    eye = jnp.eye(H, dtype=w.dtype)
    return jnp.einsum('hij,hg->higj', w, eye).reshape(H * bi, H * bj)


def _dwconv(u, w, b):
    n = u.shape[1]
    up = jnp.pad(u, ((0, 0), (CONV_LEFT, CONV_W - 1 - CONV_LEFT), (0, 0)))
    out = up[:, 0:n] * w[0]
    for j in range(1, CONV_W):
        out = out + up[:, j:j + n] * w[j]
    return out + b


def _attn_lru_mixer(proj_l, proj_c, w_out, sink, conv_w, conv_b, wa, ba, wi, bi, lam, tail_l, tail_c):
    Lc = proj_c.shape[1]
    attn_l = _attention(sink, proj_l, proj_c)
    attn_c = _ctx_attention(sink, proj_c)

    u0, g0 = ATTN_W + 2 * KV_W, ATTN_W + 2 * KV_W + LRU_W
    u = jnp.concatenate([_dwconv(proj_c[..., u0:g0], conv_w, conv_b),
                         _dwconv(proj_l[..., u0:g0], conv_w, conv_b)], axis=1)
    hf, hr = _lru(u, Lc, wa, ba, wi, bi, lam)
    return (_mix_out(attn_l, hf, hr, proj_l, w_out, Lc, *tail_l),
            _mix_out(attn_c, hf, hr, proj_c, w_out, 0, *tail_c))


def kernel(x, c, ctx, c_ctx, mod_w, mod_b, norm_mix, norm_ffn, router_w, exp_w1, exp_w3, exp_w2, mix_in, mix_out, attn_sink, lru_conv_w, lru_conv_b, lru_wa, lru_ba, lru_wi, lru_bi, lru_lam, rw_mu, rw_rkv, rw_w0, rw_w1, rw_w2, rw_a0, rw_a1, rw_a2, rw_g1, rw_g2, rw_kk, rw_ka, rw_rk, rw_gn_g, rw_gn_b, rw_wo, final_norm):
    B, S, D = x.shape
    Lc = ctx.shape[1]
    depth = mod_w.shape[0]
    assert depth == 2 and S % QBLOCK == 0
    assert LANES % B == 0 and RWKV_H % (LANES // B) == 0 and B % 8 == 0
    assert Lc % WKV_TC == 0 and S % WKV_TC == 0 and (Lc * B) % 256 == 0

    n_rows = -(-(B + 1) // 8) * 8
    cond = jnp.concatenate([jax.nn.silu(c), jax.nn.silu(c_ctx)[None],
                            jnp.zeros((n_rows - B - 1, D), F32)], axis=0)

    mods_l, mods_c = [], []
    for layer in range(depth):
        mod = _mm(cond, mod_w[layer], tm=n_rows, tn=1024, precise=True) + mod_b[layer]
        mods_l.append([t[:, None, :] for t in jnp.split(mod[:B], 6, axis=-1)])
        mods_c.append([jnp.broadcast_to(t[None], (B, 1, D)) for t in jnp.split(mod[B:B + 1], 6, axis=-1)])

    w1b, w3b, w2b = exp_w1.astype(BF16), exp_w3.astype(BF16), exp_w2.astype(BF16)
    xl, xc = x, ctx
    hl = hc = None
    for layer in range(depth):
        last = layer == depth - 1
        m_l, m_c = mods_l[layer], mods_c[layer]
        if layer % 2 == 0:
            i = layer // 2
            assert layer == 0
            w_in = mix_in[i].astype(BF16)
            proj_l = _norm_mm(xl, norm_mix[layer], m_l[0], m_l[1], w_in, rope=_rope_tables(S))
            proj_c = _norm_mm(xc, norm_mix[layer], m_c[0], m_c[1], w_in)
            tail_l = (xl, m_l[2], norm_ffn[layer], m_l[3], m_l[4], router_w[layer])
            tail_c = (xc, m_c[2], norm_ffn[layer], m_c[3], m_c[4], router_w[layer])
            (xl, hl, aff_l), (xc, hc, aff_c) = _attn_lru_mixer(
                proj_l, proj_c, mix_out[i], attn_sink[i], lru_conv_w[i], lru_conv_b[i], lru_wa[i],
                lru_ba[i], lru_wi[i], lru_bi[i], lru_lam[i], tail_l, tail_c)
        else:
            i = layer // 2
            yl = _rwkv7_mixer(hl, hc, rw_mu[i], rw_rkv[i], rw_w0[i], rw_w1[i], rw_w2[i], rw_a0[i], rw_a1[i],
                              rw_a2[i], rw_g1[i], rw_g2[i], rw_kk[i], rw_ka[i], rw_rk[i], rw_gn_g[i],
                              rw_gn_b[i], rw_wo[i])
            xl, hl, aff_l = _resid_norm_tm(xl, yl, m_l[2], norm_ffn[layer], m_l[3], m_l[4], delta_tm=True,
                                           h_tm=False, router_w=router_w[layer])
        moe_l = _moe(hl, aff_l, layer, w1b, w3b, w2b)
        if last:
            return _final_norm(xl, moe_l, m_l[5], final_norm)
        assert (layer + 1) % 2 == 1
        n_l, n_c = mods_l[layer + 1], mods_c[layer + 1]
        xl, hl = _resid_norm_tm(xl, moe_l, m_l[5], norm_mix[layer + 1], n_l[0], n_l[1], delta_tm=False, h_tm=True)
        moe_c = _moe(hc, aff_c, layer, w1b, w3b, w2b)
        xc, hc = _resid_norm_tm(xc, moe_c, m_c[5], norm_mix[layer + 1], n_c[0], n_c[1], delta_tm=False, h_tm=True)
```

```python
import functools
import math

import jax
import jax.numpy as jnp
from jax import lax
from jax.experimental import pallas as pl
from jax.experimental.pallas import tpu as pltpu

F32 = jnp.float32
BF16 = jnp.bfloat16

GRID_W = 64
N_HEADS_ATTN = 8
N_KV = 2
HEAD_DIM = 64
AXIS_DIM = HEAD_DIM // 2
ATTN_W = N_HEADS_ATTN * HEAD_DIM
KV_W = N_KV * HEAD_DIM
WINDOW = 128
QBLOCK = 128
ROPE_BASE = 10000.0
LRU_W = 512
LRU_BLOCKS = 8
CONV_W = 4
CONV_LEFT = 2
LRU_C = 8.0
RWKV_H = 16
RWKV_HD = 64
GN_EPS = 64e-5
N_EXPERTS = 16
CAPACITY = 2
NORM_EPS = 1e-6
NEG_INF = -1e30

LANES = 128
VMEM_LIMIT_BYTES = 56 * 1024 * 1024
WKV_TC = 64
MOE_ROWS = 256
MIX_OUT_ROWS = 128
def _params(*sem):
    return pltpu.CompilerParams(dimension_semantics=sem, vmem_limit_bytes=VMEM_LIMIT_BYTES)


def _mm_kernel(x_ref, w_ref, o_ref, *, precise):
    if precise:
        o_ref[...] = jnp.dot(x_ref[...], w_ref[...], precision=lax.Precision.HIGHEST,
                             preferred_element_type=F32)
    else:
        o_ref[...] = jnp.dot(x_ref[...].astype(BF16), w_ref[...].astype(BF16),
                             preferred_element_type=F32)


def _mm(x, w, *, tm=512, tn=None, precise=False):
    M, K = x.shape
    N = w.shape[1]
    tm = min(tm, M)
    tn = N if tn is None else min(tn, N)
    assert M % tm == 0 and N % tn == 0, (M, tm, N, tn)
    return pl.pallas_call(
        functools.partial(_mm_kernel, precise=precise),
        grid=(M // tm, N // tn),
        in_specs=[pl.BlockSpec((tm, K), lambda i, j: (i, 0)),
                  pl.BlockSpec((K, tn), lambda i, j: (0, j))],
        out_specs=pl.BlockSpec((tm, tn), lambda i, j: (i, j)),
        out_shape=jax.ShapeDtypeStruct((M, N), F32),
        compiler_params=_params("parallel", "parallel"),
        name="mm",
    )(x, w)


def _final_norm_kernel(x_ref, d_ref, gate_ref, g_ref, o_ref):
    x = x_ref[0] + gate_ref[0] * _from_chunk_rows(d_ref.at[0], x_ref.shape[1])
    o_ref[0] = x * lax.rsqrt(jnp.mean(x * x, axis=-1, keepdims=True) + NORM_EPS) * g_ref[...]


def _from_chunk_rows(ref, n):
    chunks = ref.shape[0] // n
    return jnp.concatenate([ref[pl.ds(j, n, stride=chunks), :] for j in range(chunks)], axis=-1)


def _to_chunk_rows(ref, x):
    n = x.shape[0]
    chunks = x.shape[1] // LANES
    for j in range(chunks):
        ref[pl.ds(j, n, stride=chunks), :] = x[:, j * LANES:(j + 1) * LANES]


def _final_norm(x, delta, gate, g):
    B, N, D = x.shape
    ts = min(512, N)
    chunks = D // LANES
    tok = pl.BlockSpec((1, ts, D), lambda b, i: (b, i, 0))
    return pl.pallas_call(
        _final_norm_kernel,
        grid=(B, N // ts),
        in_specs=[tok, pl.BlockSpec((1, ts * chunks, LANES), lambda b, i: (b, i, 0)),
                  pl.BlockSpec((1, 1, D), lambda b, i: (b, 0, 0)),
                  pl.BlockSpec((1, D), lambda b, i: (0, 0))],
        out_specs=tok,
        out_shape=jax.ShapeDtypeStruct((B, N, D), F32),
        compiler_params=_params("parallel", "parallel"),
        name="final_norm",
    )(x, delta, gate, g.reshape(1, D))


def _resid_norm_tm_kernel(*refs, delta_tm, h_tm, route):
    refs = list(refs)
    x_ref, d_ref, gate_ref, g_ref, sh_ref, sc_ref = (refs.pop(0) for _ in range(6))
    wr_ref = refs.pop(0) if route else None
    xo_ref, h_ref = refs.pop(0), refs.pop(0)
    aff_ref = refs.pop(0) if route else None
    nb, ts, _ = x_ref.shape
    for bi in range(nb):
        delta = d_ref[:, bi, :] if delta_tm else _from_chunk_rows(d_ref.at[bi], ts)
        x = x_ref[bi] + gate_ref[bi] * delta
        xo_ref[bi] = x
        y = x * lax.rsqrt(jnp.mean(x * x, axis=-1, keepdims=True) + NORM_EPS) * g_ref[...]
        h = y * (1.0 + sc_ref[bi]) + sh_ref[bi]
        if h_tm:
            h_ref[:, bi, :] = h
        else:
            _to_chunk_rows(h_ref.at[bi], h)
        if route:
            logits = lax.dot_general(wr_ref[...], h, (((1,), (1,)), ((), ())),
                                     precision=lax.Precision.HIGHEST, preferred_element_type=F32)
            e = jnp.exp(logits - jnp.max(logits, axis=0, keepdims=True))
            aff_ref[bi] = e / jnp.sum(e, axis=0, keepdims=True)


def _resid_norm_tm(x, delta, gate, g, shift, scale, *, delta_tm, h_tm, router_w=None):
    B, N, D = x.shape
    nb, ts = 8, min(LANES, N)
    chunks = D // LANES
    tok = pl.BlockSpec((nb, ts, D), lambda b, i: (b, i, 0))
    tok_rows = pl.BlockSpec((nb, ts * chunks, LANES), lambda b, i: (b, i, 0))
    tok_tm = pl.BlockSpec((ts, nb, D), lambda b, i: (i, b, 0))
    per_b = pl.BlockSpec((nb, 1, D), lambda b, i: (b, 0, 0))
    in_specs = [tok, tok_tm if delta_tm else tok_rows, per_b, pl.BlockSpec((1, D), lambda b, i: (0, 0)),
                per_b, per_b]
    args = [x, delta, gate, g.reshape(1, D), shift, scale]
    out_specs = [tok, tok_tm if h_tm else tok_rows]
    out_shape = [jax.ShapeDtypeStruct((B, N, D), F32),
                 jax.ShapeDtypeStruct((N, B, D) if h_tm else (B, N * chunks, LANES), F32)]
    route = router_w is not None
    if route:
        E = router_w.shape[1]
        in_specs.append(pl.BlockSpec((E, D), lambda b, i: (0, 0)))
        args.append(router_w.T)
        out_specs.append(pl.BlockSpec((nb, E, ts), lambda b, i: (b, 0, i)))
        out_shape.append(jax.ShapeDtypeStruct((B, E, N), F32))
    return pl.pallas_call(
        functools.partial(_resid_norm_tm_kernel, delta_tm=delta_tm, h_tm=h_tm, route=route),
        grid=(B // nb, N // ts),
        in_specs=in_specs,
        out_specs=out_specs,
        out_shape=out_shape,
        compiler_params=_params("parallel", "parallel"),
        name="resid_norm_tm",
    )(*args)


def _norm_mm_kernel(x_ref, g_ref, sh_ref, sc_ref, w_ref, *rest):
    o_ref = rest[-1]
    x = x_ref[0]
    y = x * lax.rsqrt(jnp.mean(x * x, axis=-1, keepdims=True) + NORM_EPS) * g_ref[...]
    h = (y * (1.0 + sc_ref[0]) + sh_ref[0]).astype(BF16)
    p = jnp.dot(h, w_ref[...], preferred_element_type=F32)
    if len(rest) == 3:
        cos_ref, sin_ref = rest[:2]
        width = cos_ref.shape[1]
        o_ref[0, :, :width] = _rotate(p[:, :width], cos_ref[...], sin_ref[...])
        o_ref[0, :, width:] = p[:, width:]
    else:
        o_ref[0] = p


def _norm_mm(x, g, shift, scale, w, rope=None):
    B, N, D = x.shape
    M = w.shape[1]
    ts = min(512, N)
    per_b = pl.BlockSpec((1, 1, D), lambda b, i: (b, 0, 0))
    in_specs = [pl.BlockSpec((1, ts, D), lambda b, i: (b, i, 0)),
                pl.BlockSpec((1, D), lambda b, i: (0, 0)), per_b, per_b,
                pl.BlockSpec((D, M), lambda b, i: (0, 0))]
    args = [x, g.reshape(1, D), shift, scale, w]
    if rope is not None:
        in_specs += [pl.BlockSpec((ts, rope[0].shape[1]), lambda b, i: (i, 0))] * 2
        args += list(rope)
    return pl.pallas_call(
        _norm_mm_kernel,
        grid=(B, N // ts),
        in_specs=in_specs,
        out_specs=pl.BlockSpec((1, ts, M), lambda b, i: (b, i, 0)),
        out_shape=jax.ShapeDtypeStruct((B, N, M), F32),
        compiler_params=_params("parallel", "parallel"),
        name="norm_mm",
    )(*args)


def _rotate(x, cos, sin):
    out = []
    for c in range(x.shape[1] // LANES):
        sl = slice(c * LANES, (c + 1) * LANES)
        xc = x[:, sl]
        lane = lax.broadcasted_iota(jnp.int32, xc.shape, 1)
        first_half = (lane % AXIS_DIM) < (AXIS_DIM // 2)
        partner = jnp.where(first_half, pltpu.roll(xc, LANES - AXIS_DIM // 2, 1),
                            pltpu.roll(xc, AXIS_DIM // 2, 1))
        out.append(xc * cos[:, sl] + partner * sin[:, sl])
    return jnp.concatenate(out, axis=1)


def _rope_tables(S):
    rows = S // GRID_W
    row = jnp.repeat(jnp.arange(rows), GRID_W).astype(F32)
    col = jnp.tile(jnp.arange(GRID_W), rows).astype(F32)
    inv = ROPE_BASE ** (-jnp.arange(0, AXIS_DIM, 2, dtype=F32) / AXIS_DIM)
    ang_row, ang_col = row[:, None] * inv, col[:, None] * inv

    def axis_tables(ang):
        c, s = jnp.cos(ang), jnp.sin(ang)
        return jnp.concatenate([c, c], -1), jnp.concatenate([-s, s], -1)

    cr, sr = axis_tables(ang_row)
    cc, sc = axis_tables(ang_col)
    cos_h = jnp.concatenate([cr, cc], -1)
    sin_h = jnp.concatenate([sr, sc], -1)
    q_scale = HEAD_DIM ** -0.5
    cos = jnp.concatenate([jnp.tile(cos_h, (1, N_HEADS_ATTN)) * q_scale, jnp.tile(cos_h, (1, N_KV))], -1)
    sin = jnp.concatenate([jnp.tile(sin_h, (1, N_HEADS_ATTN)) * q_scale, jnp.tile(sin_h, (1, N_KV))], -1)
    return cos, sin


def _attn_heads(q, kcat, vcat, valid, sink_ref):
    outs = []
    G = N_HEADS_ATTN // N_KV
    Q = q.shape[0]
    for j in range(N_KV):
        kj = kcat[:, j * HEAD_DIM:(j + 1) * HEAD_DIM]
        vj = vcat[:, j * HEAD_DIM:(j + 1) * HEAD_DIM]
        heads = range(j * G, (j + 1) * G)
        qs = jnp.concatenate([q[:, h * HEAD_DIM:(h + 1) * HEAD_DIM] for h in heads], axis=0).astype(BF16)
        s = lax.dot_general(qs, kj, (((1,), (1,)), ((), ())), preferred_element_type=F32)
        if valid is not None:
            s = jnp.where(valid[None], s.reshape(G, Q, -1), NEG_INF).reshape(G * Q, -1)
        sink = jnp.concatenate([jnp.full((Q, 1), sink_ref[h], F32) for h in heads], axis=0)
        m = jnp.maximum(jnp.max(s, axis=-1, keepdims=True), sink)
        e = jnp.exp(s - m)
        den = jnp.sum(e, axis=-1, keepdims=True) + jnp.exp(sink - m)
        o = jnp.dot(e.astype(BF16), vj, preferred_element_type=F32) / den
        outs.extend(o[g * Q:(g + 1) * Q] for g in range(G))
    return jnp.concatenate(outs, axis=-1)


def _attn_kernel(sink_ref, q_ref, kp_ref, kc_ref, kn_ref, vp_ref, vc_ref, vn_ref, ck_ref, cv_ref,
                 o_ref, *, S, Lc):
    i = pl.program_id(1)
    kcat = jnp.concatenate([ck_ref[0], kp_ref[0], kc_ref[0], kn_ref[0]], axis=0).astype(BF16)
    vcat = jnp.concatenate([cv_ref[0], vp_ref[0], vc_ref[0], vn_ref[0]], axis=0).astype(BF16)
    L = Lc + 3 * QBLOCK
    row = lax.broadcasted_iota(jnp.int32, (QBLOCK, L), 0)
    col = lax.broadcasted_iota(jnp.int32, (QBLOCK, L), 1)
    rel = col - Lc - QBLOCK - row
    kpos = (i - 1) * QBLOCK + col - Lc
    valid = (col < Lc) | ((jnp.abs(rel) <= WINDOW) & (kpos >= 0) & (kpos < S))
    o_ref[0] = _attn_heads(q_ref[0], kcat, vcat, valid, sink_ref)


def _attention(sink, proj_l, proj_c):
    B, S, _ = proj_l.shape
    Lc = proj_c.shape[1]
    nb = S // QBLOCK
    kcol, vcol = ATTN_W // KV_W, ATTN_W // KV_W + 1

    def blk(colblk, off):
        return pl.BlockSpec((1, QBLOCK, KV_W),
                            lambda b, i: (b, jnp.clip(i + off, 0, nb - 1), colblk))

    return pl.pallas_call(
        functools.partial(_attn_kernel, S=S, Lc=Lc),
        grid=(B, nb),
        in_specs=[pl.BlockSpec(memory_space=pltpu.SMEM),
                  pl.BlockSpec((1, QBLOCK, ATTN_W), lambda b, i: (b, i, 0)),
                  blk(kcol, -1), blk(kcol, 0), blk(kcol, 1),
                  blk(vcol, -1), blk(vcol, 0), blk(vcol, 1),
                  pl.BlockSpec((1, Lc, KV_W), lambda b, i: (b, 0, kcol)),
                  pl.BlockSpec((1, Lc, KV_W), lambda b, i: (b, 0, vcol))],
        out_specs=pl.BlockSpec((1, QBLOCK, ATTN_W), lambda b, i: (b, i, 0)),
        out_shape=jax.ShapeDtypeStruct((B, S, ATTN_W), F32),
        compiler_params=_params("parallel", "parallel"),
        name="attention",
    )(sink, proj_l, proj_l, proj_l, proj_l, proj_l, proj_l, proj_l, proj_c, proj_c)


def _ctx_attn_kernel(sink_ref, q_ref, ck_ref, cv_ref, o_ref):
    q = q_ref[0] * (HEAD_DIM ** -0.5)
    o_ref[0] = _attn_heads(q, ck_ref[0].astype(BF16), cv_ref[0].astype(BF16), None, sink_ref)


def _ctx_attention(sink, proj_c):
    B, Lc, _ = proj_c.shape
    kcol, vcol = ATTN_W // KV_W, ATTN_W // KV_W + 1
    return pl.pallas_call(
        _ctx_attn_kernel,
        grid=(B,),
        in_specs=[pl.BlockSpec(memory_space=pltpu.SMEM),
                  pl.BlockSpec((1, Lc, ATTN_W), lambda b: (b, 0, 0)),
                  pl.BlockSpec((1, Lc, KV_W), lambda b: (b, 0, kcol)),
                  pl.BlockSpec((1, Lc, KV_W), lambda b: (b, 0, vcol))],
        out_specs=pl.BlockSpec((1, Lc, ATTN_W), lambda b: (b, 0, 0)),
        out_shape=jax.ShapeDtypeStruct((B, Lc, ATTN_W), F32),
        compiler_params=_params("parallel"),
        name="ctx_attention",
    )(sink, proj_c, proj_c, proj_c)


def _seg_rev_block(j, nC, nL):
    return jnp.where(j < nC, nC - 1 - j, nC + nL - 1 - (j - nC))


def _lru_kernel(uf_ref, ur_ref, wg_ref, bg_ref, sp_ref, hf_ref, hr_ref, a_ref, b_ref, cf_ref, cr_ref, *, Tc):
    @pl.when(pl.program_id(1) == 0)
    def _():
        cf_ref[...] = jnp.zeros_like(cf_ref)
        cr_ref[...] = jnp.zeros_like(cr_ref)

    sub = 8
    row = lax.broadcasted_iota(jnp.int32, (sub, LRU_W), 0)
    for d, (u_ref, h_ref, c_ref) in enumerate(((uf_ref, hf_ref, cf_ref), (ur_ref, hr_ref, cr_ref))):
        u = u_ref[0]
        gates = jax.nn.sigmoid(jnp.dot(u.astype(BF16), wg_ref[d], preferred_element_type=F32) + bg_ref[d])
        a = jnp.exp(-(gates[:, :LRU_W] * sp_ref[d]))
        a_ref[...] = a
        b_ref[...] = jnp.sqrt(jnp.maximum(1.0 - a * a, 0.0)) * (gates[:, LRU_W:] * u)

        def tile(ti, carry, d=d, h_ref=h_ref):
            t0 = pl.multiple_of((ti if d == 0 else Tc // sub - 1 - ti) * sub, sub)
            av = a_ref[pl.ds(t0, sub), :]
            bv = b_ref[pl.ds(t0, sub), :]
            for s in (1, 2, 4):
                shift, known = (s, row >= s) if d == 0 else (sub - s, row < sub - s)
                a_prev = jnp.where(known, pltpu.roll(av, shift, 0), 1.0)
                b_prev = jnp.where(known, pltpu.roll(bv, shift, 0), 0.0)
                bv = bv + av * b_prev
                av = av * a_prev
            h = av * carry + bv
            h_ref[0, pl.ds(t0, sub), :] = h
            return h[sub - 1:sub] if d == 0 else h[0:1]

        c_ref[...] = lax.fori_loop(0, Tc // sub, tile, c_ref[...], unroll=4)


def _lru(u, Lc, wa, ba, wi, bi, lam):
    B, T, _ = u.shape
    S = T - Lc
    Tc = math.gcd(math.gcd(Lc, S), 256)
    nC, nL = Lc // Tc, S // Tc
    w_gates = jnp.stack([jnp.concatenate([_block_diag(wa[d]), _block_diag(wi[d])], axis=1)
                         for d in range(2)]).astype(BF16)
    b_gates = jnp.stack([jnp.concatenate([ba[d], bi[d]]) for d in range(2)])[:, None, :]
    decay_rate = (LRU_C * jax.nn.softplus(-lam))[:, None, :]
    fwd = pl.BlockSpec((1, Tc, LRU_W), lambda bi_, j: (bi_, j, 0))
    rev = pl.BlockSpec((1, Tc, LRU_W), lambda bi_, j: (bi_, _seg_rev_block(j, nC, nL), 0))

    def const(shape):
        return pl.BlockSpec(shape, lambda bi_, j: (0,) * len(shape))

    return pl.pallas_call(
        functools.partial(_lru_kernel, Tc=Tc),
        grid=(B, nC + nL),
        in_specs=[fwd, rev, const(w_gates.shape), const(b_gates.shape), const(decay_rate.shape)],
        out_specs=[fwd, rev],
        out_shape=[jax.ShapeDtypeStruct((B, T, LRU_W), F32)] * 2,
        scratch_shapes=[pltpu.VMEM((Tc, LRU_W), F32), pltpu.VMEM((Tc, LRU_W), F32),
                        pltpu.VMEM((1, LRU_W), F32), pltpu.VMEM((1, LRU_W), F32)],
        compiler_params=_params("parallel", "arbitrary"),
        name="lru",
    )(u, u, w_gates, b_gates, decay_rate)


def _mix_out_kernel(attn_ref, hf_ref, hr_ref, g0_ref, g1_ref, wo_ref, x_ref, gate_ref, g_ref, sh_ref, sc_ref,
                    wr_ref, xo_ref, h_ref, aff_ref):
    tm = x_ref.shape[1]
    chunks = x_ref.shape[2] // LANES
    part = min(tm, MIX_OUT_ROWS)
    for p in range(tm // part):
        rows = slice(p * part, (p + 1) * part)
        gate = jnp.concatenate([g0_ref[0, rows], g1_ref[0, rows]], axis=-1)
        rec = ((hf_ref[0, rows] + hr_ref[0, rows]) * jax.nn.gelu(gate)).astype(BF16)
        y = (jnp.dot(attn_ref[0, rows].astype(BF16), wo_ref[:ATTN_W, :], preferred_element_type=F32)
             + jnp.dot(rec, wo_ref[ATTN_W:, :], preferred_element_type=F32))
        x = x_ref[0, rows] + gate_ref[0] * y
        xo_ref[0, rows] = x
        xn = x * lax.rsqrt(jnp.mean(x * x, axis=-1, keepdims=True) + NORM_EPS) * g_ref[...]
        h = xn * (1.0 + sc_ref[0]) + sh_ref[0]
        _to_chunk_rows(h_ref.at[0, pl.ds(p * part * chunks, part * chunks)], h)
        logits = lax.dot_general(wr_ref[...], h, (((1,), (1,)), ((), ())),
                                 precision=lax.Precision.HIGHEST, preferred_element_type=F32)
        e = jnp.exp(logits - jnp.max(logits, axis=0, keepdims=True))
        aff_ref[0, :, rows] = e / jnp.sum(e, axis=0, keepdims=True)


def _mix_out(attn, hf, hr, proj, w_out, t_off, x, gate, g, shift, scale, router_w):
    B, N, _ = attn.shape
    D = w_out.shape[1]
    E = router_w.shape[1]
    chunks = D // LANES
    tm = math.gcd(math.gcd(N, 256), t_off) if t_off else min(256, N)
    half = LRU_W // 2
    gate_blk = (proj.shape[2] - LRU_W) // half
    tok = lambda b, i: (b, i, 0)
    per_b = pl.BlockSpec((1, 1, D), lambda b, i: (b, 0, 0))
    return pl.pallas_call(
        _mix_out_kernel,
        grid=(B, N // tm),
        in_specs=[pl.BlockSpec((1, tm, ATTN_W), tok),
                  pl.BlockSpec((1, tm, LRU_W), lambda b, i: (b, i + t_off // tm, 0)),
                  pl.BlockSpec((1, tm, LRU_W), lambda b, i: (b, i + t_off // tm, 0)),
                  pl.BlockSpec((1, tm, half), lambda b, i: (b, i, gate_blk)),
                  pl.BlockSpec((1, tm, half), lambda b, i: (b, i, gate_blk + 1)),
                  pl.BlockSpec(w_out.shape, lambda b, i: (0, 0)),
                  pl.BlockSpec((1, tm, D), tok), per_b,
                  pl.BlockSpec((1, D), lambda b, i: (0, 0)), per_b, per_b,
                  pl.BlockSpec((E, D), lambda b, i: (0, 0))],
        out_specs=[pl.BlockSpec((1, tm, D), tok),
                   pl.BlockSpec((1, tm * chunks, LANES), tok),
                   pl.BlockSpec((1, E, tm), lambda b, i: (b, 0, i))],
        out_shape=[jax.ShapeDtypeStruct((B, N, D), F32),
                   jax.ShapeDtypeStruct((B, N * chunks, LANES), F32),
                   jax.ShapeDtypeStruct((B, E, N), F32)],
        compiler_params=_params("parallel", "parallel"),
        name="mix_out",
    )(attn, hf, hr, proj, proj, w_out.astype(BF16), x, gate, g.reshape(1, D), shift, scale, router_w.T)


def _softplus(z):
    return jnp.maximum(z, 0.0) + jnp.log(1.0 + jnp.exp(-jnp.abs(z)))


def _nt_dot(wt, x):
    return lax.dot_general(wt, x, (((1,), (1,)), ((), ())), preferred_element_type=F32)


def _rwkv_prep_kernel(cp_ref, c_ref, cn_ref, lp_ref, l_ref, ln_ref, mu_ref, wr_ref, wk_ref, wv_ref, w1_ref,
                      w2_ref, a1_ref, a2_ref, g1_ref, g2_ref, w0_ref, a0_ref, ka_ref, rk_ref, perm_ref,
                      dec_ref, aa_ref, k_ref, v_ref, r_ref, g_ref, bc_ref, *, B, ctx_tiles, tiles):
    i = pl.program_id(0)
    tm, D = l_ref.shape
    is_ctx = i < ctx_tiles
    h = jnp.where(is_ctx, c_ref[...], l_ref[...])
    seq_start = (i == 0) | (i == ctx_tiles)
    seq_end = (i == ctx_tiles - 1) | (i == tiles - 1)
    hp = jnp.where(seq_start, 0.0, jnp.where(is_ctx, cp_ref[...], lp_ref[...]))
    hn = jnp.where(seq_end, 0.0, jnp.where(is_ctx, cn_ref[...], ln_ref[...]))
    xx = 0.5 * (jnp.concatenate([hp, h[:tm - B]], axis=0) + jnp.concatenate([h[B:], hn], axis=0)) - h

    def mix(j):
        return (h + xx * mu_ref[j:j + 1, :]).astype(BF16)

    hpg = LANES // B
    hd = RWKV_HD

    def to_scan_layout(x, ref):
        for g in range(RWKV_H // hpg):
            for c in range(tm // LANES):
                tiles = [x[(g * hpg + hh) * hd:(g * hpg + hh + 1) * hd, c * LANES:(c + 1) * LANES]
                         for hh in range(hpg)]
                for tl, tile in enumerate(_slab_transpose(tiles, B)):
                    ref[g, c * hpg + tl] = tile

    v = _nt_dot(wv_ref[...], mix(3))
    to_scan_layout(v, v_ref)
    r = _nt_dot(wr_ref[...], mix(0))
    to_scan_layout(r, r_ref)
    k = _nt_dot(wk_ref[...], mix(2))
    to_scan_layout(k, k_ref)
    lw = jnp.tanh(_nt_dot(w1_ref[...], mix(1))).astype(BF16)
    wpre = jnp.dot(w2_ref[...], lw, preferred_element_type=F32)
    la = _nt_dot(a1_ref[...], mix(4)).astype(BF16)
    apre = jnp.dot(a2_ref[...], la, preferred_element_type=F32)
    gg = jax.nn.sigmoid(_nt_dot(g1_ref[...], mix(5))).astype(BF16)
    g_ref[...] = jnp.dot(g2_ref[...], gg, preferred_element_type=F32)

    def to_scan_layout_mxu(xs, refs):
        groups = RWKV_H // hpg
        for c in range(tm // LANES):
            lhs = jnp.concatenate(
                [jnp.concatenate([x[(g * hpg + hh) * hd:(g * hpg + hh + 1) * hd, c * LANES:(c + 1) * LANES]
                                  for hh in range(hpg)], axis=1)
                 for x in xs for g in range(groups)], axis=0)
            hi = lhs.astype(BF16)
            rest = lhs - hi.astype(F32)
            mid = rest.astype(BF16)
            lo = (rest - mid.astype(F32)).astype(BF16)
            out = (jnp.dot(hi, perm_ref[...], preferred_element_type=F32)
                   + jnp.dot(mid, perm_ref[...], preferred_element_type=F32)
                   + jnp.dot(lo, perm_ref[...], preferred_element_type=F32))
            for n, ref in enumerate(refs):
                for g in range(groups):
                    blk = out[(n * groups + g) * hd:(n * groups + g + 1) * hd]
                    for tl in range(hpg):
                        ref[g, c * hpg + tl] = blk[:, tl * LANES:(tl + 1) * LANES]

    iclr, decay = [], []
    for d in range(2):
        w_log = -_softplus(-(w0_ref[d] + wpre[d * D:(d + 1) * D])) - 0.5
        decay.append(jnp.exp(-jnp.exp(w_log)))
        a = jax.nn.sigmoid(a0_ref[d] + apre[d * D:(d + 1) * D])
        iclr.append(a)
        to_scan_layout(a, aa_ref.at[d])
    to_scan_layout_mxu(decay, [dec_ref.at[0], dec_ref.at[1]])
    kd_sum = k * (2.0 + (iclr[0] + iclr[1] - 2.0) * ka_ref[...])
    bc = jnp.sum((r * kd_sum * rk_ref[...]).reshape(RWKV_H, hd, tm), axis=1)
    for g in range(RWKV_H // hpg):
        for c in range(tm // LANES):
            rows = [jnp.broadcast_to(bc[g * hpg + hh:g * hpg + hh + 1, c * LANES:(c + 1) * LANES], (8, LANES))
                    for hh in range(hpg)]
            for tl, tile in enumerate(_slab_transpose(rows, B)):
                bc_ref[g, c * hpg + tl] = tile[0:1]


def _slab_permutation(B):
    n = LANES // B
    src = jnp.arange(n * LANES).reshape(n, n, B).transpose(1, 0, 2).reshape(-1)
    return (jnp.arange(n * LANES)[:, None] == src[None, :]).astype(BF16)


def _slab_transpose(tiles, B):
    n = len(tiles)
    tiles = list(tiles)
    slab = lax.broadcasted_iota(jnp.int32, tiles[0].shape, 1) // B
    s = n // 2
    while s >= 1:
        upper = (slab & s) != 0
        for i in range(n):
            if i & s == 0:
                lo, hi = tiles[i], tiles[i + s]
                tiles[i] = jnp.where(upper, pltpu.roll(hi, s * B, 1), lo)
                tiles[i + s] = jnp.where(upper, hi, pltpu.roll(lo, LANES - s * B, 1))
        s //= 2
    return tiles


def _const_spec(shape):
    nd = len(shape)
    return pl.BlockSpec(shape, lambda i: (0,) * nd, pipeline_mode=pl.Buffered(1))


def _rwkv_prep(hc_tm, hl_tm, B, mu, w_rkv, w0, w1, w2, a0, a1, a2, g1, g2, k_a, r_k):
    D = hl_tm.shape[1]
    Lc, S = hc_tm.shape[0] // B, hl_tm.shape[0] // B
    T = Lc + S
    TB = T * B
    tm = 256
    tiles, ctx_tiles = TB // tm, Lc * B // tm
    hb = tm // B
    G = RWKV_H // (LANES // B)
    hd = RWKV_HD

    def t_bf16(w):
        return w.T.astype(BF16)

    zeros_w = jnp.zeros_like(w2[0].T)
    w2t = jnp.concatenate([jnp.concatenate([w2[0].T, zeros_w], 1),
                           jnp.concatenate([zeros_w, w2[1].T], 1)], 0).astype(BF16)
    zeros_a = jnp.zeros_like(a2[0].T)
    a2t = jnp.concatenate([jnp.concatenate([a2[0].T, zeros_a], 1),
                           jnp.concatenate([zeros_a, a2[1].T], 1)], 0).astype(BF16)
    consts = [mu, t_bf16(w_rkv[0]), t_bf16(w_rkv[1]), t_bf16(w_rkv[2]),
              t_bf16(jnp.concatenate([w1[0], w1[1]], 1)), w2t,
              t_bf16(jnp.concatenate([a1[0], a1[1]], 1)), a2t, t_bf16(g1), t_bf16(g2),
              w0.reshape(2, D, 1), a0.reshape(2, D, 1), k_a.reshape(D, 1), r_k.reshape(D, 1),
              _slab_permutation(B)]
    dir_spec = pl.BlockSpec((2, G, hb, hd, LANES), lambda i: (0, 0, i, 0, 0))
    all_spec = pl.BlockSpec((G, hb, hd, LANES), lambda i: (0, i, 0, 0))
    dir_shape = jax.ShapeDtypeStruct((2, G, T, hd, LANES), F32)
    all_shape = jax.ShapeDtypeStruct((G, T, hd, LANES), F32)

    def stream_specs(first, n_tiles, steps):
        def tile(i):
            return jnp.clip(i - first, 0, n_tiles - 1)
        return [pl.BlockSpec((B, D), lambda i: (jnp.maximum(tile(i) * hb - 1, 0), 0)),
                pl.BlockSpec((tm, D), lambda i: (tile(i), 0)),
                pl.BlockSpec((B, D), lambda i: (jnp.minimum((tile(i) + 1) * hb, steps - 1), 0))]

    return pl.pallas_call(
        functools.partial(_rwkv_prep_kernel, B=B, ctx_tiles=ctx_tiles, tiles=tiles),
        grid=(tiles,),
        in_specs=stream_specs(0, ctx_tiles, Lc) + stream_specs(ctx_tiles, tiles - ctx_tiles, S)
                 + [_const_spec(c.shape) for c in consts],
        out_specs=[dir_spec, dir_spec, all_spec, all_spec, all_spec,
                   pl.BlockSpec((D, tm), lambda i: (0, i)),
                   pl.BlockSpec((G, hb, 1, LANES), lambda i: (0, i, 0, 0))],
        out_shape=[dir_shape, dir_shape, all_shape, all_shape, all_shape,
                   jax.ShapeDtypeStruct((D, TB), F32),
                   jax.ShapeDtypeStruct((G, T, 1, LANES), F32)],
        compiler_params=_params("parallel"),
        name="rwkv_prep",
    )(hc_tm, hc_tm, hc_tm, hl_tm, hl_tm, hl_tm, *consts)


def _wkv_kernel(dec_ref, aa_ref, k_ref, v_ref, r_ref, kkc_ref, kac_ref, y_ref,
                s_ref, g_ref, p_ref, q_ref, sa_ref, *, Tc, nC):
    d = pl.program_id(0)
    j = pl.program_id(2)
    hd = RWKV_HD
    sub = 8

    @pl.when(j == 0)
    def _():
        s_ref[...] = jnp.zeros_like(s_ref)

    def partial_rows(x):
        return jnp.sum(x.reshape(hd // sub, sub, LANES), axis=0)

    pitch = hd + 1

    def put_partial(ref, v, x):
        ref[pl.ds(v, sub, stride=pitch), :] = partial_rows(x)

    def finish_rows(ref):
        acc = ref[pl.ds(0, hd), :]
        for q in range(1, sub):
            acc = acc + ref[pl.ds(q * pitch, hd), :]
        return acc

    def run(with_y):
        g_ref[...] = jnp.ones_like(g_ref)

        def step(s, _):
            tt = jnp.where(d == 0, s, Tc - 1 - s)
            aa = aa_ref[0, 0, tt]
            kk = k_ref[0, tt]
            kf = kk * kkc_ref[0]
            kn = kf * lax.rsqrt(jnp.maximum(jnp.sum(kf * kf, axis=0, keepdims=True), 1e-24))
            g_prev = g_ref[...]
            g = g_prev * dec_ref[0, 0, tt]
            g_inv = 1.0 / g
            g_ref[...] = g
            a_t = -(kn * g_prev)
            b_t = kn * aa * g_inv
            k_t = kk * (1.0 + (aa - 1.0) * kac_ref[0]) * g_inv
            r_t = r_ref[0, tt] * g if with_y else None

            def row_reduce(v, _):
                st = s_ref[v]
                put_partial(p_ref, v, st * a_t)
                if with_y:
                    put_partial(q_ref, v, st * r_t)
                return 0

            lax.fori_loop(0, hd, row_reduce, 0, unroll=64)
            sa = finish_rows(p_ref)
            sa_ref[...] = sa
            if with_y:
                b_r = jnp.sum(b_t * r_t, axis=0, keepdims=True)
                k_r = jnp.sum(k_t * r_t, axis=0, keepdims=True)
                y_ref[0, 0, tt] = finish_rows(q_ref) + sa * b_r + v_ref[0, tt] * k_r

            def row_update(v, _):
                s_ref[v] = s_ref[v] + sa_ref[pl.ds(v, 1), :] * b_t + v_ref[0, tt, pl.ds(v, 1), :] * k_t
                return 0

            lax.fori_loop(0, hd, row_update, 0, unroll=64)
            return 0

        lax.fori_loop(0, Tc, step, 0)

        def rescale(v, _):
            s_ref[v] = s_ref[v] * g_ref[...]
            return 0

        lax.fori_loop(0, hd, rescale, 0, unroll=8)

    @pl.when(j < nC)
    def _():
        run(False)

    @pl.when(j >= nC)
    def _():
        run(True)


def _wkv(dec, aa, k, v, r, k_k, k_a, B, Lc):
    _, G, T, hd, _ = dec.shape
    Tc = WKV_TC
    S = T - Lc
    nC, nL = Lc // Tc, S // Tc

    def tmap(d, j):
        return jnp.where(d == 0, j, _seg_rev_block(j, nC, nL))

    def lmap(d, j):
        jj = jnp.maximum(j - nC, 0)
        return jnp.where(d == 0, jj, nL - 1 - jj)

    dir_spec = pl.BlockSpec((1, 1, Tc, hd, LANES), lambda d, g, j: (d, g, tmap(d, j), 0, 0))
    all_spec = pl.BlockSpec((1, Tc, hd, LANES), lambda d, g, j: (g, tmap(d, j), 0, 0))
    const_spec = pl.BlockSpec((1, hd, LANES), lambda d, g, j: (g, 0, 0))
    return pl.pallas_call(
        functools.partial(_wkv_kernel, Tc=Tc, nC=nC),
        grid=(2, G, nC + nL),
        in_specs=[dir_spec, dir_spec, all_spec, all_spec, all_spec, const_spec, const_spec],
        out_specs=pl.BlockSpec((1, 1, Tc, hd, LANES), lambda d, g, j: (d, g, lmap(d, j), 0, 0)),
        out_shape=jax.ShapeDtypeStruct((2, G, S, hd, LANES), F32),
        scratch_shapes=[pltpu.VMEM((hd, hd, LANES), F32),
                        pltpu.VMEM((hd, LANES), F32),
                        pltpu.VMEM(((hd + 1) * 8, LANES), F32),
                        pltpu.VMEM(((hd + 1) * 8, LANES), F32),
                        pltpu.VMEM((hd, LANES), F32)],
        compiler_params=_params("parallel", "parallel", "arbitrary"),
        name="wkv",
    )(dec, aa, k, v, r, _lane_const(k_k, B), _lane_const(k_a, B))


def _lane_const(c, B):
    hpg = LANES // B
    return jnp.repeat(c.reshape(RWKV_H // hpg, hpg, RWKV_HD).transpose(0, 2, 1), B, axis=-1)


def _rwkv_finish_kernel(y_ref, v_ref, bc_ref, g_ref, gng_ref, gnb_ref, wo_ref, o_ref, *, B):
    G, steps, hd, _ = v_ref.shape
    hpg = LANES // B
    rows = []
    for g in range(G):
        head_cols = [[] for _ in range(hpg)]
        for c in range(steps // hpg):
            tiles = []
            for tl in range(hpg):
                t = c * hpg + tl
                y = y_ref[0, g, t] + y_ref[1, g, t]
                mean = jnp.mean(y, axis=0, keepdims=True)
                var = jnp.mean(jnp.square(y - mean), axis=0, keepdims=True)
                yn = (y - mean) * lax.rsqrt(var + GN_EPS)
                tiles.append(yn * gng_ref[g] + gnb_ref[g] + bc_ref[g, t] * v_ref[g, t])
            for hh, tile in enumerate(_slab_transpose(tiles, B)):
                head_cols[hh].append(tile)
        rows.extend(jnp.concatenate(cols, axis=1) for cols in head_cols)
    o = (jnp.concatenate(rows, axis=0) * g_ref[...]).astype(BF16)
    out_t = jnp.dot(wo_ref[...], o, preferred_element_type=F32)
    o_ref[...] = out_t.T


def _rwkv_finish(y2, v, bc, g, gn_g, gn_b, w_o, B, Lc):
    _, G, S, hd, _ = y2.shape
    D = g.shape[0]
    tm = 256
    steps = tm // B
    off = Lc // steps
    return pl.pallas_call(
        functools.partial(_rwkv_finish_kernel, B=B),
        grid=(S // steps,),
        in_specs=[pl.BlockSpec((2, G, steps, hd, LANES), lambda i: (0, 0, i, 0, 0)),
                  pl.BlockSpec((G, steps, hd, LANES), lambda i: (0, i + off, 0, 0)),
                  pl.BlockSpec((G, steps, 1, LANES), lambda i: (0, i + off, 0, 0)),
                  pl.BlockSpec((D, tm), lambda i: (0, i + off)),
                  _const_spec((G, hd, LANES)), _const_spec((G, hd, LANES)), _const_spec((D, D))],
        out_specs=pl.BlockSpec((tm, D), lambda i: (i, 0)),
        out_shape=jax.ShapeDtypeStruct((S * B, D), F32),
        compiler_params=_params("parallel"),
        name="rwkv_finish",
    )(y2, v, bc, g, _lane_const(gn_g, B), _lane_const(gn_b, B), w_o.T.astype(BF16))


def _rwkv7_mixer(hl, hc, mu, w_rkv, w0, w1, w2, a0, a1, a2, g1, g2, k_k, k_a, r_k, gn_g, gn_b, w_o):
    S, B, D = hl.shape
    Lc = hc.shape[0]
    dec, aa, k, v, r, g, bc = _rwkv_prep(hc.reshape(Lc * B, D), hl.reshape(S * B, D), B, mu, w_rkv, w0, w1,
                                         w2, a0, a1, a2, g1, g2, k_a, r_k.reshape(D))
    y2 = _wkv(dec, aa, k, v, r, k_k, k_a, B, Lc)
    return _rwkv_finish(y2, v, bc, g, gn_g, gn_b, w_o, B, Lc).reshape(S, B, D)


def _moe_kernel(idx_prev_ref, idx_ref, idx_next_ref, gate_ref, h_ref, w1_ref, w3_ref, w2_ref, o_ref,
                xa_ref, xb_ref, ya_ref, yb_ref, *, cap, chunks):
    bb = h_ref.shape[0]
    rows = bb * cap
    e = pl.program_id(1)
    n_e = pl.num_programs(1)
    batch = 8

    def token_rows(t):
        return pl.ds(pl.multiple_of(t * chunks, chunks), chunks)

    def scatter_batch(ids_ref, y_ref, first, keep):
        bi = first // cap
        slots = [first + u for u in range(batch)]
        dst = [token_rows(ids_ref[0, 0, s]) for s in slots]
        vals = [y_ref[token_rows(s), :] for s in slots]
        if keep is not None:
            vals = [jnp.where(keep, v, 0.0) for v in vals]
        sums = [o_ref[bi, dst[u], :] + vals[u] for u in range(batch)]
        for u in range(batch):
            o_ref[bi, dst[u], :] = sums[u]

    @pl.when(e == 0)
    def _():
        o_ref[...] = jnp.zeros_like(o_ref)
        yb_ref[...] = jnp.zeros_like(yb_ref)
        for bi in range(bb):
            def gather(r, _, bi=bi):
                slot = bi * cap + r
                xa_ref[token_rows(slot), :] = h_ref[bi, token_rows(idx_ref[0, 0, slot]), :]
                return 0

            lax.fori_loop(0, cap, gather, 0, unroll=8)

    def stage(x_cur, x_next, y_cur, y_prev):
        x = _from_chunk_rows(x_cur, rows).astype(BF16)
        h1 = jnp.dot(x, w1_ref[0, 0], preferred_element_type=F32)
        h3 = jnp.dot(x, w3_ref[0, 0], preferred_element_type=F32)
        hid = (h1 * jax.nn.sigmoid(h1) * h3).astype(BF16)
        _to_chunk_rows(y_cur, jnp.dot(hid, w2_ref[0, 0], preferred_element_type=F32) * gate_ref[0])
        for slot in range(rows):
            x_next[token_rows(slot), :] = h_ref[slot // cap, token_rows(idx_next_ref[0, 0, slot]), :]
        for first in range(0, rows, batch):
            scatter_batch(idx_prev_ref, y_prev, first, e > 0)

    @pl.when(e % 2 == 0)
    def _():
        stage(xa_ref, xb_ref, ya_ref, yb_ref)

    @pl.when(e % 2 == 1)
    def _():
        stage(xb_ref, xa_ref, yb_ref, ya_ref)

    def drain(y_ref):
        for first in range(0, rows, batch):
            scatter_batch(idx_ref, y_ref, first, None)

    @pl.when((e == n_e - 1) & (e % 2 == 0))
    def _():
        drain(ya_ref)

    @pl.when((e == n_e - 1) & (e % 2 == 1))
    def _():
        drain(yb_ref)


def _moe(h, aff, layer, w1, w3, w2):
    B, _, N = aff.shape
    _, E, D, FF = w1.shape
    chunks = D // LANES
    cap = CAPACITY * N // E
    bb = max(1, min(B, MOE_ROWS // cap))
    gate, idx = lax.top_k(aff, cap)

    def group(t):
        return t.reshape(B // bb, bb, E, cap).transpose(0, 2, 1, 3).reshape(B // bb * E, bb * cap)

    rows = bb * cap
    idx_g = group(idx.astype(jnp.int32))[:, None, :]

    def idx_spec(off):
        return pl.BlockSpec((1, 1, rows), lambda b, e: (b * E + jnp.clip(e + off, 0, E - 1), 0, 0),
                            memory_space=pltpu.SMEM)

    scratch = pltpu.VMEM((rows * chunks, LANES), F32)
    return pl.pallas_call(
        functools.partial(_moe_kernel, cap=cap, chunks=chunks),
        grid=(B // bb, E),
        in_specs=[idx_spec(-1), idx_spec(0), idx_spec(1),
                  pl.BlockSpec((1, rows, 1), lambda b, e: (b * E + e, 0, 0)),
                  pl.BlockSpec((bb, N * chunks, LANES), lambda b, e: (b, 0, 0), pipeline_mode=pl.Buffered(1)),
                  pl.BlockSpec((1, 1, D, FF), lambda b, e: (layer, e, 0, 0)),
                  pl.BlockSpec((1, 1, D, FF), lambda b, e: (layer, e, 0, 0)),
                  pl.BlockSpec((1, 1, FF, D), lambda b, e: (layer, e, 0, 0))],
        out_specs=pl.BlockSpec((bb, N * chunks, LANES), lambda b, e: (b, 0, 0)),
        out_shape=jax.ShapeDtypeStruct((B, N * chunks, LANES), F32),
        scratch_shapes=[scratch] * 4,
        compiler_params=_params("parallel", "arbitrary"),
        name="moe",
    )(idx_g, idx_g, idx_g, group(gate)[:, :, None], h, w1, w3, w2)


def _block_diag(w):
    H, bi, bj = w.shape
    eye = jnp.eye(H, dtype=w.dtype)
    return jnp.einsum('hij,hg->higj', w, eye).reshape(H * bi, H * bj)


def _dwconv(u, w, b):
    n = u.shape[1]
    up = jnp.pad(u, ((0, 0), (CONV_LEFT, CONV_W - 1 - CONV_LEFT), (0, 0)))
    out = up[:, 0:n] * w[0]
    for j in range(1, CONV_W):
        out = out + up[:, j:j + n] * w[j]
    return out + b


def _attn_lru_mixer(proj_l, proj_c, w_out, sink, conv_w, conv_b, wa, ba, wi, bi, lam, tail_l, tail_c):
    Lc = proj_c.shape[1]
    attn_l = _attention(sink, proj_l, proj_c)
    attn_c = _ctx_attention(sink, proj_c)

    u0, g0 = ATTN_W + 2 * KV_W, ATTN_W + 2 * KV_W + LRU_W
    u = jnp.concatenate([_dwconv(proj_c[..., u0:g0], conv_w, conv_b),
                         _dwconv(proj_l[..., u0:g0], conv_w, conv_b)], axis=1)
    hf, hr = _lru(u, Lc, wa, ba, wi, bi, lam)
    return (_mix_out(attn_l, hf, hr, proj_l, w_out, Lc, *tail_l),
            _mix_out(attn_c, hf, hr, proj_c, w_out, 0, *tail_c))


def kernel(x, c, ctx, c_ctx, mod_w, mod_b, norm_mix, norm_ffn, router_w, exp_w1, exp_w3, exp_w2, mix_in, mix_out, attn_sink, lru_conv_w, lru_conv_b, lru_wa, lru_ba, lru_wi, lru_bi, lru_lam, rw_mu, rw_rkv, rw_w0, rw_w1, rw_w2, rw_a0, rw_a1, rw_a2, rw_g1, rw_g2, rw_kk, rw_ka, rw_rk, rw_gn_g, rw_gn_b, rw_wo, final_norm):
    B, S, D = x.shape
    Lc = ctx.shape[1]
    depth = mod_w.shape[0]
    assert depth == 2 and S % QBLOCK == 0
    assert LANES % B == 0 and RWKV_H % (LANES // B) == 0 and B % 8 == 0
    assert Lc % WKV_TC == 0 and S % WKV_TC == 0 and (Lc * B) % 256 == 0

    n_rows = -(-(B + 1) // 8) * 8
    cond = jnp.concatenate([jax.nn.silu(c), jax.nn.silu(c_ctx)[None],
                            jnp.zeros((n_rows - B - 1, D), F32)], axis=0)

    mods_l, mods_c = [], []
    for layer in range(depth):
        mod = _mm(cond, mod_w[layer], tm=n_rows, tn=1024, precise=True) + mod_b[layer]
        mods_l.append([t[:, None, :] for t in jnp.split(mod[:B], 6, axis=-1)])
        mods_c.append([jnp.broadcast_to(t[None], (B, 1, D)) for t in jnp.split(mod[B:B + 1], 6, axis=-1)])

    w1b, w3b, w2b = exp_w1.astype(BF16), exp_w3.astype(BF16), exp_w2.astype(BF16)
    xl, xc = x, ctx
    hl = hc = None
    for layer in range(depth):
        last = layer == depth - 1
        m_l, m_c = mods_l[layer], mods_c[layer]
        if layer % 2 == 0:
            i = layer // 2
            assert layer == 0
            w_in = mix_in[i].astype(BF16)
            proj_l = _norm_mm(xl, norm_mix[layer], m_l[0], m_l[1], w_in, rope=_rope_tables(S))
            proj_c = _norm_mm(xc, norm_mix[layer], m_c[0], m_c[1], w_in)
            tail_l = (xl, m_l[2], norm_ffn[layer], m_l[3], m_l[4], router_w[layer])
            tail_c = (xc, m_c[2], norm_ffn[layer], m_c[3], m_c[4], router_w[layer])
            (xl, hl, aff_l), (xc, hc, aff_c) = _attn_lru_mixer(
                proj_l, proj_c, mix_out[i], attn_sink[i], lru_conv_w[i], lru_conv_b[i], lru_wa[i],
                lru_ba[i], lru_wi[i], lru_bi[i], lru_lam[i], tail_l, tail_c)
        else:
            i = layer // 2
            yl = _rwkv7_mixer(hl, hc, rw_mu[i], rw_rkv[i], rw_w0[i], rw_w1[i], rw_w2[i], rw_a0[i], rw_a1[i],
                              rw_a2[i], rw_g1[i], rw_g2[i], rw_kk[i], rw_ka[i], rw_rk[i], rw_gn_g[i],
                              rw_gn_b[i], rw_wo[i])
            xl, hl, aff_l = _resid_norm_tm(xl, yl, m_l[2], norm_ffn[layer], m_l[3], m_l[4], delta_tm=True,
                                           h_tm=False, router_w=router_w[layer])
        moe_l = _moe(hl, aff_l, layer, w1b, w3b, w2b)
        if last:
            return _final_norm(xl, moe_l, m_l[5], final_norm)
        assert (layer + 1) % 2 == 1
        n_l, n_c = mods_l[layer + 1], mods_c[layer + 1]
        xl, hl = _resid_norm_tm(xl, moe_l, m_l[5], norm_mix[layer + 1], n_l[0], n_l[1], delta_tm=False, h_tm=True)
        moe_c = _moe(hc, aff_c, layer, w1b, w3b, w2b)
        xc, hc = _resid_norm_tm(xc, moe_c, m_c[5], norm_mix[layer + 1], n_c[0], n_c[1], delta_tm=False, h_tm=True)
```
